```python
import jax, jax.numpy as jnp
from jax import lax
import numpy as np

D_MODEL = 1024
BATCH = 16
SEQ = 2048
DEPTH = 2

N_MIXERS = 2
N_MLA_LAYERS = (DEPTH + N_MIXERS - 1) // N_MIXERS
N_RNN_LAYERS = DEPTH // N_MIXERS

N_HEADS = 16
Q_LORA = 384
KV_LORA = 256
QK_NOPE = 64
QK_ROPE = 32
QK_HEAD = QK_NOPE + QK_ROPE
V_HEAD = 64
ROPE_THETA = 10000.0
Q_BLOCK = 128

D_RNN = D_MODEL
RNN_BLOCKS = 4
RNN_BW = D_RNN // RNN_BLOCKS
CONV_W = 4
LRU_C = 8.0

N_EXPERTS = 32
N_GROUPS = 8
EXPERTS_PER_GROUP = N_EXPERTS // N_GROUPS
TOP_K = 2
D_EXPERT = 512
MOE_BLOCK = 128

EPS = 1e-6

kernel_name = "hybrid_mla_rglru_grouped_moe_adaln"


def rmsnorm(x, g):
    xf = x.astype(jnp.float32)
    y = xf * lax.rsqrt(jnp.mean(xf * xf, axis=-1, keepdims=True) + EPS)
    return (y * g.astype(jnp.float32)).astype(x.dtype)


def modulate(x, g, shift, scale):
    return rmsnorm(x, g) * (1 + scale[:, None, :]) + shift[:, None, :]


def rope(x, cos, sin):
    x1, x2 = jnp.split(x, 2, axis=-1)
    c = cos.astype(x.dtype)
    s = sin.astype(x.dtype)
    return jnp.concatenate([x1 * c - x2 * s, x1 * s + x2 * c], axis=-1)


def mla_mixer(h, cos, sin, w_in, q_a_norm, kv_a_norm, w_q_b, w_kv_b, q_norm, k_norm, w_o):
    B, S, _ = h.shape
    lat = h @ w_in
    q_lat, kv_lat, k_pe = jnp.split(lat, [Q_LORA, Q_LORA + KV_LORA], axis=-1)
    q = (rmsnorm(q_lat, q_a_norm) @ w_q_b).reshape(B, S, N_HEADS, QK_HEAD)
    kv = (rmsnorm(kv_lat, kv_a_norm) @ w_kv_b).reshape(B, S, N_HEADS, QK_NOPE + V_HEAD)
    k_nope, v = jnp.split(kv, [QK_NOPE], axis=-1)
    k = jnp.concatenate([k_nope, jnp.broadcast_to(k_pe[:, :, None, :], (B, S, N_HEADS, QK_ROPE))], axis=-1)
    q = rmsnorm(q, q_norm)
    k = rmsnorm(k, k_norm)
    c4 = cos[:, :, None, :]
    s4 = sin[:, :, None, :]
    q = jnp.concatenate([q[..., :QK_NOPE], rope(q[..., QK_NOPE:], c4, s4)], axis=-1)
    k = jnp.concatenate([k[..., :QK_NOPE], rope(k[..., QK_NOPE:], c4, s4)], axis=-1)
    scale = QK_HEAD ** -0.5
    nqb = S // Q_BLOCK
    qb = q.reshape(B, nqb, Q_BLOCK, N_HEADS, QK_HEAD).transpose(1, 0, 2, 3, 4)

    def attend(q_blk):
        s = jnp.einsum('bqhd,bkhd->bhqk', q_blk, k, preferred_element_type=jnp.float32) * scale
        p = jax.nn.softmax(s, axis=-1).astype(v.dtype)
        return jnp.einsum('bhqk,bkhd->bqhd', p, v)

    o = lax.map(attend, qb)
    o = o.transpose(1, 0, 2, 3, 4).reshape(B, S, N_HEADS * V_HEAD)
    return o @ w_o


def dwconv_centred(x, w, b):
    S = x.shape[1]
    left = CONV_W // 2
    xp = jnp.pad(x, ((0, 0), (left, CONV_W - 1 - left), (0, 0)))
    return sum(xp[:, k:k + S] * w[k] for k in range(CONV_W)) + b


def block_diag(x, w, b):
    B, S, _ = x.shape
    xb = x.reshape(B, S, RNN_BLOCKS, RNN_BW)
    return jnp.einsum('bsnd,nde->bsne', xb, w).reshape(B, S, D_RNN) + b


def _lru_combine(e1, e2):
    a1, b1 = e1
    a2, b2 = e2
    return a1 * a2, a2 * b1 + b2


def lru_scan(xc, lam, w_r, b_r, w_i, b_i, reverse):
    f32 = jnp.float32
    r = jax.nn.sigmoid(block_diag(xc, w_r, b_r).astype(f32))
    i = jax.nn.sigmoid(block_diag(xc, w_i, b_i).astype(f32))
    log_a = -LRU_C * r * jax.nn.softplus(-lam.astype(f32))
    a = jnp.exp(log_a)
    mult = jnp.sqrt(-jnp.expm1(2.0 * log_a))
    S = xc.shape[1]
    first = (jnp.arange(S) == (S - 1 if reverse else 0))[None, :, None]
    mult = jnp.where(first, 1.0, mult)
    bx = mult * i * xc.astype(f32)
    _, hs = lax.associative_scan(_lru_combine, (a, bx), reverse=reverse, axis=1)
    return hs


def rglru_mixer(h, w_in, conv_w, conv_b, lam_f, w_rf, b_rf, w_if, b_if, lam_b, w_rb, b_rb, w_ib, b_ib, w_o):
    gate, xb = jnp.split(h @ w_in, 2, axis=-1)
    xc = dwconv_centred(xb, conv_w, conv_b)
    hf = lru_scan(xc, lam_f, w_rf, b_rf, w_if, b_if, reverse=False)
    hb = lru_scan(xc, lam_b, w_rb, b_rb, w_ib, b_ib, reverse=True)
    y = jax.nn.gelu(gate) * (hf + hb).astype(h.dtype)
    return y @ w_o


def route(xt, w_router, router_bias):
    T = xt.shape[0]
    scores = jax.nn.sigmoid(jnp.dot(xt, w_router, preferred_element_type=jnp.float32))
    biased = scores + router_bias.astype(jnp.float32)
    grp = biased.reshape(T, N_GROUPS, EXPERTS_PER_GROUP)
    gscore = lax.top_k(grp, TOP_K)[0].sum(-1)
    g_sel = jnp.argmax(gscore, axis=-1)
    in_group = (jnp.arange(N_EXPERTS) // EXPERTS_PER_GROUP)[None, :] == g_sel[:, None]
    masked = jnp.where(in_group, biased, -jnp.inf)
    _, idx = lax.top_k(masked, TOP_K)
    w = jnp.take_along_axis(scores, idx, axis=-1)
    w = w / jnp.sum(w, axis=-1, keepdims=True)
    return idx, w


def moe(xt, idx, wts, w_gu, w_dn):
    T, D = xt.shape
    A = T * TOP_K
    P = A + N_EXPERTS * MOE_BLOCK
    nb = P // MOE_BLOCK
    e_flat = idx.reshape(A)
    order = jnp.argsort(e_flat)
    e_sorted = e_flat[order]
    counts = jnp.bincount(e_flat, length=N_EXPERTS)
    start = jnp.cumsum(counts) - counts
    padded = ((counts + MOE_BLOCK - 1) // MOE_BLOCK) * MOE_BLOCK
    pend = jnp.cumsum(padded)
    pstart = pend - padded
    dest_sorted = pstart[e_sorted] + (jnp.arange(A) - start[e_sorted])
    dest = jnp.zeros((A,), jnp.int32).at[order].set(dest_sorted.astype(jnp.int32))
    row_tok = jnp.full((P,), T, jnp.int32).at[dest].set((jnp.arange(A) // TOP_K).astype(jnp.int32))
    xs = jnp.concatenate([xt, jnp.zeros((1, D), xt.dtype)], axis=0)[row_tok]
    blk_exp = jnp.minimum(jnp.searchsorted(pend, jnp.arange(nb) * MOE_BLOCK, side='right'), N_EXPERTS - 1)

    def expert_block(args):
        xb, e = args
        g, u = jnp.split(xb @ w_gu[e], 2, axis=-1)
        return (jax.nn.silu(g) * u) @ w_dn[e]

    ys = lax.map(expert_block, (xs.reshape(nb, MOE_BLOCK, D), blk_exp)).reshape(P, D)
    y = ys[dest].reshape(T, TOP_K, D)
    return jnp.einsum('tk,tkd->td', wts.astype(y.dtype), y)


def _normal(k, shape, fan_in, scale=1.0):
    return jax.random.normal(k, shape, jnp.float32) * (scale * fan_in ** -0.5)


def _gain(k, shape):
    return 1.0 + 0.05 * jax.random.normal(k, shape, jnp.float32)


def _small(k, shape, s=0.02):
    return s * jax.random.normal(k, shape, jnp.float32)


def _lru_lambda(k, shape):
    a0 = jax.random.uniform(k, shape, jnp.float32, 0.9, 0.999)
    p = a0 ** (1.0 / LRU_C)
    return jnp.log(p) - jnp.log1p(-p)


def setup_inputs(seed: int = 0) -> dict:
    key = jax.random.key(seed)
    ks = iter(jax.random.split(key, 48))
    NM, NR = N_MLA_LAYERS, N_RNN_LAYERS
    d = {}
    d['x'] = jax.random.normal(next(ks), (BATCH, SEQ, D_MODEL), jnp.float32)
    d['c'] = jax.random.normal(next(ks), (BATCH, D_MODEL), jnp.float32)
    offs = jax.random.randint(next(ks), (BATCH, 1), 0, 1024, jnp.int32)
    d['positions'] = (offs + jnp.arange(SEQ, dtype=jnp.int32)[None, :]).astype(jnp.int32)
    d['norm_mix'] = _gain(next(ks), (DEPTH, D_MODEL))
    d['norm_ffn'] = _gain(next(ks), (DEPTH, D_MODEL))
    d['w_ada'] = _normal(next(ks), (DEPTH, D_MODEL, 6 * D_MODEL), D_MODEL, 0.5)
    d['b_ada'] = _small(next(ks), (DEPTH, 6 * D_MODEL))
    d['mla_w_in'] = _normal(next(ks), (NM, D_MODEL, Q_LORA + KV_LORA + QK_ROPE), D_MODEL)
    d['mla_q_a_norm'] = _gain(next(ks), (NM, Q_LORA))
    d['mla_kv_a_norm'] = _gain(next(ks), (NM, KV_LORA))
    d['mla_w_q_b'] = _normal(next(ks), (NM, Q_LORA, N_HEADS * QK_HEAD), Q_LORA)
    d['mla_w_kv_b'] = _normal(next(ks), (NM, KV_LORA, N_HEADS * (QK_NOPE + V_HEAD)), KV_LORA)
    d['mla_q_norm'] = _gain(next(ks), (NM, QK_HEAD))
    d['mla_k_norm'] = _gain(next(ks), (NM, QK_HEAD))
    d['mla_w_o'] = _normal(next(ks), (NM, N_HEADS * V_HEAD, D_MODEL), N_HEADS * V_HEAD)
    d['rnn_w_in'] = _normal(next(ks), (NR, D_MODEL, 2 * D_RNN), D_MODEL)
    d['rnn_conv_w'] = _normal(next(ks), (NR, CONV_W, D_RNN), CONV_W)
    d['rnn_conv_b'] = _small(next(ks), (NR, D_RNN))
    d['rnn_lam_f'] = _lru_lambda(next(ks), (NR, D_RNN))
    d['rnn_w_rf'] = _normal(next(ks), (NR, RNN_BLOCKS, RNN_BW, RNN_BW), RNN_BW)
    d['rnn_b_rf'] = _small(next(ks), (NR, D_RNN), 0.1)
    d['rnn_w_if'] = _normal(next(ks), (NR, RNN_BLOCKS, RNN_BW, RNN_BW), RNN_BW)
    d['rnn_b_if'] = _small(next(ks), (NR, D_RNN), 0.1)
    d['rnn_lam_b'] = _lru_lambda(next(ks), (NR, D_RNN))
    d['rnn_w_rb'] = _normal(next(ks), (NR, RNN_BLOCKS, RNN_BW, RNN_BW), RNN_BW)
    d['rnn_b_rb'] = _small(next(ks), (NR, D_RNN), 0.1)
    d['rnn_w_ib'] = _normal(next(ks), (NR, RNN_BLOCKS, RNN_BW, RNN_BW), RNN_BW)
    d['rnn_b_ib'] = _small(next(ks), (NR, D_RNN), 0.1)
    d['rnn_w_o'] = _normal(next(ks), (NR, D_RNN, D_MODEL), D_RNN)
    d['w_router'] = _normal(next(ks), (D_MODEL, N_EXPERTS), D_MODEL)
    d['router_bias'] = _small(next(ks), (N_EXPERTS,), 0.01)
    d['moe_w_gu'] = _normal(next(ks), (DEPTH, N_EXPERTS, D_MODEL, 2 * D_EXPERT), D_MODEL)
    d['moe_w_dn'] = _normal(next(ks), (DEPTH, N_EXPERTS, D_EXPERT, D_MODEL), D_EXPERT)
    return d


def reference(x, c, positions, norm_mix, norm_ffn, w_ada, b_ada,
              mla_w_in, mla_q_a_norm, mla_kv_a_norm, mla_w_q_b, mla_w_kv_b, mla_q_norm, mla_k_norm, mla_w_o,
              rnn_w_in, rnn_conv_w, rnn_conv_b,
              rnn_lam_f, rnn_w_rf, rnn_b_rf, rnn_w_if, rnn_b_if,
              rnn_lam_b, rnn_w_rb, rnn_b_rb, rnn_w_ib, rnn_b_ib, rnn_w_o,
              w_router, router_bias, moe_w_gu, moe_w_dn):
    B, S, D = x.shape
    half = QK_ROPE // 2
    inv_freq = ROPE_THETA ** (-jnp.arange(half, dtype=jnp.float32) / half)
    ang = positions.astype(jnp.float32)[..., None] * inv_freq
    cos, sin = jnp.cos(ang), jnp.sin(ang)
    c_act = jax.nn.silu(c)
    for i in range(DEPTH):
        mod = c_act @ w_ada[i] + b_ada[i]
        sh1, sc1, g1, sh2, sc2, g2 = jnp.split(mod, 6, axis=-1)
        h = modulate(x, norm_mix[i], sh1, sc1)
        j = i // N_MIXERS
        if i % N_MIXERS == 0:
            m = mla_mixer(h, cos, sin, mla_w_in[j], mla_q_a_norm[j], mla_kv_a_norm[j], mla_w_q_b[j],
                          mla_w_kv_b[j], mla_q_norm[j], mla_k_norm[j], mla_w_o[j])
        else:
            m = rglru_mixer(h, rnn_w_in[j], rnn_conv_w[j], rnn_conv_b[j],
                            rnn_lam_f[j], rnn_w_rf[j], rnn_b_rf[j], rnn_w_if[j], rnn_b_if[j],
                            rnn_lam_b[j], rnn_w_rb[j], rnn_b_rb[j], rnn_w_ib[j], rnn_b_ib[j], rnn_w_o[j])
        x = x + g1[:, None, :] * m
        h = modulate(x, norm_ffn[i], sh2, sc2)
        ht = h.reshape(B * S, D)
        idx, wts = route(ht, w_router, router_bias)
        y = moe(ht, idx, wts, moe_w_gu[i], moe_w_dn[i])
        x = x + g2[:, None, :] * y.reshape(B, S, D)
    return x
```

```python
import functools

import jax
import jax.numpy as jnp
from jax import lax
from jax.experimental import pallas as pl
from jax.experimental.pallas import tpu as pltpu

F32 = jnp.float32
BF16 = jnp.bfloat16

D_MODEL = 1024
N_HEADS = 16
Q_LORA = 384
KV_LORA = 256
QK_NOPE = 64
QK_ROPE = 32
QK_HEAD = QK_NOPE + QK_ROPE
V_HEAD = 64
ROPE_THETA = 10000.0
D_RNN = D_MODEL
RNN_BLOCKS = 4
RNN_BW = D_RNN // RNN_BLOCKS
CONV_W = 4
LRU_C = 8.0
N_EXPERTS = 32
N_GROUPS = 8
EXPERTS_PER_GROUP = N_EXPERTS // N_GROUPS
TOP_K = 2
D_EXPERT = 512
EPS = 1e-6

LANES = 128
SUBLANES = 8
VMEM_LIMIT = 52 * 1024 * 1024

ROW_TILE = 256
Q_TILE = 512
MOE_TILE = 256
SCAN_ROWS = 256


def _dot(a, b):
    return jnp.dot(a, b, preferred_element_type=F32)


def _split_bf16(a):
    hi = a.astype(BF16)
    lo = (a - hi.astype(F32)).astype(BF16)
    return hi, lo


def _dot_split(a, b):
    ah, al = _split_bf16(a)
    bh, bl = _split_bf16(b)
    return _dot(ah, bh) + (_dot(ah, bl) + _dot(al, bh))


def _rms(x, gain, n):
    ms = jnp.sum(x * x, axis=-1, keepdims=True) * (1.0 / n)
    return x * lax.rsqrt(ms + EPS) * gain


def _modulate(x, gain, shift, scale):
    return _rms(x, gain, x.shape[-1]) * (1.0 + scale) + shift


def _params(*sem):
    return pltpu.CompilerParams(dimension_semantics=sem, vmem_limit_bytes=VMEM_LIMIT)


def _ada_kernel(c_ref, w_ref, b_ref, o_ref):
    c = c_ref[...]
    o_ref[...] = _dot_split(c * jax.nn.sigmoid(c), w_ref[...]) + b_ref[...]


def _ada(c, w_ada, b_ada):
    depth, d, n = w_ada.shape
    bsz = c.shape[0]
    tn = 1536
    return pl.pallas_call(
        _ada_kernel,
        grid=(depth, n // tn),
        in_specs=[
            pl.BlockSpec((bsz, d), lambda l, j: (0, 0)),
            pl.BlockSpec((None, d, tn), lambda l, j: (l, 0, j)),
            pl.BlockSpec((None, 1, tn), lambda l, j: (l, 0, j)),
        ],
        out_specs=pl.BlockSpec((None, bsz, tn), lambda l, j: (l, 0, j)),
        out_shape=jax.ShapeDtypeStruct((depth, bsz, n), F32),
        compiler_params=_params("arbitrary", "arbitrary"),
        name="adaln_mod",
    )(c, w_ada, b_ada.reshape(depth, 1, n))


def _head_finish(s, gain, cos_t, sin_t, lane, out_scale):
    y = _rms(s, gain, QK_HEAD)
    rot = jnp.where(lane < QK_NOPE + QK_ROPE // 2,
                    pltpu.roll(y, LANES - QK_ROPE // 2, 1),
                    pltpu.roll(y, QK_ROPE // 2, 1))
    out = y * cos_t + rot * sin_t
    if out_scale != 1.0:
        out = out * out_scale
    return out.astype(BF16)


def _mla_in_kernel(x_ref, g_ref, sh_ref, sc_ref, win_ref, qan_ref, kvan_ref, wq_ref, wkv_ref,
                   qn_ref, kn_ref, cos_ref, sin_ref, q_out, k_out, v_out):
    h = _modulate(x_ref[...], g_ref[...], sh_ref[...], sc_ref[...])
    lat = _dot(h.astype(BF16), win_ref[...])
    q_lat = lat[:, :Q_LORA]
    kv_lat = lat[:, Q_LORA:Q_LORA + KV_LORA]
    kpe = lat[:, Q_LORA + KV_LORA:]
    q_all = _dot(_rms(q_lat, qan_ref[...], Q_LORA).astype(BF16), wq_ref[...])
    kv_all = _dot(_rms(kv_lat, kvan_ref[...], KV_LORA).astype(BF16), wkv_ref[...])
    cos_t = cos_ref[...]
    sin_t = sin_ref[...]
    lane = lax.broadcasted_iota(jnp.int32, cos_t.shape, 1)
    qg = qn_ref[...]
    kg = kn_ref[...]
    for hh in range(N_HEADS):
        sl = slice(hh * LANES, (hh + 1) * LANES)
        q_out[hh] = _head_finish(q_all[:, sl], qg, cos_t, sin_t, lane, QK_HEAD ** -0.5)
        k_out[hh] = _head_finish(kv_all[:, sl] + kpe, kg, cos_t, sin_t, lane, 1.0)
    v_out[...] = kv_all[:, N_HEADS * LANES:].astype(BF16)


def _mla_in(x, gain, shift, scale, w_in, q_a_norm, kv_a_norm, w_q, w_kv, q_norm, k_norm, cos_t, sin_t):
    bsz, seq, d = x.shape
    tm = min(ROW_TILE, seq)
    row = lambda b, i: (b, i, 0)
    per_b = lambda b, i: (b, 0, 0)
    const = lambda b, i: (0, 0)
    full = lambda a: pl.BlockSpec(a.shape, const)
    return pl.pallas_call(
        _mla_in_kernel,
        grid=(bsz, seq // tm),
        in_specs=[
            pl.BlockSpec((None, tm, d), row),
            full(gain),
            pl.BlockSpec((None, 1, d), per_b),
            pl.BlockSpec((None, 1, d), per_b),
            full(w_in), full(q_a_norm), full(kv_a_norm), full(w_q), full(w_kv),
            full(q_norm), full(k_norm),
            pl.BlockSpec((None, tm, LANES), row),
            pl.BlockSpec((None, tm, LANES), row),
        ],
        out_specs=[
            pl.BlockSpec((None, N_HEADS, tm, LANES), lambda b, i: (b, 0, i, 0)),
            pl.BlockSpec((None, N_HEADS, tm, LANES), lambda b, i: (b, 0, i, 0)),
            pl.BlockSpec((None, tm, N_HEADS * V_HEAD), row),
        ],
        out_shape=[
            jax.ShapeDtypeStruct((bsz, N_HEADS, seq, LANES), BF16),
            jax.ShapeDtypeStruct((bsz, N_HEADS, seq, LANES), BF16),
            jax.ShapeDtypeStruct((bsz, seq, N_HEADS * V_HEAD), BF16),
        ],
        compiler_params=_params("arbitrary", "arbitrary"),
        name="mla_in",
    )(x, gain, shift, scale, w_in, q_a_norm, kv_a_norm, w_q, w_kv, q_norm, k_norm, cos_t, sin_t)


def _attn_kernel(q_ref, k_ref, v_ref, o_ref):
    v = v_ref[...]
    outs = []
    for j in range(2):
        s = lax.dot_general(q_ref[j], k_ref[j], (((1,), (1,)), ((), ())),
                            preferred_element_type=F32)
        m = jnp.max(s, axis=-1, keepdims=True)
        p = jnp.exp(s - m)
        l = jnp.sum(p, axis=-1, keepdims=True)
        outs.append(_dot(p.astype(BF16), v) / l)
    lane = lax.broadcasted_iota(jnp.int32, outs[0].shape, 1)
    o_ref[...] = jnp.where(lane < V_HEAD, outs[0], outs[1]).astype(BF16)


def _attention(q, k, v):
    bsz, _, seq, _ = q.shape
    tq = min(Q_TILE, seq)
    return pl.pallas_call(
        _attn_kernel,
        grid=(bsz, N_HEADS // 2, seq // tq),
        in_specs=[
            pl.BlockSpec((None, 2, tq, LANES), lambda b, h, i: (b, h, i, 0)),
            pl.BlockSpec((None, 2, seq, LANES), lambda b, h, i: (b, h, 0, 0)),
            pl.BlockSpec((None, seq, LANES), lambda b, h, i: (b, 0, h)),
        ],
        out_specs=pl.BlockSpec((None, tq, LANES), lambda b, h, i: (b, i, h)),
        out_shape=jax.ShapeDtypeStruct((bsz, seq, N_HEADS * V_HEAD), BF16),
        compiler_params=_params("arbitrary", "arbitrary", "arbitrary"),
        name="mla_attention",
    )(q, k, v)


def _first_index_of_max(vals):
    m = vals[0]
    for v in vals[1:]:
        m = jnp.maximum(m, v)
    idx = jnp.full(m.shape, float(len(vals) - 1), F32)
    for j in range(len(vals) - 2, -1, -1):
        idx = jnp.where(vals[j] == m, float(j), idx)
    return m, idx


def _route(h2, wr1_ref, wr2_ref, rb_ref):
    hh, hl = _split_bf16(h2)
    logits = (_dot(hh, wr1_ref[...]) + _dot(hl, wr2_ref[...])).T
    logit = logits[0:N_EXPERTS] + logits[N_EXPERTS:2 * N_EXPERTS] + logits[2 * N_EXPERTS:3 * N_EXPERTS]
    score = jax.nn.sigmoid(logit)
    biased = score + rb_ref[...]
    a = [biased[j * N_GROUPS:(j + 1) * N_GROUPS] for j in range(EXPERTS_PER_GROUP)]
    sc = [score[j * N_GROUPS:(j + 1) * N_GROUPS] for j in range(EXPERTS_PER_GROUP)]
    hi1, lo1 = jnp.maximum(a[0], a[1]), jnp.minimum(a[0], a[1])
    hi2, lo2 = jnp.maximum(a[2], a[3]), jnp.minimum(a[2], a[3])
    gscore = jnp.maximum(hi1, hi2) + jnp.maximum(jnp.minimum(hi1, hi2), jnp.maximum(lo1, lo2))
    gmax = jnp.max(gscore, axis=0, keepdims=True)
    giota = lax.broadcasted_iota(jnp.int32, gscore.shape, 0).astype(F32)
    gsel = jnp.min(jnp.where(gscore == gmax, giota, float(N_GROUPS)), axis=0, keepdims=True)
    onehot = giota == gsel
    pick = lambda t: jnp.sum(jnp.where(onehot, t, 0.0), axis=0, keepdims=True)
    bj = [pick(t) for t in a]
    sj = [pick(t) for t in sc]
    _, i1 = _first_index_of_max(bj)
    bj2 = [jnp.where(i1 == float(j), -jnp.inf, bj[j]) for j in range(EXPERTS_PER_GROUP)]
    _, i2 = _first_index_of_max(bj2)
    sel = lambda i: jnp.where(i == 0.0, sj[0], jnp.where(i == 1.0, sj[1], jnp.where(i == 2.0, sj[2], sj[3])))
    w1, w2 = sel(i1), sel(i2)
    den = w1 + w2
    base = gsel * float(EXPERTS_PER_GROUP)
    return ((base + i1).astype(jnp.int32), (base + i2).astype(jnp.int32)), (w1 / den, w2 / den)


def _mix_out_kernel(has_gate, *refs):
    if has_gate:
        a_ref, hs_ref, x_ref, wo_ref, g1_ref, g_ref, sh_ref, sc_ref, wr1_ref, wr2_ref, rb_ref, \
            x_out, h_out, idx_out, wts_out = refs
        a = (a_ref[...].astype(F32) * hs_ref[...]).astype(BF16)
    else:
        a_ref, x_ref, wo_ref, g1_ref, g_ref, sh_ref, sc_ref, wr1_ref, wr2_ref, rb_ref, \
            x_out, h_out, idx_out, wts_out = refs
        a = a_ref[...]
    x1 = x_ref[...] + g1_ref[...] * _dot(a, wo_ref[...])
    x_out[...] = x1
    h2 = _modulate(x1, g_ref[...], sh_ref[...], sc_ref[...])
    h_out[...] = h2
    idx, wts = _route(h2, wr1_ref, wr2_ref, rb_ref)
    for k in range(TOP_K):
        idx_out[k:k + 1, :] = idx[k]
        wts_out[k:k + 1, :] = wts[k]


def _mix_out(a, hs, x, w_o, gate1, gain, shift, scale, wr1, wr2, rbias):
    bsz, seq, d = x.shape
    tm = min(ROW_TILE, seq)
    row = lambda b, i: (b, i, 0)
    per_b = lambda b, i: (b, 0, 0)
    const = lambda b, i: (0, 0)
    full = lambda t: pl.BlockSpec(t.shape, const)
    vec = pl.BlockSpec((None, 1, d), per_b)
    acts = [a] if hs is None else [a, hs]
    return pl.pallas_call(
        functools.partial(_mix_out_kernel, hs is not None),
        grid=(bsz, seq // tm),
        in_specs=[pl.BlockSpec((None, tm, t.shape[-1]), row) for t in acts] + [
            pl.BlockSpec((None, tm, d), row), full(w_o), vec, full(gain), vec, vec,
            full(wr1), full(wr2), full(rbias),
        ],
        out_specs=[
            pl.BlockSpec((None, tm, d), row),
            pl.BlockSpec((None, tm, d), row),
            pl.BlockSpec((None, TOP_K, tm), lambda b, i: (b, 0, i)),
            pl.BlockSpec((None, TOP_K, tm), lambda b, i: (b, 0, i)),
        ],
        out_shape=[
            jax.ShapeDtypeStruct((bsz, seq, d), F32),
            jax.ShapeDtypeStruct((bsz, seq, d), F32),
            jax.ShapeDtypeStruct((bsz, TOP_K, seq), jnp.int32),
            jax.ShapeDtypeStruct((bsz, TOP_K, seq), F32),
        ],
        compiler_params=_params("arbitrary", "arbitrary"),
        name="mix_out_route",
    )(*acts, x, w_o, gate1, gain, shift, scale, wr1, wr2, rbias)


def _row_gather(idx_ref, n, src_hbm, dst_ref, dst_base, sem):
    def start(r, carry):
        pltpu.make_async_copy(src_hbm.at[pl.ds(idx_ref[r], 1)],
                              dst_ref.at[pl.ds(dst_base + r, 1)], sem).start()
        return carry
    lax.fori_loop(0, n, start, 0, unroll=8)

    def wait(r, carry):
        pltpu.make_async_copy(src_hbm.at[pl.ds(0, 1)],
                              dst_ref.at[pl.ds(dst_base + r, 1)], sem).wait()
        return carry
    lax.fori_loop(0, n, wait, 0, unroll=8)


def _dispatch_kernel(tok_ref, src_hbm, out_ref, sem):
    _row_gather(tok_ref.at[0], out_ref.shape[0], src_hbm, out_ref, 0, sem)


def _dispatch(row_tok, src):
    n_rows, d = row_tok.shape[0], src.shape[-1]
    nb = n_rows // MOE_TILE
    return pl.pallas_call(
        _dispatch_kernel,
        grid=(nb,),
        in_specs=[
            pl.BlockSpec((None, 1, MOE_TILE), lambda i: (i, 0, 0), memory_space=pltpu.SMEM),
            pl.BlockSpec(memory_space=pl.ANY),
        ],
        out_specs=pl.BlockSpec((MOE_TILE, d), lambda i: (i, 0)),
        out_shape=jax.ShapeDtypeStruct((n_rows, d), src.dtype),
        scratch_shapes=[pltpu.SemaphoreType.DMA(())],
        compiler_params=_params("arbitrary"),
        name="moe_dispatch",
    )(row_tok.reshape(nb, 1, MOE_TILE), src)


def _expert_kernel(blk_exp_ref, n_used_ref, xs_ref, wgu_ref, wdn_ref, ys_ref):
    i = pl.program_id(0)

    @pl.when(i < n_used_ref[0])
    def _():
        gu = _dot(xs_ref[...].astype(BF16), wgu_ref[...])
        g = gu[:, :D_EXPERT]
        u = gu[:, D_EXPERT:]
        mid = (g * jax.nn.sigmoid(g) * u).astype(BF16)
        ys_ref[...] = _dot(mid, wdn_ref[...])

    @pl.when(i >= n_used_ref[0])
    def _():
        ys_ref[...] = jnp.zeros_like(ys_ref)


def _experts(blk_exp, n_used, xs, w_gu, w_dn):
    n_rows, d = xs.shape
    nb = n_rows // MOE_TILE
    grid_spec = pltpu.PrefetchScalarGridSpec(
        num_scalar_prefetch=2,
        grid=(nb,),
        in_specs=[
            pl.BlockSpec((MOE_TILE, d), lambda i, be, nu: (i, 0)),
            pl.BlockSpec((None, d, 2 * D_EXPERT), lambda i, be, nu: (be[i], 0, 0)),
            pl.BlockSpec((None, D_EXPERT, d), lambda i, be, nu: (be[i], 0, 0)),
        ],
        out_specs=pl.BlockSpec((MOE_TILE, d), lambda i, be, nu: (i, 0)),
    )
    return pl.pallas_call(
        _expert_kernel,
        grid_spec=grid_spec,
        out_shape=jax.ShapeDtypeStruct((n_rows, d), F32),
        compiler_params=_params("arbitrary"),
        name="moe_experts",
    )(blk_exp, n_used, xs, w_gu, w_dn)


def _combine_kernel(dest_ref, ys_hbm, x_ref, wts_ref, g2_ref, x_out, buf, sem):
    tm = x_ref.shape[0]
    _row_gather(dest_ref.at[0], TOP_K * tm, ys_hbm, buf, 0, sem)
    w = wts_ref[...]
    y = w[:, 0:1] * buf[0:tm, :] + w[:, 1:2] * buf[tm:2 * tm, :]
    x_out[...] = x_ref[...] + g2_ref[...] * y


def _combine(dest_km, ys, x, wts_col, gate2):
    bsz, seq, d = x.shape
    tm = min(ROW_TILE, seq)
    nt = seq // tm
    return pl.pallas_call(
        _combine_kernel,
        grid=(bsz, nt),
        in_specs=[
            pl.BlockSpec((None, None, 1, TOP_K * tm), lambda b, i: (b, i, 0, 0), memory_space=pltpu.SMEM),
            pl.BlockSpec(memory_space=pl.ANY),
            pl.BlockSpec((None, tm, d), lambda b, i: (b, i, 0)),
            pl.BlockSpec((None, tm, TOP_K), lambda b, i: (b, i, 0)),
            pl.BlockSpec((None, 1, d), lambda b, i: (b, 0, 0)),
        ],
        out_specs=pl.BlockSpec((None, tm, d), lambda b, i: (b, i, 0)),
        out_shape=jax.ShapeDtypeStruct((bsz, seq, d), F32),
        scratch_shapes=[pltpu.VMEM((TOP_K * tm, d), F32), pltpu.SemaphoreType.DMA(())],
        compiler_params=_params("arbitrary", "arbitrary"),
        name="moe_combine",
    )(dest_km, ys, x, wts_col, gate2)


def _moe(h2, idx, wts, x, gate2, w_gu, w_dn):
    bsz, seq, d = x.shape
    n_tok = bsz * seq
    n_asg = n_tok * TOP_K
    n_rows = n_asg + N_EXPERTS * MOE_TILE
    nb = n_rows // MOE_TILE
    tm = min(ROW_TILE, seq)
    e_flat = idx.transpose(0, 2, 1).reshape(n_asg)
    onehot = (e_flat[:, None] == jnp.arange(N_EXPERTS, dtype=jnp.int32)[None, :]).astype(jnp.int32)
    csum = jnp.cumsum(onehot, axis=0)
    counts = csum[-1]
    rank = jnp.take_along_axis(csum, e_flat[:, None], axis=1)[:, 0] - 1
    padded = ((counts + MOE_TILE - 1) // MOE_TILE) * MOE_TILE
    pend = jnp.cumsum(padded)
    pstart = pend - padded
    dest = (pstart[e_flat] + rank).astype(jnp.int32)
    row_tok = jnp.zeros((n_rows,), jnp.int32).at[dest].set(
        jnp.arange(n_asg, dtype=jnp.int32) // TOP_K)
    blk_exp = jnp.minimum(
        jnp.searchsorted(pend, jnp.arange(nb, dtype=jnp.int32) * MOE_TILE, side="right"),
        N_EXPERTS - 1).astype(jnp.int32)
    n_used = (pend[-1:] // MOE_TILE).astype(jnp.int32)
    dest_km = dest.reshape(bsz, seq // tm, tm, TOP_K).transpose(0, 1, 3, 2).reshape(
        bsz, seq // tm, 1, TOP_K * tm)
    xs = _dispatch(row_tok, h2.reshape(n_tok, d))
    ys = _experts(blk_exp, n_used, xs, w_gu, w_dn)
    return _combine(dest_km, ys, x, wts.transpose(0, 2, 1), gate2)


def _rnn_in_kernel(x_ref, g_ref, sh_ref, sc_ref, w_ref, gate_out, xb_out):
    h = _modulate(x_ref[...], g_ref[...], sh_ref[...], sc_ref[...])
    u = _dot(h.astype(BF16), w_ref[...])
    gate_out[...] = jax.nn.gelu(u[:, :D_RNN]).astype(BF16)
    xb_out[...] = u[:, D_RNN:]


def _rnn_in(x, gain, shift, scale, w_in):
    bsz, seq, d = x.shape
    tm = min(ROW_TILE, seq)
    row = lambda b, i: (b, i, 0)
    per_b = lambda b, i: (b, 0, 0)
    const = lambda b, i: (0, 0)
    return pl.pallas_call(
        _rnn_in_kernel,
        grid=(bsz, seq // tm),
        in_specs=[
            pl.BlockSpec((None, tm, d), row),
            pl.BlockSpec(gain.shape, const),
            pl.BlockSpec((None, 1, d), per_b),
            pl.BlockSpec((None, 1, d), per_b),
            pl.BlockSpec(w_in.shape, const),
        ],
        out_specs=[pl.BlockSpec((None, tm, D_RNN), row), pl.BlockSpec((None, tm, D_RNN), row)],
        out_shape=[jax.ShapeDtypeStruct((bsz, seq, D_RNN), BF16),
                   jax.ShapeDtypeStruct((bsz, seq, D_RNN), F32)],
        compiler_params=_params("arbitrary", "arbitrary"),
        name="rnn_in",
    )(x, gain, shift, scale, w_in)


def _chunk_scan(a, b, reverse):
    row = lax.broadcasted_iota(jnp.int32, a.shape, 0)
    for dd in (1, 2, 4):
        if reverse:
            keep = row < SUBLANES - dd
            shift = SUBLANES - dd
        else:
            keep = row >= dd
            shift = dd
        a_sh = jnp.where(keep, pltpu.roll(a, shift, 0), 1.0)
        b_sh = jnp.where(keep, pltpu.roll(b, shift, 0), 0.0)
        b = a * b_sh + b
        a = a * a_sh
    return a, b


def _lru_kernel(xb_ref, cw_ref, cb_ref, wcat_ref, bcat_ref, lam_ref, hs_ref,
                xp_ref, af_ref, bf_ref, ab_ref, bb_ref):
    seq, c = xb_ref.shape
    pad = SUBLANES
    xp_ref[0:pad, :] = jnp.zeros((pad, c), F32)
    xp_ref[pad + seq:pad + seq + pad, :] = jnp.zeros((pad, c), F32)
    xp_ref[pad:pad + seq, :] = xb_ref[...]
    cw = cw_ref[...]
    cb = cb_ref[...]
    lam = lam_ref[...]
    neg = -lam
    softplus = jnp.maximum(neg, 0.0) + jnp.log1p(jnp.exp(-jnp.abs(neg)))
    rate = -LRU_C * softplus
    rows = min(SCAN_ROWS, seq)

    for ci in range(seq // rows):
        r0 = ci * rows
        xc = cb
        for k in range(CONV_W):
            xc = xc + xp_ref[r0 + pad - CONV_W // 2 + k:r0 + pad - CONV_W // 2 + k + rows, :] * cw[k:k + 1, :]
        z = _dot(xc.astype(BF16), wcat_ref[...]) + bcat_ref[...]
        t = r0 + lax.broadcasted_iota(jnp.int32, (rows, c), 0)
        for dirn, (a_ref, b_ref, first) in enumerate(((af_ref, bf_ref, 0), (ab_ref, bb_ref, seq - 1))):
            r = jax.nn.sigmoid(z[:, (2 * dirn) * c:(2 * dirn + 1) * c])
            ig = jax.nn.sigmoid(z[:, (2 * dirn + 1) * c:(2 * dirn + 2) * c])
            log_a = rate[dirn:dirn + 1, :] * r
            a = jnp.exp(log_a)
            mult = jnp.where(t == first, 1.0, jnp.sqrt(jnp.tanh(-log_a) * (a * a + 1.0)))
            a_ref[r0:r0 + rows, :] = a
            b_ref[r0:r0 + rows, :] = mult * ig * xc

    n_chunks = seq // SUBLANES

    def fwd(ci, h):
        r0 = pl.multiple_of(ci * SUBLANES, SUBLANES)
        a, b = _chunk_scan(af_ref[pl.ds(r0, SUBLANES), :], bf_ref[pl.ds(r0, SUBLANES), :], False)
        hv = a * h + b
        hs_ref[pl.ds(r0, SUBLANES), :] = hv
        return hv[SUBLANES - 1:SUBLANES, :]
    lax.fori_loop(0, n_chunks, fwd, jnp.zeros((1, c), F32), unroll=4)

    def bwd(ci, h):
        r0 = pl.multiple_of((n_chunks - 1 - ci) * SUBLANES, SUBLANES)
        a, b = _chunk_scan(ab_ref[pl.ds(r0, SUBLANES), :], bb_ref[pl.ds(r0, SUBLANES), :], True)
        hv = a * h + b
        hs_ref[pl.ds(r0, SUBLANES), :] += hv
        return hv[0:1, :]
    lax.fori_loop(0, n_chunks, bwd, jnp.zeros((1, c), F32), unroll=4)


def _lru(xb, conv_w, conv_b, wcat, bcat, lam):
    bsz, seq, _ = xb.shape
    c = RNN_BW
    blk = lambda b, n: (b, 0, n)
    return pl.pallas_call(
        _lru_kernel,
        grid=(bsz, RNN_BLOCKS),
        in_specs=[
            pl.BlockSpec((None, seq, c), blk),
            pl.BlockSpec((CONV_W, c), lambda b, n: (0, n)),
            pl.BlockSpec((1, c), lambda b, n: (0, n)),
            pl.BlockSpec((None, c, 4 * c), lambda b, n: (n, 0, 0)),
            pl.BlockSpec((None, 1, 4 * c), lambda b, n: (n, 0, 0)),
            pl.BlockSpec((2, c), lambda b, n: (0, n)),
        ],
        out_specs=pl.BlockSpec((None, seq, c), blk),
        out_shape=jax.ShapeDtypeStruct((bsz, seq, D_RNN), F32),
        scratch_shapes=[pltpu.VMEM((seq + 2 * SUBLANES, c), F32)] + [pltpu.VMEM((seq, c), F32)] * 4,
        compiler_params=_params("arbitrary", "arbitrary"),
        name="rglru_scan",
    )(xb, conv_w, conv_b, wcat, bcat, lam)


def _pad_cols(w, n):
    return jnp.pad(w, ((0, 0), (0, n - w.shape[1])))


def _mla_weights(w_in, w_q_b, w_kv_b, q_norm, k_norm):
    d = w_in.shape[0]
    kpe = w_in[:, Q_LORA + KV_LORA:]
    kpe_slab = jnp.concatenate(
        [jnp.zeros((d, QK_NOPE), F32), kpe, jnp.zeros((d, LANES - QK_HEAD), F32)], axis=1)
    w_in_p = jnp.concatenate([w_in[:, :Q_LORA + KV_LORA], kpe_slab], axis=1).astype(BF16)
    wq = w_q_b.reshape(Q_LORA, N_HEADS, QK_HEAD)
    wq = jnp.pad(wq, ((0, 0), (0, 0), (0, LANES - QK_HEAD))).reshape(Q_LORA, N_HEADS * LANES)
    wkv = w_kv_b.reshape(KV_LORA, N_HEADS, QK_NOPE + V_HEAD)
    wk = jnp.pad(wkv[:, :, :QK_NOPE], ((0, 0), (0, 0), (0, LANES - QK_NOPE))).reshape(KV_LORA, N_HEADS * LANES)
    wv = wkv[:, :, QK_NOPE:].reshape(KV_LORA, N_HEADS * V_HEAD)
    w_kv_p = jnp.concatenate([wk, wv], axis=1).astype(BF16)
    slab = lambda g: jnp.pad(g, (0, LANES - QK_HEAD)).reshape(1, LANES)
    return w_in_p, wq.astype(BF16), w_kv_p, slab(q_norm), slab(k_norm)


def _rope_tables(positions):
    half = QK_ROPE // 2
    inv_freq = ROPE_THETA ** (-jnp.arange(half, dtype=F32) / half)
    ang = positions.astype(F32)[..., None] * inv_freq
    cos, sin = jnp.cos(ang), jnp.sin(ang)
    lead = positions.shape + (QK_NOPE,)
    tail = positions.shape + (LANES - QK_HEAD,)
    cos_t = jnp.concatenate([jnp.ones(lead, F32), cos, cos, jnp.ones(tail, F32)], axis=-1)
    sin_t = jnp.concatenate([jnp.zeros(lead, F32), -sin, sin, jnp.zeros(tail, F32)], axis=-1)
    return cos_t, sin_t


def _router_weights(w_router, router_bias):
    perm = (jnp.arange(N_EXPERTS) % N_GROUPS) * EXPERTS_PER_GROUP + jnp.arange(N_EXPERTS) // N_GROUPS
    w = w_router[:, perm]
    hi = w.astype(BF16)
    lo = (w - hi.astype(F32)).astype(BF16)
    z = jnp.zeros_like(hi)
    wr1 = jnp.concatenate([hi, lo, z, z], axis=1)
    wr2 = jnp.concatenate([z, z, hi, z], axis=1)
    return wr1, wr2, router_bias[perm].reshape(N_EXPERTS, 1).astype(F32)


def kernel(x, c, positions, norm_mix, norm_ffn, w_ada, b_ada, mla_w_in, mla_q_a_norm, mla_kv_a_norm, mla_w_q_b, mla_w_kv_b, mla_q_norm, mla_k_norm, mla_w_o, rnn_w_in, rnn_conv_w, rnn_conv_b, rnn_lam_f, rnn_w_rf, rnn_b_rf, rnn_w_if, rnn_b_if, rnn_lam_b, rnn_w_rb, rnn_b_rb, rnn_w_ib, rnn_b_ib, rnn_w_o, w_router, router_bias, moe_w_gu, moe_w_dn):
    bsz, seq, d = x.shape
    depth = w_ada.shape[0]
    mod = _ada(c, w_ada, b_ada)
    wr1, wr2, rbias = _router_weights(w_router, router_bias)
    cos_t, sin_t = _rope_tables(positions)
    vec = lambda v: v.reshape(1, -1)
    for i in range(depth):
        sh1, sc1, g1, sh2, sc2, g2 = [mod[i, :, k * d:(k + 1) * d].reshape(bsz, 1, d) for k in range(6)]
        j = i // 2
        if i % 2 == 0:
            w_in_p, wq, wkv, qn, kn = _mla_weights(mla_w_in[j], mla_w_q_b[j], mla_w_kv_b[j],
                                                   mla_q_norm[j], mla_k_norm[j])
            q, k, v = _mla_in(x, vec(norm_mix[i]), sh1, sc1, w_in_p, vec(mla_q_a_norm[j]),
                              vec(mla_kv_a_norm[j]), wq, wkv, qn, kn, cos_t, sin_t)
            a = _attention(q, k, v)
            hs = None
            w_o = mla_w_o[j].astype(BF16)
        else:
            a, xb = _rnn_in(x, vec(norm_mix[i]), sh1, sc1, rnn_w_in[j].astype(BF16))
            wcat = jnp.concatenate([rnn_w_rf[j], rnn_w_if[j], rnn_w_rb[j], rnn_w_ib[j]], axis=-1).astype(BF16)
            bcat = jnp.stack([b.reshape(RNN_BLOCKS, RNN_BW) for b in
                              (rnn_b_rf[j], rnn_b_if[j], rnn_b_rb[j], rnn_b_ib[j])], axis=1)
            bcat = bcat.reshape(RNN_BLOCKS, 1, 4 * RNN_BW)
            lam = jnp.stack([rnn_lam_f[j], rnn_lam_b[j]], axis=0)
            hs = _lru(xb, rnn_conv_w[j], vec(rnn_conv_b[j]), wcat, bcat, lam)
            w_o = rnn_w_o[j].astype(BF16)
        x, h2, idx, wts = _mix_out(a, hs, x, w_o, g1, vec(norm_ffn[i]), sh2, sc2, wr1, wr2, rbias)
        x = _moe(h2, idx, wts, x, g2, moe_w_gu[i].astype(BF16), moe_w_dn[i].astype(BF16))
    return x
```

```python
import functools

import jax
import jax.numpy as jnp
from jax import lax
from jax.experimental import pallas as pl
from jax.experimental.pallas import tpu as pltpu

F32 = jnp.float32
BF16 = jnp.bfloat16

D_MODEL = 1024
N_HEADS = 16
Q_LORA = 384
KV_LORA = 256
QK_NOPE = 64
QK_ROPE = 32
QK_HEAD = QK_NOPE + QK_ROPE
V_HEAD = 64
ROPE_THETA = 10000.0
D_RNN = D_MODEL
RNN_BLOCKS = 4
RNN_BW = D_RNN // RNN_BLOCKS
CONV_W = 4
LRU_C = 8.0
N_EXPERTS = 32
N_GROUPS = 8
EXPERTS_PER_GROUP = N_EXPERTS // N_GROUPS
TOP_K = 2
D_EXPERT = 512
EPS = 1e-6

LANES = 128
SUBLANES = 8
VMEM_LIMIT = 52 * 1024 * 1024

ROW_TILE = 256
Q_TILE = 512
MOE_TILE = 256
SCAN_ROWS = 256


def _dot(a, b):
    return jnp.dot(a, b, preferred_element_type=F32)


def _split_bf16(a):
    hi = a.astype(BF16)
    lo = (a - hi.astype(F32)).astype(BF16)
    return hi, lo


def _dot_split(a, b):
    ah, al = _split_bf16(a)
    bh, bl = _split_bf16(b)
    return _dot(ah, bh) + (_dot(ah, bl) + _dot(al, bh))


def _rms(x, gain, n):
    ms = jnp.sum(x * x, axis=-1, keepdims=True) * (1.0 / n)
    return x * lax.rsqrt(ms + EPS) * gain


def _modulate(x, gain, shift, scale):
    return _rms(x, gain, x.shape[-1]) * (1.0 + scale) + shift


def _params(*sem):
    return pltpu.CompilerParams(dimension_semantics=sem, vmem_limit_bytes=VMEM_LIMIT)


def _ada_kernel(c_ref, w_ref, b_ref, o_ref):
    c = c_ref[...]
    o_ref[...] = _dot_split(c * jax.nn.sigmoid(c), w_ref[...]) + b_ref[...]


def _ada(c, w_ada, b_ada):
    depth, d, n = w_ada.shape
    bsz = c.shape[0]
    tn = 1536
    return pl.pallas_call(
        _ada_kernel,
        grid=(depth, n // tn),
        in_specs=[
            pl.BlockSpec((bsz, d), lambda l, j: (0, 0)),
            pl.BlockSpec((None, d, tn), lambda l, j: (l, 0, j)),
            pl.BlockSpec((None, 1, tn), lambda l, j: (l, 0, j)),
        ],
        out_specs=pl.BlockSpec((None, bsz, tn), lambda l, j: (l, 0, j)),
        out_shape=jax.ShapeDtypeStruct((depth, bsz, n), F32),
        compiler_params=_params("arbitrary", "arbitrary"),
        name="adaln_mod",
    )(c, w_ada, b_ada.reshape(depth, 1, n))


def _head_finish(s, gain, cos_t, sin_t, lane, out_scale):
    y = _rms(s, gain, QK_HEAD)
    rot = jnp.where(lane < QK_NOPE + QK_ROPE // 2,
                    pltpu.roll(y, LANES - QK_ROPE // 2, 1),
                    pltpu.roll(y, QK_ROPE // 2, 1))
    out = y * cos_t + rot * sin_t
    if out_scale != 1.0:
        out = out * out_scale
    return out.astype(BF16)


def _mla_in_kernel(x_ref, g_ref, sh_ref, sc_ref, win_ref, qan_ref, kvan_ref, wq_ref, wkv_ref,
                   qn_ref, kn_ref, cos_ref, sin_ref, q_out, k_out, v_out):
    h = _modulate(x_ref[...], g_ref[...], sh_ref[...], sc_ref[...])
    lat = _dot(h.astype(BF16), win_ref[...])
    q_lat = lat[:, :Q_LORA]
    kv_lat = lat[:, Q_LORA:Q_LORA + KV_LORA]
    kpe = lat[:, Q_LORA + KV_LORA:]
    q_all = _dot(_rms(q_lat, qan_ref[...], Q_LORA).astype(BF16), wq_ref[...])
    kv_all = _dot(_rms(kv_lat, kvan_ref[...], KV_LORA).astype(BF16), wkv_ref[...])
    cos_t = cos_ref[...]
    sin_t = sin_ref[...]
    lane = lax.broadcasted_iota(jnp.int32, cos_t.shape, 1)
    qg = qn_ref[...]
    kg = kn_ref[...]
    for hh in range(N_HEADS):
        sl = slice(hh * LANES, (hh + 1) * LANES)
        q_out[hh] = _head_finish(q_all[:, sl], qg, cos_t, sin_t, lane, QK_HEAD ** -0.5)
        k_out[hh] = _head_finish(kv_all[:, sl] + kpe, kg, cos_t, sin_t, lane, 1.0)
    v_out[...] = kv_all[:, N_HEADS * LANES:].astype(BF16)


def _mla_in(x, gain, shift, scale, w_in, q_a_norm, kv_a_norm, w_q, w_kv, q_norm, k_norm, cos_t, sin_t):
    bsz, seq, d = x.shape
    tm = min(ROW_TILE, seq)
    row = lambda b, i: (b, i, 0)
    per_b = lambda b, i: (b, 0, 0)
    const = lambda b, i: (0, 0)
    full = lambda a: pl.BlockSpec(a.shape, const)
    return pl.pallas_call(
        _mla_in_kernel,
        grid=(bsz, seq // tm),
        in_specs=[
            pl.BlockSpec((None, tm, d), row),
            full(gain),
            pl.BlockSpec((None, 1, d), per_b),
            pl.BlockSpec((None, 1, d), per_b),
            full(w_in), full(q_a_norm), full(kv_a_norm), full(w_q), full(w_kv),
            full(q_norm), full(k_norm),
            pl.BlockSpec((None, tm, LANES), row),
            pl.BlockSpec((None, tm, LANES), row),
        ],
        out_specs=[
            pl.BlockSpec((None, N_HEADS, tm, LANES), lambda b, i: (b, 0, i, 0)),
            pl.BlockSpec((None, N_HEADS, tm, LANES), lambda b, i: (b, 0, i, 0)),
            pl.BlockSpec((None, tm, N_HEADS * V_HEAD), row),
        ],
        out_shape=[
            jax.ShapeDtypeStruct((bsz, N_HEADS, seq, LANES), BF16),
            jax.ShapeDtypeStruct((bsz, N_HEADS, seq, LANES), BF16),
            jax.ShapeDtypeStruct((bsz, seq, N_HEADS * V_HEAD), BF16),
        ],
        compiler_params=_params("arbitrary", "arbitrary"),
        name="mla_in",
    )(x, gain, shift, scale, w_in, q_a_norm, kv_a_norm, w_q, w_kv, q_norm, k_norm, cos_t, sin_t)


def _attn_kernel(q_ref, k_ref, v_ref, o_ref):
    v = v_ref[...]
    outs = []
    for j in range(2):
        s = lax.dot_general(q_ref[j], k_ref[j], (((1,), (1,)), ((), ())),
                            preferred_element_type=F32)
        m = jnp.max(s, axis=-1, keepdims=True)
        p = jnp.exp(s - m)
        l = jnp.sum(p, axis=-1, keepdims=True)
        outs.append(_dot(p.astype(BF16), v) / l)
    lane = lax.broadcasted_iota(jnp.int32, outs[0].shape, 1)
    o_ref[...] = jnp.where(lane < V_HEAD, outs[0], outs[1]).astype(BF16)


def _attention(q, k, v):
    bsz, _, seq, _ = q.shape
    tq = min(Q_TILE, seq)
    return pl.pallas_call(
        _attn_kernel,
        grid=(bsz, N_HEADS // 2, seq // tq),
        in_specs=[
            pl.BlockSpec((None, 2, tq, LANES), lambda b, h, i: (b, h, i, 0)),
            pl.BlockSpec((None, 2, seq, LANES), lambda b, h, i: (b, h, 0, 0)),
            pl.BlockSpec((None, seq, LANES), lambda b, h, i: (b, 0, h)),
        ],
        out_specs=pl.BlockSpec((None, tq, LANES), lambda b, h, i: (b, i, h)),
        out_shape=jax.ShapeDtypeStruct((bsz, seq, N_HEADS * V_HEAD), BF16),
        compiler_params=_params("arbitrary", "arbitrary", "arbitrary"),
        name="mla_attention",
    )(q, k, v)


def _first_index_of_max(vals):
    m = vals[0]
    for v in vals[1:]:
        m = jnp.maximum(m, v)
    idx = jnp.full(m.shape, float(len(vals) - 1), F32)
    for j in range(len(vals) - 2, -1, -1):
        idx = jnp.where(vals[j] == m, float(j), idx)
    return m, idx


def _route(h2, wr1_ref, wr2_ref, rb_ref):
    hh, hl = _split_bf16(h2)
    logits = (_dot(hh, wr1_ref[...]) + _dot(hl, wr2_ref[...])).T
    logit = logits[0:N_EXPERTS] + logits[N_EXPERTS:2 * N_EXPERTS] + logits[2 * N_EXPERTS:3 * N_EXPERTS]
    score = jax.nn.sigmoid(logit)
    biased = score + rb_ref[...]
    a = [biased[j * N_GROUPS:(j + 1) * N_GROUPS] for j in range(EXPERTS_PER_GROUP)]
    sc = [score[j * N_GROUPS:(j + 1) * N_GROUPS] for j in range(EXPERTS_PER_GROUP)]
    hi1, lo1 = jnp.maximum(a[0], a[1]), jnp.minimum(a[0], a[1])
    hi2, lo2 = jnp.maximum(a[2], a[3]), jnp.minimum(a[2], a[3])
    gscore = jnp.maximum(hi1, hi2) + jnp.maximum(jnp.minimum(hi1, hi2), jnp.maximum(lo1, lo2))
    gmax = jnp.max(gscore, axis=0, keepdims=True)
    giota = lax.broadcasted_iota(jnp.int32, gscore.shape, 0).astype(F32)
    gsel = jnp.min(jnp.where(gscore == gmax, giota, float(N_GROUPS)), axis=0, keepdims=True)
    onehot = giota == gsel
    pick = lambda t: jnp.sum(jnp.where(onehot, t, 0.0), axis=0, keepdims=True)
    bj = [pick(t) for t in a]
    sj = [pick(t) for t in sc]
    _, i1 = _first_index_of_max(bj)
    bj2 = [jnp.where(i1 == float(j), -jnp.inf, bj[j]) for j in range(EXPERTS_PER_GROUP)]
    _, i2 = _first_index_of_max(bj2)
    sel = lambda i: jnp.where(i == 0.0, sj[0], jnp.where(i == 1.0, sj[1], jnp.where(i == 2.0, sj[2], sj[3])))
    w1, w2 = sel(i1), sel(i2)
    den = w1 + w2
    base = gsel * float(EXPERTS_PER_GROUP)
    return ((base + i1).astype(jnp.int32), (base + i2).astype(jnp.int32)), (w1 / den, w2 / den)


def _mix_out_kernel(has_gate, *refs):
    if has_gate:
        a_ref, hs_ref, x_ref, wo_ref, g1_ref, g_ref, sh_ref, sc_ref, wr1_ref, wr2_ref, rb_ref, \
            x_out, h_out, idx_out, wts_out = refs
        a = (a_ref[...].astype(F32) * hs_ref[...]).astype(BF16)
    else:
        a_ref, x_ref, wo_ref, g1_ref, g_ref, sh_ref, sc_ref, wr1_ref, wr2_ref, rb_ref, \
            x_out, h_out, idx_out, wts_out = refs
        a = a_ref[...]
    x1 = x_ref[...] + g1_ref[...] * _dot(a, wo_ref[...])
    x_out[...] = x1
    h2 = _modulate(x1, g_ref[...], sh_ref[...], sc_ref[...])
    _to_tiles(h_out, h2)
    idx, wts = _route(h2, wr1_ref, wr2_ref, rb_ref)
    for k in range(TOP_K):
        idx_out[k:k + 1, :] = idx[k]
        wts_out[k:k + 1, :] = wts[k]


def _mix_out(a, hs, x, w_o, gate1, gain, shift, scale, wr1, wr2, rbias):
    bsz, seq, d = x.shape
    tm = min(ROW_TILE, seq)
    row = lambda b, i: (b, i, 0)
    per_b = lambda b, i: (b, 0, 0)
    const = lambda b, i: (0, 0)
    full = lambda t: pl.BlockSpec(t.shape, const)
    vec = pl.BlockSpec((None, 1, d), per_b)
    acts = [a] if hs is None else [a, hs]
    return pl.pallas_call(
        functools.partial(_mix_out_kernel, hs is not None),
        grid=(bsz, seq // tm),
        in_specs=[pl.BlockSpec((None, tm, t.shape[-1]), row) for t in acts] + [
            pl.BlockSpec((None, tm, d), row), full(w_o), vec, full(gain), vec, vec,
            full(wr1), full(wr2), full(rbias),
        ],
        out_specs=[
            pl.BlockSpec((None, tm, d), row),
            pl.BlockSpec((None, tm, d // LANES, LANES), lambda b, i: (b, i, 0, 0)),
            pl.BlockSpec((None, TOP_K, tm), lambda b, i: (b, 0, i)),
            pl.BlockSpec((None, TOP_K, tm), lambda b, i: (b, 0, i)),
        ],
        out_shape=[
            jax.ShapeDtypeStruct((bsz, seq, d), F32),
            jax.ShapeDtypeStruct((bsz, seq, d // LANES, LANES), F32),
            jax.ShapeDtypeStruct((bsz, TOP_K, seq), jnp.int32),
            jax.ShapeDtypeStruct((bsz, TOP_K, seq), F32),
        ],
        compiler_params=_params("arbitrary", "arbitrary"),
        name="mix_out_route",
    )(*acts, x, w_o, gate1, gain, shift, scale, wr1, wr2, rbias)


N_SUB = D_MODEL // LANES
TABLE_CHUNK = 512


def _to_tiles(ref, val):
    for s in range(N_SUB):
        ref[:, s, :] = val[:, s * LANES:(s + 1) * LANES]


def _from_tiles(ref, lo, n):
    return jnp.concatenate([ref[lo:lo + n, s, :] for s in range(N_SUB)], axis=1)


def _tables_kernel(idx_ref, rank_ref, cnt_ref, carry):
    @pl.when(pl.program_id(0) == 0)
    def _():
        carry[...] = jnp.zeros_like(carry)

    seq = idx_ref.shape[-1]
    ch = min(TABLE_CHUNK, seq)
    tri = jnp.where(lax.broadcasted_iota(jnp.int32, (ch, ch), 0) <= lax.broadcasted_iota(jnp.int32, (ch, ch), 1),
                    1.0, 0.0).astype(BF16)
    eiota = lax.broadcasted_iota(jnp.int32, (N_EXPERTS, ch), 0)
    cnt = carry[...]
    for k in range(TOP_K):
        for c in range(seq // ch):
            sel = eiota == idx_ref[k:k + 1, c * ch:(c + 1) * ch]
            pref = _dot(jnp.where(sel, 1.0, 0.0).astype(BF16), tri) + cnt
            rank = jnp.sum(jnp.where(sel, pref, 0.0), axis=0, keepdims=True) - 1.0
            rank_ref[k:k + 1, c * ch:(c + 1) * ch] = rank.astype(jnp.int32)
            cnt = pref[:, ch - 1:ch]
    carry[...] = cnt
    cnt_ref[...] = jnp.broadcast_to(cnt, cnt_ref.shape)


def _tables(idx):
    bsz, _, seq = idx.shape
    return pl.pallas_call(
        _tables_kernel,
        grid=(bsz,),
        in_specs=[pl.BlockSpec((None, TOP_K, seq), lambda b: (b, 0, 0))],
        out_specs=[pl.BlockSpec((None, TOP_K, seq), lambda b: (b, 0, 0)),
                   pl.BlockSpec((N_EXPERTS, LANES), lambda b: (0, 0))],
        out_shape=[jax.ShapeDtypeStruct((bsz, TOP_K, seq), jnp.int32),
                   jax.ShapeDtypeStruct((N_EXPERTS, LANES), F32)],
        scratch_shapes=[pltpu.VMEM((N_EXPERTS, 1), F32)],
        compiler_params=_params("arbitrary"),
        name="moe_tables",
    )(idx)


def _scatter_kernel(dest_ref, pad_ref, src_hbm, zero_hbm, dst_hbm, sem):
    nt = pl.num_programs(1)
    n = pl.program_id(0) * nt + pl.program_id(1)
    total = pl.num_programs(0) * nt
    tm = dest_ref.shape[-1]
    n_pad = pad_ref.shape[-1]
    slot = lax.rem(n, 2)
    tok0 = n * tm
    for k in range(TOP_K):
        def start(r, carry, k=k):
            pltpu.make_async_copy(src_hbm.at[tok0 + r], dst_hbm.at[dest_ref[k, r]], sem.at[slot]).start()
            return carry
        lax.fori_loop(0, tm, start, 0, unroll=8)

    def fill(r, carry):
        pltpu.make_async_copy(zero_hbm.at[0], dst_hbm.at[pad_ref[0, r]], sem.at[slot]).start()
        return carry
    lax.fori_loop(0, n_pad, fill, 0, unroll=8)

    def drain(sl):
        cnt = TOP_K * tm + n_pad
        pltpu.make_async_copy(src_hbm.at[pl.ds(0, cnt)], dst_hbm.at[pl.ds(0, cnt)], sem.at[sl]).wait()

    @pl.when(n > 0)
    def _():
        drain(1 - slot)

    @pl.when(n == total - 1)
    def _():
        drain(slot)


def _scatter(dest, pad_rows, h2t, n_rows):
    bsz, _, seq = dest.shape
    tm = min(ROW_TILE, seq)
    nt = seq // tm
    n_pad = pad_rows.shape[0] // (bsz * nt)
    assert n_pad * bsz * nt == pad_rows.shape[0]
    return pl.pallas_call(
        _scatter_kernel,
        grid=(bsz, nt),
        in_specs=[
            pl.BlockSpec((None, TOP_K, tm), lambda b, i: (b, 0, i), memory_space=pltpu.SMEM),
            pl.BlockSpec((None, 1, n_pad), lambda b, i: (b * nt + i, 0, 0), memory_space=pltpu.SMEM),
            pl.BlockSpec(memory_space=pl.ANY),
            pl.BlockSpec(memory_space=pl.ANY),
        ],
        out_specs=pl.BlockSpec(memory_space=pl.ANY),
        out_shape=jax.ShapeDtypeStruct((n_rows, N_SUB, LANES), F32),
        scratch_shapes=[pltpu.SemaphoreType.DMA((2,))],
        compiler_params=_params("arbitrary", "arbitrary"),
        name="moe_scatter",
    )(dest, pad_rows.reshape(bsz * nt, 1, n_pad), h2t.reshape(bsz * seq, N_SUB, LANES),
      jnp.zeros((1, N_SUB, LANES), F32))


def _expert_kernel(blk_exp_ref, blk_first_ref, n_used_ref,
                   xs_ref, wgu_ref, wdn_ref, ys_ref, wgu_bf, wdn_bf):
    i = pl.program_id(0)

    @pl.when(i < n_used_ref[0])
    def _():
        @pl.when(blk_first_ref[i] == 1)
        def _():
            wgu_bf[...] = wgu_ref[...].astype(BF16)
            wdn_bf[...] = wdn_ref[...].astype(BF16)

        x = _from_tiles(xs_ref, 0, MOE_TILE).astype(BF16)
        gu = _dot(x, wgu_bf[...])
        g = gu[:, :D_EXPERT]
        u = gu[:, D_EXPERT:]
        mid = (g * jax.nn.sigmoid(g) * u).astype(BF16)
        _to_tiles(ys_ref, _dot(mid, wdn_bf[...]))

    @pl.when(i >= n_used_ref[0])
    def _():
        ys_ref[...] = jnp.zeros_like(ys_ref)


def _experts(blk_exp, blk_first, n_used, xs, w_gu, w_dn):
    n_rows = xs.shape[0]
    d = D_MODEL
    nb = n_rows // MOE_TILE
    tile = lambda i, *_: (i, 0, 0)
    grid_spec = pltpu.PrefetchScalarGridSpec(
        num_scalar_prefetch=3,
        grid=(nb,),
        in_specs=[
            pl.BlockSpec((MOE_TILE, N_SUB, LANES), tile),
            pl.BlockSpec((None, d, 2 * D_EXPERT), lambda i, be, *_: (be[i], 0, 0)),
            pl.BlockSpec((None, D_EXPERT, d), lambda i, be, *_: (be[i], 0, 0)),
        ],
        out_specs=pl.BlockSpec((MOE_TILE, N_SUB, LANES), tile),
        scratch_shapes=[pltpu.VMEM((d, 2 * D_EXPERT), BF16), pltpu.VMEM((D_EXPERT, d), BF16)],
    )
    return pl.pallas_call(
        _expert_kernel,
        grid_spec=grid_spec,
        out_shape=jax.ShapeDtypeStruct((n_rows, N_SUB, LANES), F32),
        compiler_params=_params("arbitrary"),
        name="moe_experts",
    )(blk_exp, blk_first, n_used, xs, w_gu, w_dn)


def _combine_kernel(dcur_ref, dnxt_ref, ys_hbm, x_ref, wts_ref, g2_ref, x_out, buf, sem):
    nt = pl.num_programs(1)
    n = pl.program_id(0) * nt + pl.program_id(1)
    total = pl.num_programs(0) * nt
    tm = x_ref.shape[0]
    slot = lax.rem(n, 2)

    def issue(d_ref, sl):
        for k in range(TOP_K):
            def start(r, carry, k=k):
                pltpu.make_async_copy(ys_hbm.at[d_ref[k, r]], buf.at[sl, k * tm + r], sem.at[sl]).start(priority=k)
                return carry
            lax.fori_loop(0, tm, start, 0, unroll=8)

    @pl.when(n == 0)
    def _():
        issue(dcur_ref, 0)

    @pl.when(n + 1 < total)
    def _():
        issue(dnxt_ref, 1 - slot)

    pltpu.make_async_copy(ys_hbm.at[pl.ds(0, TOP_K * tm)], buf.at[slot], sem.at[slot]).wait()
    cur = buf.at[slot]
    w = wts_ref[...]
    y = w[:, 0:1] * _from_tiles(cur, 0, tm) + w[:, 1:2] * _from_tiles(cur, tm, tm)
    x_out[...] = x_ref[...] + g2_ref[...] * y


def _combine(dest, ys, x, wts_col, gate2):
    bsz, seq, d = x.shape
    tm = min(ROW_TILE, seq)
    nt = seq // tm

    def nxt(b, i):
        n = jnp.minimum(b * nt + i + 1, bsz * nt - 1)
        return (n // nt, 0, n % nt)

    return pl.pallas_call(
        _combine_kernel,
        grid=(bsz, nt),
        in_specs=[
            pl.BlockSpec((None, TOP_K, tm), lambda b, i: (b, 0, i), memory_space=pltpu.SMEM),
            pl.BlockSpec((None, TOP_K, tm), nxt, memory_space=pltpu.SMEM),
            pl.BlockSpec(memory_space=pl.ANY),
            pl.BlockSpec((None, tm, d), lambda b, i: (b, i, 0)),
            pl.BlockSpec((None, tm, TOP_K), lambda b, i: (b, i, 0)),
            pl.BlockSpec((None, 1, d), lambda b, i: (b, 0, 0)),
        ],
        out_specs=pl.BlockSpec((None, tm, d), lambda b, i: (b, i, 0)),
        out_shape=jax.ShapeDtypeStruct((bsz, seq, d), F32),
        scratch_shapes=[pltpu.VMEM((2, TOP_K * tm, N_SUB, LANES), F32), pltpu.SemaphoreType.DMA((2,))],
        compiler_params=_params("arbitrary", "arbitrary"),
        name="moe_combine",
    )(dest, dest, ys, x, wts_col, gate2)


def _moe(h2t, idx, wts, x, gate2, w_gu, w_dn):
    bsz, seq, _ = x.shape
    n_rows = bsz * seq * TOP_K + N_EXPERTS * MOE_TILE
    nb = n_rows // MOE_TILE
    rank, cnt = _tables(idx)
    counts = cnt[:, 0].astype(jnp.int32)
    padded = ((counts + MOE_TILE - 1) // MOE_TILE) * MOE_TILE
    pend = jnp.cumsum(padded)
    pstart = pend - padded
    dest = pstart[idx] + rank
    blk_row = jnp.arange(nb, dtype=jnp.int32) * MOE_TILE
    blk_exp = jnp.minimum(jnp.searchsorted(pend, blk_row, side="right"), N_EXPERTS - 1).astype(jnp.int32)
    blk_first = (blk_row == pstart[blk_exp]).astype(jnp.int32)
    n_used = (pend[-1:] // MOE_TILE).astype(jnp.int32)
    slack_len = jnp.concatenate([padded - counts, n_rows - pend[-1:]])
    slack_row = jnp.concatenate([pstart + counts, pend[-1:]])
    slack_end = jnp.cumsum(slack_len)
    q = jnp.arange(N_EXPERTS * MOE_TILE, dtype=jnp.int32)
    seg = jnp.searchsorted(slack_end, q, side="right")
    pad_rows = (slack_row[seg] + q - (slack_end[seg] - slack_len[seg])).astype(jnp.int32)
    xs = _scatter(dest, pad_rows, h2t, n_rows)
    ys = _experts(blk_exp, blk_first, n_used, xs, w_gu, w_dn)
    return _combine(dest, ys, x, wts.transpose(0, 2, 1), gate2)


def _rnn_in_kernel(x_ref, g_ref, sh_ref, sc_ref, w_ref, gate_out, xb_out):
    h = _modulate(x_ref[...], g_ref[...], sh_ref[...], sc_ref[...])
    u = _dot(h.astype(BF16), w_ref[...])
    gate_out[...] = jax.nn.gelu(u[:, :D_RNN]).astype(BF16)
    xb_out[...] = u[:, D_RNN:]


def _rnn_in(x, gain, shift, scale, w_in):
    bsz, seq, d = x.shape
    tm = min(ROW_TILE, seq)
    row = lambda b, i: (b, i, 0)
    per_b = lambda b, i: (b, 0, 0)
    const = lambda b, i: (0, 0)
    return pl.pallas_call(
        _rnn_in_kernel,
        grid=(bsz, seq // tm),
        in_specs=[
            pl.BlockSpec((None, tm, d), row),
            pl.BlockSpec(gain.shape, const),
            pl.BlockSpec((None, 1, d), per_b),
            pl.BlockSpec((None, 1, d), per_b),
            pl.BlockSpec(w_in.shape, const),
        ],
        out_specs=[pl.BlockSpec((None, tm, D_RNN), row), pl.BlockSpec((None, tm, D_RNN), row)],
        out_shape=[jax.ShapeDtypeStruct((bsz, seq, D_RNN), BF16),
                   jax.ShapeDtypeStruct((bsz, seq, D_RNN), F32)],
        compiler_params=_params("arbitrary", "arbitrary"),
        name="rnn_in",
    )(x, gain, shift, scale, w_in)


def _chunk_scan(a, b, reverse):
    row = lax.broadcasted_iota(jnp.int32, a.shape, 0)
    for dd in (1, 2, 4):
        if reverse:
            keep = row < SUBLANES - dd
            shift = SUBLANES - dd
        else:
            keep = row >= dd
            shift = dd
        a_sh = jnp.where(keep, pltpu.roll(a, shift, 0), 1.0)
        b_sh = jnp.where(keep, pltpu.roll(b, shift, 0), 0.0)
        b = a * b_sh + b
        a = a * a_sh
    return a, b


def _lru_kernel(xb_ref, cw_ref, cb_ref, wcat_ref, bcat_ref, lam_ref, hs_ref,
                xp_ref, af_ref, bf_ref, ab_ref, bb_ref):
    seq, c = xb_ref.shape
    pad = SUBLANES
    xp_ref[0:pad, :] = jnp.zeros((pad, c), F32)
    xp_ref[pad + seq:pad + seq + pad, :] = jnp.zeros((pad, c), F32)
    xp_ref[pad:pad + seq, :] = xb_ref[...]
    cw = cw_ref[...]
    cb = cb_ref[...]
    lam = lam_ref[...]
    neg = -lam
    softplus = jnp.maximum(neg, 0.0) + jnp.log1p(jnp.exp(-jnp.abs(neg)))
    rate = -LRU_C * softplus
    rows = min(SCAN_ROWS, seq)

    for ci in range(seq // rows):
        r0 = ci * rows
        xc = cb
        for k in range(CONV_W):
            xc = xc + xp_ref[r0 + pad - CONV_W // 2 + k:r0 + pad - CONV_W // 2 + k + rows, :] * cw[k:k + 1, :]
        z = _dot(xc.astype(BF16), wcat_ref[...]) + bcat_ref[...]
        t = r0 + lax.broadcasted_iota(jnp.int32, (rows, c), 0)
        for dirn, (a_ref, b_ref, first) in enumerate(((af_ref, bf_ref, 0), (ab_ref, bb_ref, seq - 1))):
            r = jax.nn.sigmoid(z[:, (2 * dirn) * c:(2 * dirn + 1) * c])
            ig = jax.nn.sigmoid(z[:, (2 * dirn + 1) * c:(2 * dirn + 2) * c])
            log_a = rate[dirn:dirn + 1, :] * r
            a = jnp.exp(log_a)
            mult = jnp.where(t == first, 1.0, jnp.sqrt(jnp.tanh(-log_a) * (a * a + 1.0)))
            a_ref[r0:r0 + rows, :] = a
            b_ref[r0:r0 + rows, :] = mult * ig * xc

    n_chunks = seq // SUBLANES

    def fwd(ci, h):
        r0 = pl.multiple_of(ci * SUBLANES, SUBLANES)
        a, b = _chunk_scan(af_ref[pl.ds(r0, SUBLANES), :], bf_ref[pl.ds(r0, SUBLANES), :], False)
        hv = a * h + b
        hs_ref[pl.ds(r0, SUBLANES), :] = hv
        return hv[SUBLANES - 1:SUBLANES, :]
    lax.fori_loop(0, n_chunks, fwd, jnp.zeros((1, c), F32), unroll=4)

    def bwd(ci, h):
        r0 = pl.multiple_of((n_chunks - 1 - ci) * SUBLANES, SUBLANES)
        a, b = _chunk_scan(ab_ref[pl.ds(r0, SUBLANES), :], bb_ref[pl.ds(r0, SUBLANES), :], True)
        hv = a * h + b
        hs_ref[pl.ds(r0, SUBLANES), :] += hv
        return hv[0:1, :]
    lax.fori_loop(0, n_chunks, bwd, jnp.zeros((1, c), F32), unroll=4)


def _lru(xb, conv_w, conv_b, wcat, bcat, lam):
    bsz, seq, _ = xb.shape
    c = RNN_BW
    blk = lambda b, n: (b, 0, n)
    return pl.pallas_call(
        _lru_kernel,
        grid=(bsz, RNN_BLOCKS),
        in_specs=[
            pl.BlockSpec((None, seq, c), blk),
            pl.BlockSpec((CONV_W, c), lambda b, n: (0, n)),
            pl.BlockSpec((1, c), lambda b, n: (0, n)),
            pl.BlockSpec((None, c, 4 * c), lambda b, n: (n, 0, 0)),
            pl.BlockSpec((None, 1, 4 * c), lambda b, n: (n, 0, 0)),
            pl.BlockSpec((2, c), lambda b, n: (0, n)),
        ],
        out_specs=pl.BlockSpec((None, seq, c), blk),
        out_shape=jax.ShapeDtypeStruct((bsz, seq, D_RNN), F32),
        scratch_shapes=[pltpu.VMEM((seq + 2 * SUBLANES, c), F32)] + [pltpu.VMEM((seq, c), F32)] * 4,
        compiler_params=_params("arbitrary", "arbitrary"),
        name="rglru_scan",
    )(xb, conv_w, conv_b, wcat, bcat, lam)


def _pad_cols(w, n):
    return jnp.pad(w, ((0, 0), (0, n - w.shape[1])))


def _mla_weights(w_in, w_q_b, w_kv_b, q_norm, k_norm):
    d = w_in.shape[0]
    kpe = w_in[:, Q_LORA + KV_LORA:]
    kpe_slab = jnp.concatenate(
        [jnp.zeros((d, QK_NOPE), F32), kpe, jnp.zeros((d, LANES - QK_HEAD), F32)], axis=1)
    w_in_p = jnp.concatenate([w_in[:, :Q_LORA + KV_LORA], kpe_slab], axis=1).astype(BF16)
    wq = w_q_b.reshape(Q_LORA, N_HEADS, QK_HEAD)
    wq = jnp.pad(wq, ((0, 0), (0, 0), (0, LANES - QK_HEAD))).reshape(Q_LORA, N_HEADS * LANES)
    wkv = w_kv_b.reshape(KV_LORA, N_HEADS, QK_NOPE + V_HEAD)
    wk = jnp.pad(wkv[:, :, :QK_NOPE], ((0, 0), (0, 0), (0, LANES - QK_NOPE))).reshape(KV_LORA, N_HEADS * LANES)
    wv = wkv[:, :, QK_NOPE:].reshape(KV_LORA, N_HEADS * V_HEAD)
    w_kv_p = jnp.concatenate([wk, wv], axis=1).astype(BF16)
    slab = lambda g: jnp.pad(g, (0, LANES - QK_HEAD)).reshape(1, LANES)
    return w_in_p, wq.astype(BF16), w_kv_p, slab(q_norm), slab(k_norm)


def _rope_tables(positions):
    half = QK_ROPE // 2
    inv_freq = ROPE_THETA ** (-jnp.arange(half, dtype=F32) / half)
    ang = positions.astype(F32)[..., None] * inv_freq
    cos, sin = jnp.cos(ang), jnp.sin(ang)
    lead = positions.shape + (QK_NOPE,)
    tail = positions.shape + (LANES - QK_HEAD,)
    cos_t = jnp.concatenate([jnp.ones(lead, F32), cos, cos, jnp.ones(tail, F32)], axis=-1)
    sin_t = jnp.concatenate([jnp.zeros(lead, F32), -sin, sin, jnp.zeros(tail, F32)], axis=-1)
    return cos_t, sin_t


def _router_weights(w_router, router_bias):
    perm = (jnp.arange(N_EXPERTS) % N_GROUPS) * EXPERTS_PER_GROUP + jnp.arange(N_EXPERTS) // N_GROUPS
    w = w_router[:, perm]
    hi = w.astype(BF16)
    lo = (w - hi.astype(F32)).astype(BF16)
    z = jnp.zeros_like(hi)
    wr1 = jnp.concatenate([hi, lo, z, z], axis=1)
    wr2 = jnp.concatenate([z, z, hi, z], axis=1)
    return wr1, wr2, router_bias[perm].reshape(N_EXPERTS, 1).astype(F32)


def kernel(x, c, positions, norm_mix, norm_ffn, w_ada, b_ada, mla_w_in, mla_q_a_norm, mla_kv_a_norm, mla_w_q_b, mla_w_kv_b, mla_q_norm, mla_k_norm, mla_w_o, rnn_w_in, rnn_conv_w, rnn_conv_b, rnn_lam_f, rnn_w_rf, rnn_b_rf, rnn_w_if, rnn_b_if, rnn_lam_b, rnn_w_rb, rnn_b_rb, rnn_w_ib, rnn_b_ib, rnn_w_o, w_router, router_bias, moe_w_gu, moe_w_dn):
    bsz, seq, d = x.shape
    depth = w_ada.shape[0]
    mod = _ada(c, w_ada, b_ada)
    wr1, wr2, rbias = _router_weights(w_router, router_bias)
    cos_t, sin_t = _rope_tables(positions)
    vec = lambda v: v.reshape(1, -1)
    for i in range(depth):
        sh1, sc1, g1, sh2, sc2, g2 = [mod[i, :, k * d:(k + 1) * d].reshape(bsz, 1, d) for k in range(6)]
        j = i // 2
        if i % 2 == 0:
            w_in_p, wq, wkv, qn, kn = _mla_weights(mla_w_in[j], mla_w_q_b[j], mla_w_kv_b[j],
                                                   mla_q_norm[j], mla_k_norm[j])
            q, k, v = _mla_in(x, vec(norm_mix[i]), sh1, sc1, w_in_p, vec(mla_q_a_norm[j]),
                              vec(mla_kv_a_norm[j]), wq, wkv, qn, kn, cos_t, sin_t)
            a = _attention(q, k, v)
            hs = None
            w_o = mla_w_o[j].astype(BF16)
        else:
            a, xb = _rnn_in(x, vec(norm_mix[i]), sh1, sc1, rnn_w_in[j].astype(BF16))
            wcat = jnp.concatenate([rnn_w_rf[j], rnn_w_if[j], rnn_w_rb[j], rnn_w_ib[j]], axis=-1).astype(BF16)
            bcat = jnp.stack([b.reshape(RNN_BLOCKS, RNN_BW) for b in
                              (rnn_b_rf[j], rnn_b_if[j], rnn_b_rb[j], rnn_b_ib[j])], axis=1)
            bcat = bcat.reshape(RNN_BLOCKS, 1, 4 * RNN_BW)
            lam = jnp.stack([rnn_lam_f[j], rnn_lam_b[j]], axis=0)
            hs = _lru(xb, rnn_conv_w[j], vec(rnn_conv_b[j]), wcat, bcat, lam)
            w_o = rnn_w_o[j].astype(BF16)
        x, h2, idx, wts = _mix_out(a, hs, x, w_o, g1, vec(norm_ffn[i]), sh2, sc2, wr1, wr2, rbias)
        x = _moe(h2, idx, wts, x, g2, moe_w_gu[i], moe_w_dn[i])
    return x
```

```python
import functools

import jax
import jax.numpy as jnp
from jax import lax
from jax.experimental import pallas as pl
from jax.experimental.pallas import tpu as pltpu

F32 = jnp.float32
BF16 = jnp.bfloat16

D_MODEL = 1024
N_HEADS = 16
Q_LORA = 384
KV_LORA = 256
QK_NOPE = 64
QK_ROPE = 32
QK_HEAD = QK_NOPE + QK_ROPE
V_HEAD = 64
ROPE_THETA = 10000.0
D_RNN = D_MODEL
RNN_BLOCKS = 4
RNN_BW = D_RNN // RNN_BLOCKS
CONV_W = 4
LRU_C = 8.0
N_EXPERTS = 32
N_GROUPS = 8
EXPERTS_PER_GROUP = N_EXPERTS // N_GROUPS
TOP_K = 2
D_EXPERT = 512
EPS = 1e-6

LANES = 128
SUBLANES = 8
VMEM_LIMIT = 52 * 1024 * 1024

ROW_TILE = 256
Q_TILE = 512
MOE_TILE = 256
SCAN_ROWS = 256


def _dot(a, b):
    return jnp.dot(a, b, preferred_element_type=F32)


def _split_bf16(a):
    hi = a.astype(BF16)
    lo = (a - hi.astype(F32)).astype(BF16)
    return hi, lo


def _dot_split(a, b):
    ah, al = _split_bf16(a)
    bh, bl = _split_bf16(b)
    return _dot(ah, bh) + (_dot(ah, bl) + _dot(al, bh))


def _rms(x, gain, n):
    ms = jnp.sum(x * x, axis=-1, keepdims=True) * (1.0 / n)
    return x * lax.rsqrt(ms + EPS) * gain


def _modulate(x, gain, shift, scale):
    return _rms(x, gain, x.shape[-1]) * (1.0 + scale) + shift


def _params(*sem):
    return pltpu.CompilerParams(dimension_semantics=sem, vmem_limit_bytes=VMEM_LIMIT)


def _ada_kernel(c_ref, w_ref, b_ref, o_ref):
    c = c_ref[...]
    o_ref[...] = _dot_split(c * jax.nn.sigmoid(c), w_ref[...]) + b_ref[...]


def _ada(c, w_ada, b_ada):
    depth, d, n = w_ada.shape
    bsz = c.shape[0]
    tn = 1536
    return pl.pallas_call(
        _ada_kernel,
        grid=(depth, n // tn),
        in_specs=[
            pl.BlockSpec((bsz, d), lambda l, j: (0, 0)),
            pl.BlockSpec((None, d, tn), lambda l, j: (l, 0, j)),
            pl.BlockSpec((None, 1, tn), lambda l, j: (l, 0, j)),
        ],
        out_specs=pl.BlockSpec((None, bsz, tn), lambda l, j: (l, 0, j)),
        out_shape=jax.ShapeDtypeStruct((depth, bsz, n), F32),
        compiler_params=_params("arbitrary", "arbitrary"),
        name="adaln_mod",
    )(c, w_ada, b_ada.reshape(depth, 1, n))


def _head_scale(s):
    return lax.rsqrt(jnp.sum(s * s, axis=-1, keepdims=True) * (1.0 / QK_HEAD) + EPS)


def _mla_in_kernel(x_ref, g_ref, sh_ref, sc_ref, win_ref, qan_ref, kvan_ref, wq_ref, wqr_ref, wkv_ref,
                   qn_ref, kn_ref, cos_ref, sin_ref, q_out, k_out, v_out):
    h = _modulate(x_ref[...], g_ref[...], sh_ref[...], sc_ref[...])
    lat = _dot(h.astype(BF16), win_ref[...])
    q_lat = lat[:, :Q_LORA]
    kv_lat = lat[:, Q_LORA:Q_LORA + KV_LORA]
    kpe = lat[:, Q_LORA + KV_LORA:Q_LORA + KV_LORA + LANES]
    kpe_rot = lat[:, Q_LORA + KV_LORA + LANES:]
    qn = _rms(q_lat, qan_ref[...], Q_LORA).astype(BF16)
    q_all = _dot(qn, wq_ref[...])
    q_rot = _dot(qn, wqr_ref[...])
    kv_all = _dot(_rms(kv_lat, kvan_ref[...], KV_LORA).astype(BF16), wkv_ref[...])
    cos_t = cos_ref[...]
    sin_t = sin_ref[...]
    q_scale = QK_HEAD ** -0.5
    cq = cos_t * (qn_ref[0:1, :] * q_scale)
    sq = sin_t * (qn_ref[1:2, :] * q_scale)
    ck = cos_t * kn_ref[0:1, :]
    k_rot_term = kpe_rot * (sin_t * kn_ref[1:2, :])
    for hh in range(N_HEADS):
        sl = slice(hh * LANES, (hh + 1) * LANES)
        s = q_all[:, sl]
        q_out[hh] = ((s * cq + q_rot[:, sl] * sq) * _head_scale(s)).astype(BF16)
        s = kv_all[:, sl] + kpe
        k_out[hh] = ((s * ck + k_rot_term) * _head_scale(s)).astype(BF16)
    v_out[...] = kv_all[:, N_HEADS * LANES:].astype(BF16)


def _mla_in(x, gain, shift, scale, w_in, q_a_norm, kv_a_norm, w_q, w_q_rot, w_kv, q_norm, k_norm, cos_t, sin_t):
    bsz, seq, d = x.shape
    tm = min(ROW_TILE, seq)
    row = lambda b, i: (b, i, 0)
    per_b = lambda b, i: (b, 0, 0)
    const = lambda b, i: (0, 0)
    full = lambda a: pl.BlockSpec(a.shape, const)
    return pl.pallas_call(
        _mla_in_kernel,
        grid=(bsz, seq // tm),
        in_specs=[
            pl.BlockSpec((None, tm, d), row),
            full(gain),
            pl.BlockSpec((None, 1, d), per_b),
            pl.BlockSpec((None, 1, d), per_b),
            full(w_in), full(q_a_norm), full(kv_a_norm), full(w_q), full(w_q_rot), full(w_kv),
            full(q_norm), full(k_norm),
            pl.BlockSpec((None, tm, LANES), row),
            pl.BlockSpec((None, tm, LANES), row),
        ],
        out_specs=[
            pl.BlockSpec((None, N_HEADS, tm, LANES), lambda b, i: (b, 0, i, 0)),
            pl.BlockSpec((None, N_HEADS, tm, LANES), lambda b, i: (b, 0, i, 0)),
            pl.BlockSpec((None, tm, N_HEADS * V_HEAD), row),
        ],
        out_shape=[
            jax.ShapeDtypeStruct((bsz, N_HEADS, seq, LANES), BF16),
            jax.ShapeDtypeStruct((bsz, N_HEADS, seq, LANES), BF16),
            jax.ShapeDtypeStruct((bsz, seq, N_HEADS * V_HEAD), BF16),
        ],
        compiler_params=_params("arbitrary", "arbitrary"),
        name="mla_in",
    )(x, gain, shift, scale, w_in, q_a_norm, kv_a_norm, w_q, w_q_rot, w_kv, q_norm, k_norm, cos_t, sin_t)


def _attn_kernel(q_ref, k_ref, v_ref, o_ref):
    v = v_ref[...]
    outs = []
    for j in range(2):
        s = lax.dot_general(q_ref[j], k_ref[j], (((1,), (1,)), ((), ())),
                            preferred_element_type=F32)
        m = jnp.max(s, axis=-1, keepdims=True)
        p = jnp.exp(s - m)
        l = jnp.sum(p, axis=-1, keepdims=True)
        outs.append(_dot(p.astype(BF16), v) / l)
    lane = lax.broadcasted_iota(jnp.int32, outs[0].shape, 1)
    o_ref[...] = jnp.where(lane < V_HEAD, outs[0], outs[1]).astype(BF16)


def _attention(q, k, v):
    bsz, _, seq, _ = q.shape
    tq = min(Q_TILE, seq)
    return pl.pallas_call(
        _attn_kernel,
        grid=(bsz, N_HEADS // 2, seq // tq),
        in_specs=[
            pl.BlockSpec((None, 2, tq, LANES), lambda b, h, i: (b, h, i, 0)),
            pl.BlockSpec((None, 2, seq, LANES), lambda b, h, i: (b, h, 0, 0)),
            pl.BlockSpec((None, seq, LANES), lambda b, h, i: (b, 0, h)),
        ],
        out_specs=pl.BlockSpec((None, tq, LANES), lambda b, h, i: (b, i, h)),
        out_shape=jax.ShapeDtypeStruct((bsz, seq, N_HEADS * V_HEAD), BF16),
        compiler_params=_params("arbitrary", "arbitrary", "arbitrary"),
        name="mla_attention",
    )(q, k, v)


def _first_index_of_max(vals):
    m = vals[0]
    for v in vals[1:]:
        m = jnp.maximum(m, v)
    idx = jnp.full(m.shape, float(len(vals) - 1), F32)
    for j in range(len(vals) - 2, -1, -1):
        idx = jnp.where(vals[j] == m, float(j), idx)
    return m, idx


def _route(h2, wr1_ref, wr2_ref, rb_ref):
    hh, hl = _split_bf16(h2)
    logits = (_dot(hh, wr1_ref[...]) + _dot(hl, wr2_ref[...])).T
    logit = logits[0:N_EXPERTS] + logits[N_EXPERTS:2 * N_EXPERTS] + logits[2 * N_EXPERTS:3 * N_EXPERTS]
    score = jax.nn.sigmoid(logit)
    biased = score + rb_ref[...]
    a = [biased[j * N_GROUPS:(j + 1) * N_GROUPS] for j in range(EXPERTS_PER_GROUP)]
    sc = [score[j * N_GROUPS:(j + 1) * N_GROUPS] for j in range(EXPERTS_PER_GROUP)]
    hi1, lo1 = jnp.maximum(a[0], a[1]), jnp.minimum(a[0], a[1])
    hi2, lo2 = jnp.maximum(a[2], a[3]), jnp.minimum(a[2], a[3])
    gscore = jnp.maximum(hi1, hi2) + jnp.maximum(jnp.minimum(hi1, hi2), jnp.maximum(lo1, lo2))
    gmax = jnp.max(gscore, axis=0, keepdims=True)
    giota = lax.broadcasted_iota(jnp.int32, gscore.shape, 0).astype(F32)
    gsel = jnp.min(jnp.where(gscore == gmax, giota, float(N_GROUPS)), axis=0, keepdims=True)
    onehot = giota == gsel
    pick = lambda t: jnp.sum(jnp.where(onehot, t, 0.0), axis=0, keepdims=True)
    bj = [pick(t) for t in a]
    sj = [pick(t) for t in sc]
    _, i1 = _first_index_of_max(bj)
    bj2 = [jnp.where(i1 == float(j), -jnp.inf, bj[j]) for j in range(EXPERTS_PER_GROUP)]
    _, i2 = _first_index_of_max(bj2)
    sel = lambda i: jnp.where(i == 0.0, sj[0], jnp.where(i == 1.0, sj[1], jnp.where(i == 2.0, sj[2], sj[3])))
    w1, w2 = sel(i1), sel(i2)
    den = w1 + w2
    base = gsel * float(EXPERTS_PER_GROUP)
    return ((base + i1).astype(jnp.int32), (base + i2).astype(jnp.int32)), (w1 / den, w2 / den)


def _mix_out_kernel(has_gate, *refs):
    if has_gate:
        a_ref, hs_ref, x_ref, wo_ref, g1_ref, g_ref, sh_ref, sc_ref, wr1_ref, wr2_ref, rb_ref, \
            x_out, h_out, idx_out, wts_out = refs
        a = (a_ref[...].astype(F32) * hs_ref[...]).astype(BF16)
    else:
        a_ref, x_ref, wo_ref, g1_ref, g_ref, sh_ref, sc_ref, wr1_ref, wr2_ref, rb_ref, \
            x_out, h_out, idx_out, wts_out = refs
        a = a_ref[...]
    x1 = x_ref[...] + g1_ref[...] * _dot(a, wo_ref[...])
    x_out[...] = x1
    h2 = _modulate(x1, g_ref[...], sh_ref[...], sc_ref[...])
    _to_tiles(h_out, h2)
    idx, wts = _route(h2, wr1_ref, wr2_ref, rb_ref)
    for k in range(TOP_K):
        idx_out[k:k + 1, :] = idx[k]
        wts_out[k:k + 1, :] = wts[k]


def _mix_out(a, hs, x, w_o, gate1, gain, shift, scale, wr1, wr2, rbias):
    bsz, seq, d = x.shape
    tm = min(ROW_TILE, seq)
    row = lambda b, i: (b, i, 0)
    per_b = lambda b, i: (b, 0, 0)
    const = lambda b, i: (0, 0)
    full = lambda t: pl.BlockSpec(t.shape, const)
    vec = pl.BlockSpec((None, 1, d), per_b)
    acts = [a] if hs is None else [a, hs]
    return pl.pallas_call(
        functools.partial(_mix_out_kernel, hs is not None),
        grid=(bsz, seq // tm),
        in_specs=[pl.BlockSpec((None, tm, t.shape[-1]), row) for t in acts] + [
            pl.BlockSpec((None, tm, d), row), full(w_o), vec, full(gain), vec, vec,
            full(wr1), full(wr2), full(rbias),
        ],
        out_specs=[
            pl.BlockSpec((None, tm, d), row),
            pl.BlockSpec((None, tm, d // LANES, LANES), lambda b, i: (b, i, 0, 0)),
            pl.BlockSpec((None, TOP_K, tm), lambda b, i: (b, 0, i)),
            pl.BlockSpec((None, TOP_K, tm), lambda b, i: (b, 0, i)),
        ],
        out_shape=[
            jax.ShapeDtypeStruct((bsz, seq, d), F32),
            jax.ShapeDtypeStruct((bsz, seq, d // LANES, LANES), F32),
            jax.ShapeDtypeStruct((bsz, TOP_K, seq), jnp.int32),
            jax.ShapeDtypeStruct((bsz, TOP_K, seq), F32),
        ],
        compiler_params=_params("arbitrary", "arbitrary"),
        name="mix_out_route",
    )(*acts, x, w_o, gate1, gain, shift, scale, wr1, wr2, rbias)


N_SUB = D_MODEL // LANES
TABLE_CHUNK = 512


def _to_tiles(ref, val):
    for s in range(N_SUB):
        ref[:, s, :] = val[:, s * LANES:(s + 1) * LANES]


def _from_tiles(ref, lo, n):
    return jnp.concatenate([ref[lo:lo + n, s, :] for s in range(N_SUB)], axis=1)


def _tables_kernel(idx_ref, rank_ref, cnt_ref, carry):
    @pl.when(pl.program_id(0) == 0)
    def _():
        carry[...] = jnp.zeros_like(carry)

    seq = idx_ref.shape[-1]
    ch = min(TABLE_CHUNK, seq)
    tri = jnp.where(lax.broadcasted_iota(jnp.int32, (ch, ch), 0) <= lax.broadcasted_iota(jnp.int32, (ch, ch), 1),
                    1.0, 0.0).astype(BF16)
    eiota = lax.broadcasted_iota(jnp.int32, (N_EXPERTS, ch), 0)
    cnt = carry[...]
    for k in range(TOP_K):
        for c in range(seq // ch):
            sel = eiota == idx_ref[k:k + 1, c * ch:(c + 1) * ch]
            pref = _dot(jnp.where(sel, 1.0, 0.0).astype(BF16), tri) + cnt
            rank = jnp.sum(jnp.where(sel, pref, 0.0), axis=0, keepdims=True) - 1.0
            rank_ref[k:k + 1, c * ch:(c + 1) * ch] = rank.astype(jnp.int32)
            cnt = pref[:, ch - 1:ch]
    carry[...] = cnt
    cnt_ref[...] = jnp.broadcast_to(cnt, cnt_ref.shape)


def _tables(idx):
    bsz, _, seq = idx.shape
    return pl.pallas_call(
        _tables_kernel,
        grid=(bsz,),
        in_specs=[pl.BlockSpec((None, TOP_K, seq), lambda b: (b, 0, 0))],
        out_specs=[pl.BlockSpec((None, TOP_K, seq), lambda b: (b, 0, 0)),
                   pl.BlockSpec((N_EXPERTS, LANES), lambda b: (0, 0))],
        out_shape=[jax.ShapeDtypeStruct((bsz, TOP_K, seq), jnp.int32),
                   jax.ShapeDtypeStruct((N_EXPERTS, LANES), F32)],
        scratch_shapes=[pltpu.VMEM((N_EXPERTS, 1), F32)],
        compiler_params=_params("arbitrary"),
        name="moe_tables",
    )(idx)


def _scatter_kernel(dest_ref, pad_ref, src_ref, zero_ref, dst_hbm, sem):
    tm = dest_ref.shape[-1]
    n_pad = pad_ref.shape[-1]
    for k in range(TOP_K):
        def start(r, carry, k=k):
            pltpu.make_async_copy(src_ref.at[r], dst_hbm.at[dest_ref[k, r]], sem).start(priority=k)
            return carry
        lax.fori_loop(0, tm, start, 0, unroll=8)

    def fill(r, carry):
        pltpu.make_async_copy(zero_ref.at[0], dst_hbm.at[pad_ref[0, r]], sem).start()
        return carry
    lax.fori_loop(0, n_pad, fill, 0, unroll=8)

    left = TOP_K * tm + n_pad
    while left > 0:
        cnt = min(tm, left)
        pltpu.make_async_copy(src_ref.at[pl.ds(0, cnt)], dst_hbm.at[pl.ds(0, cnt)], sem).wait()
        left -= cnt


def _scatter(dest, pad_rows, h2t, n_rows):
    bsz, _, seq = dest.shape
    tm = min(ROW_TILE, seq)
    nt = seq // tm
    n_pad = pad_rows.shape[0] // (bsz * nt)
    assert n_pad * bsz * nt == pad_rows.shape[0]
    return pl.pallas_call(
        _scatter_kernel,
        grid=(bsz, nt),
        in_specs=[
            pl.BlockSpec((None, TOP_K, tm), lambda b, i: (b, 0, i), memory_space=pltpu.SMEM),
            pl.BlockSpec((None, 1, n_pad), lambda b, i: (b * nt + i, 0, 0), memory_space=pltpu.SMEM),
            pl.BlockSpec((tm, N_SUB, LANES), lambda b, i: (b * nt + i, 0, 0)),
            pl.BlockSpec((1, N_SUB, LANES), lambda b, i: (0, 0, 0)),
        ],
        out_specs=pl.BlockSpec(memory_space=pl.ANY),
        out_shape=jax.ShapeDtypeStruct((n_rows, N_SUB, LANES), F32),
        scratch_shapes=[pltpu.SemaphoreType.DMA(())],
        compiler_params=_params("arbitrary", "arbitrary"),
        name="moe_scatter",
    )(dest, pad_rows.reshape(bsz * nt, 1, n_pad), h2t.reshape(bsz * seq, N_SUB, LANES),
      jnp.zeros((1, N_SUB, LANES), F32))


def _expert_kernel(blk_exp_ref, blk_first_ref, n_used_ref,
                   xs_ref, wgu_ref, wdn_ref, ys_ref, wgu_bf, wdn_bf):
    i = pl.program_id(0)

    @pl.when(i < n_used_ref[0])
    def _():
        @pl.when(blk_first_ref[i] == 1)
        def _():
            wgu_bf[...] = wgu_ref[...].astype(BF16)
            wdn_bf[...] = wdn_ref[...].astype(BF16)

        x = _from_tiles(xs_ref, 0, MOE_TILE).astype(BF16)
        gu = _dot(x, wgu_bf[...])
        g = gu[:, :D_EXPERT]
        u = gu[:, D_EXPERT:]
        mid = (g * jax.nn.sigmoid(g) * u).astype(BF16)
        _to_tiles(ys_ref, _dot(mid, wdn_bf[...]))

    @pl.when(i >= n_used_ref[0])
    def _():
        ys_ref[...] = jnp.zeros_like(ys_ref)


def _experts(blk_exp, blk_first, n_used, xs, w_gu, w_dn):
    n_rows = xs.shape[0]
    d = D_MODEL
    nb = n_rows // MOE_TILE
    tile = lambda i, *_: (i, 0, 0)
    grid_spec = pltpu.PrefetchScalarGridSpec(
        num_scalar_prefetch=3,
        grid=(nb,),
        in_specs=[
            pl.BlockSpec((MOE_TILE, N_SUB, LANES), tile),
            pl.BlockSpec((None, d, 2 * D_EXPERT), lambda i, be, *_: (be[i], 0, 0)),
            pl.BlockSpec((None, D_EXPERT, d), lambda i, be, *_: (be[i], 0, 0)),
        ],
        out_specs=pl.BlockSpec((MOE_TILE, N_SUB, LANES), tile),
        scratch_shapes=[pltpu.VMEM((d, 2 * D_EXPERT), BF16), pltpu.VMEM((D_EXPERT, d), BF16)],
    )
    return pl.pallas_call(
        _expert_kernel,
        grid_spec=grid_spec,
        out_shape=jax.ShapeDtypeStruct((n_rows, N_SUB, LANES), F32),
        compiler_params=_params("arbitrary"),
        name="moe_experts",
    )(blk_exp, blk_first, n_used, xs, w_gu, w_dn)


def _combine_kernel(dcur_ref, dnxt_ref, ys_hbm, x_ref, wts_ref, g2_ref, x_out, buf, sem):
    nt = pl.num_programs(1)
    n = pl.program_id(0) * nt + pl.program_id(1)
    total = pl.num_programs(0) * nt
    tm = x_ref.shape[0]
    slot = lax.rem(n, 2)

    def issue(d_ref, sl):
        for k in range(TOP_K):
            def start(r, carry, k=k):
                pltpu.make_async_copy(ys_hbm.at[d_ref[k, r]], buf.at[sl, k * tm + r], sem.at[sl]).start(priority=k)
                return carry
            lax.fori_loop(0, tm, start, 0, unroll=8)

    @pl.when(n == 0)
    def _():
        issue(dcur_ref, 0)

    @pl.when(n + 1 < total)
    def _():
        issue(dnxt_ref, 1 - slot)

    pltpu.make_async_copy(ys_hbm.at[pl.ds(0, TOP_K * tm)], buf.at[slot], sem.at[slot]).wait()
    cur = buf.at[slot]
    w = wts_ref[...]
    y = w[:, 0:1] * _from_tiles(cur, 0, tm) + w[:, 1:2] * _from_tiles(cur, tm, tm)
    x_out[...] = x_ref[...] + g2_ref[...] * y


def _combine(dest, ys, x, wts_col, gate2):
    bsz, seq, d = x.shape
    tm = min(ROW_TILE, seq)
    nt = seq // tm

    def nxt(b, i):
        n = jnp.minimum(b * nt + i + 1, bsz * nt - 1)
        return (n // nt, 0, n % nt)

    return pl.pallas_call(
        _combine_kernel,
        grid=(bsz, nt),
        in_specs=[
            pl.BlockSpec((None, TOP_K, tm), lambda b, i: (b, 0, i), memory_space=pltpu.SMEM),
            pl.BlockSpec((None, TOP_K, tm), nxt, memory_space=pltpu.SMEM),
            pl.BlockSpec(memory_space=pl.ANY),
            pl.BlockSpec((None, tm, d), lambda b, i: (b, i, 0)),
            pl.BlockSpec((None, tm, TOP_K), lambda b, i: (b, i, 0)),
            pl.BlockSpec((None, 1, d), lambda b, i: (b, 0, 0)),
        ],
        out_specs=pl.BlockSpec((None, tm, d), lambda b, i: (b, i, 0)),
        out_shape=jax.ShapeDtypeStruct((bsz, seq, d), F32),
        scratch_shapes=[pltpu.VMEM((2, TOP_K * tm, N_SUB, LANES), F32), pltpu.SemaphoreType.DMA((2,))],
        compiler_params=_params("arbitrary", "arbitrary"),
        name="moe_combine",
    )(dest, dest, ys, x, wts_col, gate2)


def _lookup(table, keys):
    hit = keys[..., None] == jnp.arange(table.shape[0], dtype=jnp.int32)
    return jnp.sum(jnp.where(hit, table, 0), axis=-1).astype(jnp.int32)


def _count_le(bounds, q):
    return jnp.sum((bounds <= q[..., None]).astype(jnp.int32), axis=-1)


def _moe(h2t, idx, wts, x, gate2, w_gu, w_dn):
    bsz, seq, _ = x.shape
    n_rows = bsz * seq * TOP_K + N_EXPERTS * MOE_TILE
    nb = n_rows // MOE_TILE
    rank, cnt = _tables(idx)
    counts = cnt[:, 0].astype(jnp.int32)
    padded = ((counts + MOE_TILE - 1) // MOE_TILE) * MOE_TILE
    pend = jnp.cumsum(padded)
    pstart = pend - padded
    dest = _lookup(pstart, idx) + rank
    blk_row = jnp.arange(nb, dtype=jnp.int32) * MOE_TILE
    blk_exp = jnp.minimum(_count_le(pend, blk_row), N_EXPERTS - 1)
    blk_first = (blk_row == _lookup(pstart, blk_exp)).astype(jnp.int32)
    n_used = (pend[-1:] // MOE_TILE).astype(jnp.int32)
    slack_len = jnp.concatenate([padded - counts, n_rows - pend[-1:]])
    slack_row = jnp.concatenate([pstart + counts, pend[-1:]])
    slack_end = jnp.cumsum(slack_len)
    q = jnp.arange(N_EXPERTS * MOE_TILE, dtype=jnp.int32)
    seg = _count_le(slack_end, q)
    pad_rows = _lookup(slack_row - (slack_end - slack_len), seg) + q
    xs = _scatter(dest, pad_rows, h2t, n_rows)
    ys = _experts(blk_exp, blk_first, n_used, xs, w_gu, w_dn)
    return _combine(dest, ys, x, wts.transpose(0, 2, 1), gate2)


def _rnn_in_kernel(x_ref, g_ref, sh_ref, sc_ref, w_ref, gate_out, xb_out):
    h = _modulate(x_ref[...], g_ref[...], sh_ref[...], sc_ref[...])
    u = _dot(h.astype(BF16), w_ref[...])
    gate_out[...] = jax.nn.gelu(u[:, :D_RNN]).astype(BF16)
    xb_out[...] = u[:, D_RNN:]


def _rnn_in(x, gain, shift, scale, w_in):
    bsz, seq, d = x.shape
    tm = min(ROW_TILE, seq)
    row = lambda b, i: (b, i, 0)
    per_b = lambda b, i: (b, 0, 0)
    const = lambda b, i: (0, 0)
    return pl.pallas_call(
        _rnn_in_kernel,
        grid=(bsz, seq // tm),
        in_specs=[
            pl.BlockSpec((None, tm, d), row),
            pl.BlockSpec(gain.shape, const),
            pl.BlockSpec((None, 1, d), per_b),
            pl.BlockSpec((None, 1, d), per_b),
            pl.BlockSpec(w_in.shape, const),
        ],
        out_specs=[pl.BlockSpec((None, tm, D_RNN), row), pl.BlockSpec((None, tm, D_RNN), row)],
        out_shape=[jax.ShapeDtypeStruct((bsz, seq, D_RNN), BF16),
                   jax.ShapeDtypeStruct((bsz, seq, D_RNN), F32)],
        compiler_params=_params("arbitrary", "arbitrary"),
        name="rnn_in",
    )(x, gain, shift, scale, w_in)


def _chunk_scan(a, b, reverse):
    row = lax.broadcasted_iota(jnp.int32, a.shape, 0)
    for dd in (1, 2, 4):
        if reverse:
            keep = row < SUBLANES - dd
            shift = SUBLANES - dd
        else:
            keep = row >= dd
            shift = dd
        a_sh = jnp.where(keep, pltpu.roll(a, shift, 0), 1.0)
        b_sh = jnp.where(keep, pltpu.roll(b, shift, 0), 0.0)
        b = a * b_sh + b
        a = a * a_sh
    return a, b


def _lru_kernel(xb_ref, cw_ref, cb_ref, wcat_ref, bcat_ref, lam_ref, hs_ref,
                xp_ref, af_ref, bf_ref, ab_ref, bb_ref):
    seq, c = xb_ref.shape
    pad = SUBLANES
    xp_ref[0:pad, :] = jnp.zeros((pad, c), F32)
    xp_ref[pad + seq:pad + seq + pad, :] = jnp.zeros((pad, c), F32)
    xp_ref[pad:pad + seq, :] = xb_ref[...]
    cw = cw_ref[...]
    cb = cb_ref[...]
    lam = lam_ref[...]
    neg = -lam
    softplus = jnp.maximum(neg, 0.0) + jnp.log1p(jnp.exp(-jnp.abs(neg)))
    rate = -LRU_C * softplus
    rows = min(SCAN_ROWS, seq)

    for ci in range(seq // rows):
        r0 = ci * rows
        xc = cb
        for k in range(CONV_W):
            xc = xc + xp_ref[r0 + pad - CONV_W // 2 + k:r0 + pad - CONV_W // 2 + k + rows, :] * cw[k:k + 1, :]
        z = _dot(xc.astype(BF16), wcat_ref[...]) + bcat_ref[...]
        t = r0 + lax.broadcasted_iota(jnp.int32, (rows, c), 0)
        for dirn, (a_ref, b_ref, first) in enumerate(((af_ref, bf_ref, 0), (ab_ref, bb_ref, seq - 1))):
            r = jax.nn.sigmoid(z[:, (2 * dirn) * c:(2 * dirn + 1) * c])
            ig = jax.nn.sigmoid(z[:, (2 * dirn + 1) * c:(2 * dirn + 2) * c])
            log_a = rate[dirn:dirn + 1, :] * r
            a = jnp.exp(log_a)
            mult = jnp.where(t == first, 1.0, jnp.sqrt(jnp.tanh(-log_a) * (a * a + 1.0)))
            a_ref[r0:r0 + rows, :] = a
            b_ref[r0:r0 + rows, :] = mult * ig * xc

    n_chunks = seq // SUBLANES

    def fwd(ci, h):
        r0 = pl.multiple_of(ci * SUBLANES, SUBLANES)
        a, b = _chunk_scan(af_ref[pl.ds(r0, SUBLANES), :], bf_ref[pl.ds(r0, SUBLANES), :], False)
        hv = a * h + b
        hs_ref[pl.ds(r0, SUBLANES), :] = hv
        return hv[SUBLANES - 1:SUBLANES, :]
    lax.fori_loop(0, n_chunks, fwd, jnp.zeros((1, c), F32), unroll=4)

    def bwd(ci, h):
        r0 = pl.multiple_of((n_chunks - 1 - ci) * SUBLANES, SUBLANES)
        a, b = _chunk_scan(ab_ref[pl.ds(r0, SUBLANES), :], bb_ref[pl.ds(r0, SUBLANES), :], True)
        hv = a * h + b
        hs_ref[pl.ds(r0, SUBLANES), :] += hv
        return hv[0:1, :]
    lax.fori_loop(0, n_chunks, bwd, jnp.zeros((1, c), F32), unroll=4)


def _lru(xb, conv_w, conv_b, wcat, bcat, lam):
    bsz, seq, _ = xb.shape
    c = RNN_BW
    blk = lambda b, n: (b, 0, n)
    return pl.pallas_call(
        _lru_kernel,
        grid=(bsz, RNN_BLOCKS),
        in_specs=[
            pl.BlockSpec((None, seq, c), blk),
            pl.BlockSpec((CONV_W, c), lambda b, n: (0, n)),
            pl.BlockSpec((1, c), lambda b, n: (0, n)),
            pl.BlockSpec((None, c, 4 * c), lambda b, n: (n, 0, 0)),
            pl.BlockSpec((None, 1, 4 * c), lambda b, n: (n, 0, 0)),
            pl.BlockSpec((2, c), lambda b, n: (0, n)),
        ],
        out_specs=pl.BlockSpec((None, seq, c), blk),
        out_shape=jax.ShapeDtypeStruct((bsz, seq, D_RNN), F32),
        scratch_shapes=[pltpu.VMEM((seq + 2 * SUBLANES, c), F32)] + [pltpu.VMEM((seq, c), F32)] * 4,
        compiler_params=_params("arbitrary", "arbitrary"),
        name="rglru_scan",
    )(xb, conv_w, conv_b, wcat, bcat, lam)


def _pad_cols(w, n):
    return jnp.pad(w, ((0, 0), (0, n - w.shape[1])))


def _mla_weights(w_in, w_q_b, w_kv_b, q_norm, k_norm):
    half = QK_ROPE // 2

    def slab(t):
        return jnp.pad(t, [(0, 0)] * (t.ndim - 1) + [(0, LANES - QK_HEAD)])

    def rot_slab(t):
        rope = t[..., QK_NOPE:]
        swapped = jnp.concatenate([jnp.zeros_like(t[..., :QK_NOPE]), rope[..., half:], rope[..., :half]], axis=-1)
        return slab(swapped)

    kpe = jnp.pad(w_in[:, Q_LORA + KV_LORA:], ((0, 0), (QK_NOPE, 0)))
    w_in_p = jnp.concatenate([w_in[:, :Q_LORA + KV_LORA], slab(kpe), rot_slab(kpe)], axis=1).astype(BF16)
    wq = w_q_b.reshape(Q_LORA, N_HEADS, QK_HEAD)
    wq_p = slab(wq).reshape(Q_LORA, N_HEADS * LANES).astype(BF16)
    wq_rot = rot_slab(wq).reshape(Q_LORA, N_HEADS * LANES).astype(BF16)
    wkv = w_kv_b.reshape(KV_LORA, N_HEADS, QK_NOPE + V_HEAD)
    wk = jnp.pad(wkv[:, :, :QK_NOPE], ((0, 0), (0, 0), (0, LANES - QK_NOPE))).reshape(KV_LORA, N_HEADS * LANES)
    wv = wkv[:, :, QK_NOPE:].reshape(KV_LORA, N_HEADS * V_HEAD)
    w_kv_p = jnp.concatenate([wk, wv], axis=1).astype(BF16)
    gains = lambda g: jnp.stack([slab(g), rot_slab(g)], axis=0)
    return w_in_p, wq_p, wq_rot, w_kv_p, gains(q_norm), gains(k_norm)


def _rope_tables(positions):
    half = QK_ROPE // 2
    inv_freq = ROPE_THETA ** (-jnp.arange(half, dtype=F32) / half)
    ang = positions.astype(F32)[..., None] * inv_freq
    cos, sin = jnp.cos(ang), jnp.sin(ang)
    lead = positions.shape + (QK_NOPE,)
    tail = positions.shape + (LANES - QK_HEAD,)
    cos_t = jnp.concatenate([jnp.ones(lead, F32), cos, cos, jnp.ones(tail, F32)], axis=-1)
    sin_t = jnp.concatenate([jnp.zeros(lead, F32), -sin, sin, jnp.zeros(tail, F32)], axis=-1)
    return cos_t, sin_t


def _router_weights(w_router, router_bias):
    perm = (jnp.arange(N_EXPERTS) % N_GROUPS) * EXPERTS_PER_GROUP + jnp.arange(N_EXPERTS) // N_GROUPS
    w = w_router[:, perm]
    hi = w.astype(BF16)
    lo = (w - hi.astype(F32)).astype(BF16)
    z = jnp.zeros_like(hi)
    wr1 = jnp.concatenate([hi, lo, z, z], axis=1)
    wr2 = jnp.concatenate([z, z, hi, z], axis=1)
    return wr1, wr2, router_bias[perm].reshape(N_EXPERTS, 1).astype(F32)


def kernel(x, c, positions, norm_mix, norm_ffn, w_ada, b_ada, mla_w_in, mla_q_a_norm, mla_kv_a_norm, mla_w_q_b, mla_w_kv_b, mla_q_norm, mla_k_norm, mla_w_o, rnn_w_in, rnn_conv_w, rnn_conv_b, rnn_lam_f, rnn_w_rf, rnn_b_rf, rnn_w_if, rnn_b_if, rnn_lam_b, rnn_w_rb, rnn_b_rb, rnn_w_ib, rnn_b_ib, rnn_w_o, w_router, router_bias, moe_w_gu, moe_w_dn):
    bsz, seq, d = x.shape
    depth = w_ada.shape[0]
    mod = _ada(c, w_ada, b_ada)
    wr1, wr2, rbias = _router_weights(w_router, router_bias)
    cos_t, sin_t = _rope_tables(positions)
    vec = lambda v: v.reshape(1, -1)
    for i in range(depth):
        sh1, sc1, g1, sh2, sc2, g2 = [mod[i, :, k * d:(k + 1) * d].reshape(bsz, 1, d) for k in range(6)]
        j = i // 2
        if i % 2 == 0:
            w_in_p, wq, wq_rot, wkv, qn, kn = _mla_weights(mla_w_in[j], mla_w_q_b[j], mla_w_kv_b[j],
                                                           mla_q_norm[j], mla_k_norm[j])
            q, k, v = _mla_in(x, vec(norm_mix[i]), sh1, sc1, w_in_p, vec(mla_q_a_norm[j]),
                              vec(mla_kv_a_norm[j]), wq, wq_rot, wkv, qn, kn, cos_t, sin_t)
            a = _attention(q, k, v)
            hs = None
            w_o = mla_w_o[j].astype(BF16)
        else:
            a, xb = _rnn_in(x, vec(norm_mix[i]), sh1, sc1, rnn_w_in[j].astype(BF16))
            wcat = jnp.concatenate([rnn_w_rf[j], rnn_w_if[j], rnn_w_rb[j], rnn_w_ib[j]], axis=-1).astype(BF16)
            bcat = jnp.stack([b.reshape(RNN_BLOCKS, RNN_BW) for b in
                              (rnn_b_rf[j], rnn_b_if[j], rnn_b_rb[j], rnn_b_ib[j])], axis=1)
            bcat = bcat.reshape(RNN_BLOCKS, 1, 4 * RNN_BW)
            lam = jnp.stack([rnn_lam_f[j], rnn_lam_b[j]], axis=0)
            hs = _lru(xb, rnn_conv_w[j], vec(rnn_conv_b[j]), wcat, bcat, lam)
            w_o = rnn_w_o[j].astype(BF16)
        x, h2, idx, wts = _mix_out(a, hs, x, w_o, g1, vec(norm_ffn[i]), sh2, sc2, wr1, wr2, rbias)
        x = _moe(h2, idx, wts, x, g2, moe_w_gu[i], moe_w_dn[i])
    return x
```

```python
import functools

import jax
import jax.numpy as jnp
from jax import lax
from jax.experimental import pallas as pl
from jax.experimental.pallas import tpu as pltpu

F32 = jnp.float32
BF16 = jnp.bfloat16

D_MODEL = 1024
N_HEADS = 16
Q_LORA = 384
KV_LORA = 256
QK_NOPE = 64
QK_ROPE = 32
QK_HEAD = QK_NOPE + QK_ROPE
V_HEAD = 64
ROPE_THETA = 10000.0
D_RNN = D_MODEL
RNN_BLOCKS = 4
RNN_BW = D_RNN // RNN_BLOCKS
CONV_W = 4
LRU_C = 8.0
N_EXPERTS = 32
N_GROUPS = 8
EXPERTS_PER_GROUP = N_EXPERTS // N_GROUPS
TOP_K = 2
D_EXPERT = 512
EPS = 1e-6

LANES = 128
SUBLANES = 8
VMEM_LIMIT = 52 * 1024 * 1024

ROW_TILE = 256
Q_TILE = 512
MOE_TILE = 256
SCAN_ROWS = 256


def _dot(a, b):
    return jnp.dot(a, b, preferred_element_type=F32)


def _split_bf16(a):
    hi = a.astype(BF16)
    lo = (a - hi.astype(F32)).astype(BF16)
    return hi, lo


def _dot_split(a, b):
    ah, al = _split_bf16(a)
    bh, bl = _split_bf16(b)
    return _dot(ah, bh) + (_dot(ah, bl) + _dot(al, bh))


def _rms(x, gain, n):
    ms = jnp.sum(x * x, axis=-1, keepdims=True) * (1.0 / n)
    return x * lax.rsqrt(ms + EPS) * gain


def _modulate(x, gain, shift, scale):
    return _rms(x, gain, x.shape[-1]) * (1.0 + scale) + shift


def _params(*sem):
    return pltpu.CompilerParams(dimension_semantics=sem, vmem_limit_bytes=VMEM_LIMIT)


def _ada_kernel(c_ref, w_ref, b_ref, o_ref):
    c = c_ref[...]
    o_ref[...] = _dot_split(c * jax.nn.sigmoid(c), w_ref[...]) + b_ref[...]


def _ada(c, w_ada, b_ada):
    depth, d, n = w_ada.shape
    bsz = c.shape[0]
    tn = 1536
    return pl.pallas_call(
        _ada_kernel,
        grid=(depth, n // tn),
        in_specs=[
            pl.BlockSpec((bsz, d), lambda l, j: (0, 0)),
            pl.BlockSpec((None, d, tn), lambda l, j: (l, 0, j)),
            pl.BlockSpec((None, 1, tn), lambda l, j: (l, 0, j)),
        ],
        out_specs=pl.BlockSpec((None, bsz, tn), lambda l, j: (l, 0, j)),
        out_shape=jax.ShapeDtypeStruct((depth, bsz, n), F32),
        compiler_params=_params("arbitrary", "arbitrary"),
        name="adaln_mod",
    )(c, w_ada, b_ada.reshape(depth, 1, n))


def _head_scale(s):
    return lax.rsqrt(jnp.sum(s * s, axis=-1, keepdims=True) * (1.0 / QK_HEAD) + EPS)


def _mla_in_kernel(x_ref, g_ref, sh_ref, sc_ref, win_ref, qan_ref, kvan_ref, wq_ref, wqr_ref, wkv_ref,
                   qn_ref, kn_ref, cos_ref, sin_ref, q_out, k_out, v_out):
    h = _modulate(x_ref[...], g_ref[...], sh_ref[...], sc_ref[...])
    lat = _dot(h.astype(BF16), win_ref[...])
    q_lat = lat[:, :Q_LORA]
    kv_lat = lat[:, Q_LORA:Q_LORA + KV_LORA]
    kpe = lat[:, Q_LORA + KV_LORA:Q_LORA + KV_LORA + LANES]
    kpe_rot = lat[:, Q_LORA + KV_LORA + LANES:]
    qn = _rms(q_lat, qan_ref[...], Q_LORA).astype(BF16)
    q_all = _dot(qn, wq_ref[...])
    q_rot = _dot(qn, wqr_ref[...])
    kv_all = _dot(_rms(kv_lat, kvan_ref[...], KV_LORA).astype(BF16), wkv_ref[...])
    cos_t = cos_ref[...]
    sin_t = sin_ref[...]
    q_scale = QK_HEAD ** -0.5
    cq = cos_t * (qn_ref[0:1, :] * q_scale)
    sq = sin_t * (qn_ref[1:2, :] * q_scale)
    ck = cos_t * kn_ref[0:1, :]
    k_rot_term = kpe_rot * (sin_t * kn_ref[1:2, :])
    for hh in range(N_HEADS):
        sl = slice(hh * LANES, (hh + 1) * LANES)
        s = q_all[:, sl]
        q_out[hh] = ((s * cq + q_rot[:, sl] * sq) * _head_scale(s)).astype(BF16)
        s = kv_all[:, sl] + kpe
        k_out[hh] = ((s * ck + k_rot_term) * _head_scale(s)).astype(BF16)
    v_out[...] = kv_all[:, N_HEADS * LANES:].astype(BF16)


def _mla_in(x, gain, shift, scale, w_in, q_a_norm, kv_a_norm, w_q, w_q_rot, w_kv, q_norm, k_norm, cos_t, sin_t):
    bsz, seq, d = x.shape
    tm = min(ROW_TILE, seq)
    row = lambda b, i: (b, i, 0)
    per_b = lambda b, i: (b, 0, 0)
    const = lambda b, i: (0, 0)
    full = lambda a: pl.BlockSpec(a.shape, const)
    return pl.pallas_call(
        _mla_in_kernel,
        grid=(bsz, seq // tm),
        in_specs=[
            pl.BlockSpec((None, tm, d), row),
            full(gain),
            pl.BlockSpec((None, 1, d), per_b),
            pl.BlockSpec((None, 1, d), per_b),
            full(w_in), full(q_a_norm), full(kv_a_norm), full(w_q), full(w_q_rot), full(w_kv),
            full(q_norm), full(k_norm),
            pl.BlockSpec((None, tm, LANES), row),
            pl.BlockSpec((None, tm, LANES), row),
        ],
        out_specs=[
            pl.BlockSpec((None, N_HEADS, tm, LANES), lambda b, i: (b, 0, i, 0)),
            pl.BlockSpec((None, N_HEADS, tm, LANES), lambda b, i: (b, 0, i, 0)),
            pl.BlockSpec((None, tm, N_HEADS * V_HEAD), row),
        ],
        out_shape=[
            jax.ShapeDtypeStruct((bsz, N_HEADS, seq, LANES), BF16),
            jax.ShapeDtypeStruct((bsz, N_HEADS, seq, LANES), BF16),
            jax.ShapeDtypeStruct((bsz, seq, N_HEADS * V_HEAD), BF16),
        ],
        compiler_params=_params("arbitrary", "arbitrary"),
        name="mla_in",
    )(x, gain, shift, scale, w_in, q_a_norm, kv_a_norm, w_q, w_q_rot, w_kv, q_norm, k_norm, cos_t, sin_t)


def _attn_kernel(q_ref, k_ref, v_ref, o_ref):
    v = v_ref[...]
    outs = []
    for j in range(2):
        s = lax.dot_general(q_ref[j], k_ref[j], (((1,), (1,)), ((), ())),
                            preferred_element_type=F32)
        m = jnp.max(s, axis=-1, keepdims=True)
        p = jnp.exp(s - m)
        l = jnp.sum(p, axis=-1, keepdims=True)
        outs.append(_dot(p.astype(BF16), v) / l)
    lane = lax.broadcasted_iota(jnp.int32, outs[0].shape, 1)
    o_ref[...] = jnp.where(lane < V_HEAD, outs[0], outs[1]).astype(BF16)


def _attention(q, k, v):
    bsz, _, seq, _ = q.shape
    tq = min(Q_TILE, seq)
    return pl.pallas_call(
        _attn_kernel,
        grid=(bsz, N_HEADS // 2, seq // tq),
        in_specs=[
            pl.BlockSpec((None, 2, tq, LANES), lambda b, h, i: (b, h, i, 0)),
            pl.BlockSpec((None, 2, seq, LANES), lambda b, h, i: (b, h, 0, 0)),
            pl.BlockSpec((None, seq, LANES), lambda b, h, i: (b, 0, h)),
        ],
        out_specs=pl.BlockSpec((None, tq, LANES), lambda b, h, i: (b, i, h)),
        out_shape=jax.ShapeDtypeStruct((bsz, seq, N_HEADS * V_HEAD), BF16),
        compiler_params=_params("arbitrary", "arbitrary", "arbitrary"),
        name="mla_attention",
    )(q, k, v)


def _first_index_of_max(vals):
    m = vals[0]
    for v in vals[1:]:
        m = jnp.maximum(m, v)
    idx = jnp.full(m.shape, float(len(vals) - 1), F32)
    for j in range(len(vals) - 2, -1, -1):
        idx = jnp.where(vals[j] == m, float(j), idx)
    return m, idx


def _route(h2, wr1_ref, wr2_ref, rb_ref):
    hh, hl = _split_bf16(h2)
    logits = (_dot(hh, wr1_ref[...]) + _dot(hl, wr2_ref[...])).T
    logit = logits[0:N_EXPERTS] + logits[N_EXPERTS:2 * N_EXPERTS] + logits[2 * N_EXPERTS:3 * N_EXPERTS]
    score = jax.nn.sigmoid(logit)
    biased = score + rb_ref[...]
    a = [biased[j * N_GROUPS:(j + 1) * N_GROUPS] for j in range(EXPERTS_PER_GROUP)]
    sc = [score[j * N_GROUPS:(j + 1) * N_GROUPS] for j in range(EXPERTS_PER_GROUP)]
    hi1, lo1 = jnp.maximum(a[0], a[1]), jnp.minimum(a[0], a[1])
    hi2, lo2 = jnp.maximum(a[2], a[3]), jnp.minimum(a[2], a[3])
    gscore = jnp.maximum(hi1, hi2) + jnp.maximum(jnp.minimum(hi1, hi2), jnp.maximum(lo1, lo2))
    gmax = jnp.max(gscore, axis=0, keepdims=True)
    giota = lax.broadcasted_iota(jnp.int32, gscore.shape, 0).astype(F32)
    gsel = jnp.min(jnp.where(gscore == gmax, giota, float(N_GROUPS)), axis=0, keepdims=True)
    onehot = giota == gsel
    pick = lambda t: jnp.sum(jnp.where(onehot, t, 0.0), axis=0, keepdims=True)
    bj = [pick(t) for t in a]
    sj = [pick(t) for t in sc]
    _, i1 = _first_index_of_max(bj)
    bj2 = [jnp.where(i1 == float(j), -jnp.inf, bj[j]) for j in range(EXPERTS_PER_GROUP)]
    _, i2 = _first_index_of_max(bj2)
    sel = lambda i: jnp.where(i == 0.0, sj[0], jnp.where(i == 1.0, sj[1], jnp.where(i == 2.0, sj[2], sj[3])))
    w1, w2 = sel(i1), sel(i2)
    den = w1 + w2
    base = gsel * float(EXPERTS_PER_GROUP)
    return ((base + i1).astype(jnp.int32), (base + i2).astype(jnp.int32)), (w1 / den, w2 / den)


def _mix_out_kernel(has_gate, *refs):
    if has_gate:
        a_ref, hs_ref, x_ref, wo_ref, g1_ref, g_ref, sh_ref, sc_ref, wr1_ref, wr2_ref, rb_ref, \
            x_out, h_out, idx_out, wts_out = refs
        a = (a_ref[...].astype(F32) * hs_ref[...]).astype(BF16)
    else:
        a_ref, x_ref, wo_ref, g1_ref, g_ref, sh_ref, sc_ref, wr1_ref, wr2_ref, rb_ref, \
            x_out, h_out, idx_out, wts_out = refs
        a = a_ref[...]
    x1 = x_ref[...] + g1_ref[...] * _dot(a, wo_ref[...])
    x_out[...] = x1
    h2 = _modulate(x1, g_ref[...], sh_ref[...], sc_ref[...])
    _to_tiles(h_out, h2)
    idx, wts = _route(h2, wr1_ref, wr2_ref, rb_ref)
    for k in range(TOP_K):
        idx_out[k:k + 1, :] = idx[k]
        wts_out[k:k + 1, :] = wts[k]


def _mix_out(a, hs, x, w_o, gate1, gain, shift, scale, wr1, wr2, rbias):
    bsz, seq, d = x.shape
    tm = min(ROW_TILE, seq)
    row = lambda b, i: (b, i, 0)
    per_b = lambda b, i: (b, 0, 0)
    const = lambda b, i: (0, 0)
    full = lambda t: pl.BlockSpec(t.shape, const)
    vec = pl.BlockSpec((None, 1, d), per_b)
    acts = [a] if hs is None else [a, hs]
    return pl.pallas_call(
        functools.partial(_mix_out_kernel, hs is not None),
        grid=(bsz, seq // tm),
        in_specs=[pl.BlockSpec((None, tm, t.shape[-1]), row) for t in acts] + [
            pl.BlockSpec((None, tm, d), row), full(w_o), vec, full(gain), vec, vec,
            full(wr1), full(wr2), full(rbias),
        ],
        out_specs=[
            pl.BlockSpec((None, tm, d), row),
            pl.BlockSpec((None, tm * (d // LANES), LANES), row),
            pl.BlockSpec((None, TOP_K, tm), lambda b, i: (b, 0, i)),
            pl.BlockSpec((None, TOP_K, tm), lambda b, i: (b, 0, i)),
        ],
        out_shape=[
            jax.ShapeDtypeStruct((bsz, seq, d), F32),
            jax.ShapeDtypeStruct((bsz, seq * (d // LANES), LANES), F32),
            jax.ShapeDtypeStruct((bsz, TOP_K, seq), jnp.int32),
            jax.ShapeDtypeStruct((bsz, TOP_K, seq), F32),
        ],
        compiler_params=_params("arbitrary", "arbitrary"),
        name="mix_out_route",
    )(*acts, x, w_o, gate1, gain, shift, scale, wr1, wr2, rbias)


N_SUB = D_MODEL // LANES
TABLE_CHUNK = 512


def _to_tiles(ref, val):
    n = val.shape[0]
    for s in range(N_SUB):
        ref[pl.ds(s, n, stride=N_SUB), :] = val[:, s * LANES:(s + 1) * LANES]


def _from_tiles(ref, lo, n):
    return jnp.concatenate([ref[pl.ds(lo * N_SUB + s, n, stride=N_SUB), :] for s in range(N_SUB)], axis=1)


def _tables_kernel(idx_ref, rank_ref, cnt_ref, carry):
    @pl.when(pl.program_id(0) == 0)
    def _():
        carry[...] = jnp.zeros_like(carry)

    seq = idx_ref.shape[-1]
    ch = min(TABLE_CHUNK, seq)
    tri = jnp.where(lax.broadcasted_iota(jnp.int32, (ch, ch), 0) <= lax.broadcasted_iota(jnp.int32, (ch, ch), 1),
                    1.0, 0.0).astype(BF16)
    eiota = lax.broadcasted_iota(jnp.int32, (N_EXPERTS, ch), 0)
    cnt = carry[...]
    for k in range(TOP_K):
        for c in range(seq // ch):
            sel = eiota == idx_ref[k:k + 1, c * ch:(c + 1) * ch]
            pref = _dot(jnp.where(sel, 1.0, 0.0).astype(BF16), tri) + cnt
            rank = jnp.sum(jnp.where(sel, pref, 0.0), axis=0, keepdims=True) - 1.0
            rank_ref[k:k + 1, c * ch:(c + 1) * ch] = rank.astype(jnp.int32)
            cnt = pref[:, ch - 1:ch]
    carry[...] = cnt
    cnt_ref[...] = jnp.broadcast_to(cnt, cnt_ref.shape)


def _tables(idx):
    bsz, _, seq = idx.shape
    return pl.pallas_call(
        _tables_kernel,
        grid=(bsz,),
        in_specs=[pl.BlockSpec((None, TOP_K, seq), lambda b: (b, 0, 0))],
        out_specs=[pl.BlockSpec((None, TOP_K, seq), lambda b: (b, 0, 0)),
                   pl.BlockSpec((N_EXPERTS, LANES), lambda b: (0, 0))],
        out_shape=[jax.ShapeDtypeStruct((bsz, TOP_K, seq), jnp.int32),
                   jax.ShapeDtypeStruct((N_EXPERTS, LANES), F32)],
        scratch_shapes=[pltpu.VMEM((N_EXPERTS, 1), F32)],
        compiler_params=_params("arbitrary"),
        name="moe_tables",
    )(idx)


def _scatter_kernel(dest_ref, pad_ref, src_ref, zero_ref, dst_hbm, sem):
    tm = dest_ref.shape[-1]
    n_pad = pad_ref.shape[-1]
    for k in range(TOP_K):
        def start(r, carry, k=k):
            pltpu.make_async_copy(src_ref.at[r], dst_hbm.at[dest_ref[k, r]], sem).start(priority=k)
            return carry
        lax.fori_loop(0, tm, start, 0, unroll=8)

    def fill(r, carry):
        pltpu.make_async_copy(zero_ref.at[0], dst_hbm.at[pad_ref[0, r]], sem).start()
        return carry
    lax.fori_loop(0, n_pad, fill, 0, unroll=8)

    left = TOP_K * tm + n_pad
    while left > 0:
        cnt = min(tm, left)
        pltpu.make_async_copy(src_ref.at[pl.ds(0, cnt)], dst_hbm.at[pl.ds(0, cnt)], sem).wait()
        left -= cnt


def _scatter(dest, pad_rows, h2t, n_rows):
    bsz, _, seq = dest.shape
    tm = min(ROW_TILE, seq)
    nt = seq // tm
    n_pad = pad_rows.shape[0] // (bsz * nt)
    assert n_pad * bsz * nt == pad_rows.shape[0]
    return pl.pallas_call(
        _scatter_kernel,
        grid=(bsz, nt),
        in_specs=[
            pl.BlockSpec((None, TOP_K, tm), lambda b, i: (b, 0, i), memory_space=pltpu.SMEM),
            pl.BlockSpec((None, 1, n_pad), lambda b, i: (b * nt + i, 0, 0), memory_space=pltpu.SMEM),
            pl.BlockSpec((tm, N_SUB, LANES), lambda b, i: (b * nt + i, 0, 0)),
            pl.BlockSpec((1, N_SUB, LANES), lambda b, i: (0, 0, 0)),
        ],
        out_specs=pl.BlockSpec(memory_space=pl.ANY),
        out_shape=jax.ShapeDtypeStruct((n_rows, N_SUB, LANES), F32),
        scratch_shapes=[pltpu.SemaphoreType.DMA(())],
        compiler_params=_params("arbitrary", "arbitrary"),
        name="moe_scatter",
    )(dest, pad_rows.reshape(bsz * nt, 1, n_pad), h2t.reshape(bsz * seq, N_SUB, LANES),
      jnp.zeros((1, N_SUB, LANES), F32))


def _expert_kernel(blk_exp_ref, blk_first_ref, n_used_ref,
                   xs_ref, wgu_ref, wdn_ref, ys_ref, wgu_bf, wdn_bf):
    i = pl.program_id(0)

    @pl.when(i < n_used_ref[0])
    def _():
        @pl.when(blk_first_ref[i] == 1)
        def _():
            wgu_bf[...] = wgu_ref[...].astype(BF16)
            wdn_bf[...] = wdn_ref[...].astype(BF16)

        x = _from_tiles(xs_ref, 0, MOE_TILE).astype(BF16)
        gu = _dot(x, wgu_bf[...])
        g = gu[:, :D_EXPERT]
        u = gu[:, D_EXPERT:]
        mid = (g * jax.nn.sigmoid(g) * u).astype(BF16)
        _to_tiles(ys_ref, _dot(mid, wdn_bf[...]))

    @pl.when(i >= n_used_ref[0])
    def _():
        ys_ref[...] = jnp.zeros_like(ys_ref)


def _experts(blk_exp, blk_first, n_used, xs, w_gu, w_dn, layer):
    d = D_MODEL
    nb = xs.shape[0] // (MOE_TILE * N_SUB)
    tile = lambda i, *_: (i, 0)
    grid_spec = pltpu.PrefetchScalarGridSpec(
        num_scalar_prefetch=3,
        grid=(nb,),
        in_specs=[
            pl.BlockSpec((MOE_TILE * N_SUB, LANES), tile),
            pl.BlockSpec((None, None, d, 2 * D_EXPERT), lambda i, be, *_: (layer, be[i], 0, 0)),
            pl.BlockSpec((None, None, D_EXPERT, d), lambda i, be, *_: (layer, be[i], 0, 0)),
        ],
        out_specs=pl.BlockSpec((MOE_TILE * N_SUB, LANES), tile),
        scratch_shapes=[pltpu.VMEM((d, 2 * D_EXPERT), BF16), pltpu.VMEM((D_EXPERT, d), BF16)],
    )
    return pl.pallas_call(
        _expert_kernel,
        grid_spec=grid_spec,
        out_shape=jax.ShapeDtypeStruct(xs.shape, F32),
        compiler_params=_params("arbitrary"),
        name="moe_experts",
    )(blk_exp, blk_first, n_used, xs, w_gu, w_dn)


def _combine_kernel(dcur_ref, dnxt_ref, ys_hbm, x_ref, wts_ref, g2_ref, x_out, buf, sem):
    nt = pl.num_programs(1)
    n = pl.program_id(0) * nt + pl.program_id(1)
    total = pl.num_programs(0) * nt
    tm = x_ref.shape[0]
    slot = lax.rem(n, 2)

    def issue(d_ref, sl):
        for k in range(TOP_K):
            def start(r, carry, k=k):
                src = ys_hbm.at[pl.ds(pl.multiple_of(d_ref[k, r], N_SUB), N_SUB)]
                dst = buf.at[sl, pl.ds(pl.multiple_of((k * tm + r) * N_SUB, N_SUB), N_SUB)]
                pltpu.make_async_copy(src, dst, sem.at[sl]).start(priority=k)
                return carry
            lax.fori_loop(0, tm, start, 0, unroll=8)

    @pl.when(n == 0)
    def _():
        issue(dcur_ref, 0)

    @pl.when(n + 1 < total)
    def _():
        issue(dnxt_ref, 1 - slot)

    pltpu.make_async_copy(ys_hbm.at[pl.ds(0, TOP_K * tm * N_SUB)], buf.at[slot], sem.at[slot]).wait()
    cur = buf.at[slot]
    w = wts_ref[...]
    y = w[:, 0:1] * _from_tiles(cur, 0, tm) + w[:, 1:2] * _from_tiles(cur, tm, tm)
    x_out[...] = x_ref[...] + g2_ref[...] * y


def _combine(dest8, ys, x, wts_col, gate2):
    bsz, seq, d = x.shape
    tm = min(ROW_TILE, seq)
    nt = seq // tm

    def nxt(b, i):
        n = jnp.minimum(b * nt + i + 1, bsz * nt - 1)
        return (n // nt, 0, n % nt)

    return pl.pallas_call(
        _combine_kernel,
        grid=(bsz, nt),
        in_specs=[
            pl.BlockSpec((None, TOP_K, tm), lambda b, i: (b, 0, i), memory_space=pltpu.SMEM),
            pl.BlockSpec((None, TOP_K, tm), nxt, memory_space=pltpu.SMEM),
            pl.BlockSpec(memory_space=pl.ANY),
            pl.BlockSpec((None, tm, d), lambda b, i: (b, i, 0)),
            pl.BlockSpec((None, tm, TOP_K), lambda b, i: (b, i, 0)),
            pl.BlockSpec((None, 1, d), lambda b, i: (b, 0, 0)),
        ],
        out_specs=pl.BlockSpec((None, tm, d), lambda b, i: (b, i, 0)),
        out_shape=jax.ShapeDtypeStruct((bsz, seq, d), F32),
        scratch_shapes=[pltpu.VMEM((2, TOP_K * tm * N_SUB, LANES), F32), pltpu.SemaphoreType.DMA((2,))],
        compiler_params=_params("arbitrary", "arbitrary"),
        name="moe_combine",
    )(dest8, dest8, ys, x, wts_col, gate2)


def _lookup(table, keys):
    hit = keys[..., None] == jnp.arange(table.shape[0], dtype=jnp.int32)
    return jnp.sum(jnp.where(hit, table, 0), axis=-1).astype(jnp.int32)


def _count_le(bounds, q):
    return jnp.sum((bounds <= q[..., None]).astype(jnp.int32), axis=-1)


def _moe(h2t, idx, wts, x, gate2, w_gu, w_dn, layer):
    bsz, seq, _ = x.shape
    n_rows = bsz * seq * TOP_K + N_EXPERTS * MOE_TILE
    nb = n_rows // MOE_TILE
    rank, cnt = _tables(idx)
    counts = cnt[:, 0].astype(jnp.int32)
    padded = ((counts + MOE_TILE - 1) // MOE_TILE) * MOE_TILE
    pend = jnp.cumsum(padded)
    pstart = pend - padded
    dest = _lookup(pstart, idx) + rank
    blk_row = jnp.arange(nb, dtype=jnp.int32) * MOE_TILE
    blk_exp = jnp.minimum(_count_le(pend, blk_row), N_EXPERTS - 1)
    blk_first = (blk_row == _lookup(pstart, blk_exp)).astype(jnp.int32)
    n_used = (pend[-1:] // MOE_TILE).astype(jnp.int32)
    slack_len = jnp.concatenate([padded - counts, n_rows - pend[-1:]])
    slack_row = jnp.concatenate([pstart + counts, pend[-1:]])
    slack_end = jnp.cumsum(slack_len)
    q = jnp.arange(N_EXPERTS * MOE_TILE, dtype=jnp.int32)
    seg = _count_le(slack_end, q)
    pad_rows = _lookup(slack_row - (slack_end - slack_len), seg) + q
    xs = _scatter(dest, pad_rows, h2t, n_rows)
    ys = _experts(blk_exp, blk_first, n_used, xs.reshape(n_rows * N_SUB, LANES), w_gu, w_dn, layer)
    return _combine(dest * N_SUB, ys, x, wts.transpose(0, 2, 1), gate2)


def _rnn_in_kernel(x_ref, g_ref, sh_ref, sc_ref, w_ref, gate_out, xb_out):
    h = _modulate(x_ref[...], g_ref[...], sh_ref[...], sc_ref[...])
    u = _dot(h.astype(BF16), w_ref[...])
    gate_out[...] = jax.nn.gelu(u[:, :D_RNN]).astype(BF16)
    xb_out[...] = u[:, D_RNN:]


def _rnn_in(x, gain, shift, scale, w_in):
    bsz, seq, d = x.shape
    tm = min(ROW_TILE, seq)
    row = lambda b, i: (b, i, 0)
    per_b = lambda b, i: (b, 0, 0)
    const = lambda b, i: (0, 0)
    return pl.pallas_call(
        _rnn_in_kernel,
        grid=(bsz, seq // tm),
        in_specs=[
            pl.BlockSpec((None, tm, d), row),
            pl.BlockSpec(gain.shape, const),
            pl.BlockSpec((None, 1, d), per_b),
            pl.BlockSpec((None, 1, d), per_b),
            pl.BlockSpec(w_in.shape, const),
        ],
        out_specs=[pl.BlockSpec((None, tm, D_RNN), row), pl.BlockSpec((None, tm, D_RNN), row)],
        out_shape=[jax.ShapeDtypeStruct((bsz, seq, D_RNN), BF16),
                   jax.ShapeDtypeStruct((bsz, seq, D_RNN), F32)],
        compiler_params=_params("arbitrary", "arbitrary"),
        name="rnn_in",
    )(x, gain, shift, scale, w_in)


def _chunk_scan(a, b, reverse):
    row = lax.broadcasted_iota(jnp.int32, a.shape, 0)
    for dd in (1, 2, 4):
        if reverse:
            keep = row < SUBLANES - dd
            shift = SUBLANES - dd
        else:
            keep = row >= dd
            shift = dd
        a_sh = jnp.where(keep, pltpu.roll(a, shift, 0), 1.0)
        b_sh = jnp.where(keep, pltpu.roll(b, shift, 0), 0.0)
        b = a * b_sh + b
        a = a * a_sh
    return a, b


def _lru_kernel(xb_ref, cw_ref, cb_ref, wcat_ref, bcat_ref, lam_ref, hs_ref,
                xp_ref, af_ref, bf_ref, ab_ref, bb_ref):
    seq, c = xb_ref.shape
    pad = SUBLANES
    xp_ref[0:pad, :] = jnp.zeros((pad, c), F32)
    xp_ref[pad + seq:pad + seq + pad, :] = jnp.zeros((pad, c), F32)
    xp_ref[pad:pad + seq, :] = xb_ref[...]
    cw = cw_ref[...]
    cb = cb_ref[...]
    lam = lam_ref[...]
    neg = -lam
    softplus = jnp.maximum(neg, 0.0) + jnp.log1p(jnp.exp(-jnp.abs(neg)))
    rate = -LRU_C * softplus
    rows = min(SCAN_ROWS, seq)

    for ci in range(seq // rows):
        r0 = ci * rows
        xc = cb
        for k in range(CONV_W):
            xc = xc + xp_ref[r0 + pad - CONV_W // 2 + k:r0 + pad - CONV_W // 2 + k + rows, :] * cw[k:k + 1, :]
        z = _dot(xc.astype(BF16), wcat_ref[...]) + bcat_ref[...]
        t = r0 + lax.broadcasted_iota(jnp.int32, (rows, c), 0)
        for dirn, (a_ref, b_ref, first) in enumerate(((af_ref, bf_ref, 0), (ab_ref, bb_ref, seq - 1))):
            r = jax.nn.sigmoid(z[:, (2 * dirn) * c:(2 * dirn + 1) * c])
            ig = jax.nn.sigmoid(z[:, (2 * dirn + 1) * c:(2 * dirn + 2) * c])
            log_a = rate[dirn:dirn + 1, :] * r
            a = jnp.exp(log_a)
            mult = jnp.where(t == first, 1.0, jnp.sqrt(jnp.tanh(-log_a) * (a * a + 1.0)))
            a_ref[r0:r0 + rows, :] = a
            b_ref[r0:r0 + rows, :] = mult * ig * xc

    n_chunks = seq // SUBLANES

    def fwd(ci, h):
        r0 = pl.multiple_of(ci * SUBLANES, SUBLANES)
        a, b = _chunk_scan(af_ref[pl.ds(r0, SUBLANES), :], bf_ref[pl.ds(r0, SUBLANES), :], False)
        hv = a * h + b
        hs_ref[pl.ds(r0, SUBLANES), :] = hv
        return hv[SUBLANES - 1:SUBLANES, :]
    lax.fori_loop(0, n_chunks, fwd, jnp.zeros((1, c), F32), unroll=4)

    def bwd(ci, h):
        r0 = pl.multiple_of((n_chunks - 1 - ci) * SUBLANES, SUBLANES)
        a, b = _chunk_scan(ab_ref[pl.ds(r0, SUBLANES), :], bb_ref[pl.ds(r0, SUBLANES), :], True)
        hv = a * h + b
        hs_ref[pl.ds(r0, SUBLANES), :] += hv
        return hv[0:1, :]
    lax.fori_loop(0, n_chunks, bwd, jnp.zeros((1, c), F32), unroll=4)


def _lru(xb, conv_w, conv_b, wcat, bcat, lam):
    bsz, seq, _ = xb.shape
    c = RNN_BW
    blk = lambda b, n: (b, 0, n)
    return pl.pallas_call(
        _lru_kernel,
        grid=(bsz, RNN_BLOCKS),
        in_specs=[
            pl.BlockSpec((None, seq, c), blk),
            pl.BlockSpec((CONV_W, c), lambda b, n: (0, n)),
            pl.BlockSpec((1, c), lambda b, n: (0, n)),
            pl.BlockSpec((None, c, 4 * c), lambda b, n: (n, 0, 0)),
            pl.BlockSpec((None, 1, 4 * c), lambda b, n: (n, 0, 0)),
            pl.BlockSpec((2, c), lambda b, n: (0, n)),
        ],
        out_specs=pl.BlockSpec((None, seq, c), blk),
        out_shape=jax.ShapeDtypeStruct((bsz, seq, D_RNN), F32),
        scratch_shapes=[pltpu.VMEM((seq + 2 * SUBLANES, c), F32)] + [pltpu.VMEM((seq, c), F32)] * 4,
        compiler_params=_params("arbitrary", "arbitrary"),
        name="rglru_scan",
    )(xb, conv_w, conv_b, wcat, bcat, lam)


def _pad_cols(w, n):
    return jnp.pad(w, ((0, 0), (0, n - w.shape[1])))


def _mla_weights(w_in, w_q_b, w_kv_b, q_norm, k_norm):
    half = QK_ROPE // 2

    def slab(t):
        return jnp.pad(t, [(0, 0)] * (t.ndim - 1) + [(0, LANES - QK_HEAD)])

    def rot_slab(t):
        rope = t[..., QK_NOPE:]
        swapped = jnp.concatenate([jnp.zeros_like(t[..., :QK_NOPE]), rope[..., half:], rope[..., :half]], axis=-1)
        return slab(swapped)

    kpe = jnp.pad(w_in[:, Q_LORA + KV_LORA:], ((0, 0), (QK_NOPE, 0)))
    w_in_p = jnp.concatenate([w_in[:, :Q_LORA + KV_LORA], slab(kpe), rot_slab(kpe)], axis=1).astype(BF16)
    wq = w_q_b.reshape(Q_LORA, N_HEADS, QK_HEAD)
    wq_p = slab(wq).reshape(Q_LORA, N_HEADS * LANES).astype(BF16)
    wq_rot = rot_slab(wq).reshape(Q_LORA, N_HEADS * LANES).astype(BF16)
    wkv = w_kv_b.reshape(KV_LORA, N_HEADS, QK_NOPE + V_HEAD)
    wk = jnp.pad(wkv[:, :, :QK_NOPE], ((0, 0), (0, 0), (0, LANES - QK_NOPE))).reshape(KV_LORA, N_HEADS * LANES)
    wv = wkv[:, :, QK_NOPE:].reshape(KV_LORA, N_HEADS * V_HEAD)
    w_kv_p = jnp.concatenate([wk, wv], axis=1).astype(BF16)
    gains = lambda g: jnp.stack([slab(g), rot_slab(g)], axis=0)
    return w_in_p, wq_p, wq_rot, w_kv_p, gains(q_norm), gains(k_norm)


def _rope_tables(positions):
    half = QK_ROPE // 2
    inv_freq = ROPE_THETA ** (-jnp.arange(half, dtype=F32) / half)
    ang = positions.astype(F32)[..., None] * inv_freq
    cos, sin = jnp.cos(ang), jnp.sin(ang)
    lead = positions.shape + (QK_NOPE,)
    tail = positions.shape + (LANES - QK_HEAD,)
    cos_t = jnp.concatenate([jnp.ones(lead, F32), cos, cos, jnp.ones(tail, F32)], axis=-1)
    sin_t = jnp.concatenate([jnp.zeros(lead, F32), -sin, sin, jnp.zeros(tail, F32)], axis=-1)
    return cos_t, sin_t


def _router_weights(w_router, router_bias):
    perm = (jnp.arange(N_EXPERTS) % N_GROUPS) * EXPERTS_PER_GROUP + jnp.arange(N_EXPERTS) // N_GROUPS
    w = w_router[:, perm]
    hi = w.astype(BF16)
    lo = (w - hi.astype(F32)).astype(BF16)
    z = jnp.zeros_like(hi)
    wr1 = jnp.concatenate([hi, lo, z, z], axis=1)
    wr2 = jnp.concatenate([z, z, hi, z], axis=1)
    return wr1, wr2, router_bias[perm].reshape(N_EXPERTS, 1).astype(F32)


def kernel(x, c, positions, norm_mix, norm_ffn, w_ada, b_ada, mla_w_in, mla_q_a_norm, mla_kv_a_norm, mla_w_q_b, mla_w_kv_b, mla_q_norm, mla_k_norm, mla_w_o, rnn_w_in, rnn_conv_w, rnn_conv_b, rnn_lam_f, rnn_w_rf, rnn_b_rf, rnn_w_if, rnn_b_if, rnn_lam_b, rnn_w_rb, rnn_b_rb, rnn_w_ib, rnn_b_ib, rnn_w_o, w_router, router_bias, moe_w_gu, moe_w_dn):
    bsz, seq, d = x.shape
    depth = w_ada.shape[0]
    mod = _ada(c, w_ada, b_ada)
    wr1, wr2, rbias = _router_weights(w_router, router_bias)
    cos_t, sin_t = _rope_tables(positions)
    vec = lambda v: v.reshape(1, -1)
    for i in range(depth):
        sh1, sc1, g1, sh2, sc2, g2 = [mod[i, :, k * d:(k + 1) * d].reshape(bsz, 1, d) for k in range(6)]
        j = i // 2
        if i % 2 == 0:
            w_in_p, wq, wq_rot, wkv, qn, kn = _mla_weights(mla_w_in[j], mla_w_q_b[j], mla_w_kv_b[j],
                                                           mla_q_norm[j], mla_k_norm[j])
            q, k, v = _mla_in(x, vec(norm_mix[i]), sh1, sc1, w_in_p, vec(mla_q_a_norm[j]),
                              vec(mla_kv_a_norm[j]), wq, wq_rot, wkv, qn, kn, cos_t, sin_t)
            a = _attention(q, k, v)
            hs = None
            w_o = mla_w_o[j].astype(BF16)
        else:
            a, xb = _rnn_in(x, vec(norm_mix[i]), sh1, sc1, rnn_w_in[j].astype(BF16))
            wcat = jnp.concatenate([rnn_w_rf[j], rnn_w_if[j], rnn_w_rb[j], rnn_w_ib[j]], axis=-1).astype(BF16)
            bcat = jnp.stack([b.reshape(RNN_BLOCKS, RNN_BW) for b in
                              (rnn_b_rf[j], rnn_b_if[j], rnn_b_rb[j], rnn_b_ib[j])], axis=1)
            bcat = bcat.reshape(RNN_BLOCKS, 1, 4 * RNN_BW)
            lam = jnp.stack([rnn_lam_f[j], rnn_lam_b[j]], axis=0)
            hs = _lru(xb, rnn_conv_w[j], vec(rnn_conv_b[j]), wcat, bcat, lam)
            w_o = rnn_w_o[j].astype(BF16)
        x, h2, idx, wts = _mix_out(a, hs, x, w_o, g1, vec(norm_ffn[i]), sh2, sc2, wr1, wr2, rbias)
        x = _moe(h2, idx, wts, x, g2, moe_w_gu, moe_w_dn, i)
    return x
```

```python
import functools

import jax
import jax.numpy as jnp
from jax import lax
from jax.experimental import pallas as pl
from jax.experimental.pallas import tpu as pltpu

F32 = jnp.float32
BF16 = jnp.bfloat16

D_MODEL = 1024
N_HEADS = 16
Q_LORA = 384
KV_LORA = 256
QK_NOPE = 64
QK_ROPE = 32
QK_HEAD = QK_NOPE + QK_ROPE
V_HEAD = 64
ROPE_THETA = 10000.0
D_RNN = D_MODEL
RNN_BLOCKS = 4
RNN_BW = D_RNN // RNN_BLOCKS
CONV_W = 4
LRU_C = 8.0
N_EXPERTS = 32
N_GROUPS = 8
EXPERTS_PER_GROUP = N_EXPERTS // N_GROUPS
TOP_K = 2
D_EXPERT = 512
EPS = 1e-6
LOG2_E = 1.4426950408889634

LANES = 128
SUBLANES = 8
VMEM_LIMIT = 52 * 1024 * 1024

ROW_TILE = 512
Q_TILE = 512
MOE_TILE = 256
SCAN_ROWS = 256


def _dot(a, b):
    return jnp.dot(a, b, preferred_element_type=F32)


def _split_bf16(a):
    hi = a.astype(BF16)
    lo = (a - hi.astype(F32)).astype(BF16)
    return hi, lo


def _dot_split(a, b):
    ah, al = _split_bf16(a)
    bh, bl = _split_bf16(b)
    return _dot(ah, bh) + (_dot(ah, bl) + _dot(al, bh))


def _rms(x, gain, n):
    ms = jnp.sum(x * x, axis=-1, keepdims=True) * (1.0 / n)
    return x * lax.rsqrt(ms + EPS) * gain


def _modulate(x, gain, shift, scale):
    return _rms(x, gain, x.shape[-1]) * (1.0 + scale) + shift


def _params(*sem):
    return pltpu.CompilerParams(dimension_semantics=sem, vmem_limit_bytes=VMEM_LIMIT)


def _ada_kernel(c_ref, w_ref, b_ref, o_ref):
    c = c_ref[...]
    o_ref[...] = _dot_split(c * jax.nn.sigmoid(c), w_ref[...]) + b_ref[...]


def _ada(c, w_ada, b_ada):
    depth, d, n = w_ada.shape
    bsz = c.shape[0]
    tn = 1536
    return pl.pallas_call(
        _ada_kernel,
        grid=(depth, n // tn),
        in_specs=[
            pl.BlockSpec((bsz, d), lambda l, j: (0, 0)),
            pl.BlockSpec((None, d, tn), lambda l, j: (l, 0, j)),
            pl.BlockSpec((None, 1, tn), lambda l, j: (l, 0, j)),
        ],
        out_specs=pl.BlockSpec((None, bsz, tn), lambda l, j: (l, 0, j)),
        out_shape=jax.ShapeDtypeStruct((depth, bsz, n), F32),
        compiler_params=_params("arbitrary", "arbitrary"),
        name="adaln_mod",
    )(c, w_ada, b_ada.reshape(depth, 1, n))


def _head_scale(s):
    return lax.rsqrt(jnp.sum(s * s, axis=-1, keepdims=True) * (1.0 / QK_HEAD) + EPS)


def _mla_in_kernel(x_ref, g_ref, sh_ref, sc_ref, win_ref, qan_ref, kvan_ref, wq_ref, wqr_ref, wkv_ref,
                   qn_ref, kn_ref, cos_ref, sin_ref, q_out, k_out, v_out):
    h = _modulate(x_ref[...], g_ref[...], sh_ref[...], sc_ref[...])
    lat = _dot(h.astype(BF16), win_ref[...])
    q_lat = lat[:, :Q_LORA]
    kv_lat = lat[:, Q_LORA:Q_LORA + KV_LORA]
    kpe = lat[:, Q_LORA + KV_LORA:Q_LORA + KV_LORA + LANES]
    kpe_rot = lat[:, Q_LORA + KV_LORA + LANES:]
    qn = _rms(q_lat, qan_ref[...], Q_LORA).astype(BF16)
    q_all = _dot(qn, wq_ref[...])
    q_rot = _dot(qn, wqr_ref[...])
    kv_all = _dot(_rms(kv_lat, kvan_ref[...], KV_LORA).astype(BF16), wkv_ref[...])
    cos_t = cos_ref[...]
    sin_t = sin_ref[...]
    q_scale = LOG2_E * QK_HEAD ** -0.5
    cq = cos_t * (qn_ref[0:1, :] * q_scale)
    sq = sin_t * (qn_ref[1:2, :] * q_scale)
    ck = cos_t * kn_ref[0:1, :]
    k_rot_term = kpe_rot * (sin_t * kn_ref[1:2, :])
    for hh in range(N_HEADS):
        sl = slice(hh * LANES, (hh + 1) * LANES)
        s = q_all[:, sl]
        q_out[hh] = ((s * cq + q_rot[:, sl] * sq) * _head_scale(s)).astype(BF16)
        s = kv_all[:, sl] + kpe
        k_out[hh] = ((s * ck + k_rot_term) * _head_scale(s)).astype(BF16)
    v_out[...] = kv_all[:, N_HEADS * LANES:].astype(BF16)


def _mla_in(x, gain, shift, scale, w_in, q_a_norm, kv_a_norm, w_q, w_q_rot, w_kv, q_norm, k_norm, cos_t, sin_t):
    bsz, seq, d = x.shape
    tm = min(ROW_TILE, seq)
    row = lambda b, i: (b, i, 0)
    per_b = lambda b, i: (b, 0, 0)
    const = lambda b, i: (0, 0)
    full = lambda a: pl.BlockSpec(a.shape, const)
    return pl.pallas_call(
        _mla_in_kernel,
        grid=(bsz, seq // tm),
        in_specs=[
            pl.BlockSpec((None, tm, d), row),
            full(gain),
            pl.BlockSpec((None, 1, d), per_b),
            pl.BlockSpec((None, 1, d), per_b),
            full(w_in), full(q_a_norm), full(kv_a_norm), full(w_q), full(w_q_rot), full(w_kv),
            full(q_norm), full(k_norm),
            pl.BlockSpec((None, tm, LANES), row),
            pl.BlockSpec((None, tm, LANES), row),
        ],
        out_specs=[
            pl.BlockSpec((None, N_HEADS, tm, LANES), lambda b, i: (b, 0, i, 0)),
            pl.BlockSpec((None, N_HEADS, tm, LANES), lambda b, i: (b, 0, i, 0)),
            pl.BlockSpec((None, tm, N_HEADS * V_HEAD), row),
        ],
        out_shape=[
            jax.ShapeDtypeStruct((bsz, N_HEADS, seq, LANES), BF16),
            jax.ShapeDtypeStruct((bsz, N_HEADS, seq, LANES), BF16),
            jax.ShapeDtypeStruct((bsz, seq, N_HEADS * V_HEAD), BF16),
        ],
        compiler_params=_params("arbitrary", "arbitrary"),
        name="mla_in",
    )(x, gain, shift, scale, w_in, q_a_norm, kv_a_norm, w_q, w_q_rot, w_kv, q_norm, k_norm, cos_t, sin_t)


def _attn_kernel(q_ref, k_ref, v_ref, o_ref):
    v = v_ref[...]
    lane_v = lax.broadcasted_iota(jnp.int32, v.shape, 1)
    outs = []
    for j in range(2):
        mine = (lane_v < V_HEAD) if j == 0 else (lane_v >= V_HEAD)
        vj = jnp.where(mine, v, jnp.ones((), BF16))
        s = lax.dot_general(q_ref[j], k_ref[j], (((1,), (1,)), ((), ())),
                            preferred_element_type=F32)
        m = jnp.max(s, axis=-1, keepdims=True)
        o = _dot(jnp.exp2(s - m).astype(BF16), vj)
        denom = o[:, V_HEAD:V_HEAD + 1] if j == 0 else o[:, 0:1]
        outs.append(o / denom)
    lane = lax.broadcasted_iota(jnp.int32, outs[0].shape, 1)
    o_ref[...] = jnp.where(lane < V_HEAD, outs[0], outs[1]).astype(BF16)


def _attention(q, k, v):
    bsz, _, seq, _ = q.shape
    tq = min(Q_TILE, seq)
    return pl.pallas_call(
        _attn_kernel,
        grid=(bsz, N_HEADS // 2, seq // tq),
        in_specs=[
            pl.BlockSpec((None, 2, tq, LANES), lambda b, h, i: (b, h, i, 0)),
            pl.BlockSpec((None, 2, seq, LANES), lambda b, h, i: (b, h, 0, 0)),
            pl.BlockSpec((None, seq, LANES), lambda b, h, i: (b, 0, h)),
        ],
        out_specs=pl.BlockSpec((None, tq, LANES), lambda b, h, i: (b, i, h)),
        out_shape=jax.ShapeDtypeStruct((bsz, seq, N_HEADS * V_HEAD), BF16),
        compiler_params=_params("arbitrary", "arbitrary", "arbitrary"),
        name="mla_attention",
    )(q, k, v)


def _first_index_of_max(vals):
    m = vals[0]
    for v in vals[1:]:
        m = jnp.maximum(m, v)
    idx = jnp.full(m.shape, float(len(vals) - 1), F32)
    for j in range(len(vals) - 2, -1, -1):
        idx = jnp.where(vals[j] == m, float(j), idx)
    return m, idx


def _route(h2, wr1_ref, wr2_ref, rb_ref):
    hh, hl = _split_bf16(h2)
    logits = (_dot(hh, wr1_ref[...]) + _dot(hl, wr2_ref[...])).T
    logit = logits[0:N_EXPERTS] + logits[N_EXPERTS:2 * N_EXPERTS] + logits[2 * N_EXPERTS:3 * N_EXPERTS]
    score = jax.nn.sigmoid(logit)
    biased = score + rb_ref[...]
    a = [biased[j * N_GROUPS:(j + 1) * N_GROUPS] for j in range(EXPERTS_PER_GROUP)]
    sc = [score[j * N_GROUPS:(j + 1) * N_GROUPS] for j in range(EXPERTS_PER_GROUP)]
    hi1, lo1 = jnp.maximum(a[0], a[1]), jnp.minimum(a[0], a[1])
    hi2, lo2 = jnp.maximum(a[2], a[3]), jnp.minimum(a[2], a[3])
    gscore = jnp.maximum(hi1, hi2) + jnp.maximum(jnp.minimum(hi1, hi2), jnp.maximum(lo1, lo2))
    gmax = jnp.max(gscore, axis=0, keepdims=True)
    giota = lax.broadcasted_iota(jnp.int32, gscore.shape, 0).astype(F32)
    gsel = jnp.min(jnp.where(gscore == gmax, giota, float(N_GROUPS)), axis=0, keepdims=True)
    onehot = giota == gsel
    pick = lambda t: jnp.sum(jnp.where(onehot, t, 0.0), axis=0, keepdims=True)
    bj = [pick(t) for t in a]
    sj = [pick(t) for t in sc]
    _, i1 = _first_index_of_max(bj)
    bj2 = [jnp.where(i1 == float(j), -jnp.inf, bj[j]) for j in range(EXPERTS_PER_GROUP)]
    _, i2 = _first_index_of_max(bj2)
    sel = lambda i: jnp.where(i == 0.0, sj[0], jnp.where(i == 1.0, sj[1], jnp.where(i == 2.0, sj[2], sj[3])))
    w1, w2 = sel(i1), sel(i2)
    den = w1 + w2
    base = gsel * float(EXPERTS_PER_GROUP)
    return ((base + i1).astype(jnp.int32), (base + i2).astype(jnp.int32)), (w1 / den, w2 / den)


def _mix_out_kernel(has_gate, *refs):
    if has_gate:
        a_ref, hs_ref, x_ref, wo_ref, g1_ref, g_ref, sh_ref, sc_ref, wr1_ref, wr2_ref, rb_ref, \
            x_out, h_out, idx_out, wts_out = refs
        a = (a_ref[...].astype(F32) * hs_ref[...]).astype(BF16)
    else:
        a_ref, x_ref, wo_ref, g1_ref, g_ref, sh_ref, sc_ref, wr1_ref, wr2_ref, rb_ref, \
            x_out, h_out, idx_out, wts_out = refs
        a = a_ref[...]
    x1 = x_ref[...] + g1_ref[...] * _dot(a, wo_ref[...])
    x_out[...] = x1
    h2 = _modulate(x1, g_ref[...], sh_ref[...], sc_ref[...])
    _to_tiles(h_out, h2)
    idx, wts = _route(h2, wr1_ref, wr2_ref, rb_ref)
    for k in range(TOP_K):
        idx_out[k:k + 1, :] = idx[k]
        wts_out[k:k + 1, :] = wts[k]


def _mix_out(a, hs, x, w_o, gate1, gain, shift, scale, wr1, wr2, rbias):
    bsz, seq, d = x.shape
    tm = min(ROW_TILE, seq)
    row = lambda b, i: (b, i, 0)
    per_b = lambda b, i: (b, 0, 0)
    const = lambda b, i: (0, 0)
    full = lambda t: pl.BlockSpec(t.shape, const)
    vec = pl.BlockSpec((None, 1, d), per_b)
    acts = [a] if hs is None else [a, hs]
    return pl.pallas_call(
        functools.partial(_mix_out_kernel, hs is not None),
        grid=(bsz, seq // tm),
        in_specs=[pl.BlockSpec((None, tm, t.shape[-1]), row) for t in acts] + [
            pl.BlockSpec((None, tm, d), row), full(w_o), vec, full(gain), vec, vec,
            full(wr1), full(wr2), full(rbias),
        ],
        out_specs=[
            pl.BlockSpec((None, tm, d), row),
            pl.BlockSpec((None, tm * (d // LANES), LANES), row),
            pl.BlockSpec((None, TOP_K, tm), lambda b, i: (b, 0, i)),
            pl.BlockSpec((None, TOP_K, tm), lambda b, i: (b, 0, i)),
        ],
        out_shape=[
            jax.ShapeDtypeStruct((bsz, seq, d), F32),
            jax.ShapeDtypeStruct((bsz, seq * (d // LANES), LANES), F32),
            jax.ShapeDtypeStruct((bsz, TOP_K, seq), jnp.int32),
            jax.ShapeDtypeStruct((bsz, TOP_K, seq), F32),
        ],
        compiler_params=_params("arbitrary", "arbitrary"),
        name="mix_out_route",
    )(*acts, x, w_o, gate1, gain, shift, scale, wr1, wr2, rbias)


N_SUB = D_MODEL // LANES
TABLE_CHUNK = 512


def _to_tiles(ref, val):
    n = val.shape[0]
    for s in range(N_SUB):
        ref[pl.ds(s, n, stride=N_SUB), :] = val[:, s * LANES:(s + 1) * LANES]


def _from_tiles(ref, lo, n):
    return jnp.concatenate([ref[pl.ds(lo * N_SUB + s, n, stride=N_SUB), :] for s in range(N_SUB)], axis=1)


def _tables_kernel(idx_ref, rank_ref, cnt_ref, carry):
    @pl.when(pl.program_id(0) == 0)
    def _():
        carry[...] = jnp.zeros_like(carry)

    seq = idx_ref.shape[-1]
    ch = min(TABLE_CHUNK, seq)
    tri = jnp.where(lax.broadcasted_iota(jnp.int32, (ch, ch), 0) <= lax.broadcasted_iota(jnp.int32, (ch, ch), 1),
                    1.0, 0.0).astype(BF16)
    eiota = lax.broadcasted_iota(jnp.int32, (N_EXPERTS, ch), 0)
    cnt = carry[...]
    for k in range(TOP_K):
        for c in range(seq // ch):
            sel = eiota == idx_ref[k:k + 1, c * ch:(c + 1) * ch]
            pref = _dot(jnp.where(sel, 1.0, 0.0).astype(BF16), tri) + cnt
            rank = jnp.sum(jnp.where(sel, pref, 0.0), axis=0, keepdims=True) - 1.0
            rank_ref[k:k + 1, c * ch:(c + 1) * ch] = rank.astype(jnp.int32)
            cnt = pref[:, ch - 1:ch]
    carry[...] = cnt
    cnt_ref[...] = jnp.broadcast_to(cnt, cnt_ref.shape)


def _tables(idx):
    bsz, _, seq = idx.shape
    return pl.pallas_call(
        _tables_kernel,
        grid=(bsz,),
        in_specs=[pl.BlockSpec((None, TOP_K, seq), lambda b: (b, 0, 0))],
        out_specs=[pl.BlockSpec((None, TOP_K, seq), lambda b: (b, 0, 0)),
                   pl.BlockSpec((N_EXPERTS, LANES), lambda b: (0, 0))],
        out_shape=[jax.ShapeDtypeStruct((bsz, TOP_K, seq), jnp.int32),
                   jax.ShapeDtypeStruct((N_EXPERTS, LANES), F32)],
        scratch_shapes=[pltpu.VMEM((N_EXPERTS, 1), F32)],
        compiler_params=_params("arbitrary"),
        name="moe_tables",
    )(idx)


def _scatter_kernel(dest_ref, pad_ref, src_ref, zero_ref, dst_hbm, sem):
    tm = dest_ref.shape[-1]
    n_pad = pad_ref.shape[-1]
    for k in range(TOP_K):
        def start(r, carry, k=k):
            pltpu.make_async_copy(src_ref.at[r], dst_hbm.at[dest_ref[k, r]], sem).start(priority=k)
            return carry
        lax.fori_loop(0, tm, start, 0, unroll=8)

    def fill(r, carry):
        pltpu.make_async_copy(zero_ref.at[0], dst_hbm.at[pad_ref[0, r]], sem).start()
        return carry
    lax.fori_loop(0, n_pad, fill, 0, unroll=8)

    left = TOP_K * tm + n_pad
    while left > 0:
        cnt = min(tm, left)
        pltpu.make_async_copy(src_ref.at[pl.ds(0, cnt)], dst_hbm.at[pl.ds(0, cnt)], sem).wait()
        left -= cnt


def _scatter(dest, pad_rows, h2t, n_rows):
    bsz, _, seq = dest.shape
    tm = min(ROW_TILE, seq)
    nt = seq // tm
    n_pad = pad_rows.shape[0] // (bsz * nt)
    assert n_pad * bsz * nt == pad_rows.shape[0]
    return pl.pallas_call(
        _scatter_kernel,
        grid=(bsz, nt),
        in_specs=[
            pl.BlockSpec((None, TOP_K, tm), lambda b, i: (b, 0, i), memory_space=pltpu.SMEM),
            pl.BlockSpec((None, 1, n_pad), lambda b, i: (b * nt + i, 0, 0), memory_space=pltpu.SMEM),
            pl.BlockSpec((tm, N_SUB, LANES), lambda b, i: (b * nt + i, 0, 0)),
            pl.BlockSpec((1, N_SUB, LANES), lambda b, i: (0, 0, 0)),
        ],
        out_specs=pl.BlockSpec(memory_space=pl.ANY),
        out_shape=jax.ShapeDtypeStruct((n_rows, N_SUB, LANES), F32),
        scratch_shapes=[pltpu.SemaphoreType.DMA(())],
        compiler_params=_params("arbitrary", "arbitrary"),
        name="moe_scatter",
    )(dest, pad_rows.reshape(bsz * nt, 1, n_pad), h2t.reshape(bsz * seq, N_SUB, LANES),
      jnp.zeros((1, N_SUB, LANES), F32))


def _expert_kernel(blk_exp_ref, blk_first_ref, n_used_ref,
                   xs_ref, wgu_ref, wdn_ref, ys_ref, wgu_bf, wdn_bf):
    i = pl.program_id(0)

    @pl.when(i < n_used_ref[0])
    def _():
        @pl.when(blk_first_ref[i] == 1)
        def _():
            wgu_bf[...] = wgu_ref[...].astype(BF16)
            wdn_bf[...] = wdn_ref[...].astype(BF16)

        x = _from_tiles(xs_ref, 0, MOE_TILE).astype(BF16)
        gu = _dot(x, wgu_bf[...])
        g = gu[:, :D_EXPERT]
        u = gu[:, D_EXPERT:]
        mid = (g * jax.nn.sigmoid(g) * u).astype(BF16)
        _to_tiles(ys_ref, _dot(mid, wdn_bf[...]))

    @pl.when(i >= n_used_ref[0])
    def _():
        ys_ref[...] = jnp.zeros_like(ys_ref)


def _experts(blk_exp, blk_first, n_used, xs, w_gu, w_dn, layer):
    d = D_MODEL
    nb = xs.shape[0] // (MOE_TILE * N_SUB)
    tile = lambda i, *_: (i, 0)
    grid_spec = pltpu.PrefetchScalarGridSpec(
        num_scalar_prefetch=3,
        grid=(nb,),
        in_specs=[
            pl.BlockSpec((MOE_TILE * N_SUB, LANES), tile),
            pl.BlockSpec((None, None, d, 2 * D_EXPERT), lambda i, be, *_: (layer, be[i], 0, 0)),
            pl.BlockSpec((None, None, D_EXPERT, d), lambda i, be, *_: (layer, be[i], 0, 0)),
        ],
        out_specs=pl.BlockSpec((MOE_TILE * N_SUB, LANES), tile),
        scratch_shapes=[pltpu.VMEM((d, 2 * D_EXPERT), BF16), pltpu.VMEM((D_EXPERT, d), BF16)],
    )
    return pl.pallas_call(
        _expert_kernel,
        grid_spec=grid_spec,
        out_shape=jax.ShapeDtypeStruct(xs.shape, F32),
        compiler_params=_params("arbitrary"),
        name="moe_experts",
    )(blk_exp, blk_first, n_used, xs, w_gu, w_dn)


def _combine_kernel(dcur_ref, dnxt_ref, ys_hbm, x_ref, wts_ref, g2_ref, x_out, buf, sem):
    nt = pl.num_programs(1)
    n = pl.program_id(0) * nt + pl.program_id(1)
    total = pl.num_programs(0) * nt
    tm = x_ref.shape[0]
    slot = lax.rem(n, 2)

    def issue(d_ref, sl):
        for k in range(TOP_K):
            def start(r, carry, k=k):
                src = ys_hbm.at[pl.ds(pl.multiple_of(d_ref[k, r], N_SUB), N_SUB)]
                dst = buf.at[sl, pl.ds(pl.multiple_of((k * tm + r) * N_SUB, N_SUB), N_SUB)]
                pltpu.make_async_copy(src, dst, sem.at[sl]).start(priority=k)
                return carry
            lax.fori_loop(0, tm, start, 0, unroll=8)

    @pl.when(n == 0)
    def _():
        issue(dcur_ref, 0)

    @pl.when(n + 1 < total)
    def _():
        issue(dnxt_ref, 1 - slot)

    pltpu.make_async_copy(ys_hbm.at[pl.ds(0, TOP_K * tm * N_SUB)], buf.at[slot], sem.at[slot]).wait()
    cur = buf.at[slot]
    w = wts_ref[...]
    y = w[:, 0:1] * _from_tiles(cur, 0, tm) + w[:, 1:2] * _from_tiles(cur, tm, tm)
    x_out[...] = x_ref[...] + g2_ref[...] * y


def _combine(dest8, ys, x, wts_col, gate2):
    bsz, seq, d = x.shape
    tm = min(ROW_TILE, seq)
    nt = seq // tm

    def nxt(b, i):
        n = jnp.minimum(b * nt + i + 1, bsz * nt - 1)
        return (n // nt, 0, n % nt)

    return pl.pallas_call(
        _combine_kernel,
        grid=(bsz, nt),
        in_specs=[
            pl.BlockSpec((None, TOP_K, tm), lambda b, i: (b, 0, i), memory_space=pltpu.SMEM),
            pl.BlockSpec((None, TOP_K, tm), nxt, memory_space=pltpu.SMEM),
            pl.BlockSpec(memory_space=pl.ANY),
            pl.BlockSpec((None, tm, d), lambda b, i: (b, i, 0)),
            pl.BlockSpec((None, tm, TOP_K), lambda b, i: (b, i, 0)),
            pl.BlockSpec((None, 1, d), lambda b, i: (b, 0, 0)),
        ],
        out_specs=pl.BlockSpec((None, tm, d), lambda b, i: (b, i, 0)),
        out_shape=jax.ShapeDtypeStruct((bsz, seq, d), F32),
        scratch_shapes=[pltpu.VMEM((2, TOP_K * tm * N_SUB, LANES), F32), pltpu.SemaphoreType.DMA((2,))],
        compiler_params=_params("arbitrary", "arbitrary"),
        name="moe_combine",
    )(dest8, dest8, ys, x, wts_col, gate2)


def _lookup(table, keys):
    hit = keys[..., None] == jnp.arange(table.shape[0], dtype=jnp.int32)
    return jnp.sum(jnp.where(hit, table, 0), axis=-1).astype(jnp.int32)


def _count_le(bounds, q):
    return jnp.sum((bounds <= q[..., None]).astype(jnp.int32), axis=-1)


def _moe(h2t, idx, wts, x, gate2, w_gu, w_dn, layer):
    bsz, seq, _ = x.shape
    n_rows = bsz * seq * TOP_K + N_EXPERTS * MOE_TILE
    nb = n_rows // MOE_TILE
    rank, cnt = _tables(idx)
    counts = cnt[:, 0].astype(jnp.int32)
    padded = ((counts + MOE_TILE - 1) // MOE_TILE) * MOE_TILE
    pend = jnp.cumsum(padded)
    pstart = pend - padded
    dest = _lookup(pstart, idx) + rank
    blk_row = jnp.arange(nb, dtype=jnp.int32) * MOE_TILE
    blk_exp = jnp.minimum(_count_le(pend, blk_row), N_EXPERTS - 1)
    blk_first = (blk_row == _lookup(pstart, blk_exp)).astype(jnp.int32)
    n_used = (pend[-1:] // MOE_TILE).astype(jnp.int32)
    slack_len = jnp.concatenate([padded - counts, n_rows - pend[-1:]])
    slack_row = jnp.concatenate([pstart + counts, pend[-1:]])
    slack_end = jnp.cumsum(slack_len)
    q = jnp.arange(N_EXPERTS * MOE_TILE, dtype=jnp.int32)
    seg = _count_le(slack_end, q)
    pad_rows = _lookup(slack_row - (slack_end - slack_len), seg) + q
    xs = _scatter(dest, pad_rows, h2t, n_rows)
    ys = _experts(blk_exp, blk_first, n_used, xs.reshape(n_rows * N_SUB, LANES), w_gu, w_dn, layer)
    return _combine(dest * N_SUB, ys, x, wts.transpose(0, 2, 1), gate2)


def _rnn_in_kernel(x_ref, g_ref, sh_ref, sc_ref, w_ref, gate_out, xb_out):
    h = _modulate(x_ref[...], g_ref[...], sh_ref[...], sc_ref[...])
    u = _dot(h.astype(BF16), w_ref[...])
    gate_out[...] = jax.nn.gelu(u[:, :D_RNN]).astype(BF16)
    xb_out[...] = u[:, D_RNN:]


def _rnn_in(x, gain, shift, scale, w_in):
    bsz, seq, d = x.shape
    tm = min(ROW_TILE, seq)
    row = lambda b, i: (b, i, 0)
    per_b = lambda b, i: (b, 0, 0)
    const = lambda b, i: (0, 0)
    return pl.pallas_call(
        _rnn_in_kernel,
        grid=(bsz, seq // tm),
        in_specs=[
            pl.BlockSpec((None, tm, d), row),
            pl.BlockSpec(gain.shape, const),
            pl.BlockSpec((None, 1, d), per_b),
            pl.BlockSpec((None, 1, d), per_b),
            pl.BlockSpec(w_in.shape, const),
        ],
        out_specs=[pl.BlockSpec((None, tm, D_RNN), row), pl.BlockSpec((None, tm, D_RNN), row)],
        out_shape=[jax.ShapeDtypeStruct((bsz, seq, D_RNN), BF16),
                   jax.ShapeDtypeStruct((bsz, seq, D_RNN), F32)],
        compiler_params=_params("arbitrary", "arbitrary"),
        name="rnn_in",
    )(x, gain, shift, scale, w_in)


def _lru_kernel(xb_ref, cw_ref, cb_ref, wcat_ref, bcat_ref, lam_ref, hs_ref,
                xi_ref, af_ref, bf_ref, ab_ref, bb_ref, hf_ref, hb_ref, sum_ref):
    seq, c = xb_ref.shape
    seg_len = seq // SUBLANES
    n_slab = c // LANES
    n_rows = seg_len * SUBLANES
    halo = (CONV_W // 2) * SUBLANES
    row = lax.broadcasted_iota(jnp.int32, (SUBLANES, LANES), 0)
    for sl in range(n_slab):
        lanes = slice(sl * LANES, (sl + 1) * LANES)
        for g in range(SUBLANES):
            xi_ref[sl, pl.ds(halo + g, seg_len, stride=SUBLANES), :] = xb_ref[g * seg_len:(g + 1) * seg_len, lanes]
        for back in (1, 2):
            prev = xi_ref[sl, halo + (seg_len - back) * SUBLANES:halo + (seg_len - back + 1) * SUBLANES, :]
            xi_ref[sl, halo - back * SUBLANES:halo - (back - 1) * SUBLANES, :] = jnp.where(
                row == 0, 0.0, pltpu.roll(prev, 1, 0))
        nxt = xi_ref[sl, halo:halo + SUBLANES, :]
        xi_ref[sl, halo + n_rows:halo + n_rows + SUBLANES, :] = jnp.where(
            row == SUBLANES - 1, 0.0, pltpu.roll(nxt, SUBLANES - 1, 0))

    cw = cw_ref[...]
    cb = cb_ref[...]
    lam = lam_ref[...]
    neg = -lam
    softplus = jnp.maximum(neg, 0.0) + jnp.log1p(jnp.exp(-jnp.abs(neg)))
    half_rate = (-0.5 * LRU_C) * softplus
    rows = min(SCAN_ROWS, n_rows)
    n_chunks = n_rows // rows

    for ci in range(n_chunks):
        i0 = ci * rows
        taps = []
        for k in range(CONV_W):
            lo = halo + i0 + (k - CONV_W // 2) * SUBLANES
            taps.append(jnp.concatenate([xi_ref[sl, lo:lo + rows, :] for sl in range(n_slab)], axis=1))
        xc = cb
        for k in range(CONV_W):
            xc = xc + taps[k] * cw[k:k + 1, :]
        xcb = xc.astype(BF16)
        xh = 0.5 * xc
        for dirn, (a_ref, b_ref) in enumerate(((af_ref, bf_ref), (ab_ref, bb_ref))):
            cols = slice(2 * dirn * c, 2 * (dirn + 1) * c)
            th = jnp.tanh(_dot(xcb, wcat_ref[:, cols]) + bcat_ref[:, cols])
            hr = half_rate[dirn:dirn + 1, :]
            log_a = hr * th[:, :c] + hr
            a = jnp.exp(log_a)
            mult = jnp.sqrt(jnp.tanh(-log_a) * (a * a + 1.0))
            if dirn == 0 and ci == 0:
                mult = jnp.where(lax.broadcasted_iota(jnp.int32, mult.shape, 0) == 0, 1.0, mult)
            if dirn == 1 and ci == n_chunks - 1:
                mult = jnp.where(lax.broadcasted_iota(jnp.int32, mult.shape, 0) == rows - 1, 1.0, mult)
            b = mult * (th[:, c:] + 1.0) * xh
            for sl in range(n_slab):
                a_ref[sl, i0:i0 + rows, :] = a[:, sl * LANES:(sl + 1) * LANES]
                b_ref[sl, i0:i0 + rows, :] = b[:, sl * LANES:(sl + 1) * LANES]

    def step_rows(cidx):
        fwd = pl.ds(pl.multiple_of(cidx * SUBLANES, SUBLANES), SUBLANES)
        bwd = pl.ds(pl.multiple_of((seg_len - 1 - cidx) * SUBLANES, SUBLANES), SUBLANES)
        return fwd, bwd

    zero = jnp.zeros((SUBLANES, LANES), F32)
    one = jnp.ones((SUBLANES, LANES), F32)

    def totals(cidx, carry):
        fwd, bwd = step_rows(cidx)
        out = []
        for sl in range(n_slab):
            hf, pf, hb, pb = carry[sl]
            af, ab = af_ref[sl, fwd, :], ab_ref[sl, bwd, :]
            out.append((af * hf + bf_ref[sl, fwd, :], af * pf, ab * hb + bb_ref[sl, bwd, :], ab * pb))
        return tuple(out)
    tot = lax.fori_loop(0, seg_len, totals, tuple((zero, one, zero, one) for _ in range(n_slab)), unroll=8)

    enter = []
    for sl in range(n_slab):
        hf, pf, hb, pb = tot[sl]
        cf, cbk = zero, zero
        for _ in range(SUBLANES - 1):
            cf = jnp.where(row == 0, 0.0, pltpu.roll(hf + pf * cf, 1, 0))
            cbk = jnp.where(row == SUBLANES - 1, 0.0, pltpu.roll(hb + pb * cbk, SUBLANES - 1, 0))
        enter.append((cf, cbk))

    def states(meet, cidx, carry):
        fwd, bwd = step_rows(cidx)
        out = []
        for sl in range(n_slab):
            hf, hb = carry[sl]
            hf = af_ref[sl, fwd, :] * hf + bf_ref[sl, fwd, :]
            hb = ab_ref[sl, bwd, :] * hb + bb_ref[sl, bwd, :]
            if meet:
                sum_ref[sl, fwd, :] = hf + hb_ref[sl, fwd, :]
                sum_ref[sl, bwd, :] = hb + hf_ref[sl, bwd, :]
            else:
                hf_ref[sl, fwd, :] = hf
                hb_ref[sl, bwd, :] = hb
            out.append((hf, hb))
        return tuple(out)
    mid = lax.fori_loop(0, seg_len // 2, functools.partial(states, False), tuple(enter), unroll=8)
    lax.fori_loop(seg_len // 2, seg_len, functools.partial(states, True), mid, unroll=8)

    for g in range(SUBLANES):
        for sl in range(n_slab):
            hs_ref[g * seg_len:(g + 1) * seg_len, sl * LANES:(sl + 1) * LANES] = (
                sum_ref[sl, pl.ds(g, seg_len, stride=SUBLANES), :])


def _lru(xb, conv_w, conv_b, wcat, bcat, lam):
    bsz, seq, _ = xb.shape
    c = RNN_BW
    blk = lambda b, n: (b, 0, n)
    return pl.pallas_call(
        _lru_kernel,
        grid=(bsz, RNN_BLOCKS),
        in_specs=[
            pl.BlockSpec((None, seq, c), blk),
            pl.BlockSpec((CONV_W, c), lambda b, n: (0, n)),
            pl.BlockSpec((1, c), lambda b, n: (0, n)),
            pl.BlockSpec((None, c, 4 * c), lambda b, n: (n, 0, 0)),
            pl.BlockSpec((None, 1, 4 * c), lambda b, n: (n, 0, 0)),
            pl.BlockSpec((2, c), lambda b, n: (0, n)),
        ],
        out_specs=pl.BlockSpec((None, seq, c), blk),
        out_shape=jax.ShapeDtypeStruct((bsz, seq, D_RNN), F32),
        scratch_shapes=[pltpu.VMEM((c // LANES, seq + (CONV_W - 1) * SUBLANES, LANES), F32)]
        + [pltpu.VMEM((c // LANES, seq, LANES), F32)] * 7,
        compiler_params=_params("arbitrary", "arbitrary"),
        name="rglru_scan",
    )(xb, conv_w, conv_b, wcat, bcat, lam)


def _mla_weights(w_in, w_q_b, w_kv_b, q_norm, k_norm):
    half = QK_ROPE // 2

    def slab(t):
        return jnp.pad(t, [(0, 0)] * (t.ndim - 1) + [(0, LANES - QK_HEAD)])

    def rot_slab(t):
        rope = t[..., QK_NOPE:]
        swapped = jnp.concatenate([jnp.zeros_like(t[..., :QK_NOPE]), rope[..., half:], rope[..., :half]], axis=-1)
        return slab(swapped)

    kpe = jnp.pad(w_in[:, Q_LORA + KV_LORA:], ((0, 0), (QK_NOPE, 0)))
    w_in_p = jnp.concatenate([w_in[:, :Q_LORA + KV_LORA], slab(kpe), rot_slab(kpe)], axis=1).astype(BF16)
    wq = w_q_b.reshape(Q_LORA, N_HEADS, QK_HEAD)
    wq_p = slab(wq).reshape(Q_LORA, N_HEADS * LANES).astype(BF16)
    wq_rot = rot_slab(wq).reshape(Q_LORA, N_HEADS * LANES).astype(BF16)
    wkv = w_kv_b.reshape(KV_LORA, N_HEADS, QK_NOPE + V_HEAD)
    wk = jnp.pad(wkv[:, :, :QK_NOPE], ((0, 0), (0, 0), (0, LANES - QK_NOPE))).reshape(KV_LORA, N_HEADS * LANES)
    wv = wkv[:, :, QK_NOPE:].reshape(KV_LORA, N_HEADS * V_HEAD)
    w_kv_p = jnp.concatenate([wk, wv], axis=1).astype(BF16)
    gains = lambda g: jnp.stack([slab(g), rot_slab(g)], axis=0)
    return w_in_p, wq_p, wq_rot, w_kv_p, gains(q_norm), gains(k_norm)


def _rope_tables(positions):
    half = QK_ROPE // 2
    inv_freq = ROPE_THETA ** (-jnp.arange(half, dtype=F32) / half)
    ang = positions.astype(F32)[..., None] * inv_freq
    cos, sin = jnp.cos(ang), jnp.sin(ang)
    lead = positions.shape + (QK_NOPE,)
    tail = positions.shape + (LANES - QK_HEAD,)
    cos_t = jnp.concatenate([jnp.ones(lead, F32), cos, cos, jnp.ones(tail, F32)], axis=-1)
    sin_t = jnp.concatenate([jnp.zeros(lead, F32), -sin, sin, jnp.zeros(tail, F32)], axis=-1)
    return cos_t, sin_t


def _router_weights(w_router, router_bias):
    perm = (jnp.arange(N_EXPERTS) % N_GROUPS) * EXPERTS_PER_GROUP + jnp.arange(N_EXPERTS) // N_GROUPS
    w = w_router[:, perm]
    hi = w.astype(BF16)
    lo = (w - hi.astype(F32)).astype(BF16)
    z = jnp.zeros_like(hi)
    wr1 = jnp.concatenate([hi, lo, z, z], axis=1)
    wr2 = jnp.concatenate([z, z, hi, z], axis=1)
    return wr1, wr2, router_bias[perm].reshape(N_EXPERTS, 1).astype(F32)


def kernel(x, c, positions, norm_mix, norm_ffn, w_ada, b_ada, mla_w_in, mla_q_a_norm, mla_kv_a_norm, mla_w_q_b, mla_w_kv_b, mla_q_norm, mla_k_norm, mla_w_o, rnn_w_in, rnn_conv_w, rnn_conv_b, rnn_lam_f, rnn_w_rf, rnn_b_rf, rnn_w_if, rnn_b_if, rnn_lam_b, rnn_w_rb, rnn_b_rb, rnn_w_ib, rnn_b_ib, rnn_w_o, w_router, router_bias, moe_w_gu, moe_w_dn):
    bsz, seq, d = x.shape
    depth = w_ada.shape[0]
    mod = _ada(c, w_ada, b_ada)
    wr1, wr2, rbias = _router_weights(w_router, router_bias)
    cos_t, sin_t = _rope_tables(positions)
    vec = lambda v: v.reshape(1, -1)
    for i in range(depth):
        sh1, sc1, g1, sh2, sc2, g2 = [mod[i, :, k * d:(k + 1) * d].reshape(bsz, 1, d) for k in range(6)]
        j = i // 2
        if i % 2 == 0:
            w_in_p, wq, wq_rot, wkv, qn, kn = _mla_weights(mla_w_in[j], mla_w_q_b[j], mla_w_kv_b[j],
                                                           mla_q_norm[j], mla_k_norm[j])
            q, k, v = _mla_in(x, vec(norm_mix[i]), sh1, sc1, w_in_p, vec(mla_q_a_norm[j]),
                              vec(mla_kv_a_norm[j]), wq, wq_rot, wkv, qn, kn, cos_t, sin_t)
            a = _attention(q, k, v)
            hs = None
            w_o = mla_w_o[j].astype(BF16)
        else:
            a, xb = _rnn_in(x, vec(norm_mix[i]), sh1, sc1, rnn_w_in[j].astype(BF16))
            wcat = (0.5 * jnp.concatenate([rnn_w_rf[j], rnn_w_if[j], rnn_w_rb[j], rnn_w_ib[j]], axis=-1)).astype(BF16)
            bcat = jnp.stack([b.reshape(RNN_BLOCKS, RNN_BW) for b in
                              (rnn_b_rf[j], rnn_b_if[j], rnn_b_rb[j], rnn_b_ib[j])], axis=1)
            bcat = 0.5 * bcat.reshape(RNN_BLOCKS, 1, 4 * RNN_BW)
            lam = jnp.stack([rnn_lam_f[j], rnn_lam_b[j]], axis=0)
            hs = _lru(xb, rnn_conv_w[j], vec(rnn_conv_b[j]), wcat, bcat, lam)
            w_o = rnn_w_o[j].astype(BF16)
        x, h2, idx, wts = _mix_out(a, hs, x, w_o, g1, vec(norm_ffn[i]), sh2, sc2, wr1, wr2, rbias)
        x = _moe(h2, idx, wts, x, g2, moe_w_gu, moe_w_dn, i)
    return x
```

```python
import functools

import jax
import jax.numpy as jnp
from jax import lax
from jax.experimental import pallas as pl
from jax.experimental.pallas import tpu as pltpu

F32 = jnp.float32
BF16 = jnp.bfloat16

D_MODEL = 1024
N_HEADS = 16
Q_LORA = 384
KV_LORA = 256
QK_NOPE = 64
QK_ROPE = 32
QK_HEAD = QK_NOPE + QK_ROPE
V_HEAD = 64
ROPE_THETA = 10000.0
D_RNN = D_MODEL
RNN_BLOCKS = 4
RNN_BW = D_RNN // RNN_BLOCKS
CONV_W = 4
LRU_C = 8.0
N_EXPERTS = 32
N_GROUPS = 8
EXPERTS_PER_GROUP = N_EXPERTS // N_GROUPS
TOP_K = 2
D_EXPERT = 512
EPS = 1e-6
LOG2_E = 1.4426950408889634

LANES = 128
SUBLANES = 8
VMEM_LIMIT = 52 * 1024 * 1024

ROW_TILE = 512
Q_TILE = 1024
Q_SUB = 256
MOE_TILE = 256
SCAN_ROWS = 256


def _dot(a, b):
    return jnp.dot(a, b, preferred_element_type=F32)


def _split_bf16(a):
    hi = a.astype(BF16)
    lo = (a - hi.astype(F32)).astype(BF16)
    return hi, lo


def _dot_split(a, b):
    ah, al = _split_bf16(a)
    bh, bl = _split_bf16(b)
    return _dot(ah, bh) + (_dot(ah, bl) + _dot(al, bh))


def _rms(x, gain, n):
    ms = jnp.sum(x * x, axis=-1, keepdims=True) * (1.0 / n)
    return x * lax.rsqrt(ms + EPS) * gain


def _modulate(x, gain, shift, scale):
    return _rms(x, gain, x.shape[-1]) * (1.0 + scale) + shift


def _params(*sem):
    return pltpu.CompilerParams(dimension_semantics=sem, vmem_limit_bytes=VMEM_LIMIT)


def _ada_kernel(c_ref, w_ref, b_ref, o_ref):
    c = c_ref[...]
    o_ref[...] = _dot_split(c * jax.nn.sigmoid(c), w_ref[...]) + b_ref[...]


def _ada(c, w_ada, b_ada):
    depth, d, n = w_ada.shape
    bsz = c.shape[0]
    tn = 1536
    return pl.pallas_call(
        _ada_kernel,
        grid=(depth, n // tn),
        in_specs=[
            pl.BlockSpec((bsz, d), lambda l, j: (0, 0)),
            pl.BlockSpec((None, d, tn), lambda l, j: (l, 0, j)),
            pl.BlockSpec((None, 1, tn), lambda l, j: (l, 0, j)),
        ],
        out_specs=pl.BlockSpec((None, bsz, tn), lambda l, j: (l, 0, j)),
        out_shape=jax.ShapeDtypeStruct((depth, bsz, n), F32),
        compiler_params=_params("arbitrary", "arbitrary"),
        name="adaln_mod",
    )(c, w_ada, b_ada.reshape(depth, 1, n))


def _head_scale(s):
    return lax.rsqrt(jnp.sum(s * s, axis=-1, keepdims=True) * (1.0 / QK_HEAD) + EPS)


def _mla_in_kernel(x_ref, g_ref, sh_ref, sc_ref, win_ref, qan_ref, kvan_ref, wq_ref, wqr_ref, wkv_ref,
                   qn_ref, kn_ref, cos_ref, sin_ref, q_out, k_out, v_out):
    h = _modulate(x_ref[...], g_ref[...], sh_ref[...], sc_ref[...])
    lat = _dot(h.astype(BF16), win_ref[...])
    q_lat = lat[:, :Q_LORA]
    kv_lat = lat[:, Q_LORA:Q_LORA + KV_LORA]
    kpe = lat[:, Q_LORA + KV_LORA:Q_LORA + KV_LORA + LANES]
    kpe_rot = lat[:, Q_LORA + KV_LORA + LANES:]
    qn = _rms(q_lat, qan_ref[...], Q_LORA).astype(BF16)
    q_all = _dot(qn, wq_ref[...])
    q_rot = _dot(qn, wqr_ref[...])
    kv_all = _dot(_rms(kv_lat, kvan_ref[...], KV_LORA).astype(BF16), wkv_ref[...])
    cos_t = cos_ref[...]
    sin_t = sin_ref[...]
    q_scale = LOG2_E * QK_HEAD ** -0.5
    cq = cos_t * (qn_ref[0:1, :] * q_scale)
    sq = sin_t * (qn_ref[1:2, :] * q_scale)
    ck = cos_t * kn_ref[0:1, :]
    k_rot_term = kpe_rot * (sin_t * kn_ref[1:2, :])
    for hh in range(N_HEADS):
        sl = slice(hh * LANES, (hh + 1) * LANES)
        s = q_all[:, sl]
        q_out[hh] = ((s * cq + q_rot[:, sl] * sq) * _head_scale(s)).astype(BF16)
        s = kv_all[:, sl] + kpe
        k_out[hh] = ((s * ck + k_rot_term) * _head_scale(s)).astype(BF16)
    v_out[...] = kv_all[:, N_HEADS * LANES:].astype(BF16)


def _mla_in(x, gain, shift, scale, w_in, q_a_norm, kv_a_norm, w_q, w_q_rot, w_kv, q_norm, k_norm, cos_t, sin_t):
    bsz, seq, d = x.shape
    tm = min(ROW_TILE, seq)
    row = lambda b, i: (b, i, 0)
    per_b = lambda b, i: (b, 0, 0)
    const = lambda b, i: (0, 0)
    full = lambda a: pl.BlockSpec(a.shape, const)
    return pl.pallas_call(
        _mla_in_kernel,
        grid=(bsz, seq // tm),
        in_specs=[
            pl.BlockSpec((None, tm, d), row),
            full(gain),
            pl.BlockSpec((None, 1, d), per_b),
            pl.BlockSpec((None, 1, d), per_b),
            full(w_in), full(q_a_norm), full(kv_a_norm), full(w_q), full(w_q_rot), full(w_kv),
            full(q_norm), full(k_norm),
            pl.BlockSpec((None, tm, LANES), row),
            pl.BlockSpec((None, tm, LANES), row),
        ],
        out_specs=[
            pl.BlockSpec((None, N_HEADS, tm, LANES), lambda b, i: (b, 0, i, 0)),
            pl.BlockSpec((None, N_HEADS, tm, LANES), lambda b, i: (b, 0, i, 0)),
            pl.BlockSpec((None, tm, N_HEADS * V_HEAD), row),
        ],
        out_shape=[
            jax.ShapeDtypeStruct((bsz, N_HEADS, seq, LANES), BF16),
            jax.ShapeDtypeStruct((bsz, N_HEADS, seq, LANES), BF16),
            jax.ShapeDtypeStruct((bsz, seq, N_HEADS * V_HEAD), BF16),
        ],
        compiler_params=_params("arbitrary", "arbitrary"),
        name="mla_in",
    )(x, gain, shift, scale, w_in, q_a_norm, kv_a_norm, w_q, w_q_rot, w_kv, q_norm, k_norm, cos_t, sin_t)


def _attn_kernel(q_ref, k_ref, v_ref, o_ref):
    v = v_ref[...]
    lane_v = lax.broadcasted_iota(jnp.int32, v.shape, 1)
    v_heads = [jnp.where(lane_v < V_HEAD, v, jnp.ones((), BF16)), jnp.where(lane_v >= V_HEAD, v, jnp.ones((), BF16))]
    lane = lax.broadcasted_iota(jnp.int32, (Q_SUB, LANES), 1)
    for i in range(q_ref.shape[1] // Q_SUB):
        rows = slice(i * Q_SUB, (i + 1) * Q_SUB)
        outs = []
        for j in range(2):
            s = lax.dot_general(q_ref[j, rows, :], k_ref[j], (((1,), (1,)), ((), ())),
                                preferred_element_type=F32)
            m = jnp.max(s, axis=-1, keepdims=True)
            o = _dot(jnp.exp2(s - m).astype(BF16), v_heads[j])
            denom = o[:, V_HEAD:V_HEAD + 1] if j == 0 else o[:, 0:1]
            outs.append(o / denom)
        o_ref[rows, :] = jnp.where(lane < V_HEAD, outs[0], outs[1]).astype(BF16)


def _attention(q, k, v):
    bsz, _, seq, _ = q.shape
    tq = min(Q_TILE, seq)
    assert tq % Q_SUB == 0
    return pl.pallas_call(
        _attn_kernel,
        grid=(bsz, N_HEADS // 2, seq // tq),
        in_specs=[
            pl.BlockSpec((None, 2, tq, LANES), lambda b, h, i: (b, h, i, 0)),
            pl.BlockSpec((None, 2, seq, LANES), lambda b, h, i: (b, h, 0, 0)),
            pl.BlockSpec((None, seq, LANES), lambda b, h, i: (b, 0, h)),
        ],
        out_specs=pl.BlockSpec((None, tq, LANES), lambda b, h, i: (b, i, h)),
        out_shape=jax.ShapeDtypeStruct((bsz, seq, N_HEADS * V_HEAD), BF16),
        compiler_params=_params("arbitrary", "arbitrary", "arbitrary"),
        name="mla_attention",
    )(q, k, v)


def _first_index_of_max(vals):
    m = vals[0]
    for v in vals[1:]:
        m = jnp.maximum(m, v)
    idx = jnp.full(m.shape, float(len(vals) - 1), F32)
    for j in range(len(vals) - 2, -1, -1):
        idx = jnp.where(vals[j] == m, float(j), idx)
    return m, idx


def _route(h2, wr1_ref, wr2_ref, rb_ref):
    hh, hl = _split_bf16(h2)
    logits = (_dot(hh, wr1_ref[...]) + _dot(hl, wr2_ref[...])).T
    logit = logits[0:N_EXPERTS] + logits[N_EXPERTS:2 * N_EXPERTS] + logits[2 * N_EXPERTS:3 * N_EXPERTS]
    score = jax.nn.sigmoid(logit)
    biased = score + rb_ref[...]
    a = [biased[j * N_GROUPS:(j + 1) * N_GROUPS] for j in range(EXPERTS_PER_GROUP)]
    sc = [score[j * N_GROUPS:(j + 1) * N_GROUPS] for j in range(EXPERTS_PER_GROUP)]
    hi1, lo1 = jnp.maximum(a[0], a[1]), jnp.minimum(a[0], a[1])
    hi2, lo2 = jnp.maximum(a[2], a[3]), jnp.minimum(a[2], a[3])
    gscore = jnp.maximum(hi1, hi2) + jnp.maximum(jnp.minimum(hi1, hi2), jnp.maximum(lo1, lo2))
    gmax = jnp.max(gscore, axis=0, keepdims=True)
    giota = lax.broadcasted_iota(jnp.int32, gscore.shape, 0).astype(F32)
    gsel = jnp.min(jnp.where(gscore == gmax, giota, float(N_GROUPS)), axis=0, keepdims=True)
    onehot = giota == gsel
    pick = lambda t: jnp.sum(jnp.where(onehot, t, 0.0), axis=0, keepdims=True)
    bj = [pick(t) for t in a]
    sj = [pick(t) for t in sc]
    _, i1 = _first_index_of_max(bj)
    bj2 = [jnp.where(i1 == float(j), -jnp.inf, bj[j]) for j in range(EXPERTS_PER_GROUP)]
    _, i2 = _first_index_of_max(bj2)
    sel = lambda i: jnp.where(i == 0.0, sj[0], jnp.where(i == 1.0, sj[1], jnp.where(i == 2.0, sj[2], sj[3])))
    w1, w2 = sel(i1), sel(i2)
    den = w1 + w2
    base = gsel * float(EXPERTS_PER_GROUP)
    return ((base + i1).astype(jnp.int32), (base + i2).astype(jnp.int32)), (w1 / den, w2 / den)


def _mix_out_kernel(has_gate, *refs):
    if has_gate:
        a_ref, hs_ref, x_ref, wo_ref, g1_ref, g_ref, sh_ref, sc_ref, wr1_ref, wr2_ref, rb_ref, \
            x_out, h_out, idx_out, wts_out = refs
        a = (a_ref[...].astype(F32) * hs_ref[...]).astype(BF16)
    else:
        a_ref, x_ref, wo_ref, g1_ref, g_ref, sh_ref, sc_ref, wr1_ref, wr2_ref, rb_ref, \
            x_out, h_out, idx_out, wts_out = refs
        a = a_ref[...]
    x1 = x_ref[...] + g1_ref[...] * _dot(a, wo_ref[...])
    x_out[...] = x1
    h2 = _modulate(x1, g_ref[...], sh_ref[...], sc_ref[...])
    _to_tiles(h_out, h2)
    idx, wts = _route(h2, wr1_ref, wr2_ref, rb_ref)
    for k in range(TOP_K):
        idx_out[k:k + 1, :] = idx[k]
        wts_out[k:k + 1, :] = wts[k]


def _mix_out(a, hs, x, w_o, gate1, gain, shift, scale, wr1, wr2, rbias):
    bsz, seq, d = x.shape
    tm = min(ROW_TILE, seq)
    row = lambda b, i: (b, i, 0)
    per_b = lambda b, i: (b, 0, 0)
    const = lambda b, i: (0, 0)
    full = lambda t: pl.BlockSpec(t.shape, const)
    vec = pl.BlockSpec((None, 1, d), per_b)
    acts = [a] if hs is None else [a, hs]
    return pl.pallas_call(
        functools.partial(_mix_out_kernel, hs is not None),
        grid=(bsz, seq // tm),
        in_specs=[pl.BlockSpec((None, tm, t.shape[-1]), row) for t in acts] + [
            pl.BlockSpec((None, tm, d), row), full(w_o), vec, full(gain), vec, vec,
            full(wr1), full(wr2), full(rbias),
        ],
        out_specs=[
            pl.BlockSpec((None, tm, d), row),
            pl.BlockSpec((None, tm * (d // LANES), LANES), row),
            pl.BlockSpec((None, TOP_K, tm), lambda b, i: (b, 0, i)),
            pl.BlockSpec((None, TOP_K, tm), lambda b, i: (b, 0, i)),
        ],
        out_shape=[
            jax.ShapeDtypeStruct((bsz, seq, d), F32),
            jax.ShapeDtypeStruct((bsz, seq * (d // LANES), LANES), F32),
            jax.ShapeDtypeStruct((bsz, TOP_K, seq), jnp.int32),
            jax.ShapeDtypeStruct((bsz, TOP_K, seq), F32),
        ],
        compiler_params=_params("arbitrary", "arbitrary"),
        name="mix_out_route",
    )(*acts, x, w_o, gate1, gain, shift, scale, wr1, wr2, rbias)


N_SUB = D_MODEL // LANES
TABLE_CHUNK = 512


def _to_tiles(ref, val):
    n = val.shape[0]
    for s in range(N_SUB):
        ref[pl.ds(s, n, stride=N_SUB), :] = val[:, s * LANES:(s + 1) * LANES]


def _from_tiles(ref, lo, n):
    return jnp.concatenate([ref[pl.ds(lo * N_SUB + s, n, stride=N_SUB), :] for s in range(N_SUB)], axis=1)


def _tables_kernel(idx_ref, rank_ref, cnt_ref, carry):
    @pl.when(pl.program_id(0) == 0)
    def _():
        carry[...] = jnp.zeros_like(carry)

    seq = idx_ref.shape[-1]
    ch = min(TABLE_CHUNK, seq)
    tri = jnp.where(lax.broadcasted_iota(jnp.int32, (ch, ch), 0) <= lax.broadcasted_iota(jnp.int32, (ch, ch), 1),
                    1.0, 0.0).astype(BF16)
    eiota = lax.broadcasted_iota(jnp.int32, (N_EXPERTS, ch), 0)
    cnt = carry[...]
    for k in range(TOP_K):
        for c in range(seq // ch):
            sel = eiota == idx_ref[k:k + 1, c * ch:(c + 1) * ch]
            pref = _dot(jnp.where(sel, 1.0, 0.0).astype(BF16), tri) + cnt
            rank = jnp.sum(jnp.where(sel, pref, 0.0), axis=0, keepdims=True) - 1.0
            rank_ref[k:k + 1, c * ch:(c + 1) * ch] = rank.astype(jnp.int32)
            cnt = pref[:, ch - 1:ch]
    carry[...] = cnt
    cnt_ref[...] = jnp.broadcast_to(cnt, cnt_ref.shape)


def _tables(idx):
    bsz, _, seq = idx.shape
    return pl.pallas_call(
        _tables_kernel,
        grid=(bsz,),
        in_specs=[pl.BlockSpec((None, TOP_K, seq), lambda b: (b, 0, 0))],
        out_specs=[pl.BlockSpec((None, TOP_K, seq), lambda b: (b, 0, 0)),
                   pl.BlockSpec((N_EXPERTS, LANES), lambda b: (0, 0))],
        out_shape=[jax.ShapeDtypeStruct((bsz, TOP_K, seq), jnp.int32),
                   jax.ShapeDtypeStruct((N_EXPERTS, LANES), F32)],
        scratch_shapes=[pltpu.VMEM((N_EXPERTS, 1), F32)],
        compiler_params=_params("arbitrary"),
        name="moe_tables",
    )(idx)


def _scatter_kernel(dest_ref, pad_ref, src_ref, zero_ref, dst_hbm, sem):
    tm = src_ref.shape[0]
    n_pad = pad_ref.shape[-1]
    for k in range(TOP_K):
        for c in range(tm // LANES):
            def start(j, carry, k=k, c=c):
                row = dest_ref[0, k * tm + c * LANES + j]
                pltpu.make_async_copy(src_ref.at[c * LANES + j], dst_hbm.at[row], sem).start(priority=k)
                return carry
            lax.fori_loop(0, LANES, start, 0, unroll=8)

    def fill(r, carry):
        pltpu.make_async_copy(zero_ref.at[0], dst_hbm.at[pad_ref[0, r]], sem).start()
        return carry
    lax.fori_loop(0, n_pad, fill, 0, unroll=8)

    left = TOP_K * tm + n_pad
    while left > 0:
        cnt = min(tm, left)
        pltpu.make_async_copy(src_ref.at[pl.ds(0, cnt)], dst_hbm.at[pl.ds(0, cnt)], sem).wait()
        left -= cnt


def _index_blocks(table, tm):
    bsz, _, seq = table.shape
    nt = seq // tm
    t = table.reshape(bsz, TOP_K, nt, tm).transpose(0, 2, 1, 3)
    return t.reshape(bsz * nt, 1, TOP_K * tm), (None, 1, TOP_K * tm)


def _scatter(dest, pad_rows, h2t, n_rows):
    bsz, _, seq = dest.shape
    tm = min(ROW_TILE, seq)
    nt = seq // tm
    n_pad = pad_rows.shape[0] // (bsz * nt)
    assert n_pad * bsz * nt == pad_rows.shape[0]
    dest4, dest_block = _index_blocks(dest, tm)
    return pl.pallas_call(
        _scatter_kernel,
        grid=(bsz, nt),
        in_specs=[
            pl.BlockSpec(dest_block, lambda b, i: (b * nt + i, 0, 0), memory_space=pltpu.SMEM),
            pl.BlockSpec((None, 1, n_pad), lambda b, i: (b * nt + i, 0, 0), memory_space=pltpu.SMEM),
            pl.BlockSpec((tm, N_SUB, LANES), lambda b, i: (b * nt + i, 0, 0)),
            pl.BlockSpec((1, N_SUB, LANES), lambda b, i: (0, 0, 0)),
        ],
        out_specs=pl.BlockSpec(memory_space=pl.ANY),
        out_shape=jax.ShapeDtypeStruct((n_rows, N_SUB, LANES), F32),
        scratch_shapes=[pltpu.SemaphoreType.DMA(())],
        compiler_params=_params("arbitrary", "arbitrary"),
        name="moe_scatter",
    )(dest4, pad_rows.reshape(bsz * nt, 1, n_pad), h2t.reshape(bsz * seq, N_SUB, LANES),
      jnp.zeros((1, N_SUB, LANES), F32))


def _expert_kernel(blk_exp_ref, blk_first_ref, n_used_ref,
                   xs_ref, wgu_ref, wdn_ref, ys_ref, wgu_bf, wdn_bf):
    i = pl.program_id(0)

    @pl.when(i < n_used_ref[0])
    def _():
        @pl.when(blk_first_ref[i] == 1)
        def _():
            wgu_bf[...] = wgu_ref[...].astype(BF16)
            wdn_bf[...] = wdn_ref[...].astype(BF16)

        x = _from_tiles(xs_ref, 0, MOE_TILE).astype(BF16)
        gu = _dot(x, wgu_bf[...])
        g = gu[:, :D_EXPERT]
        u = gu[:, D_EXPERT:]
        mid = (g * jax.nn.sigmoid(g) * u).astype(BF16)
        _to_tiles(ys_ref, _dot(mid, wdn_bf[...]))

    @pl.when(i >= n_used_ref[0])
    def _():
        ys_ref[...] = jnp.zeros_like(ys_ref)


def _experts(blk_exp, blk_first, n_used, xs, w_gu, w_dn, layer):
    d = D_MODEL
    nb = xs.shape[0] // (MOE_TILE * N_SUB)
    tile = lambda i, *_: (i, 0)
    grid_spec = pltpu.PrefetchScalarGridSpec(
        num_scalar_prefetch=3,
        grid=(nb,),
        in_specs=[
            pl.BlockSpec((MOE_TILE * N_SUB, LANES), tile),
            pl.BlockSpec((None, None, d, 2 * D_EXPERT), lambda i, be, *_: (layer, be[i], 0, 0)),
            pl.BlockSpec((None, None, D_EXPERT, d), lambda i, be, *_: (layer, be[i], 0, 0)),
        ],
        out_specs=pl.BlockSpec((MOE_TILE * N_SUB, LANES), tile),
        scratch_shapes=[pltpu.VMEM((d, 2 * D_EXPERT), BF16), pltpu.VMEM((D_EXPERT, d), BF16)],
    )
    return pl.pallas_call(
        _expert_kernel,
        grid_spec=grid_spec,
        out_shape=jax.ShapeDtypeStruct(xs.shape, F32),
        compiler_params=_params("arbitrary"),
        name="moe_experts",
    )(blk_exp, blk_first, n_used, xs, w_gu, w_dn)


def _combine_kernel(dcur_ref, dnxt_ref, ys_hbm, x_ref, wts_ref, g2_ref, x_out, buf, sem):
    nt = pl.num_programs(1)
    n = pl.program_id(0) * nt + pl.program_id(1)
    total = pl.num_programs(0) * nt
    tm = x_ref.shape[0]
    slot = lax.rem(n, 2)

    def issue(d_ref, sl):
        for k in range(TOP_K):
            for c in range(tm // LANES):
                def start(j, carry, k=k, c=c):
                    src = ys_hbm.at[pl.ds(pl.multiple_of(d_ref[0, k * tm + c * LANES + j], N_SUB), N_SUB)]
                    dst = buf.at[sl, pl.ds(pl.multiple_of((k * tm + c * LANES + j) * N_SUB, N_SUB), N_SUB)]
                    pltpu.make_async_copy(src, dst, sem.at[sl]).start(priority=k)
                    return carry
                lax.fori_loop(0, LANES, start, 0, unroll=8)

    @pl.when(n == 0)
    def _():
        issue(dcur_ref, 0)

    @pl.when(n + 1 < total)
    def _():
        issue(dnxt_ref, 1 - slot)

    pltpu.make_async_copy(ys_hbm.at[pl.ds(0, TOP_K * tm * N_SUB)], buf.at[slot], sem.at[slot]).wait()
    cur = buf.at[slot]
    w = wts_ref[...]
    y = w[:, 0:1] * _from_tiles(cur, 0, tm) + w[:, 1:2] * _from_tiles(cur, tm, tm)
    x_out[...] = x_ref[...] + g2_ref[...] * y


def _combine(dest8, ys, x, wts_col, gate2):
    bsz, seq, d = x.shape
    tm = min(ROW_TILE, seq)
    nt = seq // tm

    def nxt(b, i):
        return (jnp.minimum(b * nt + i + 1, bsz * nt - 1), 0, 0)

    dest8, dest_block = _index_blocks(dest8, tm)
    return pl.pallas_call(
        _combine_kernel,
        grid=(bsz, nt),
        in_specs=[
            pl.BlockSpec(dest_block, lambda b, i: (b * nt + i, 0, 0), memory_space=pltpu.SMEM),
            pl.BlockSpec(dest_block, nxt, memory_space=pltpu.SMEM),
            pl.BlockSpec(memory_space=pl.ANY),
            pl.BlockSpec((None, tm, d), lambda b, i: (b, i, 0)),
            pl.BlockSpec((None, tm, TOP_K), lambda b, i: (b, i, 0)),
            pl.BlockSpec((None, 1, d), lambda b, i: (b, 0, 0)),
        ],
        out_specs=pl.BlockSpec((None, tm, d), lambda b, i: (b, i, 0)),
        out_shape=jax.ShapeDtypeStruct((bsz, seq, d), F32),
        scratch_shapes=[pltpu.VMEM((2, TOP_K * tm * N_SUB, LANES), F32), pltpu.SemaphoreType.DMA((2,))],
        compiler_params=_params("arbitrary", "arbitrary"),
        name="moe_combine",
    )(dest8, dest8, ys, x, wts_col, gate2)


def _lookup(table, keys):
    hit = keys[..., None] == jnp.arange(table.shape[0], dtype=jnp.int32)
    return jnp.sum(jnp.where(hit, table, 0), axis=-1).astype(jnp.int32)


def _count_le(bounds, q):
    return jnp.sum((bounds <= q[..., None]).astype(jnp.int32), axis=-1)


def _moe(h2t, idx, wts, x, gate2, w_gu, w_dn, layer):
    bsz, seq, _ = x.shape
    n_rows = bsz * seq * TOP_K + N_EXPERTS * MOE_TILE
    nb = n_rows // MOE_TILE
    rank, cnt = _tables(idx)
    counts = cnt[:, 0].astype(jnp.int32)
    padded = ((counts + MOE_TILE - 1) // MOE_TILE) * MOE_TILE
    pend = jnp.cumsum(padded)
    pstart = pend - padded
    dest = _lookup(pstart, idx) + rank
    blk_row = jnp.arange(nb, dtype=jnp.int32) * MOE_TILE
    blk_exp = jnp.minimum(_count_le(pend, blk_row), N_EXPERTS - 1)
    blk_first = (blk_row == _lookup(pstart, blk_exp)).astype(jnp.int32)
    n_used = (pend[-1:] // MOE_TILE).astype(jnp.int32)
    slack_len = jnp.concatenate([padded - counts, n_rows - pend[-1:]])
    slack_row = jnp.concatenate([pstart + counts, pend[-1:]])
    slack_end = jnp.cumsum(slack_len)
    q = jnp.arange(N_EXPERTS * MOE_TILE, dtype=jnp.int32)
    seg = _count_le(slack_end, q)
    pad_rows = _lookup(slack_row - (slack_end - slack_len), seg) + q
    xs = _scatter(dest, pad_rows, h2t, n_rows)
    ys = _experts(blk_exp, blk_first, n_used, xs.reshape(n_rows * N_SUB, LANES), w_gu, w_dn, layer)
    return _combine(dest * N_SUB, ys, x, wts.transpose(0, 2, 1), gate2)


def _rnn_in_kernel(x_ref, g_ref, sh_ref, sc_ref, w_ref, gate_out, xb_out):
    h = _modulate(x_ref[...], g_ref[...], sh_ref[...], sc_ref[...])
    u = _dot(h.astype(BF16), w_ref[...])
    gate_out[...] = jax.nn.gelu(u[:, :D_RNN]).astype(BF16)
    xb_out[...] = u[:, D_RNN:]


def _rnn_in(x, gain, shift, scale, w_in):
    bsz, seq, d = x.shape
    tm = min(ROW_TILE, seq)
    row = lambda b, i: (b, i, 0)
    per_b = lambda b, i: (b, 0, 0)
    const = lambda b, i: (0, 0)
    return pl.pallas_call(
        _rnn_in_kernel,
        grid=(bsz, seq // tm),
        in_specs=[
            pl.BlockSpec((None, tm, d), row),
            pl.BlockSpec(gain.shape, const),
            pl.BlockSpec((None, 1, d), per_b),
            pl.BlockSpec((None, 1, d), per_b),
            pl.BlockSpec(w_in.shape, const),
        ],
        out_specs=[pl.BlockSpec((None, tm, D_RNN), row), pl.BlockSpec((None, tm, D_RNN), row)],
        out_shape=[jax.ShapeDtypeStruct((bsz, seq, D_RNN), BF16),
                   jax.ShapeDtypeStruct((bsz, seq, D_RNN), F32)],
        compiler_params=_params("arbitrary", "arbitrary"),
        name="rnn_in",
    )(x, gain, shift, scale, w_in)


def _lru_kernel(xb_ref, cw_ref, cb_ref, wcat_ref, bcat_ref, lam_ref, hs_ref,
                xi_ref, af_ref, bf_ref, ab_ref, bb_ref, hf_ref, hb_ref, sum_ref):
    seq, c = xb_ref.shape
    seg_len = seq // SUBLANES
    n_slab = c // LANES
    n_rows = seg_len * SUBLANES
    halo = (CONV_W // 2) * SUBLANES
    row = lax.broadcasted_iota(jnp.int32, (SUBLANES, LANES), 0)
    for sl in range(n_slab):
        lanes = slice(sl * LANES, (sl + 1) * LANES)
        for g in range(SUBLANES):
            xi_ref[sl, pl.ds(halo + g, seg_len, stride=SUBLANES), :] = xb_ref[g * seg_len:(g + 1) * seg_len, lanes]
        for back in (1, 2):
            prev = xi_ref[sl, halo + (seg_len - back) * SUBLANES:halo + (seg_len - back + 1) * SUBLANES, :]
            xi_ref[sl, halo - back * SUBLANES:halo - (back - 1) * SUBLANES, :] = jnp.where(
                row == 0, 0.0, pltpu.roll(prev, 1, 0))
        nxt = xi_ref[sl, halo:halo + SUBLANES, :]
        xi_ref[sl, halo + n_rows:halo + n_rows + SUBLANES, :] = jnp.where(
            row == SUBLANES - 1, 0.0, pltpu.roll(nxt, SUBLANES - 1, 0))

    cw = cw_ref[...]
    cb = cb_ref[...]
    lam = lam_ref[...]
    neg = -lam
    softplus = jnp.maximum(neg, 0.0) + jnp.log1p(jnp.exp(-jnp.abs(neg)))
    half_rate = (-0.5 * LRU_C) * softplus
    rows = min(SCAN_ROWS, n_rows)
    n_chunks = n_rows // rows

    for ci in range(n_chunks):
        i0 = ci * rows
        taps = []
        for k in range(CONV_W):
            lo = halo + i0 + (k - CONV_W // 2) * SUBLANES
            taps.append(jnp.concatenate([xi_ref[sl, lo:lo + rows, :] for sl in range(n_slab)], axis=1))
        xc = cb
        for k in range(CONV_W):
            xc = xc + taps[k] * cw[k:k + 1, :]
        xcb = xc.astype(BF16)
        xh = 0.5 * xc
        for dirn, (a_ref, b_ref) in enumerate(((af_ref, bf_ref), (ab_ref, bb_ref))):
            cols = slice(2 * dirn * c, 2 * (dirn + 1) * c)
            th = jnp.tanh(_dot(xcb, wcat_ref[:, cols]) + bcat_ref[:, cols])
            hr = half_rate[dirn:dirn + 1, :]
            log_a = hr * th[:, :c] + hr
            a = jnp.exp(log_a)
            mult = jnp.sqrt(jnp.tanh(-log_a) * (a * a + 1.0))
            if dirn == 0 and ci == 0:
                mult = jnp.where(lax.broadcasted_iota(jnp.int32, mult.shape, 0) == 0, 1.0, mult)
            if dirn == 1 and ci == n_chunks - 1:
                mult = jnp.where(lax.broadcasted_iota(jnp.int32, mult.shape, 0) == rows - 1, 1.0, mult)
            b = mult * (th[:, c:] + 1.0) * xh
            for sl in range(n_slab):
                a_ref[sl, i0:i0 + rows, :] = a[:, sl * LANES:(sl + 1) * LANES]
                b_ref[sl, i0:i0 + rows, :] = b[:, sl * LANES:(sl + 1) * LANES]

    def step_rows(cidx):
        fwd = pl.ds(pl.multiple_of(cidx * SUBLANES, SUBLANES), SUBLANES)
        bwd = pl.ds(pl.multiple_of((seg_len - 1 - cidx) * SUBLANES, SUBLANES), SUBLANES)
        return fwd, bwd

    zero = jnp.zeros((SUBLANES, LANES), F32)
    one = jnp.ones((SUBLANES, LANES), F32)

    def totals(cidx, carry):
        fwd, bwd = step_rows(cidx)
        out = []
        for sl in range(n_slab):
            hf, pf, hb, pb = carry[sl]
            af, ab = af_ref[sl, fwd, :], ab_ref[sl, bwd, :]
            out.append((af * hf + bf_ref[sl, fwd, :], af * pf, ab * hb + bb_ref[sl, bwd, :], ab * pb))
        return tuple(out)
    tot = lax.fori_loop(0, seg_len, totals, tuple((zero, one, zero, one) for _ in range(n_slab)), unroll=8)

    enter = []
    for sl in range(n_slab):
        hf, pf, hb, pb = tot[sl]
        cf, cbk = zero, zero
        for _ in range(SUBLANES - 1):
            cf = jnp.where(row == 0, 0.0, pltpu.roll(hf + pf * cf, 1, 0))
            cbk = jnp.where(row == SUBLANES - 1, 0.0, pltpu.roll(hb + pb * cbk, SUBLANES - 1, 0))
        enter.append((cf, cbk))

    def states(meet, cidx, carry):
        fwd, bwd = step_rows(cidx)
        out = []
        for sl in range(n_slab):
            hf, hb = carry[sl]
            hf = af_ref[sl, fwd, :] * hf + bf_ref[sl, fwd, :]
            hb = ab_ref[sl, bwd, :] * hb + bb_ref[sl, bwd, :]
            if meet:
                sum_ref[sl, fwd, :] = hf + hb_ref[sl, fwd, :]
                sum_ref[sl, bwd, :] = hb + hf_ref[sl, bwd, :]
            else:
                hf_ref[sl, fwd, :] = hf
                hb_ref[sl, bwd, :] = hb
            out.append((hf, hb))
        return tuple(out)
    mid = lax.fori_loop(0, seg_len // 2, functools.partial(states, False), tuple(enter), unroll=8)
    lax.fori_loop(seg_len // 2, seg_len, functools.partial(states, True), mid, unroll=8)

    for g in range(SUBLANES):
        for sl in range(n_slab):
            hs_ref[g * seg_len:(g + 1) * seg_len, sl * LANES:(sl + 1) * LANES] = (
                sum_ref[sl, pl.ds(g, seg_len, stride=SUBLANES), :])


def _lru(xb, conv_w, conv_b, wcat, bcat, lam):
    bsz, seq, _ = xb.shape
    c = RNN_BW
    blk = lambda b, n: (b, 0, n)
    return pl.pallas_call(
        _lru_kernel,
        grid=(bsz, RNN_BLOCKS),
        in_specs=[
            pl.BlockSpec((None, seq, c), blk),
            pl.BlockSpec((CONV_W, c), lambda b, n: (0, n)),
            pl.BlockSpec((1, c), lambda b, n: (0, n)),
            pl.BlockSpec((None, c, 4 * c), lambda b, n: (n, 0, 0)),
            pl.BlockSpec((None, 1, 4 * c), lambda b, n: (n, 0, 0)),
            pl.BlockSpec((2, c), lambda b, n: (0, n)),
        ],
        out_specs=pl.BlockSpec((None, seq, c), blk),
        out_shape=jax.ShapeDtypeStruct((bsz, seq, D_RNN), F32),
        scratch_shapes=[pltpu.VMEM((c // LANES, seq + (CONV_W - 1) * SUBLANES, LANES), F32)]
        + [pltpu.VMEM((c // LANES, seq, LANES), F32)] * 7,
        compiler_params=_params("arbitrary", "arbitrary"),
        name="rglru_scan",
    )(xb, conv_w, conv_b, wcat, bcat, lam)


def _mla_weights(w_in, w_q_b, w_kv_b, q_norm, k_norm):
    half = QK_ROPE // 2

    def slab(t):
        return jnp.pad(t, [(0, 0)] * (t.ndim - 1) + [(0, LANES - QK_HEAD)])

    def rot_slab(t):
        rope = t[..., QK_NOPE:]
        swapped = jnp.concatenate([jnp.zeros_like(t[..., :QK_NOPE]), rope[..., half:], rope[..., :half]], axis=-1)
        return slab(swapped)

    kpe = jnp.pad(w_in[:, Q_LORA + KV_LORA:], ((0, 0), (QK_NOPE, 0)))
    w_in_p = jnp.concatenate([w_in[:, :Q_LORA + KV_LORA], slab(kpe), rot_slab(kpe)], axis=1).astype(BF16)
    wq = w_q_b.reshape(Q_LORA, N_HEADS, QK_HEAD)
    wq_p = slab(wq).reshape(Q_LORA, N_HEADS * LANES).astype(BF16)
    wq_rot = rot_slab(wq).reshape(Q_LORA, N_HEADS * LANES).astype(BF16)
    wkv = w_kv_b.reshape(KV_LORA, N_HEADS, QK_NOPE + V_HEAD)
    wk = jnp.pad(wkv[:, :, :QK_NOPE], ((0, 0), (0, 0), (0, LANES - QK_NOPE))).reshape(KV_LORA, N_HEADS * LANES)
    wv = wkv[:, :, QK_NOPE:].reshape(KV_LORA, N_HEADS * V_HEAD)
    w_kv_p = jnp.concatenate([wk, wv], axis=1).astype(BF16)
    gains = lambda g: jnp.stack([slab(g), rot_slab(g)], axis=0)
    return w_in_p, wq_p, wq_rot, w_kv_p, gains(q_norm), gains(k_norm)


def _rope_tables(positions):
    half = QK_ROPE // 2
    inv_freq = ROPE_THETA ** (-jnp.arange(half, dtype=F32) / half)
    ang = positions.astype(F32)[..., None] * inv_freq
    cos, sin = jnp.cos(ang), jnp.sin(ang)
    lead = positions.shape + (QK_NOPE,)
    tail = positions.shape + (LANES - QK_HEAD,)
    cos_t = jnp.concatenate([jnp.ones(lead, F32), cos, cos, jnp.ones(tail, F32)], axis=-1)
    sin_t = jnp.concatenate([jnp.zeros(lead, F32), -sin, sin, jnp.zeros(tail, F32)], axis=-1)
    return cos_t, sin_t


def _router_weights(w_router, router_bias):
    perm = (jnp.arange(N_EXPERTS) % N_GROUPS) * EXPERTS_PER_GROUP + jnp.arange(N_EXPERTS) // N_GROUPS
    w = w_router[:, perm]
    hi = w.astype(BF16)
    lo = (w - hi.astype(F32)).astype(BF16)
    z = jnp.zeros_like(hi)
    wr1 = jnp.concatenate([hi, lo, z, z], axis=1)
    wr2 = jnp.concatenate([z, z, hi, z], axis=1)
    return wr1, wr2, router_bias[perm].reshape(N_EXPERTS, 1).astype(F32)


def kernel(x, c, positions, norm_mix, norm_ffn, w_ada, b_ada, mla_w_in, mla_q_a_norm, mla_kv_a_norm, mla_w_q_b, mla_w_kv_b, mla_q_norm, mla_k_norm, mla_w_o, rnn_w_in, rnn_conv_w, rnn_conv_b, rnn_lam_f, rnn_w_rf, rnn_b_rf, rnn_w_if, rnn_b_if, rnn_lam_b, rnn_w_rb, rnn_b_rb, rnn_w_ib, rnn_b_ib, rnn_w_o, w_router, router_bias, moe_w_gu, moe_w_dn):
    bsz, seq, d = x.shape
    depth = w_ada.shape[0]
    mod = _ada(c, w_ada, b_ada)
    wr1, wr2, rbias = _router_weights(w_router, router_bias)
    cos_t, sin_t = _rope_tables(positions)
    vec = lambda v: v.reshape(1, -1)
    for i in range(depth):
        sh1, sc1, g1, sh2, sc2, g2 = [mod[i, :, k * d:(k + 1) * d].reshape(bsz, 1, d) for k in range(6)]
        j = i // 2
        if i % 2 == 0:
            w_in_p, wq, wq_rot, wkv, qn, kn = _mla_weights(mla_w_in[j], mla_w_q_b[j], mla_w_kv_b[j],
                                                           mla_q_norm[j], mla_k_norm[j])
            q, k, v = _mla_in(x, vec(norm_mix[i]), sh1, sc1, w_in_p, vec(mla_q_a_norm[j]),
                              vec(mla_kv_a_norm[j]), wq, wq_rot, wkv, qn, kn, cos_t, sin_t)
            a = _attention(q, k, v)
            hs = None
            w_o = mla_w_o[j].astype(BF16)
        else:
            a, xb = _rnn_in(x, vec(norm_mix[i]), sh1, sc1, rnn_w_in[j].astype(BF16))
            wcat = (0.5 * jnp.concatenate([rnn_w_rf[j], rnn_w_if[j], rnn_w_rb[j], rnn_w_ib[j]], axis=-1)).astype(BF16)
            bcat = jnp.stack([b.reshape(RNN_BLOCKS, RNN_BW) for b in
                              (rnn_b_rf[j], rnn_b_if[j], rnn_b_rb[j], rnn_b_ib[j])], axis=1)
            bcat = 0.5 * bcat.reshape(RNN_BLOCKS, 1, 4 * RNN_BW)
            lam = jnp.stack([rnn_lam_f[j], rnn_lam_b[j]], axis=0)
            hs = _lru(xb, rnn_conv_w[j], vec(rnn_conv_b[j]), wcat, bcat, lam)
            w_o = rnn_w_o[j].astype(BF16)
        x, h2, idx, wts = _mix_out(a, hs, x, w_o, g1, vec(norm_ffn[i]), sh2, sc2, wr1, wr2, rbias)
        x = _moe(h2, idx, wts, x, g2, moe_w_gu, moe_w_dn, i)
    return x
```

```python
import functools

import jax
import jax.numpy as jnp
from jax import lax
from jax.experimental import pallas as pl
from jax.experimental.pallas import tpu as pltpu

F32 = jnp.float32
BF16 = jnp.bfloat16

D_MODEL = 1024
N_HEADS = 16
Q_LORA = 384
KV_LORA = 256
QK_NOPE = 64
QK_ROPE = 32
QK_HEAD = QK_NOPE + QK_ROPE
V_HEAD = 64
ROPE_THETA = 10000.0
D_RNN = D_MODEL
RNN_BLOCKS = 4
RNN_BW = D_RNN // RNN_BLOCKS
CONV_W = 4
LRU_C = 8.0
N_EXPERTS = 32
N_GROUPS = 8
EXPERTS_PER_GROUP = N_EXPERTS // N_GROUPS
TOP_K = 2
D_EXPERT = 512
EPS = 1e-6
LOG2_E = 1.4426950408889634

LANES = 128
SUBLANES = 8
VMEM_LIMIT = 52 * 1024 * 1024

ROW_TILE = 512
ROW_SUB = 256
Q_TILE = 2048
Q_SUB = 256
MOE_TILE = 512
SCAN_ROWS = 256


def _dot(a, b):
    return jnp.dot(a, b, preferred_element_type=F32)


def _split_bf16(a):
    hi = a.astype(BF16)
    lo = (a - hi.astype(F32)).astype(BF16)
    return hi, lo


def _dot_split(a, b):
    ah, al = _split_bf16(a)
    bh, bl = _split_bf16(b)
    return _dot(ah, bh) + (_dot(ah, bl) + _dot(al, bh))


def _rms(x, gain, n):
    ms = jnp.sum(x * x, axis=-1, keepdims=True) * (1.0 / n)
    return x * lax.rsqrt(ms + EPS) * gain


def _modulate(x, gain, shift, scale):
    return _rms(x, gain, x.shape[-1]) * (1.0 + scale) + shift


def _params(*sem):
    return pltpu.CompilerParams(dimension_semantics=sem, vmem_limit_bytes=VMEM_LIMIT)


def _ada_kernel(c_ref, w_ref, b_ref, o_ref):
    c = c_ref[...]
    o_ref[...] = _dot_split(c * jax.nn.sigmoid(c), w_ref[...]) + b_ref[...]


def _ada(c, w_ada, b_ada):
    depth, d, n = w_ada.shape
    bsz = c.shape[0]
    tn = 1536
    return pl.pallas_call(
        _ada_kernel,
        grid=(depth, n // tn),
        in_specs=[
            pl.BlockSpec((bsz, d), lambda l, j: (0, 0)),
            pl.BlockSpec((None, d, tn), lambda l, j: (l, 0, j)),
            pl.BlockSpec((None, 1, tn), lambda l, j: (l, 0, j)),
        ],
        out_specs=pl.BlockSpec((None, bsz, tn), lambda l, j: (l, 0, j)),
        out_shape=jax.ShapeDtypeStruct((depth, bsz, n), F32),
        compiler_params=_params("arbitrary", "arbitrary"),
        name="adaln_mod",
    )(c, w_ada, b_ada.reshape(depth, 1, n))


def _head_scale(s):
    return lax.rsqrt(jnp.sum(s * s, axis=-1, keepdims=True) * (1.0 / QK_HEAD) + EPS)


def _mla_in_kernel(x_ref, g_ref, sh_ref, sc_ref, win_ref, qan_ref, kvan_ref, wq_ref, wqr_ref, wkv_ref,
                   qn_ref, kn_ref, cos_ref, sin_ref, q_out, k_out, v_out):
    h = _modulate(x_ref[...], g_ref[...], sh_ref[...], sc_ref[...])
    lat = _dot(h.astype(BF16), win_ref[...])
    q_lat = lat[:, :Q_LORA]
    kv_lat = lat[:, Q_LORA:Q_LORA + KV_LORA]
    kpe = lat[:, Q_LORA + KV_LORA:Q_LORA + KV_LORA + LANES]
    kpe_rot = lat[:, Q_LORA + KV_LORA + LANES:]
    qn = _rms(q_lat, qan_ref[...], Q_LORA).astype(BF16)
    q_all = _dot(qn, wq_ref[...])
    q_rot = _dot(qn, wqr_ref[...])
    kv_all = _dot(_rms(kv_lat, kvan_ref[...], KV_LORA).astype(BF16), wkv_ref[...])
    cos_t = cos_ref[...]
    sin_t = sin_ref[...]
    q_scale = LOG2_E * QK_HEAD ** -0.5
    cq = cos_t * (qn_ref[0:1, :] * q_scale)
    sq = sin_t * (qn_ref[1:2, :] * q_scale)
    ck = cos_t * kn_ref[0:1, :]
    k_rot_term = kpe_rot * (sin_t * kn_ref[1:2, :])
    for hh in range(N_HEADS):
        sl = slice(hh * LANES, (hh + 1) * LANES)
        s = q_all[:, sl]
        q_out[hh] = ((s * cq + q_rot[:, sl] * sq) * _head_scale(s)).astype(BF16)
        s = kv_all[:, sl] + kpe
        k_out[hh] = ((s * ck + k_rot_term) * _head_scale(s)).astype(BF16)
    v_out[...] = kv_all[:, N_HEADS * LANES:].astype(BF16)


def _mla_in(x, gain, shift, scale, w_in, q_a_norm, kv_a_norm, w_q, w_q_rot, w_kv, q_norm, k_norm, cos_t, sin_t):
    bsz, seq, d = x.shape
    tm = min(ROW_TILE, seq)
    row = lambda b, i: (b, i, 0)
    per_b = lambda b, i: (b, 0, 0)
    const = lambda b, i: (0, 0)
    full = lambda a: pl.BlockSpec(a.shape, const)
    return pl.pallas_call(
        _mla_in_kernel,
        grid=(bsz, seq // tm),
        in_specs=[
            pl.BlockSpec((None, tm, d), row),
            full(gain),
            pl.BlockSpec((None, 1, d), per_b),
            pl.BlockSpec((None, 1, d), per_b),
            full(w_in), full(q_a_norm), full(kv_a_norm), full(w_q), full(w_q_rot), full(w_kv),
            full(q_norm), full(k_norm),
            pl.BlockSpec((None, tm, LANES), row),
            pl.BlockSpec((None, tm, LANES), row),
        ],
        out_specs=[
            pl.BlockSpec((None, N_HEADS, tm, LANES), lambda b, i: (b, 0, i, 0)),
            pl.BlockSpec((None, N_HEADS, tm, LANES), lambda b, i: (b, 0, i, 0)),
            pl.BlockSpec((None, tm, N_HEADS * V_HEAD), row),
        ],
        out_shape=[
            jax.ShapeDtypeStruct((bsz, N_HEADS, seq, LANES), BF16),
            jax.ShapeDtypeStruct((bsz, N_HEADS, seq, LANES), BF16),
            jax.ShapeDtypeStruct((bsz, seq, N_HEADS * V_HEAD), BF16),
        ],
        compiler_params=_params("arbitrary", "arbitrary"),
        name="mla_in",
    )(x, gain, shift, scale, w_in, q_a_norm, kv_a_norm, w_q, w_q_rot, w_kv, q_norm, k_norm, cos_t, sin_t)


def _attn_kernel(q_ref, k_ref, v_ref, o_ref):
    v = v_ref[...]
    lane_v = lax.broadcasted_iota(jnp.int32, v.shape, 1)
    v_heads = [jnp.where(lane_v < V_HEAD, v, jnp.ones((), BF16)), jnp.where(lane_v >= V_HEAD, v, jnp.ones((), BF16))]
    lane = lax.broadcasted_iota(jnp.int32, (Q_SUB, LANES), 1)
    for i in range(q_ref.shape[1] // Q_SUB):
        rows = slice(i * Q_SUB, (i + 1) * Q_SUB)
        outs = []
        for j in range(2):
            s = lax.dot_general(q_ref[j, rows, :], k_ref[j], (((1,), (1,)), ((), ())),
                                preferred_element_type=F32)
            m = jnp.max(s, axis=-1, keepdims=True)
            o = _dot(jnp.exp2(s - m).astype(BF16), v_heads[j])
            denom = o[:, V_HEAD:V_HEAD + 1] if j == 0 else o[:, 0:1]
            outs.append(o / denom)
        o_ref[rows, :] = jnp.where(lane < V_HEAD, outs[0], outs[1]).astype(BF16)


def _attention(q, k, v):
    bsz, _, seq, _ = q.shape
    tq = min(Q_TILE, seq)
    assert tq % Q_SUB == 0
    return pl.pallas_call(
        _attn_kernel,
        grid=(bsz, N_HEADS // 2, seq // tq),
        in_specs=[
            pl.BlockSpec((None, 2, tq, LANES), lambda b, h, i: (b, h, i, 0)),
            pl.BlockSpec((None, 2, seq, LANES), lambda b, h, i: (b, h, 0, 0)),
            pl.BlockSpec((None, seq, LANES), lambda b, h, i: (b, 0, h)),
        ],
        out_specs=pl.BlockSpec((None, tq, LANES), lambda b, h, i: (b, i, h)),
        out_shape=jax.ShapeDtypeStruct((bsz, seq, N_HEADS * V_HEAD), BF16),
        compiler_params=_params("arbitrary", "arbitrary", "arbitrary"),
        name="mla_attention",
    )(q, k, v)


def _first_index_of_max(vals):
    m = vals[0]
    for v in vals[1:]:
        m = jnp.maximum(m, v)
    idx = jnp.full(m.shape, float(len(vals) - 1), F32)
    for j in range(len(vals) - 2, -1, -1):
        idx = jnp.where(vals[j] == m, float(j), idx)
    return m, idx


def _route(h2, wr1_ref, wr2_ref, rb_ref):
    hh, hl = _split_bf16(h2)
    logits = (_dot(hh, wr1_ref[...]) + _dot(hl, wr2_ref[...])).T
    logit = logits[0:N_EXPERTS] + logits[N_EXPERTS:2 * N_EXPERTS] + logits[2 * N_EXPERTS:3 * N_EXPERTS]
    score = jax.nn.sigmoid(logit)
    biased = score + rb_ref[...]
    a = [biased[j * N_GROUPS:(j + 1) * N_GROUPS] for j in range(EXPERTS_PER_GROUP)]
    sc = [score[j * N_GROUPS:(j + 1) * N_GROUPS] for j in range(EXPERTS_PER_GROUP)]
    hi1, lo1 = jnp.maximum(a[0], a[1]), jnp.minimum(a[0], a[1])
    hi2, lo2 = jnp.maximum(a[2], a[3]), jnp.minimum(a[2], a[3])
    gscore = jnp.maximum(hi1, hi2) + jnp.maximum(jnp.minimum(hi1, hi2), jnp.maximum(lo1, lo2))
    gmax = jnp.max(gscore, axis=0, keepdims=True)
    giota = lax.broadcasted_iota(jnp.int32, gscore.shape, 0).astype(F32)
    gsel = jnp.min(jnp.where(gscore == gmax, giota, float(N_GROUPS)), axis=0, keepdims=True)
    onehot = giota == gsel
    pick = lambda t: jnp.sum(jnp.where(onehot, t, 0.0), axis=0, keepdims=True)
    bj = [pick(t) for t in a]
    sj = [pick(t) for t in sc]
    _, i1 = _first_index_of_max(bj)
    bj2 = [jnp.where(i1 == float(j), -jnp.inf, bj[j]) for j in range(EXPERTS_PER_GROUP)]
    _, i2 = _first_index_of_max(bj2)
    sel = lambda i: jnp.where(i == 0.0, sj[0], jnp.where(i == 1.0, sj[1], jnp.where(i == 2.0, sj[2], sj[3])))
    w1, w2 = sel(i1), sel(i2)
    den = w1 + w2
    base = gsel * float(EXPERTS_PER_GROUP)
    return ((base + i1).astype(jnp.int32), (base + i2).astype(jnp.int32)), (w1 / den, w2 / den)


def _mix_out_kernel(has_gate, *refs):
    if has_gate:
        a_ref, hs_ref, x_ref, wo_ref, g1_ref, g_ref, sh_ref, sc_ref, wr1_ref, wr2_ref, rb_ref, \
            x_out, h_out, idx_out, wts_out = refs
    else:
        a_ref, x_ref, wo_ref, g1_ref, g_ref, sh_ref, sc_ref, wr1_ref, wr2_ref, rb_ref, \
            x_out, h_out, idx_out, wts_out = refs
    tm = x_ref.shape[0]
    sub = min(ROW_SUB, tm)
    for i in range(tm // sub):
        rows = slice(i * sub, (i + 1) * sub)
        if has_gate:
            a = (a_ref[rows, :].astype(F32) * hs_ref[rows, :]).astype(BF16)
        else:
            a = a_ref[rows, :]
        x1 = x_ref[rows, :] + g1_ref[...] * _dot(a, wo_ref[...])
        x_out[rows, :] = x1
        h2 = _modulate(x1, g_ref[...], sh_ref[...], sc_ref[...])
        _to_tiles(h_out, h2, i * sub)
        idx, wts = _route(h2, wr1_ref, wr2_ref, rb_ref)
        for k in range(TOP_K):
            idx_out[k:k + 1, rows] = idx[k]
            wts_out[k:k + 1, rows] = wts[k]


def _mix_out(a, hs, x, w_o, gate1, gain, shift, scale, wr1, wr2, rbias):
    bsz, seq, d = x.shape
    tm = min(ROW_TILE, seq)
    row = lambda b, i: (b, i, 0)
    per_b = lambda b, i: (b, 0, 0)
    const = lambda b, i: (0, 0)
    full = lambda t: pl.BlockSpec(t.shape, const)
    vec = pl.BlockSpec((None, 1, d), per_b)
    acts = [a] if hs is None else [a, hs]
    return pl.pallas_call(
        functools.partial(_mix_out_kernel, hs is not None),
        grid=(bsz, seq // tm),
        in_specs=[pl.BlockSpec((None, tm, t.shape[-1]), row) for t in acts] + [
            pl.BlockSpec((None, tm, d), row), full(w_o), vec, full(gain), vec, vec,
            full(wr1), full(wr2), full(rbias),
        ],
        out_specs=[
            pl.BlockSpec((None, tm, d), row),
            pl.BlockSpec((None, tm * (d // LANES), LANES), row),
            pl.BlockSpec((None, TOP_K, tm), lambda b, i: (b, 0, i)),
            pl.BlockSpec((None, TOP_K, tm), lambda b, i: (b, 0, i)),
        ],
        out_shape=[
            jax.ShapeDtypeStruct((bsz, seq, d), F32),
            jax.ShapeDtypeStruct((bsz, seq * (d // LANES), LANES), F32),
            jax.ShapeDtypeStruct((bsz, TOP_K, seq), jnp.int32),
            jax.ShapeDtypeStruct((bsz, TOP_K, seq), F32),
        ],
        compiler_params=_params("arbitrary", "arbitrary"),
        name="mix_out_route",
    )(*acts, x, w_o, gate1, gain, shift, scale, wr1, wr2, rbias)


N_SUB = D_MODEL // LANES
TABLE_CHUNK = 512


def _to_tiles(ref, val, lo=0):
    n = val.shape[0]
    for s in range(N_SUB):
        ref[pl.ds(lo * N_SUB + s, n, stride=N_SUB), :] = val[:, s * LANES:(s + 1) * LANES]


def _from_tiles(ref, lo, n):
    return jnp.concatenate([ref[pl.ds(lo * N_SUB + s, n, stride=N_SUB), :] for s in range(N_SUB)], axis=1)


def _tables_kernel(idx_ref, rank_ref, cnt_ref, carry):
    @pl.when(pl.program_id(0) == 0)
    def _():
        carry[...] = jnp.zeros_like(carry)

    seq = idx_ref.shape[-1]
    ch = min(TABLE_CHUNK, seq)
    tri = jnp.where(lax.broadcasted_iota(jnp.int32, (ch, ch), 0) <= lax.broadcasted_iota(jnp.int32, (ch, ch), 1),
                    1.0, 0.0).astype(BF16)
    eiota = lax.broadcasted_iota(jnp.int32, (N_EXPERTS, ch), 0)
    cnt = carry[...]
    for k in range(TOP_K):
        for c in range(seq // ch):
            sel = eiota == idx_ref[k:k + 1, c * ch:(c + 1) * ch]
            pref = _dot(jnp.where(sel, 1.0, 0.0).astype(BF16), tri) + cnt
            rank = jnp.sum(jnp.where(sel, pref, 0.0), axis=0, keepdims=True) - 1.0
            rank_ref[k:k + 1, c * ch:(c + 1) * ch] = rank.astype(jnp.int32)
            cnt = pref[:, ch - 1:ch]
    carry[...] = cnt
    cnt_ref[...] = jnp.broadcast_to(cnt, cnt_ref.shape)


def _tables(idx):
    bsz, _, seq = idx.shape
    return pl.pallas_call(
        _tables_kernel,
        grid=(bsz,),
        in_specs=[pl.BlockSpec((None, TOP_K, seq), lambda b: (b, 0, 0))],
        out_specs=[pl.BlockSpec((None, TOP_K, seq), lambda b: (b, 0, 0)),
                   pl.BlockSpec((N_EXPERTS, LANES), lambda b: (0, 0))],
        out_shape=[jax.ShapeDtypeStruct((bsz, TOP_K, seq), jnp.int32),
                   jax.ShapeDtypeStruct((N_EXPERTS, LANES), F32)],
        scratch_shapes=[pltpu.VMEM((N_EXPERTS, 1), F32)],
        compiler_params=_params("arbitrary"),
        name="moe_tables",
    )(idx)


def _scatter_kernel(dest_ref, pad_ref, src_ref, zero_ref, dst_hbm, sem):
    tm = src_ref.shape[0]
    n_pad = pad_ref.shape[-1]
    for k in range(TOP_K):
        for c in range(tm // LANES):
            def start(j, carry, k=k, c=c):
                row = dest_ref[0, k * tm + c * LANES + j]
                pltpu.make_async_copy(src_ref.at[c * LANES + j], dst_hbm.at[row], sem).start(priority=k)
                return carry
            lax.fori_loop(0, LANES, start, 0, unroll=8)

    def fill(r, carry):
        pltpu.make_async_copy(zero_ref.at[0], dst_hbm.at[pad_ref[0, r]], sem).start()
        return carry
    lax.fori_loop(0, n_pad, fill, 0, unroll=8)

    left = TOP_K * tm + n_pad
    while left > 0:
        cnt = min(tm, left)
        pltpu.make_async_copy(src_ref.at[pl.ds(0, cnt)], dst_hbm.at[pl.ds(0, cnt)], sem).wait()
        left -= cnt


def _index_blocks(table, tm):
    bsz, _, seq = table.shape
    nt = seq // tm
    t = table.reshape(bsz, TOP_K, nt, tm).transpose(0, 2, 1, 3)
    return t.reshape(bsz * nt, 1, TOP_K * tm), (None, 1, TOP_K * tm)


def _scatter(dest, pad_rows, h2t, n_rows):
    bsz, _, seq = dest.shape
    tm = min(ROW_TILE, seq)
    nt = seq // tm
    n_pad = pad_rows.shape[0] // (bsz * nt)
    assert n_pad * bsz * nt == pad_rows.shape[0]
    dest4, dest_block = _index_blocks(dest, tm)
    return pl.pallas_call(
        _scatter_kernel,
        grid=(bsz, nt),
        in_specs=[
            pl.BlockSpec(dest_block, lambda b, i: (b * nt + i, 0, 0), memory_space=pltpu.SMEM),
            pl.BlockSpec((None, 1, n_pad), lambda b, i: (b * nt + i, 0, 0), memory_space=pltpu.SMEM),
            pl.BlockSpec((tm, N_SUB, LANES), lambda b, i: (b * nt + i, 0, 0)),
            pl.BlockSpec((1, N_SUB, LANES), lambda b, i: (0, 0, 0)),
        ],
        out_specs=pl.BlockSpec(memory_space=pl.ANY),
        out_shape=jax.ShapeDtypeStruct((n_rows, N_SUB, LANES), F32),
        scratch_shapes=[pltpu.SemaphoreType.DMA(())],
        compiler_params=_params("arbitrary", "arbitrary"),
        name="moe_scatter",
    )(dest4, pad_rows.reshape(bsz * nt, 1, n_pad), h2t.reshape(bsz * seq, N_SUB, LANES),
      jnp.zeros((1, N_SUB, LANES), F32))


def _expert_kernel(blk_exp_ref, blk_first_ref, n_used_ref,
                   xs_ref, wgu_ref, wdn_ref, ys_ref, wgu_bf, wdn_bf):
    i = pl.program_id(0)

    @pl.when(i < n_used_ref[0])
    def _():
        @pl.when(blk_first_ref[i] == 1)
        def _():
            wgu_bf[...] = wgu_ref[...].astype(BF16)
            wdn_bf[...] = wdn_ref[...].astype(BF16)

        x = _from_tiles(xs_ref, 0, MOE_TILE).astype(BF16)
        gu = _dot(x, wgu_bf[...])
        g = gu[:, :D_EXPERT]
        u = gu[:, D_EXPERT:]
        mid = (g * jax.nn.sigmoid(g) * u).astype(BF16)
        _to_tiles(ys_ref, _dot(mid, wdn_bf[...]))

    @pl.when(i >= n_used_ref[0])
    def _():
        ys_ref[...] = jnp.zeros_like(ys_ref)


def _experts(blk_exp, blk_first, n_used, xs, w_gu, w_dn, layer):
    d = D_MODEL
    nb = xs.shape[0] // (MOE_TILE * N_SUB)
    tile = lambda i, *_: (i, 0)
    grid_spec = pltpu.PrefetchScalarGridSpec(
        num_scalar_prefetch=3,
        grid=(nb,),
        in_specs=[
            pl.BlockSpec((MOE_TILE * N_SUB, LANES), tile),
            pl.BlockSpec((None, None, d, 2 * D_EXPERT), lambda i, be, *_: (layer, be[i], 0, 0)),
            pl.BlockSpec((None, None, D_EXPERT, d), lambda i, be, *_: (layer, be[i], 0, 0)),
        ],
        out_specs=pl.BlockSpec((MOE_TILE * N_SUB, LANES), tile),
        scratch_shapes=[pltpu.VMEM((d, 2 * D_EXPERT), BF16), pltpu.VMEM((D_EXPERT, d), BF16)],
    )
    return pl.pallas_call(
        _expert_kernel,
        grid_spec=grid_spec,
        out_shape=jax.ShapeDtypeStruct(xs.shape, F32),
        compiler_params=_params("arbitrary"),
        name="moe_experts",
    )(blk_exp, blk_first, n_used, xs, w_gu, w_dn)


def _combine_kernel(dcur_ref, dnxt_ref, ys_hbm, x_ref, wts_ref, g2_ref, x_out, buf, sem):
    nt = pl.num_programs(1)
    n = pl.program_id(0) * nt + pl.program_id(1)
    total = pl.num_programs(0) * nt
    tm = x_ref.shape[0]
    slot = lax.rem(n, 2)

    def issue(d_ref, sl):
        for k in range(TOP_K):
            for c in range(tm // LANES):
                def start(j, carry, k=k, c=c):
                    src = ys_hbm.at[pl.ds(pl.multiple_of(d_ref[0, k * tm + c * LANES + j], N_SUB), N_SUB)]
                    dst = buf.at[sl, pl.ds(pl.multiple_of((k * tm + c * LANES + j) * N_SUB, N_SUB), N_SUB)]
                    pltpu.make_async_copy(src, dst, sem.at[sl]).start(priority=k)
                    return carry
                lax.fori_loop(0, LANES, start, 0, unroll=8)

    @pl.when(n == 0)
    def _():
        issue(dcur_ref, 0)

    @pl.when(n + 1 < total)
    def _():
        issue(dnxt_ref, 1 - slot)

    pltpu.make_async_copy(ys_hbm.at[pl.ds(0, TOP_K * tm * N_SUB)], buf.at[slot], sem.at[slot]).wait()
    cur = buf.at[slot]
    w = wts_ref[...]
    y = w[:, 0:1] * _from_tiles(cur, 0, tm) + w[:, 1:2] * _from_tiles(cur, tm, tm)
    x_out[...] = x_ref[...] + g2_ref[...] * y


def _combine(dest8, ys, x, wts_col, gate2):
    bsz, seq, d = x.shape
    tm = min(ROW_TILE, seq)
    nt = seq // tm

    def nxt(b, i):
        return (jnp.minimum(b * nt + i + 1, bsz * nt - 1), 0, 0)

    dest8, dest_block = _index_blocks(dest8, tm)
    return pl.pallas_call(
        _combine_kernel,
        grid=(bsz, nt),
        in_specs=[
            pl.BlockSpec(dest_block, lambda b, i: (b * nt + i, 0, 0), memory_space=pltpu.SMEM),
            pl.BlockSpec(dest_block, nxt, memory_space=pltpu.SMEM),
            pl.BlockSpec(memory_space=pl.ANY),
            pl.BlockSpec((None, tm, d), lambda b, i: (b, i, 0)),
            pl.BlockSpec((None, tm, TOP_K), lambda b, i: (b, i, 0)),
            pl.BlockSpec((None, 1, d), lambda b, i: (b, 0, 0)),
        ],
        out_specs=pl.BlockSpec((None, tm, d), lambda b, i: (b, i, 0)),
        out_shape=jax.ShapeDtypeStruct((bsz, seq, d), F32),
        scratch_shapes=[pltpu.VMEM((2, TOP_K * tm * N_SUB, LANES), F32), pltpu.SemaphoreType.DMA((2,))],
        compiler_params=_params("arbitrary", "arbitrary"),
        name="moe_combine",
    )(dest8, dest8, ys, x, wts_col, gate2)


def _lookup(table, keys):
    hit = keys[..., None] == jnp.arange(table.shape[0], dtype=jnp.int32)
    return jnp.sum(jnp.where(hit, table, 0), axis=-1).astype(jnp.int32)


def _count_le(bounds, q):
    return jnp.sum((bounds <= q[..., None]).astype(jnp.int32), axis=-1)


def _moe(h2t, idx, wts, x, gate2, w_gu, w_dn, layer):
    bsz, seq, _ = x.shape
    n_rows = bsz * seq * TOP_K + N_EXPERTS * MOE_TILE
    nb = n_rows // MOE_TILE
    rank, cnt = _tables(idx)
    counts = cnt[:, 0].astype(jnp.int32)
    padded = ((counts + MOE_TILE - 1) // MOE_TILE) * MOE_TILE
    pend = jnp.cumsum(padded)
    pstart = pend - padded
    dest = _lookup(pstart, idx) + rank
    blk_row = jnp.arange(nb, dtype=jnp.int32) * MOE_TILE
    blk_exp = jnp.minimum(_count_le(pend, blk_row), N_EXPERTS - 1)
    blk_first = (blk_row == _lookup(pstart, blk_exp)).astype(jnp.int32)
    n_used = (pend[-1:] // MOE_TILE).astype(jnp.int32)
    slack_len = jnp.concatenate([padded - counts, n_rows - pend[-1:]])
    slack_row = jnp.concatenate([pstart + counts, pend[-1:]])
    slack_end = jnp.cumsum(slack_len)
    q = jnp.arange(N_EXPERTS * MOE_TILE, dtype=jnp.int32)
    seg = _count_le(slack_end, q)
    pad_rows = _lookup(slack_row - (slack_end - slack_len), seg) + q
    xs = _scatter(dest, pad_rows, h2t, n_rows)
    ys = _experts(blk_exp, blk_first, n_used, xs.reshape(n_rows * N_SUB, LANES), w_gu, w_dn, layer)
    return _combine(dest * N_SUB, ys, x, wts.transpose(0, 2, 1), gate2)


def _rnn_in_kernel(x_ref, g_ref, sh_ref, sc_ref, w_ref, gate_out, xb_out):
    tm = x_ref.shape[0]
    sub = min(ROW_SUB, tm)
    for i in range(tm // sub):
        rows = slice(i * sub, (i + 1) * sub)
        h = _modulate(x_ref[rows, :], g_ref[...], sh_ref[...], sc_ref[...])
        u = _dot(h.astype(BF16), w_ref[...])
        gate_out[rows, :] = jax.nn.gelu(u[:, :D_RNN]).astype(BF16)
        xb_out[rows, :] = u[:, D_RNN:]


def _rnn_in(x, gain, shift, scale, w_in):
    bsz, seq, d = x.shape
    tm = min(ROW_TILE, seq)
    row = lambda b, i: (b, i, 0)
    per_b = lambda b, i: (b, 0, 0)
    const = lambda b, i: (0, 0)
    return pl.pallas_call(
        _rnn_in_kernel,
        grid=(bsz, seq // tm),
        in_specs=[
            pl.BlockSpec((None, tm, d), row),
            pl.BlockSpec(gain.shape, const),
            pl.BlockSpec((None, 1, d), per_b),
            pl.BlockSpec((None, 1, d), per_b),
            pl.BlockSpec(w_in.shape, const),
        ],
        out_specs=[pl.BlockSpec((None, tm, D_RNN), row), pl.BlockSpec((None, tm, D_RNN), row)],
        out_shape=[jax.ShapeDtypeStruct((bsz, seq, D_RNN), BF16),
                   jax.ShapeDtypeStruct((bsz, seq, D_RNN), F32)],
        compiler_params=_params("arbitrary", "arbitrary"),
        name="rnn_in",
    )(x, gain, shift, scale, w_in)


def _lru_kernel(xb_ref, cw_ref, cb_ref, wcat_ref, bcat_ref, lam_ref, hs_ref,
                xi_ref, af_ref, bf_ref, ab_ref, bb_ref, hf_ref, hb_ref, sum_ref):
    seq, c = xb_ref.shape
    seg_len = seq // SUBLANES
    n_slab = c // LANES
    n_rows = seg_len * SUBLANES
    halo = (CONV_W // 2) * SUBLANES
    row = lax.broadcasted_iota(jnp.int32, (SUBLANES, LANES), 0)
    for sl in range(n_slab):
        lanes = slice(sl * LANES, (sl + 1) * LANES)
        for g in range(SUBLANES):
            xi_ref[sl, pl.ds(halo + g, seg_len, stride=SUBLANES), :] = xb_ref[g * seg_len:(g + 1) * seg_len, lanes]
        for back in (1, 2):
            prev = xi_ref[sl, halo + (seg_len - back) * SUBLANES:halo + (seg_len - back + 1) * SUBLANES, :]
            xi_ref[sl, halo - back * SUBLANES:halo - (back - 1) * SUBLANES, :] = jnp.where(
                row == 0, 0.0, pltpu.roll(prev, 1, 0))
        nxt = xi_ref[sl, halo:halo + SUBLANES, :]
        xi_ref[sl, halo + n_rows:halo + n_rows + SUBLANES, :] = jnp.where(
            row == SUBLANES - 1, 0.0, pltpu.roll(nxt, SUBLANES - 1, 0))

    cw = cw_ref[...]
    cb = cb_ref[...]
    lam = lam_ref[...]
    neg = -lam
    softplus = jnp.maximum(neg, 0.0) + jnp.log1p(jnp.exp(-jnp.abs(neg)))
    half_rate = (-0.5 * LRU_C) * softplus
    rows = min(SCAN_ROWS, n_rows)
    n_chunks = n_rows // rows

    for ci in range(n_chunks):
        i0 = ci * rows
        taps = []
        for k in range(CONV_W):
            lo = halo + i0 + (k - CONV_W // 2) * SUBLANES
            taps.append(jnp.concatenate([xi_ref[sl, lo:lo + rows, :] for sl in range(n_slab)], axis=1))
        xc = cb
        for k in range(CONV_W):
            xc = xc + taps[k] * cw[k:k + 1, :]
        xcb = xc.astype(BF16)
        xh = 0.5 * xc
        for dirn, (a_ref, b_ref) in enumerate(((af_ref, bf_ref), (ab_ref, bb_ref))):
            cols = slice(2 * dirn * c, 2 * (dirn + 1) * c)
            th = jnp.tanh(_dot(xcb, wcat_ref[:, cols]) + bcat_ref[:, cols])
            hr = half_rate[dirn:dirn + 1, :]
            log_a = hr * th[:, :c] + hr
            a = jnp.exp(log_a)
            mult = jnp.sqrt(jnp.tanh(-log_a) * (a * a + 1.0))
            if dirn == 0 and ci == 0:
                mult = jnp.where(lax.broadcasted_iota(jnp.int32, mult.shape, 0) == 0, 1.0, mult)
            if dirn == 1 and ci == n_chunks - 1:
                mult = jnp.where(lax.broadcasted_iota(jnp.int32, mult.shape, 0) == rows - 1, 1.0, mult)
            b = mult * (th[:, c:] + 1.0) * xh
            for sl in range(n_slab):
                a_ref[sl, i0:i0 + rows, :] = a[:, sl * LANES:(sl + 1) * LANES]
                b_ref[sl, i0:i0 + rows, :] = b[:, sl * LANES:(sl + 1) * LANES]

    def step_rows(cidx):
        fwd = pl.ds(pl.multiple_of(cidx * SUBLANES, SUBLANES), SUBLANES)
        bwd = pl.ds(pl.multiple_of((seg_len - 1 - cidx) * SUBLANES, SUBLANES), SUBLANES)
        return fwd, bwd

    zero = jnp.zeros((SUBLANES, LANES), F32)
    one = jnp.ones((SUBLANES, LANES), F32)

    def totals(cidx, carry):
        fwd, bwd = step_rows(cidx)
        out = []
        for sl in range(n_slab):
            hf, pf, hb, pb = carry[sl]
            af, ab = af_ref[sl, fwd, :], ab_ref[sl, bwd, :]
            out.append((af * hf + bf_ref[sl, fwd, :], af * pf, ab * hb + bb_ref[sl, bwd, :], ab * pb))
        return tuple(out)
    tot = lax.fori_loop(0, seg_len, totals, tuple((zero, one, zero, one) for _ in range(n_slab)), unroll=8)

    enter = []
    for sl in range(n_slab):
        hf, pf, hb, pb = tot[sl]
        cf, cbk = zero, zero
        for _ in range(SUBLANES - 1):
            cf = jnp.where(row == 0, 0.0, pltpu.roll(hf + pf * cf, 1, 0))
            cbk = jnp.where(row == SUBLANES - 1, 0.0, pltpu.roll(hb + pb * cbk, SUBLANES - 1, 0))
        enter.append((cf, cbk))

    def states(meet, cidx, carry):
        fwd, bwd = step_rows(cidx)
        out = []
        for sl in range(n_slab):
            hf, hb = carry[sl]
            hf = af_ref[sl, fwd, :] * hf + bf_ref[sl, fwd, :]
            hb = ab_ref[sl, bwd, :] * hb + bb_ref[sl, bwd, :]
            if meet:
                sum_ref[sl, fwd, :] = hf + hb_ref[sl, fwd, :]
                sum_ref[sl, bwd, :] = hb + hf_ref[sl, bwd, :]
            else:
                hf_ref[sl, fwd, :] = hf
                hb_ref[sl, bwd, :] = hb
            out.append((hf, hb))
        return tuple(out)
    mid = lax.fori_loop(0, seg_len // 2, functools.partial(states, False), tuple(enter), unroll=8)
    lax.fori_loop(seg_len // 2, seg_len, functools.partial(states, True), mid, unroll=8)

    for g in range(SUBLANES):
        for sl in range(n_slab):
            hs_ref[g * seg_len:(g + 1) * seg_len, sl * LANES:(sl + 1) * LANES] = (
                sum_ref[sl, pl.ds(g, seg_len, stride=SUBLANES), :])


def _lru(xb, conv_w, conv_b, wcat, bcat, lam):
    bsz, seq, _ = xb.shape
    c = RNN_BW
    blk = lambda b, n: (b, 0, n)
    return pl.pallas_call(
        _lru_kernel,
        grid=(bsz, RNN_BLOCKS),
        in_specs=[
            pl.BlockSpec((None, seq, c), blk),
            pl.BlockSpec((CONV_W, c), lambda b, n: (0, n)),
            pl.BlockSpec((1, c), lambda b, n: (0, n)),
            pl.BlockSpec((None, c, 4 * c), lambda b, n: (n, 0, 0)),
            pl.BlockSpec((None, 1, 4 * c), lambda b, n: (n, 0, 0)),
            pl.BlockSpec((2, c), lambda b, n: (0, n)),
        ],
        out_specs=pl.BlockSpec((None, seq, c), blk),
        out_shape=jax.ShapeDtypeStruct((bsz, seq, D_RNN), F32),
        scratch_shapes=[pltpu.VMEM((c // LANES, seq + (CONV_W - 1) * SUBLANES, LANES), F32)]
        + [pltpu.VMEM((c // LANES, seq, LANES), F32)] * 7,
        compiler_params=_params("arbitrary", "arbitrary"),
        name="rglru_scan",
    )(xb, conv_w, conv_b, wcat, bcat, lam)


def _mla_weights(w_in, w_q_b, w_kv_b, q_norm, k_norm):
    half = QK_ROPE // 2

    def slab(t):
        return jnp.pad(t, [(0, 0)] * (t.ndim - 1) + [(0, LANES - QK_HEAD)])

    def rot_slab(t):
        rope = t[..., QK_NOPE:]
        swapped = jnp.concatenate([jnp.zeros_like(t[..., :QK_NOPE]), rope[..., half:], rope[..., :half]], axis=-1)
        return slab(swapped)

    kpe = jnp.pad(w_in[:, Q_LORA + KV_LORA:], ((0, 0), (QK_NOPE, 0)))
    w_in_p = jnp.concatenate([w_in[:, :Q_LORA + KV_LORA], slab(kpe), rot_slab(kpe)], axis=1).astype(BF16)
    wq = w_q_b.reshape(Q_LORA, N_HEADS, QK_HEAD)
    wq_p = slab(wq).reshape(Q_LORA, N_HEADS * LANES).astype(BF16)
    wq_rot = rot_slab(wq).reshape(Q_LORA, N_HEADS * LANES).astype(BF16)
    wkv = w_kv_b.reshape(KV_LORA, N_HEADS, QK_NOPE + V_HEAD)
    wk = jnp.pad(wkv[:, :, :QK_NOPE], ((0, 0), (0, 0), (0, LANES - QK_NOPE))).reshape(KV_LORA, N_HEADS * LANES)
    wv = wkv[:, :, QK_NOPE:].reshape(KV_LORA, N_HEADS * V_HEAD)
    w_kv_p = jnp.concatenate([wk, wv], axis=1).astype(BF16)
    gains = lambda g: jnp.stack([slab(g), rot_slab(g)], axis=0)
    return w_in_p, wq_p, wq_rot, w_kv_p, gains(q_norm), gains(k_norm)


def _rope_tables(positions):
    half = QK_ROPE // 2
    inv_freq = ROPE_THETA ** (-jnp.arange(half, dtype=F32) / half)
    ang = positions.astype(F32)[..., None] * inv_freq
    cos, sin = jnp.cos(ang), jnp.sin(ang)
    lead = positions.shape + (QK_NOPE,)
    tail = positions.shape + (LANES - QK_HEAD,)
    cos_t = jnp.concatenate([jnp.ones(lead, F32), cos, cos, jnp.ones(tail, F32)], axis=-1)
    sin_t = jnp.concatenate([jnp.zeros(lead, F32), -sin, sin, jnp.zeros(tail, F32)], axis=-1)
    return cos_t, sin_t


def _router_weights(w_router, router_bias):
    perm = (jnp.arange(N_EXPERTS) % N_GROUPS) * EXPERTS_PER_GROUP + jnp.arange(N_EXPERTS) // N_GROUPS
    w = w_router[:, perm]
    hi = w.astype(BF16)
    lo = (w - hi.astype(F32)).astype(BF16)
    z = jnp.zeros_like(hi)
    wr1 = jnp.concatenate([hi, lo, z, z], axis=1)
    wr2 = jnp.concatenate([z, z, hi, z], axis=1)
    return wr1, wr2, router_bias[perm].reshape(N_EXPERTS, 1).astype(F32)


def kernel(x, c, positions, norm_mix, norm_ffn, w_ada, b_ada, mla_w_in, mla_q_a_norm, mla_kv_a_norm, mla_w_q_b, mla_w_kv_b, mla_q_norm, mla_k_norm, mla_w_o, rnn_w_in, rnn_conv_w, rnn_conv_b, rnn_lam_f, rnn_w_rf, rnn_b_rf, rnn_w_if, rnn_b_if, rnn_lam_b, rnn_w_rb, rnn_b_rb, rnn_w_ib, rnn_b_ib, rnn_w_o, w_router, router_bias, moe_w_gu, moe_w_dn):
    bsz, seq, d = x.shape
    depth = w_ada.shape[0]
    mod = _ada(c, w_ada, b_ada)
    wr1, wr2, rbias = _router_weights(w_router, router_bias)
    cos_t, sin_t = _rope_tables(positions)
    vec = lambda v: v.reshape(1, -1)
    for i in range(depth):
        sh1, sc1, g1, sh2, sc2, g2 = [mod[i, :, k * d:(k + 1) * d].reshape(bsz, 1, d) for k in range(6)]
        j = i // 2
        if i % 2 == 0:
            w_in_p, wq, wq_rot, wkv, qn, kn = _mla_weights(mla_w_in[j], mla_w_q_b[j], mla_w_kv_b[j],
                                                           mla_q_norm[j], mla_k_norm[j])
            q, k, v = _mla_in(x, vec(norm_mix[i]), sh1, sc1, w_in_p, vec(mla_q_a_norm[j]),
                              vec(mla_kv_a_norm[j]), wq, wq_rot, wkv, qn, kn, cos_t, sin_t)
            a = _attention(q, k, v)
            hs = None
            w_o = mla_w_o[j].astype(BF16)
        else:
            a, xb = _rnn_in(x, vec(norm_mix[i]), sh1, sc1, rnn_w_in[j].astype(BF16))
            wcat = (0.5 * jnp.concatenate([rnn_w_rf[j], rnn_w_if[j], rnn_w_rb[j], rnn_w_ib[j]], axis=-1)).astype(BF16)
            bcat = jnp.stack([b.reshape(RNN_BLOCKS, RNN_BW) for b in
                              (rnn_b_rf[j], rnn_b_if[j], rnn_b_rb[j], rnn_b_ib[j])], axis=1)
            bcat = 0.5 * bcat.reshape(RNN_BLOCKS, 1, 4 * RNN_BW)
            lam = jnp.stack([rnn_lam_f[j], rnn_lam_b[j]], axis=0)
            hs = _lru(xb, rnn_conv_w[j], vec(rnn_conv_b[j]), wcat, bcat, lam)
            w_o = rnn_w_o[j].astype(BF16)
        x, h2, idx, wts = _mix_out(a, hs, x, w_o, g1, vec(norm_ffn[i]), sh2, sc2, wr1, wr2, rbias)
        x = _moe(h2, idx, wts, x, g2, moe_w_gu, moe_w_dn, i)
    return x
```

```python
import functools

import jax
import jax.numpy as jnp
from jax import lax
from jax.experimental import pallas as pl
from jax.experimental.pallas import tpu as pltpu

F32 = jnp.float32
BF16 = jnp.bfloat16

D_MODEL = 1024
N_HEADS = 16
Q_LORA = 384
KV_LORA = 256
QK_NOPE = 64
QK_ROPE = 32
QK_HEAD = QK_NOPE + QK_ROPE
V_HEAD = 64
ROPE_THETA = 10000.0
D_RNN = D_MODEL
RNN_BLOCKS = 4
RNN_BW = D_RNN // RNN_BLOCKS
CONV_W = 4
LRU_C = 8.0
N_EXPERTS = 32
N_GROUPS = 8
EXPERTS_PER_GROUP = N_EXPERTS // N_GROUPS
TOP_K = 2
D_EXPERT = 512
EPS = 1e-6
LOG2_E = 1.4426950408889634

LANES = 128
SUBLANES = 8
VMEM_LIMIT = 52 * 1024 * 1024

ROW_TILE = 512
ROW_SUB = 256
Q_TILE = 2048
Q_SUB = 256
MOE_TILE = 512
SCAN_ROWS = 256


def _dot(a, b):
    return jnp.dot(a, b, preferred_element_type=F32)


def _split_bf16(a):
    hi = a.astype(BF16)
    lo = (a - hi.astype(F32)).astype(BF16)
    return hi, lo


def _dot_split(a, b):
    ah, al = _split_bf16(a)
    bh, bl = _split_bf16(b)
    return _dot(ah, bh) + (_dot(ah, bl) + _dot(al, bh))


def _rms(x, gain, n):
    ms = jnp.sum(x * x, axis=-1, keepdims=True) * (1.0 / n)
    return x * lax.rsqrt(ms + EPS) * gain


def _modulate(x, gain, shift, scale):
    return _rms(x, gain, x.shape[-1]) * (1.0 + scale) + shift


def _params(*sem):
    return pltpu.CompilerParams(dimension_semantics=sem, vmem_limit_bytes=VMEM_LIMIT)


def _ada_kernel(c_ref, w_ref, b_ref, o_ref):
    c = c_ref[...]
    o_ref[...] = _dot_split(c * jax.nn.sigmoid(c), w_ref[...]) + b_ref[...]


def _ada(c, w_ada, b_ada):
    depth, d, n = w_ada.shape
    bsz = c.shape[0]
    tn = 1536
    return pl.pallas_call(
        _ada_kernel,
        grid=(depth, n // tn),
        in_specs=[
            pl.BlockSpec((bsz, d), lambda l, j: (0, 0)),
            pl.BlockSpec((None, d, tn), lambda l, j: (l, 0, j)),
            pl.BlockSpec((None, 1, tn), lambda l, j: (l, 0, j)),
        ],
        out_specs=pl.BlockSpec((None, bsz, tn), lambda l, j: (l, 0, j)),
        out_shape=jax.ShapeDtypeStruct((depth, bsz, n), F32),
        compiler_params=_params("arbitrary", "arbitrary"),
        name="adaln_mod",
    )(c, w_ada, b_ada.reshape(depth, 1, n))


def _head_scale(s):
    return lax.rsqrt(jnp.sum(s * s, axis=-1, keepdims=True) * (1.0 / QK_HEAD) + EPS)


def _mla_in_kernel(x_ref, g_ref, sh_ref, sc_ref, win_ref, qan_ref, kvan_ref, wq_ref, wqr_ref, wkv_ref,
                   qn_ref, kn_ref, cos_ref, sin_ref, q_out, k_out, v_out):
    h = _modulate(x_ref[...], g_ref[...], sh_ref[...], sc_ref[...])
    lat = _dot(h.astype(BF16), win_ref[...])
    q_lat = lat[:, :Q_LORA]
    kv_lat = lat[:, Q_LORA:Q_LORA + KV_LORA]
    kpe = lat[:, Q_LORA + KV_LORA:Q_LORA + KV_LORA + LANES]
    kpe_rot = lat[:, Q_LORA + KV_LORA + LANES:]
    qn = _rms(q_lat, qan_ref[...], Q_LORA).astype(BF16)
    q_all = _dot(qn, wq_ref[...])
    q_rot = _dot(qn, wqr_ref[...])
    kv_all = _dot(_rms(kv_lat, kvan_ref[...], KV_LORA).astype(BF16), wkv_ref[...])
    cos_t = cos_ref[...]
    sin_t = sin_ref[...]
    q_scale = LOG2_E * QK_HEAD ** -0.5
    cq = cos_t * (qn_ref[0:1, :] * q_scale)
    sq = sin_t * (qn_ref[1:2, :] * q_scale)
    ck = cos_t * kn_ref[0:1, :]
    k_rot_term = kpe_rot * (sin_t * kn_ref[1:2, :])
    for hh in range(N_HEADS):
        sl = slice(hh * LANES, (hh + 1) * LANES)
        s = q_all[:, sl]
        q_out[hh] = ((s * cq + q_rot[:, sl] * sq) * _head_scale(s)).astype(BF16)
        s = kv_all[:, sl] + kpe
        k_out[hh] = ((s * ck + k_rot_term) * _head_scale(s)).astype(BF16)
    v_out[...] = kv_all[:, N_HEADS * LANES:].astype(BF16)


def _mla_in(x, gain, shift, scale, w_in, q_a_norm, kv_a_norm, w_q, w_q_rot, w_kv, q_norm, k_norm, cos_t, sin_t):
    bsz, seq, d = x.shape
    tm = min(ROW_TILE, seq)
    row = lambda b, i: (b, i, 0)
    per_b = lambda b, i: (b, 0, 0)
    const = lambda b, i: (0, 0)
    full = lambda a: pl.BlockSpec(a.shape, const)
    return pl.pallas_call(
        _mla_in_kernel,
        grid=(bsz, seq // tm),
        in_specs=[
            pl.BlockSpec((None, tm, d), row),
            full(gain),
            pl.BlockSpec((None, 1, d), per_b),
            pl.BlockSpec((None, 1, d), per_b),
            full(w_in), full(q_a_norm), full(kv_a_norm), full(w_q), full(w_q_rot), full(w_kv),
            full(q_norm), full(k_norm),
            pl.BlockSpec((None, tm, LANES), row),
            pl.BlockSpec((None, tm, LANES), row),
        ],
        out_specs=[
            pl.BlockSpec((None, N_HEADS, tm, LANES), lambda b, i: (b, 0, i, 0)),
            pl.BlockSpec((None, N_HEADS, tm, LANES), lambda b, i: (b, 0, i, 0)),
            pl.BlockSpec((None, tm, N_HEADS * V_HEAD), row),
        ],
        out_shape=[
            jax.ShapeDtypeStruct((bsz, N_HEADS, seq, LANES), BF16),
            jax.ShapeDtypeStruct((bsz, N_HEADS, seq, LANES), BF16),
            jax.ShapeDtypeStruct((bsz, seq, N_HEADS * V_HEAD), BF16),
        ],
        compiler_params=_params("arbitrary", "arbitrary"),
        name="mla_in",
    )(x, gain, shift, scale, w_in, q_a_norm, kv_a_norm, w_q, w_q_rot, w_kv, q_norm, k_norm, cos_t, sin_t)


def _attn_kernel(q_ref, k_ref, v_ref, o_ref):
    v = v_ref[...]
    lane_v = lax.broadcasted_iota(jnp.int32, v.shape, 1)
    v_heads = [jnp.where(lane_v < V_HEAD, v, jnp.ones((), BF16)), jnp.where(lane_v >= V_HEAD, v, jnp.ones((), BF16))]
    lane = lax.broadcasted_iota(jnp.int32, (Q_SUB, LANES), 1)
    for i in range(q_ref.shape[1] // Q_SUB):
        rows = slice(i * Q_SUB, (i + 1) * Q_SUB)
        outs = []
        for j in range(2):
            s = lax.dot_general(q_ref[j, rows, :], k_ref[j], (((1,), (1,)), ((), ())),
                                preferred_element_type=F32)
            m = jnp.max(s, axis=-1, keepdims=True)
            o = _dot(jnp.exp2(s - m).astype(BF16), v_heads[j])
            denom = o[:, V_HEAD:V_HEAD + 1] if j == 0 else o[:, 0:1]
            outs.append(o / denom)
        o_ref[rows, :] = jnp.where(lane < V_HEAD, outs[0], outs[1]).astype(BF16)


def _attention(q, k, v):
    bsz, _, seq, _ = q.shape
    tq = min(Q_TILE, seq)
    assert tq % Q_SUB == 0
    return pl.pallas_call(
        _attn_kernel,
        grid=(bsz, N_HEADS // 2, seq // tq),
        in_specs=[
            pl.BlockSpec((None, 2, tq, LANES), lambda b, h, i: (b, h, i, 0)),
            pl.BlockSpec((None, 2, seq, LANES), lambda b, h, i: (b, h, 0, 0)),
            pl.BlockSpec((None, seq, LANES), lambda b, h, i: (b, 0, h)),
        ],
        out_specs=pl.BlockSpec((None, tq, LANES), lambda b, h, i: (b, i, h)),
        out_shape=jax.ShapeDtypeStruct((bsz, seq, N_HEADS * V_HEAD), BF16),
        compiler_params=_params("arbitrary", "arbitrary", "arbitrary"),
        name="mla_attention",
    )(q, k, v)


def _first_index_of_max(vals):
    m = vals[0]
    for v in vals[1:]:
        m = jnp.maximum(m, v)
    idx = jnp.full(m.shape, float(len(vals) - 1), F32)
    for j in range(len(vals) - 2, -1, -1):
        idx = jnp.where(vals[j] == m, float(j), idx)
    return m, idx


def _route(h2, wr1_ref, wr2_ref, rb_ref):
    hh, hl = _split_bf16(h2)
    logits = (_dot(hh, wr1_ref[...]) + _dot(hl, wr2_ref[...])).T
    logit = logits[0:N_EXPERTS] + logits[N_EXPERTS:2 * N_EXPERTS] + logits[2 * N_EXPERTS:3 * N_EXPERTS]
    score = jax.nn.sigmoid(logit)
    biased = score + rb_ref[...]
    a = [biased[j * N_GROUPS:(j + 1) * N_GROUPS] for j in range(EXPERTS_PER_GROUP)]
    sc = [score[j * N_GROUPS:(j + 1) * N_GROUPS] for j in range(EXPERTS_PER_GROUP)]
    hi1, lo1 = jnp.maximum(a[0], a[1]), jnp.minimum(a[0], a[1])
    hi2, lo2 = jnp.maximum(a[2], a[3]), jnp.minimum(a[2], a[3])
    gscore = jnp.maximum(hi1, hi2) + jnp.maximum(jnp.minimum(hi1, hi2), jnp.maximum(lo1, lo2))
    gmax = jnp.max(gscore, axis=0, keepdims=True)
    giota = lax.broadcasted_iota(jnp.int32, gscore.shape, 0).astype(F32)
    gsel = jnp.min(jnp.where(gscore == gmax, giota, float(N_GROUPS)), axis=0, keepdims=True)
    onehot = giota == gsel
    pick = lambda t: jnp.sum(jnp.where(onehot, t, 0.0), axis=0, keepdims=True)
    bj = [pick(t) for t in a]
    sj = [pick(t) for t in sc]
    _, i1 = _first_index_of_max(bj)
    bj2 = [jnp.where(i1 == float(j), -jnp.inf, bj[j]) for j in range(EXPERTS_PER_GROUP)]
    _, i2 = _first_index_of_max(bj2)
    sel = lambda i: jnp.where(i == 0.0, sj[0], jnp.where(i == 1.0, sj[1], jnp.where(i == 2.0, sj[2], sj[3])))
    w1, w2 = sel(i1), sel(i2)
    den = w1 + w2
    base = gsel * float(EXPERTS_PER_GROUP)
    return ((base + i1).astype(jnp.int32), (base + i2).astype(jnp.int32)), (w1 / den, w2 / den)


def _mix_out_kernel(has_gate, *refs):
    if has_gate:
        a_ref, hs_ref, x_ref, wo_ref, g1_ref, g_ref, sh_ref, sc_ref, wr1_ref, wr2_ref, rb_ref, \
            x_out, h_out, idx_out, wts_out = refs
    else:
        a_ref, x_ref, wo_ref, g1_ref, g_ref, sh_ref, sc_ref, wr1_ref, wr2_ref, rb_ref, \
            x_out, h_out, idx_out, wts_out = refs
    tm = x_ref.shape[0]
    sub = min(ROW_SUB, tm)
    for i in range(tm // sub):
        rows = slice(i * sub, (i + 1) * sub)
        if has_gate:
            a = (a_ref[rows, :].astype(F32) * hs_ref[rows, :]).astype(BF16)
        else:
            a = a_ref[rows, :]
        x1 = x_ref[rows, :] + g1_ref[...] * _dot(a, wo_ref[...])
        x_out[rows, :] = x1
        h2 = _modulate(x1, g_ref[...], sh_ref[...], sc_ref[...])
        _to_tiles(h_out, h2, i * sub)
        idx, wts = _route(h2, wr1_ref, wr2_ref, rb_ref)
        for k in range(TOP_K):
            idx_out[k:k + 1, rows] = idx[k]
            wts_out[k:k + 1, rows] = wts[k]


def _mix_out(a, hs, x, w_o, gate1, gain, shift, scale, wr1, wr2, rbias):
    bsz, seq, d = x.shape
    tm = min(ROW_TILE, seq)
    row = lambda b, i: (b, i, 0)
    per_b = lambda b, i: (b, 0, 0)
    const = lambda b, i: (0, 0)
    full = lambda t: pl.BlockSpec(t.shape, const)
    vec = pl.BlockSpec((None, 1, d), per_b)
    acts = [a] if hs is None else [a, hs]
    return pl.pallas_call(
        functools.partial(_mix_out_kernel, hs is not None),
        grid=(bsz, seq // tm),
        in_specs=[pl.BlockSpec((None, tm, t.shape[-1]), row) for t in acts] + [
            pl.BlockSpec((None, tm, d), row), full(w_o), vec, full(gain), vec, vec,
            full(wr1), full(wr2), full(rbias),
        ],
        out_specs=[
            pl.BlockSpec((None, tm, d), row),
            pl.BlockSpec((None, tm * (d // LANES), LANES), row),
            pl.BlockSpec((None, TOP_K, tm), lambda b, i: (b, 0, i)),
            pl.BlockSpec((None, TOP_K, tm), lambda b, i: (b, 0, i)),
        ],
        out_shape=[
            jax.ShapeDtypeStruct((bsz, seq, d), F32),
            jax.ShapeDtypeStruct((bsz, seq * (d // LANES), LANES), F32),
            jax.ShapeDtypeStruct((bsz, TOP_K, seq), jnp.int32),
            jax.ShapeDtypeStruct((bsz, TOP_K, seq), F32),
        ],
        compiler_params=_params("arbitrary", "arbitrary"),
        name="mix_out_route",
    )(*acts, x, w_o, gate1, gain, shift, scale, wr1, wr2, rbias)


N_SUB = D_MODEL // LANES
TABLE_CHUNK = 512


def _to_tiles(ref, val, lo=0):
    n = val.shape[0]
    for s in range(N_SUB):
        ref[pl.ds(lo * N_SUB + s, n, stride=N_SUB), :] = val[:, s * LANES:(s + 1) * LANES]


def _from_tiles(ref, lo, n):
    return jnp.concatenate([ref[pl.ds(lo * N_SUB + s, n, stride=N_SUB), :] for s in range(N_SUB)], axis=1)


def _tables_kernel(idx_ref, rank_ref, cnt_ref, carry):
    @pl.when(pl.program_id(0) == 0)
    def _():
        carry[...] = jnp.zeros_like(carry)

    seq = idx_ref.shape[-1]
    ch = min(TABLE_CHUNK, seq)
    tri = jnp.where(lax.broadcasted_iota(jnp.int32, (ch, ch), 0) <= lax.broadcasted_iota(jnp.int32, (ch, ch), 1),
                    1.0, 0.0).astype(BF16)
    eiota = lax.broadcasted_iota(jnp.int32, (N_EXPERTS, ch), 0)
    cnt = carry[...]
    for k in range(TOP_K):
        for c in range(seq // ch):
            sel = eiota == idx_ref[k:k + 1, c * ch:(c + 1) * ch]
            pref = _dot(jnp.where(sel, 1.0, 0.0).astype(BF16), tri) + cnt
            rank = jnp.sum(jnp.where(sel, pref, 0.0), axis=0, keepdims=True) - 1.0
            rank_ref[k:k + 1, c * ch:(c + 1) * ch] = rank.astype(jnp.int32)
            cnt = pref[:, ch - 1:ch]
    carry[...] = cnt
    cnt_ref[...] = jnp.broadcast_to(cnt, cnt_ref.shape)


def _tables(idx):
    bsz, _, seq = idx.shape
    return pl.pallas_call(
        _tables_kernel,
        grid=(bsz,),
        in_specs=[pl.BlockSpec((None, TOP_K, seq), lambda b: (b, 0, 0))],
        out_specs=[pl.BlockSpec((None, TOP_K, seq), lambda b: (b, 0, 0)),
                   pl.BlockSpec((N_EXPERTS, LANES), lambda b: (0, 0))],
        out_shape=[jax.ShapeDtypeStruct((bsz, TOP_K, seq), jnp.int32),
                   jax.ShapeDtypeStruct((N_EXPERTS, LANES), F32)],
        scratch_shapes=[pltpu.VMEM((N_EXPERTS, 1), F32)],
        compiler_params=_params("arbitrary"),
        name="moe_tables",
    )(idx)


def _zero_runs(step, total, pad_start_ref, pad_len_ref, tail_ref):
    ops = []
    for m in range(-(-2 * N_EXPERTS // total)):
        u = step + m * total
        e = jnp.minimum(u, N_EXPERTS - 1)
        length = jnp.where(u < N_EXPERTS, pad_len_ref[e], 0)
        first = pad_start_ref[e]
        for bit in reversed(range(MOE_TILE.bit_length() - 1)):
            done = lax.shift_left(lax.shift_right_logical(length, bit + 1), bit + 1)
            ops.append((lax.bitwise_and(lax.shift_right_logical(length, bit), 1) == 1, first + done, 1 << bit))
        t = u - N_EXPERTS
        ops.append(((t >= 0) & (t < tail_ref[1]), tail_ref[0] + t * MOE_TILE, MOE_TILE))
    return ops


def _scatter_kernel(total, pad_start_ref, pad_len_ref, tail_ref, dest_ref, src_hbm, dst_hbm,
                    buf, zbuf, sem_in, sem_out, sem_z):
    n = pl.program_id(0) * pl.num_programs(1) + pl.program_id(1)
    tm = buf.shape[1]
    slot = lax.rem(n, 3)

    def load(step, sl):
        return pltpu.make_async_copy(src_hbm.at[pl.ds(step * tm, tm)], buf.at[sl], sem_in.at[sl])

    def drain(sl):
        for _ in range(TOP_K):
            pltpu.make_async_copy(buf.at[sl], dst_hbm.at[pl.ds(0, tm)], sem_out.at[sl]).wait()

    def zero_fill(step, wait):
        for pred, first, rows in _zero_runs(step, total, pad_start_ref, pad_len_ref, tail_ref):
            @pl.when(pred)
            def _(first=first, rows=rows):
                cp = pltpu.make_async_copy(zbuf.at[pl.ds(0, rows)], dst_hbm.at[pl.ds(first, rows)], sem_z)
                cp.wait() if wait else cp.start()

    @pl.when(n == 0)
    def _():
        zbuf[...] = jnp.zeros_like(zbuf)
        load(0, 0).start()
        if total > 1:
            load(1, 1).start()

    load(n, slot).wait()
    for k in range(TOP_K):
        for c in range(tm // LANES):
            def start(j, carry, k=k, c=c):
                row = dest_ref[0, k * tm + c * LANES + j]
                pltpu.make_async_copy(buf.at[slot, c * LANES + j], dst_hbm.at[row], sem_out.at[slot]).start(priority=k)
                return carry
            lax.fori_loop(0, LANES, start, 0, unroll=8)
    zero_fill(n, wait=False)

    @pl.when(n > 0)
    def _():
        drain(lax.rem(n + 2, 3))
        zero_fill(n - 1, wait=True)

    @pl.when(n + 2 < total)
    def _():
        load(n + 2, lax.rem(n + 2, 3)).start()

    @pl.when(n == total - 1)
    def _():
        drain(slot)
        zero_fill(n, wait=True)


def _index_blocks(table, tm):
    bsz, _, seq = table.shape
    nt = seq // tm
    t = table.reshape(bsz, TOP_K, nt, tm).transpose(0, 2, 1, 3)
    return t.reshape(bsz * nt, 1, TOP_K * tm), (None, 1, TOP_K * tm)


def _scatter(dest, pad_start, pad_len, tail, h2t, n_rows):
    bsz, _, seq = dest.shape
    tm = min(ROW_TILE, seq)
    nt = seq // tm
    dest4, dest_block = _index_blocks(dest, tm)
    grid_spec = pltpu.PrefetchScalarGridSpec(
        num_scalar_prefetch=3,
        grid=(bsz, nt),
        in_specs=[
            pl.BlockSpec(dest_block, lambda b, i, *_: (b * nt + i, 0, 0), memory_space=pltpu.SMEM),
            pl.BlockSpec(memory_space=pl.ANY),
        ],
        out_specs=pl.BlockSpec(memory_space=pl.ANY),
        scratch_shapes=[pltpu.VMEM((3, tm, N_SUB, LANES), F32), pltpu.VMEM((MOE_TILE, N_SUB, LANES), F32),
                        pltpu.SemaphoreType.DMA((3,)), pltpu.SemaphoreType.DMA((3,)), pltpu.SemaphoreType.DMA(())],
    )
    return pl.pallas_call(
        functools.partial(_scatter_kernel, bsz * nt),
        grid_spec=grid_spec,
        out_shape=jax.ShapeDtypeStruct((n_rows, N_SUB, LANES), F32),
        compiler_params=_params("arbitrary", "arbitrary"),
        name="moe_scatter",
    )(pad_start, pad_len, tail, dest4, h2t.reshape(bsz * seq, N_SUB, LANES))


def _expert_kernel(blk_exp_ref, blk_first_ref, n_used_ref,
                   xs_ref, wgu_ref, wdn_ref, ys_ref, wgu_bf, wdn_bf):
    i = pl.program_id(0)

    @pl.when(i < n_used_ref[0])
    def _():
        @pl.when(blk_first_ref[i] == 1)
        def _():
            wgu_bf[...] = wgu_ref[...].astype(BF16)
            wdn_bf[...] = wdn_ref[...].astype(BF16)

        x = _from_tiles(xs_ref, 0, MOE_TILE).astype(BF16)
        gu = _dot(x, wgu_bf[...])
        g = gu[:, :D_EXPERT]
        u = gu[:, D_EXPERT:]
        mid = (g * jax.nn.sigmoid(g) * u).astype(BF16)
        _to_tiles(ys_ref, _dot(mid, wdn_bf[...]))

    @pl.when(i >= n_used_ref[0])
    def _():
        ys_ref[...] = jnp.zeros_like(ys_ref)


def _experts(blk_exp, blk_first, n_used, xs, w_gu, w_dn, layer):
    d = D_MODEL
    nb = xs.shape[0] // (MOE_TILE * N_SUB)
    tile = lambda i, *_: (i, 0)
    grid_spec = pltpu.PrefetchScalarGridSpec(
        num_scalar_prefetch=3,
        grid=(nb,),
        in_specs=[
            pl.BlockSpec((MOE_TILE * N_SUB, LANES), tile),
            pl.BlockSpec((None, None, d, 2 * D_EXPERT), lambda i, be, *_: (layer, be[i], 0, 0)),
            pl.BlockSpec((None, None, D_EXPERT, d), lambda i, be, *_: (layer, be[i], 0, 0)),
        ],
        out_specs=pl.BlockSpec((MOE_TILE * N_SUB, LANES), tile),
        scratch_shapes=[pltpu.VMEM((d, 2 * D_EXPERT), BF16), pltpu.VMEM((D_EXPERT, d), BF16)],
    )
    return pl.pallas_call(
        _expert_kernel,
        grid_spec=grid_spec,
        out_shape=jax.ShapeDtypeStruct(xs.shape, F32),
        compiler_params=_params("arbitrary"),
        name="moe_experts",
    )(blk_exp, blk_first, n_used, xs, w_gu, w_dn)


def _combine_kernel(dcur_ref, dnxt_ref, ys_hbm, x_ref, wts_ref, g2_ref, x_out, buf, sem):
    nt = pl.num_programs(1)
    n = pl.program_id(0) * nt + pl.program_id(1)
    total = pl.num_programs(0) * nt
    tm = x_ref.shape[0]
    slot = lax.rem(n, 2)

    def issue(d_ref, sl):
        for k in range(TOP_K):
            for c in range(tm // LANES):
                def start(j, carry, k=k, c=c):
                    src = ys_hbm.at[pl.ds(pl.multiple_of(d_ref[0, k * tm + c * LANES + j], N_SUB), N_SUB)]
                    dst = buf.at[sl, pl.ds(pl.multiple_of((k * tm + c * LANES + j) * N_SUB, N_SUB), N_SUB)]
                    pltpu.make_async_copy(src, dst, sem.at[sl]).start(priority=k)
                    return carry
                lax.fori_loop(0, LANES, start, 0, unroll=8)

    @pl.when(n == 0)
    def _():
        issue(dcur_ref, 0)

    @pl.when(n + 1 < total)
    def _():
        issue(dnxt_ref, 1 - slot)

    pltpu.make_async_copy(ys_hbm.at[pl.ds(0, TOP_K * tm * N_SUB)], buf.at[slot], sem.at[slot]).wait()
    cur = buf.at[slot]
    w = wts_ref[...]
    y = w[:, 0:1] * _from_tiles(cur, 0, tm) + w[:, 1:2] * _from_tiles(cur, tm, tm)
    x_out[...] = x_ref[...] + g2_ref[...] * y


def _combine(dest8, ys, x, wts_col, gate2):
    bsz, seq, d = x.shape
    tm = min(ROW_TILE, seq)
    nt = seq // tm

    def nxt(b, i):
        return (jnp.minimum(b * nt + i + 1, bsz * nt - 1), 0, 0)

    dest8, dest_block = _index_blocks(dest8, tm)
    return pl.pallas_call(
        _combine_kernel,
        grid=(bsz, nt),
        in_specs=[
            pl.BlockSpec(dest_block, lambda b, i: (b * nt + i, 0, 0), memory_space=pltpu.SMEM),
            pl.BlockSpec(dest_block, nxt, memory_space=pltpu.SMEM),
            pl.BlockSpec(memory_space=pl.ANY),
            pl.BlockSpec((None, tm, d), lambda b, i: (b, i, 0)),
            pl.BlockSpec((None, tm, TOP_K), lambda b, i: (b, i, 0)),
            pl.BlockSpec((None, 1, d), lambda b, i: (b, 0, 0)),
        ],
        out_specs=pl.BlockSpec((None, tm, d), lambda b, i: (b, i, 0)),
        out_shape=jax.ShapeDtypeStruct((bsz, seq, d), F32),
        scratch_shapes=[pltpu.VMEM((2, TOP_K * tm * N_SUB, LANES), F32), pltpu.SemaphoreType.DMA((2,))],
        compiler_params=_params("arbitrary", "arbitrary"),
        name="moe_combine",
    )(dest8, dest8, ys, x, wts_col, gate2)


def _lookup(table, keys):
    hit = keys[..., None] == jnp.arange(table.shape[0], dtype=jnp.int32)
    return jnp.sum(jnp.where(hit, table, 0), axis=-1).astype(jnp.int32)


def _count_le(bounds, q):
    return jnp.sum((bounds <= q[..., None]).astype(jnp.int32), axis=-1)


def _moe(h2t, idx, wts, x, gate2, w_gu, w_dn, layer):
    bsz, seq, _ = x.shape
    n_rows = bsz * seq * TOP_K + N_EXPERTS * MOE_TILE
    nb = n_rows // MOE_TILE
    rank, cnt = _tables(idx)
    counts = cnt[:, 0].astype(jnp.int32)
    padded = ((counts + MOE_TILE - 1) // MOE_TILE) * MOE_TILE
    pend = jnp.cumsum(padded)
    pstart = pend - padded
    dest = _lookup(pstart, idx) + rank
    blk_row = jnp.arange(nb, dtype=jnp.int32) * MOE_TILE
    blk_exp = jnp.minimum(_count_le(pend, blk_row), N_EXPERTS - 1)
    blk_first = (blk_row == _lookup(pstart, blk_exp)).astype(jnp.int32)
    n_used = (pend[-1:] // MOE_TILE).astype(jnp.int32)
    tail = jnp.concatenate([pend[-1:], (n_rows - pend[-1:]) // MOE_TILE]).astype(jnp.int32)
    xs = _scatter(dest, (pstart + counts).astype(jnp.int32), (padded - counts).astype(jnp.int32), tail, h2t, n_rows)
    ys = _experts(blk_exp, blk_first, n_used, xs.reshape(n_rows * N_SUB, LANES), w_gu, w_dn, layer)
    return _combine(dest * N_SUB, ys, x, wts.transpose(0, 2, 1), gate2)


def _rnn_in_kernel(x_ref, g_ref, sh_ref, sc_ref, w_ref, gate_out, xb_out):
    h = _modulate(x_ref[...], g_ref[...], sh_ref[...], sc_ref[...])
    u = _dot(h.astype(BF16), w_ref[...])
    gate_out[...] = jax.nn.gelu(u[:, :D_RNN]).astype(BF16)
    xb_out[...] = u[:, D_RNN:]


def _rnn_in(x, gain, shift, scale, w_in):
    bsz, seq, d = x.shape
    tm = min(ROW_TILE, seq)
    row = lambda b, i: (b, i, 0)
    per_b = lambda b, i: (b, 0, 0)
    const = lambda b, i: (0, 0)
    return pl.pallas_call(
        _rnn_in_kernel,
        grid=(bsz, seq // tm),
        in_specs=[
            pl.BlockSpec((None, tm, d), row),
            pl.BlockSpec(gain.shape, const),
            pl.BlockSpec((None, 1, d), per_b),
            pl.BlockSpec((None, 1, d), per_b),
            pl.BlockSpec(w_in.shape, const),
        ],
        out_specs=[pl.BlockSpec((None, tm, D_RNN), row), pl.BlockSpec((None, tm, D_RNN), row)],
        out_shape=[jax.ShapeDtypeStruct((bsz, seq, D_RNN), BF16),
                   jax.ShapeDtypeStruct((bsz, seq, D_RNN), F32)],
        compiler_params=_params("arbitrary", "arbitrary"),
        name="rnn_in",
    )(x, gain, shift, scale, w_in)


def _lru_kernel(xb_ref, cw_ref, cb_ref, wcat_ref, bcat_ref, lam_ref, hs_ref,
                xi_ref, af_ref, bf_ref, ab_ref, bb_ref, hf_ref, hb_ref, sum_ref):
    seq, c = xb_ref.shape
    seg_len = seq // SUBLANES
    n_slab = c // LANES
    n_rows = seg_len * SUBLANES
    halo = (CONV_W // 2) * SUBLANES
    row = lax.broadcasted_iota(jnp.int32, (SUBLANES, LANES), 0)
    for sl in range(n_slab):
        lanes = slice(sl * LANES, (sl + 1) * LANES)
        for g in range(SUBLANES):
            xi_ref[sl, pl.ds(halo + g, seg_len, stride=SUBLANES), :] = xb_ref[g * seg_len:(g + 1) * seg_len, lanes]
        for back in (1, 2):
            prev = xi_ref[sl, halo + (seg_len - back) * SUBLANES:halo + (seg_len - back + 1) * SUBLANES, :]
            xi_ref[sl, halo - back * SUBLANES:halo - (back - 1) * SUBLANES, :] = jnp.where(
                row == 0, 0.0, pltpu.roll(prev, 1, 0))
        nxt = xi_ref[sl, halo:halo + SUBLANES, :]
        xi_ref[sl, halo + n_rows:halo + n_rows + SUBLANES, :] = jnp.where(
            row == SUBLANES - 1, 0.0, pltpu.roll(nxt, SUBLANES - 1, 0))

    cw = cw_ref[...]
    cb = cb_ref[...]
    lam = lam_ref[...]
    neg = -lam
    softplus = jnp.maximum(neg, 0.0) + jnp.log1p(jnp.exp(-jnp.abs(neg)))
    half_rate = (-0.5 * LRU_C) * softplus
    rows = min(SCAN_ROWS, n_rows)
    n_chunks = n_rows // rows

    for ci in range(n_chunks):
        i0 = ci * rows
        taps = []
        for k in range(CONV_W):
            lo = halo + i0 + (k - CONV_W // 2) * SUBLANES
            taps.append(jnp.concatenate([xi_ref[sl, lo:lo + rows, :] for sl in range(n_slab)], axis=1))
        xc = cb
        for k in range(CONV_W):
            xc = xc + taps[k] * cw[k:k + 1, :]
        xcb = xc.astype(BF16)
        xh = 0.5 * xc
        for dirn, (a_ref, b_ref) in enumerate(((af_ref, bf_ref), (ab_ref, bb_ref))):
            cols = slice(2 * dirn * c, 2 * (dirn + 1) * c)
            th = jnp.tanh(_dot(xcb, wcat_ref[:, cols]) + bcat_ref[:, cols])
            hr = half_rate[dirn:dirn + 1, :]
            log_a = hr * th[:, :c] + hr
            a = jnp.exp(log_a)
            mult = jnp.sqrt(jnp.tanh(-log_a) * (a * a + 1.0))
            if dirn == 0 and ci == 0:
                mult = jnp.where(lax.broadcasted_iota(jnp.int32, mult.shape, 0) == 0, 1.0, mult)
            if dirn == 1 and ci == n_chunks - 1:
                mult = jnp.where(lax.broadcasted_iota(jnp.int32, mult.shape, 0) == rows - 1, 1.0, mult)
            b = mult * (th[:, c:] + 1.0) * xh
            for sl in range(n_slab):
                a_ref[sl, i0:i0 + rows, :] = a[:, sl * LANES:(sl + 1) * LANES]
                b_ref[sl, i0:i0 + rows, :] = b[:, sl * LANES:(sl + 1) * LANES]

    def step_rows(cidx):
        fwd = pl.ds(pl.multiple_of(cidx * SUBLANES, SUBLANES), SUBLANES)
        bwd = pl.ds(pl.multiple_of((seg_len - 1 - cidx) * SUBLANES, SUBLANES), SUBLANES)
        return fwd, bwd

    zero = jnp.zeros((SUBLANES, LANES), F32)
    one = jnp.ones((SUBLANES, LANES), F32)

    def totals(cidx, carry):
        fwd, bwd = step_rows(cidx)
        out = []
        for sl in range(n_slab):
            hf, pf, hb, pb = carry[sl]
            af, ab = af_ref[sl, fwd, :], ab_ref[sl, bwd, :]
            out.append((af * hf + bf_ref[sl, fwd, :], af * pf, ab * hb + bb_ref[sl, bwd, :], ab * pb))
        return tuple(out)
    tot = lax.fori_loop(0, seg_len, totals, tuple((zero, one, zero, one) for _ in range(n_slab)), unroll=8)

    enter = []
    for sl in range(n_slab):
        hf, pf, hb, pb = tot[sl]
        cf, cbk = zero, zero
        for _ in range(SUBLANES - 1):
            cf = jnp.where(row == 0, 0.0, pltpu.roll(hf + pf * cf, 1, 0))
            cbk = jnp.where(row == SUBLANES - 1, 0.0, pltpu.roll(hb + pb * cbk, SUBLANES - 1, 0))
        enter.append((cf, cbk))

    def states(meet, cidx, carry):
        fwd, bwd = step_rows(cidx)
        out = []
        for sl in range(n_slab):
            hf, hb = carry[sl]
            hf = af_ref[sl, fwd, :] * hf + bf_ref[sl, fwd, :]
            hb = ab_ref[sl, bwd, :] * hb + bb_ref[sl, bwd, :]
            if meet:
                sum_ref[sl, fwd, :] = hf + hb_ref[sl, fwd, :]
                sum_ref[sl, bwd, :] = hb + hf_ref[sl, bwd, :]
            else:
                hf_ref[sl, fwd, :] = hf
                hb_ref[sl, bwd, :] = hb
            out.append((hf, hb))
        return tuple(out)
    mid = lax.fori_loop(0, seg_len // 2, functools.partial(states, False), tuple(enter), unroll=8)
    lax.fori_loop(seg_len // 2, seg_len, functools.partial(states, True), mid, unroll=8)

    for g in range(SUBLANES):
        for sl in range(n_slab):
            hs_ref[g * seg_len:(g + 1) * seg_len, sl * LANES:(sl + 1) * LANES] = (
                sum_ref[sl, pl.ds(g, seg_len, stride=SUBLANES), :])


def _lru(xb, conv_w, conv_b, wcat, bcat, lam):
    bsz, seq, _ = xb.shape
    c = RNN_BW
    blk = lambda b, n: (b, 0, n)
    return pl.pallas_call(
        _lru_kernel,
        grid=(bsz, RNN_BLOCKS),
        in_specs=[
            pl.BlockSpec((None, seq, c), blk),
            pl.BlockSpec((CONV_W, c), lambda b, n: (0, n)),
            pl.BlockSpec((1, c), lambda b, n: (0, n)),
            pl.BlockSpec((None, c, 4 * c), lambda b, n: (n, 0, 0)),
            pl.BlockSpec((None, 1, 4 * c), lambda b, n: (n, 0, 0)),
            pl.BlockSpec((2, c), lambda b, n: (0, n)),
        ],
        out_specs=pl.BlockSpec((None, seq, c), blk),
        out_shape=jax.ShapeDtypeStruct((bsz, seq, D_RNN), F32),
        scratch_shapes=[pltpu.VMEM((c // LANES, seq + (CONV_W - 1) * SUBLANES, LANES), F32)]
        + [pltpu.VMEM((c // LANES, seq, LANES), F32)] * 7,
        compiler_params=_params("arbitrary", "arbitrary"),
        name="rglru_scan",
    )(xb, conv_w, conv_b, wcat, bcat, lam)


def _mla_weights(w_in, w_q_b, w_kv_b, q_norm, k_norm):
    half = QK_ROPE // 2

    def slab(t):
        return jnp.pad(t, [(0, 0)] * (t.ndim - 1) + [(0, LANES - QK_HEAD)])

    def rot_slab(t):
        rope = t[..., QK_NOPE:]
        swapped = jnp.concatenate([jnp.zeros_like(t[..., :QK_NOPE]), rope[..., half:], rope[..., :half]], axis=-1)
        return slab(swapped)

    kpe = jnp.pad(w_in[:, Q_LORA + KV_LORA:], ((0, 0), (QK_NOPE, 0)))
    w_in_p = jnp.concatenate([w_in[:, :Q_LORA + KV_LORA], slab(kpe), rot_slab(kpe)], axis=1).astype(BF16)
    wq = w_q_b.reshape(Q_LORA, N_HEADS, QK_HEAD)
    wq_p = slab(wq).reshape(Q_LORA, N_HEADS * LANES).astype(BF16)
    wq_rot = rot_slab(wq).reshape(Q_LORA, N_HEADS * LANES).astype(BF16)
    wkv = w_kv_b.reshape(KV_LORA, N_HEADS, QK_NOPE + V_HEAD)
    wk = jnp.pad(wkv[:, :, :QK_NOPE], ((0, 0), (0, 0), (0, LANES - QK_NOPE))).reshape(KV_LORA, N_HEADS * LANES)
    wv = wkv[:, :, QK_NOPE:].reshape(KV_LORA, N_HEADS * V_HEAD)
    w_kv_p = jnp.concatenate([wk, wv], axis=1).astype(BF16)
    gains = lambda g: jnp.stack([slab(g), rot_slab(g)], axis=0)
    return w_in_p, wq_p, wq_rot, w_kv_p, gains(q_norm), gains(k_norm)


def _rope_tables(positions):
    half = QK_ROPE // 2
    inv_freq = ROPE_THETA ** (-jnp.arange(half, dtype=F32) / half)
    ang = positions.astype(F32)[..., None] * inv_freq
    cos, sin = jnp.cos(ang), jnp.sin(ang)
    lead = positions.shape + (QK_NOPE,)
    tail = positions.shape + (LANES - QK_HEAD,)
    cos_t = jnp.concatenate([jnp.ones(lead, F32), cos, cos, jnp.ones(tail, F32)], axis=-1)
    sin_t = jnp.concatenate([jnp.zeros(lead, F32), -sin, sin, jnp.zeros(tail, F32)], axis=-1)
    return cos_t, sin_t


def _router_weights(w_router, router_bias):
    perm = (jnp.arange(N_EXPERTS) % N_GROUPS) * EXPERTS_PER_GROUP + jnp.arange(N_EXPERTS) // N_GROUPS
    w = w_router[:, perm]
    hi = w.astype(BF16)
    lo = (w - hi.astype(F32)).astype(BF16)
    z = jnp.zeros_like(hi)
    wr1 = jnp.concatenate([hi, lo, z, z], axis=1)
    wr2 = jnp.concatenate([z, z, hi, z], axis=1)
    return wr1, wr2, router_bias[perm].reshape(N_EXPERTS, 1).astype(F32)


def kernel(x, c, positions, norm_mix, norm_ffn, w_ada, b_ada, mla_w_in, mla_q_a_norm, mla_kv_a_norm, mla_w_q_b, mla_w_kv_b, mla_q_norm, mla_k_norm, mla_w_o, rnn_w_in, rnn_conv_w, rnn_conv_b, rnn_lam_f, rnn_w_rf, rnn_b_rf, rnn_w_if, rnn_b_if, rnn_lam_b, rnn_w_rb, rnn_b_rb, rnn_w_ib, rnn_b_ib, rnn_w_o, w_router, router_bias, moe_w_gu, moe_w_dn):
    bsz, seq, d = x.shape
    depth = w_ada.shape[0]
    mod = _ada(c, w_ada, b_ada)
    wr1, wr2, rbias = _router_weights(w_router, router_bias)
    cos_t, sin_t = _rope_tables(positions)
    vec = lambda v: v.reshape(1, -1)
    for i in range(depth):
        sh1, sc1, g1, sh2, sc2, g2 = [mod[i, :, k * d:(k + 1) * d].reshape(bsz, 1, d) for k in range(6)]
        j = i // 2
        if i % 2 == 0:
            w_in_p, wq, wq_rot, wkv, qn, kn = _mla_weights(mla_w_in[j], mla_w_q_b[j], mla_w_kv_b[j],
                                                           mla_q_norm[j], mla_k_norm[j])
            q, k, v = _mla_in(x, vec(norm_mix[i]), sh1, sc1, w_in_p, vec(mla_q_a_norm[j]),
                              vec(mla_kv_a_norm[j]), wq, wq_rot, wkv, qn, kn, cos_t, sin_t)
            a = _attention(q, k, v)
            hs = None
            w_o = mla_w_o[j].astype(BF16)
        else:
            a, xb = _rnn_in(x, vec(norm_mix[i]), sh1, sc1, rnn_w_in[j].astype(BF16))
            wcat = (0.5 * jnp.concatenate([rnn_w_rf[j], rnn_w_if[j], rnn_w_rb[j], rnn_w_ib[j]], axis=-1)).astype(BF16)
            bcat = jnp.stack([b.reshape(RNN_BLOCKS, RNN_BW) for b in
                              (rnn_b_rf[j], rnn_b_if[j], rnn_b_rb[j], rnn_b_ib[j])], axis=1)
            bcat = 0.5 * bcat.reshape(RNN_BLOCKS, 1, 4 * RNN_BW)
            lam = jnp.stack([rnn_lam_f[j], rnn_lam_b[j]], axis=0)
            hs = _lru(xb, rnn_conv_w[j], vec(rnn_conv_b[j]), wcat, bcat, lam)
            w_o = rnn_w_o[j].astype(BF16)
        x, h2, idx, wts = _mix_out(a, hs, x, w_o, g1, vec(norm_ffn[i]), sh2, sc2, wr1, wr2, rbias)
        x = _moe(h2, idx, wts, x, g2, moe_w_gu, moe_w_dn, i)
    return x
```

```python
import functools

import jax
import jax.numpy as jnp
from jax import lax
from jax.experimental import pallas as pl
from jax.experimental.pallas import tpu as pltpu

F32 = jnp.float32
BF16 = jnp.bfloat16

D_MODEL = 1024
N_HEADS = 16
Q_LORA = 384
KV_LORA = 256
QK_NOPE = 64
QK_ROPE = 32
QK_HEAD = QK_NOPE + QK_ROPE
V_HEAD = 64
ROPE_THETA = 10000.0
D_RNN = D_MODEL
RNN_BLOCKS = 4
RNN_BW = D_RNN // RNN_BLOCKS
CONV_W = 4
LRU_C = 8.0
N_EXPERTS = 32
N_GROUPS = 8
EXPERTS_PER_GROUP = N_EXPERTS // N_GROUPS
TOP_K = 2
D_EXPERT = 512
EPS = 1e-6
LOG2_E = 1.4426950408889634

LANES = 128
SUBLANES = 8
VMEM_LIMIT = 52 * 1024 * 1024

ROW_TILE = 512
ROW_SUB = 256
Q_TILE = 2048
Q_SUB = 256
MOE_TILE = 512
SCAN_ROWS = 256


def _dot(a, b):
    return jnp.dot(a, b, preferred_element_type=F32)


def _split_bf16(a):
    hi = a.astype(BF16)
    lo = (a - hi.astype(F32)).astype(BF16)
    return hi, lo


def _dot_split(a, b):
    ah, al = _split_bf16(a)
    bh, bl = _split_bf16(b)
    return _dot(ah, bh) + (_dot(ah, bl) + _dot(al, bh))


def _rms(x, gain, n):
    ms = jnp.sum(x * x, axis=-1, keepdims=True) * (1.0 / n)
    return x * lax.rsqrt(ms + EPS) * gain


def _modulate(x, gain, shift, scale):
    return _rms(x, gain, x.shape[-1]) * (1.0 + scale) + shift


def _params(*sem):
    return pltpu.CompilerParams(dimension_semantics=sem, vmem_limit_bytes=VMEM_LIMIT)


def _ada_kernel(c_ref, w_ref, b_ref, o_ref):
    c = c_ref[...]
    o_ref[...] = _dot_split(c * jax.nn.sigmoid(c), w_ref[...]) + b_ref[...]


def _ada(c, w_ada, b_ada):
    depth, d, n = w_ada.shape
    bsz = c.shape[0]
    tn = 1536
    return pl.pallas_call(
        _ada_kernel,
        grid=(depth, n // tn),
        in_specs=[
            pl.BlockSpec((bsz, d), lambda l, j: (0, 0)),
            pl.BlockSpec((None, d, tn), lambda l, j: (l, 0, j)),
            pl.BlockSpec((None, 1, tn), lambda l, j: (l, 0, j)),
        ],
        out_specs=pl.BlockSpec((None, bsz, tn), lambda l, j: (l, 0, j)),
        out_shape=jax.ShapeDtypeStruct((depth, bsz, n), F32),
        compiler_params=_params("arbitrary", "arbitrary"),
        name="adaln_mod",
    )(c, w_ada, b_ada.reshape(depth, 1, n))


def _head_scale(s):
    return lax.rsqrt(jnp.sum(s * s, axis=-1, keepdims=True) * (1.0 / QK_HEAD) + EPS)


def _mla_in_kernel(x_ref, g_ref, sh_ref, sc_ref, win_ref, qan_ref, kvan_ref, wq_ref, wqr_ref, wkv_ref,
                   qn_ref, kn_ref, cos_ref, sin_ref, q_out, k_out, v_out):
    h = _modulate(x_ref[...], g_ref[...], sh_ref[...], sc_ref[...])
    lat = _dot(h.astype(BF16), win_ref[...])
    q_lat = lat[:, :Q_LORA]
    kv_lat = lat[:, Q_LORA:Q_LORA + KV_LORA]
    kpe = lat[:, Q_LORA + KV_LORA:Q_LORA + KV_LORA + LANES]
    kpe_rot = lat[:, Q_LORA + KV_LORA + LANES:]
    qn = _rms(q_lat, qan_ref[...], Q_LORA).astype(BF16)
    q_all = _dot(qn, wq_ref[...])
    q_rot = _dot(qn, wqr_ref[...])
    kv_all = _dot(_rms(kv_lat, kvan_ref[...], KV_LORA).astype(BF16), wkv_ref[...])
    cos_t = cos_ref[...]
    sin_t = sin_ref[...]
    q_scale = LOG2_E * QK_HEAD ** -0.5
    cq = cos_t * (qn_ref[0:1, :] * q_scale)
    sq = sin_t * (qn_ref[1:2, :] * q_scale)
    ck = cos_t * kn_ref[0:1, :]
    k_rot_term = kpe_rot * (sin_t * kn_ref[1:2, :])
    for hh in range(N_HEADS):
        sl = slice(hh * LANES, (hh + 1) * LANES)
        s = q_all[:, sl]
        q_out[hh] = ((s * cq + q_rot[:, sl] * sq) * _head_scale(s)).astype(BF16)
        s = kv_all[:, sl] + kpe
        k_out[hh] = ((s * ck + k_rot_term) * _head_scale(s)).astype(BF16)
    v_out[...] = kv_all[:, N_HEADS * LANES:].astype(BF16)


def _mla_in(x, gain, shift, scale, w_in, q_a_norm, kv_a_norm, w_q, w_q_rot, w_kv, q_norm, k_norm, cos_t, sin_t):
    bsz, seq, d = x.shape
    tm = min(ROW_TILE, seq)
    row = lambda b, i: (b, i, 0)
    per_b = lambda b, i: (b, 0, 0)
    const = lambda b, i: (0, 0)
    full = lambda a: pl.BlockSpec(a.shape, const)
    return pl.pallas_call(
        _mla_in_kernel,
        grid=(bsz, seq // tm),
        in_specs=[
            pl.BlockSpec((None, tm, d), row),
            full(gain),
            pl.BlockSpec((None, 1, d), per_b),
            pl.BlockSpec((None, 1, d), per_b),
            full(w_in), full(q_a_norm), full(kv_a_norm), full(w_q), full(w_q_rot), full(w_kv),
            full(q_norm), full(k_norm),
            pl.BlockSpec((None, tm, LANES), row),
            pl.BlockSpec((None, tm, LANES), row),
        ],
        out_specs=[
            pl.BlockSpec((None, N_HEADS, tm, LANES), lambda b, i: (b, 0, i, 0)),
            pl.BlockSpec((None, N_HEADS, tm, LANES), lambda b, i: (b, 0, i, 0)),
            pl.BlockSpec((None, tm, N_HEADS * V_HEAD), row),
        ],
        out_shape=[
            jax.ShapeDtypeStruct((bsz, N_HEADS, seq, LANES), BF16),
            jax.ShapeDtypeStruct((bsz, N_HEADS, seq, LANES), BF16),
            jax.ShapeDtypeStruct((bsz, seq, N_HEADS * V_HEAD), BF16),
        ],
        compiler_params=_params("arbitrary", "arbitrary"),
        name="mla_in",
    )(x, gain, shift, scale, w_in, q_a_norm, kv_a_norm, w_q, w_q_rot, w_kv, q_norm, k_norm, cos_t, sin_t)


def _attn_kernel(q_ref, k_ref, v_ref, o_ref):
    v = v_ref[...]
    lane_v = lax.broadcasted_iota(jnp.int32, v.shape, 1)
    v_heads = [jnp.where(lane_v < V_HEAD, v, jnp.ones((), BF16)), jnp.where(lane_v >= V_HEAD, v, jnp.ones((), BF16))]
    lane = lax.broadcasted_iota(jnp.int32, (Q_SUB, LANES), 1)
    for i in range(q_ref.shape[1] // Q_SUB):
        rows = slice(i * Q_SUB, (i + 1) * Q_SUB)
        outs = []
        for j in range(2):
            s = lax.dot_general(q_ref[j, rows, :], k_ref[j], (((1,), (1,)), ((), ())),
                                preferred_element_type=F32)
            m = jnp.max(s, axis=-1, keepdims=True)
            o = _dot(jnp.exp2(s - m).astype(BF16), v_heads[j])
            denom = o[:, V_HEAD:V_HEAD + 1] if j == 0 else o[:, 0:1]
            outs.append(o / denom)
        o_ref[rows, :] = jnp.where(lane < V_HEAD, outs[0], outs[1]).astype(BF16)


def _attention(q, k, v):
    bsz, _, seq, _ = q.shape
    tq = min(Q_TILE, seq)
    assert tq % Q_SUB == 0
    return pl.pallas_call(
        _attn_kernel,
        grid=(bsz, N_HEADS // 2, seq // tq),
        in_specs=[
            pl.BlockSpec((None, 2, tq, LANES), lambda b, h, i: (b, h, i, 0)),
            pl.BlockSpec((None, 2, seq, LANES), lambda b, h, i: (b, h, 0, 0)),
            pl.BlockSpec((None, seq, LANES), lambda b, h, i: (b, 0, h)),
        ],
        out_specs=pl.BlockSpec((None, tq, LANES), lambda b, h, i: (b, i, h)),
        out_shape=jax.ShapeDtypeStruct((bsz, seq, N_HEADS * V_HEAD), BF16),
        compiler_params=_params("arbitrary", "arbitrary", "arbitrary"),
        name="mla_attention",
    )(q, k, v)


def _first_index_of_max(vals):
    m = vals[0]
    for v in vals[1:]:
        m = jnp.maximum(m, v)
    idx = jnp.full(m.shape, float(len(vals) - 1), F32)
    for j in range(len(vals) - 2, -1, -1):
        idx = jnp.where(vals[j] == m, float(j), idx)
    return m, idx


def _route(h2, wr1_ref, wr2_ref, rb_ref):
    hh, hl = _split_bf16(h2)
    logits = (_dot(hh, wr1_ref[...]) + _dot(hl, wr2_ref[...])).T
    logit = logits[0:N_EXPERTS] + logits[N_EXPERTS:2 * N_EXPERTS] + logits[2 * N_EXPERTS:3 * N_EXPERTS]
    score = jax.nn.sigmoid(logit)
    biased = score + rb_ref[...]
    a = [biased[j * N_GROUPS:(j + 1) * N_GROUPS] for j in range(EXPERTS_PER_GROUP)]
    sc = [score[j * N_GROUPS:(j + 1) * N_GROUPS] for j in range(EXPERTS_PER_GROUP)]
    hi1, lo1 = jnp.maximum(a[0], a[1]), jnp.minimum(a[0], a[1])
    hi2, lo2 = jnp.maximum(a[2], a[3]), jnp.minimum(a[2], a[3])
    gscore = jnp.maximum(hi1, hi2) + jnp.maximum(jnp.minimum(hi1, hi2), jnp.maximum(lo1, lo2))
    gmax = jnp.max(gscore, axis=0, keepdims=True)
    giota = lax.broadcasted_iota(jnp.int32, gscore.shape, 0).astype(F32)
    gsel = jnp.min(jnp.where(gscore == gmax, giota, float(N_GROUPS)), axis=0, keepdims=True)
    onehot = giota == gsel
    pick = lambda t: jnp.sum(jnp.where(onehot, t, 0.0), axis=0, keepdims=True)
    bj = [pick(t) for t in a]
    sj = [pick(t) for t in sc]
    _, i1 = _first_index_of_max(bj)
    bj2 = [jnp.where(i1 == float(j), -jnp.inf, bj[j]) for j in range(EXPERTS_PER_GROUP)]
    _, i2 = _first_index_of_max(bj2)
    sel = lambda i: jnp.where(i == 0.0, sj[0], jnp.where(i == 1.0, sj[1], jnp.where(i == 2.0, sj[2], sj[3])))
    w1, w2 = sel(i1), sel(i2)
    den = w1 + w2
    base = gsel * float(EXPERTS_PER_GROUP)
    return ((base + i1).astype(jnp.int32), (base + i2).astype(jnp.int32)), (w1 / den, w2 / den)


def _mix_out_kernel(has_gate, *refs):
    if has_gate:
        a_ref, hs_ref, x_ref, wo_ref, g1_ref, g_ref, sh_ref, sc_ref, wr1_ref, wr2_ref, rb_ref, \
            x_out, h_out, idx_out, wts_out = refs
    else:
        a_ref, x_ref, wo_ref, g1_ref, g_ref, sh_ref, sc_ref, wr1_ref, wr2_ref, rb_ref, \
            x_out, h_out, idx_out, wts_out = refs
    tm = x_ref.shape[0]
    sub = min(ROW_SUB, tm)
    for i in range(tm // sub):
        rows = slice(i * sub, (i + 1) * sub)
        if has_gate:
            a = (a_ref[rows, :].astype(F32) * hs_ref[rows, :]).astype(BF16)
        else:
            a = a_ref[rows, :]
        x1 = x_ref[rows, :] + g1_ref[...] * _dot(a, wo_ref[...])
        x_out[rows, :] = x1
        h2 = _modulate(x1, g_ref[...], sh_ref[...], sc_ref[...])
        _to_tiles(h_out, h2, i * sub)
        idx, wts = _route(h2, wr1_ref, wr2_ref, rb_ref)
        for k in range(TOP_K):
            idx_out[k:k + 1, rows] = idx[k]
            wts_out[k:k + 1, rows] = wts[k]


def _mix_out(a, hs, x, w_o, gate1, gain, shift, scale, wr1, wr2, rbias):
    bsz, seq, d = x.shape
    tm = min(ROW_TILE, seq)
    row = lambda b, i: (b, i, 0)
    per_b = lambda b, i: (b, 0, 0)
    const = lambda b, i: (0, 0)
    full = lambda t: pl.BlockSpec(t.shape, const)
    vec = pl.BlockSpec((None, 1, d), per_b)
    acts = [a] if hs is None else [a, hs]
    return pl.pallas_call(
        functools.partial(_mix_out_kernel, hs is not None),
        grid=(bsz, seq // tm),
        in_specs=[pl.BlockSpec((None, tm, t.shape[-1]), row) for t in acts] + [
            pl.BlockSpec((None, tm, d), row), full(w_o), vec, full(gain), vec, vec,
            full(wr1), full(wr2), full(rbias),
        ],
        out_specs=[
            pl.BlockSpec((None, tm, d), row),
            pl.BlockSpec((None, tm * (d // LANES), LANES), row),
            pl.BlockSpec((None, TOP_K, tm), lambda b, i: (b, 0, i)),
            pl.BlockSpec((None, TOP_K, tm), lambda b, i: (b, 0, i)),
        ],
        out_shape=[
            jax.ShapeDtypeStruct((bsz, seq, d), F32),
            jax.ShapeDtypeStruct((bsz, seq * (d // LANES), LANES), F32),
            jax.ShapeDtypeStruct((bsz, TOP_K, seq), jnp.int32),
            jax.ShapeDtypeStruct((bsz, TOP_K, seq), F32),
        ],
        compiler_params=_params("arbitrary", "arbitrary"),
        name="mix_out_route",
    )(*acts, x, w_o, gate1, gain, shift, scale, wr1, wr2, rbias)


N_SUB = D_MODEL // LANES
TABLE_CHUNK = 512


def _to_tiles(ref, val, lo=0):
    n = val.shape[0]
    for s in range(N_SUB):
        ref[pl.ds(lo * N_SUB + s, n, stride=N_SUB), :] = val[:, s * LANES:(s + 1) * LANES]


def _from_tiles(ref, lo, n):
    return jnp.concatenate([ref[pl.ds(lo * N_SUB + s, n, stride=N_SUB), :] for s in range(N_SUB)], axis=1)


def _tables_kernel(idx_ref, rank_ref, cnt_ref, carry):
    @pl.when(pl.program_id(0) == 0)
    def _():
        carry[...] = jnp.zeros_like(carry)

    seq = idx_ref.shape[-1]
    ch = min(TABLE_CHUNK, seq)
    tri = jnp.where(lax.broadcasted_iota(jnp.int32, (ch, ch), 0) <= lax.broadcasted_iota(jnp.int32, (ch, ch), 1),
                    1.0, 0.0).astype(BF16)
    eiota = lax.broadcasted_iota(jnp.int32, (N_EXPERTS, ch), 0)
    cnt = carry[...]
    for k in range(TOP_K):
        for c in range(seq // ch):
            sel = eiota == idx_ref[k:k + 1, c * ch:(c + 1) * ch]
            pref = _dot(jnp.where(sel, 1.0, 0.0).astype(BF16), tri) + cnt
            rank = jnp.sum(jnp.where(sel, pref, 0.0), axis=0, keepdims=True) - 1.0
            rank_ref[k:k + 1, c * ch:(c + 1) * ch] = rank.astype(jnp.int32)
            cnt = pref[:, ch - 1:ch]
    carry[...] = cnt
    cnt_ref[...] = jnp.broadcast_to(cnt, cnt_ref.shape)


def _tables(idx):
    bsz, _, seq = idx.shape
    return pl.pallas_call(
        _tables_kernel,
        grid=(bsz,),
        in_specs=[pl.BlockSpec((None, TOP_K, seq), lambda b: (b, 0, 0))],
        out_specs=[pl.BlockSpec((None, TOP_K, seq), lambda b: (b, 0, 0)),
                   pl.BlockSpec((N_EXPERTS, LANES), lambda b: (0, 0))],
        out_shape=[jax.ShapeDtypeStruct((bsz, TOP_K, seq), jnp.int32),
                   jax.ShapeDtypeStruct((N_EXPERTS, LANES), F32)],
        scratch_shapes=[pltpu.VMEM((N_EXPERTS, 1), F32)],
        compiler_params=_params("arbitrary"),
        name="moe_tables",
    )(idx)


def _zero_runs(step, total, pad_start_ref, pad_len_ref, tail_ref):
    ops = []
    for m in range(-(-2 * N_EXPERTS // total)):
        u = step + m * total
        e = jnp.minimum(u, N_EXPERTS - 1)
        length = jnp.where(u < N_EXPERTS, pad_len_ref[e], 0)
        first = pad_start_ref[e]
        for bit in reversed(range(MOE_TILE.bit_length() - 1)):
            done = lax.shift_left(lax.shift_right_logical(length, bit + 1), bit + 1)
            ops.append((lax.bitwise_and(lax.shift_right_logical(length, bit), 1) == 1, first + done, 1 << bit))
        t = u - N_EXPERTS
        ops.append(((t >= 0) & (t < tail_ref[1]), tail_ref[0] + t * MOE_TILE, MOE_TILE))
    return ops


def _scatter_kernel(total, pad_start_ref, pad_len_ref, tail_ref, dest_ref, src_hbm, dst_hbm,
                    buf, zbuf, sem_in, sem_out, sem_z):
    n = pl.program_id(0) * pl.num_programs(1) + pl.program_id(1)
    tm = buf.shape[1]
    slot = lax.rem(n, 3)

    def load(step, sl):
        return pltpu.make_async_copy(src_hbm.at[pl.ds(step * tm, tm)], buf.at[sl], sem_in.at[sl])

    def drain(sl):
        for _ in range(TOP_K):
            pltpu.make_async_copy(buf.at[sl], dst_hbm.at[pl.ds(0, tm)], sem_out.at[sl]).wait()

    def zero_fill(step, wait):
        for pred, first, rows in _zero_runs(step, total, pad_start_ref, pad_len_ref, tail_ref):
            @pl.when(pred)
            def _(first=first, rows=rows):
                cp = pltpu.make_async_copy(zbuf.at[pl.ds(0, rows)], dst_hbm.at[pl.ds(first, rows)], sem_z)
                cp.wait() if wait else cp.start()

    @pl.when(n == 0)
    def _():
        zbuf[...] = jnp.zeros_like(zbuf)
        load(0, 0).start()
        if total > 1:
            load(1, 1).start()

    load(n, slot).wait()
    for k in range(TOP_K):
        for c in range(tm // LANES):
            def start(j, carry, k=k, c=c):
                row = dest_ref[0, k * tm + c * LANES + j]
                pltpu.make_async_copy(buf.at[slot, c * LANES + j], dst_hbm.at[row], sem_out.at[slot]).start(priority=k)
                return carry
            lax.fori_loop(0, LANES, start, 0, unroll=8)
    zero_fill(n, wait=False)

    @pl.when(n > 0)
    def _():
        drain(lax.rem(n + 2, 3))
        zero_fill(n - 1, wait=True)

    @pl.when(n + 2 < total)
    def _():
        load(n + 2, lax.rem(n + 2, 3)).start()

    @pl.when(n == total - 1)
    def _():
        drain(slot)
        zero_fill(n, wait=True)


def _index_blocks(table, tm):
    bsz, _, seq = table.shape
    nt = seq // tm
    t = table.reshape(bsz, TOP_K, nt, tm).transpose(0, 2, 1, 3)
    return t.reshape(bsz * nt, 1, TOP_K * tm), (None, 1, TOP_K * tm)


def _scatter(dest, pad_start, pad_len, tail, h2t, n_rows):
    bsz, _, seq = dest.shape
    tm = min(ROW_TILE, seq)
    nt = seq // tm
    dest4, dest_block = _index_blocks(dest, tm)
    grid_spec = pltpu.PrefetchScalarGridSpec(
        num_scalar_prefetch=3,
        grid=(bsz, nt),
        in_specs=[
            pl.BlockSpec(dest_block, lambda b, i, *_: (b * nt + i, 0, 0), memory_space=pltpu.SMEM),
            pl.BlockSpec(memory_space=pl.ANY),
        ],
        out_specs=pl.BlockSpec(memory_space=pl.ANY),
        scratch_shapes=[pltpu.VMEM((3, tm, N_SUB, LANES), F32), pltpu.VMEM((MOE_TILE, N_SUB, LANES), F32),
                        pltpu.SemaphoreType.DMA((3,)), pltpu.SemaphoreType.DMA((3,)), pltpu.SemaphoreType.DMA(())],
    )
    return pl.pallas_call(
        functools.partial(_scatter_kernel, bsz * nt),
        grid_spec=grid_spec,
        out_shape=jax.ShapeDtypeStruct((n_rows, N_SUB, LANES), F32),
        compiler_params=_params("arbitrary", "arbitrary"),
        name="moe_scatter",
    )(pad_start, pad_len, tail, dest4, h2t.reshape(bsz * seq, N_SUB, LANES))


def _expert_kernel(blk_exp_ref, blk_first_ref, n_used_ref,
                   xs_ref, wgu_ref, wdn_ref, ys_ref, wgu_bf, wdn_bf):
    i = pl.program_id(0)

    @pl.when(i < n_used_ref[0])
    def _():
        @pl.when(blk_first_ref[i] == 1)
        def _():
            wgu_bf[...] = wgu_ref[...].astype(BF16)
            wdn_bf[...] = wdn_ref[...].astype(BF16)

        x = _from_tiles(xs_ref, 0, MOE_TILE).astype(BF16)
        gu = _dot(x, wgu_bf[...])
        g = gu[:, :D_EXPERT]
        u = gu[:, D_EXPERT:]
        mid = (g * jax.nn.sigmoid(g) * u).astype(BF16)
        _to_tiles(ys_ref, _dot(mid, wdn_bf[...]))

    @pl.when(i >= n_used_ref[0])
    def _():
        ys_ref[...] = jnp.zeros_like(ys_ref)


def _experts(blk_exp, blk_first, n_used, xs, w_gu, w_dn, layer):
    d = D_MODEL
    nb = xs.shape[0] // (MOE_TILE * N_SUB)
    tile = lambda i, *_: (i, 0)
    grid_spec = pltpu.PrefetchScalarGridSpec(
        num_scalar_prefetch=3,
        grid=(nb,),
        in_specs=[
            pl.BlockSpec((MOE_TILE * N_SUB, LANES), tile),
            pl.BlockSpec((None, None, d, 2 * D_EXPERT), lambda i, be, *_: (layer, be[i], 0, 0)),
            pl.BlockSpec((None, None, D_EXPERT, d), lambda i, be, *_: (layer, be[i], 0, 0)),
        ],
        out_specs=pl.BlockSpec((MOE_TILE * N_SUB, LANES), tile),
        scratch_shapes=[pltpu.VMEM((d, 2 * D_EXPERT), BF16), pltpu.VMEM((D_EXPERT, d), BF16)],
    )
    return pl.pallas_call(
        _expert_kernel,
        grid_spec=grid_spec,
        out_shape=jax.ShapeDtypeStruct(xs.shape, F32),
        compiler_params=_params("arbitrary"),
        name="moe_experts",
    )(blk_exp, blk_first, n_used, xs, w_gu, w_dn)


def _combine_kernel(dcur_ref, dnxt_ref, ys_hbm, x_ref, wts_ref, g2_ref, x_out, buf0, buf1, sem):
    nt = pl.num_programs(1)
    n = pl.program_id(0) * nt + pl.program_id(1)
    total = pl.num_programs(0) * nt
    tm = x_ref.shape[0]
    bufs = (buf0, buf1)

    def copy(d_ref, sl, k, r):
        src = ys_hbm.at[pl.ds(pl.multiple_of(d_ref[0, k * tm + r], N_SUB), N_SUB)]
        dst = bufs[sl].at[pl.ds(pl.multiple_of((k * tm + r) * N_SUB, N_SUB), N_SUB)]
        return pltpu.make_async_copy(src, dst, sem.at[sl])

    def drain(sl):
        pltpu.make_async_copy(ys_hbm.at[pl.ds(0, TOP_K * tm * N_SUB)], bufs[sl], sem.at[sl]).wait()

    @pl.when(n == 0)
    def _():
        for k in range(TOP_K):
            def start(r, carry, k=k):
                copy(dcur_ref, 0, k, r).start(priority=k)
                return carry
            lax.fori_loop(0, tm, start, 0, unroll=8)

    def step(sl):
        drain(sl)
        for r in range(tm):
            for k in range(TOP_K):
                copy(dnxt_ref, 1 - sl, k, r).start(priority=k)
        w = wts_ref[...]
        y = w[:, 0:1] * _from_tiles(bufs[sl], 0, tm) + w[:, 1:2] * _from_tiles(bufs[sl], tm, tm)
        x_out[...] = x_ref[...] + g2_ref[...] * y

        @pl.when(n == total - 1)
        def _():
            drain(1 - sl)

    for sl in range(2):
        pl.when(lax.rem(n, 2) == sl)(functools.partial(step, sl))


def _combine(dest8, ys, x, wts_col, gate2):
    bsz, seq, d = x.shape
    tm = min(ROW_TILE, seq)
    nt = seq // tm

    def nxt(b, i):
        return (jnp.minimum(b * nt + i + 1, bsz * nt - 1), 0, 0)

    dest8, dest_block = _index_blocks(dest8, tm)
    return pl.pallas_call(
        _combine_kernel,
        grid=(bsz, nt),
        in_specs=[
            pl.BlockSpec(dest_block, lambda b, i: (b * nt + i, 0, 0), memory_space=pltpu.SMEM),
            pl.BlockSpec(dest_block, nxt, memory_space=pltpu.SMEM),
            pl.BlockSpec(memory_space=pl.ANY),
            pl.BlockSpec((None, tm, d), lambda b, i: (b, i, 0)),
            pl.BlockSpec((None, tm, TOP_K), lambda b, i: (b, i, 0)),
            pl.BlockSpec((None, 1, d), lambda b, i: (b, 0, 0)),
        ],
        out_specs=pl.BlockSpec((None, tm, d), lambda b, i: (b, i, 0)),
        out_shape=jax.ShapeDtypeStruct((bsz, seq, d), F32),
        scratch_shapes=[pltpu.VMEM((TOP_K * tm * N_SUB, LANES), F32), pltpu.VMEM((TOP_K * tm * N_SUB, LANES), F32),
                        pltpu.SemaphoreType.DMA((2,))],
        compiler_params=_params("arbitrary", "arbitrary"),
        name="moe_combine",
    )(dest8, dest8, ys, x, wts_col, gate2)


def _lookup(table, keys):
    hit = keys[..., None] == jnp.arange(table.shape[0], dtype=jnp.int32)
    return jnp.sum(jnp.where(hit, table, 0), axis=-1).astype(jnp.int32)


def _count_le(bounds, q):
    return jnp.sum((bounds <= q[..., None]).astype(jnp.int32), axis=-1)


def _moe(h2t, idx, wts, x, gate2, w_gu, w_dn, layer):
    bsz, seq, _ = x.shape
    n_rows = bsz * seq * TOP_K + N_EXPERTS * MOE_TILE
    nb = n_rows // MOE_TILE
    rank, cnt = _tables(idx)
    counts = cnt[:, 0].astype(jnp.int32)
    padded = ((counts + MOE_TILE - 1) // MOE_TILE) * MOE_TILE
    pend = jnp.cumsum(padded)
    pstart = pend - padded
    dest = _lookup(pstart, idx) + rank
    blk_row = jnp.arange(nb, dtype=jnp.int32) * MOE_TILE
    blk_exp = jnp.minimum(_count_le(pend, blk_row), N_EXPERTS - 1)
    blk_first = (blk_row == _lookup(pstart, blk_exp)).astype(jnp.int32)
    n_used = (pend[-1:] // MOE_TILE).astype(jnp.int32)
    tail = jnp.concatenate([pend[-1:], (n_rows - pend[-1:]) // MOE_TILE]).astype(jnp.int32)
    xs = _scatter(dest, (pstart + counts).astype(jnp.int32), (padded - counts).astype(jnp.int32), tail, h2t, n_rows)
    ys = _experts(blk_exp, blk_first, n_used, xs.reshape(n_rows * N_SUB, LANES), w_gu, w_dn, layer)
    return _combine(dest * N_SUB, ys, x, wts.transpose(0, 2, 1), gate2)


def _rnn_in_kernel(x_ref, g_ref, sh_ref, sc_ref, w_ref, gate_out, xb_out):
    h = _modulate(x_ref[...], g_ref[...], sh_ref[...], sc_ref[...])
    u = _dot(h.astype(BF16), w_ref[...])
    gate_out[...] = jax.nn.gelu(u[:, :D_RNN]).astype(BF16)
    xb_out[...] = u[:, D_RNN:]


def _rnn_in(x, gain, shift, scale, w_in):
    bsz, seq, d = x.shape
    tm = min(ROW_TILE, seq)
    row = lambda b, i: (b, i, 0)
    per_b = lambda b, i: (b, 0, 0)
    const = lambda b, i: (0, 0)
    return pl.pallas_call(
        _rnn_in_kernel,
        grid=(bsz, seq // tm),
        in_specs=[
            pl.BlockSpec((None, tm, d), row),
            pl.BlockSpec(gain.shape, const),
            pl.BlockSpec((None, 1, d), per_b),
            pl.BlockSpec((None, 1, d), per_b),
            pl.BlockSpec(w_in.shape, const),
        ],
        out_specs=[pl.BlockSpec((None, tm, D_RNN), row), pl.BlockSpec((None, tm, D_RNN), row)],
        out_shape=[jax.ShapeDtypeStruct((bsz, seq, D_RNN), BF16),
                   jax.ShapeDtypeStruct((bsz, seq, D_RNN), F32)],
        compiler_params=_params("arbitrary", "arbitrary"),
        name="rnn_in",
    )(x, gain, shift, scale, w_in)


def _lru_kernel(xb_ref, cw_ref, cb_ref, wcat_ref, bcat_ref, lam_ref, hs_ref,
                xi_ref, af_ref, bf_ref, ab_ref, bb_ref, hf_ref, hb_ref, sum_ref):
    seq, c = xb_ref.shape
    seg_len = seq // SUBLANES
    n_slab = c // LANES
    n_rows = seg_len * SUBLANES
    halo = (CONV_W // 2) * SUBLANES
    row = lax.broadcasted_iota(jnp.int32, (SUBLANES, LANES), 0)
    for sl in range(n_slab):
        lanes = slice(sl * LANES, (sl + 1) * LANES)
        for g in range(SUBLANES):
            xi_ref[sl, pl.ds(halo + g, seg_len, stride=SUBLANES), :] = xb_ref[g * seg_len:(g + 1) * seg_len, lanes]
        for back in (1, 2):
            prev = xi_ref[sl, halo + (seg_len - back) * SUBLANES:halo + (seg_len - back + 1) * SUBLANES, :]
            xi_ref[sl, halo - back * SUBLANES:halo - (back - 1) * SUBLANES, :] = jnp.where(
                row == 0, 0.0, pltpu.roll(prev, 1, 0))
        nxt = xi_ref[sl, halo:halo + SUBLANES, :]
        xi_ref[sl, halo + n_rows:halo + n_rows + SUBLANES, :] = jnp.where(
            row == SUBLANES - 1, 0.0, pltpu.roll(nxt, SUBLANES - 1, 0))

    cw = cw_ref[...]
    cb = cb_ref[...]
    lam = lam_ref[...]
    neg = -lam
    softplus = jnp.maximum(neg, 0.0) + jnp.log1p(jnp.exp(-jnp.abs(neg)))
    half_rate = (-0.5 * LRU_C) * softplus
    rows = min(SCAN_ROWS, n_rows)
    n_chunks = n_rows // rows

    for ci in range(n_chunks):
        i0 = ci * rows
        taps = []
        for k in range(CONV_W):
            lo = halo + i0 + (k - CONV_W // 2) * SUBLANES
            taps.append(jnp.concatenate([xi_ref[sl, lo:lo + rows, :] for sl in range(n_slab)], axis=1))
        xc = cb
        for k in range(CONV_W):
            xc = xc + taps[k] * cw[k:k + 1, :]
        xcb = xc.astype(BF16)
        xh = 0.5 * xc
        for dirn, (a_ref, b_ref) in enumerate(((af_ref, bf_ref), (ab_ref, bb_ref))):
            cols = slice(2 * dirn * c, 2 * (dirn + 1) * c)
            th = jnp.tanh(_dot(xcb, wcat_ref[:, cols]) + bcat_ref[:, cols])
            hr = half_rate[dirn:dirn + 1, :]
            log_a = hr * th[:, :c] + hr
            a = jnp.exp(log_a)
            mult = jnp.sqrt(jnp.tanh(-log_a) * (a * a + 1.0))
            if dirn == 0 and ci == 0:
                mult = jnp.where(lax.broadcasted_iota(jnp.int32, mult.shape, 0) == 0, 1.0, mult)
            if dirn == 1 and ci == n_chunks - 1:
                mult = jnp.where(lax.broadcasted_iota(jnp.int32, mult.shape, 0) == rows - 1, 1.0, mult)
            b = mult * (th[:, c:] + 1.0) * xh
            for sl in range(n_slab):
                a_ref[sl, i0:i0 + rows, :] = a[:, sl * LANES:(sl + 1) * LANES]
                b_ref[sl, i0:i0 + rows, :] = b[:, sl * LANES:(sl + 1) * LANES]

    def step_rows(cidx):
        fwd = pl.ds(pl.multiple_of(cidx * SUBLANES, SUBLANES), SUBLANES)
        bwd = pl.ds(pl.multiple_of((seg_len - 1 - cidx) * SUBLANES, SUBLANES), SUBLANES)
        return fwd, bwd

    zero = jnp.zeros((SUBLANES, LANES), F32)
    one = jnp.ones((SUBLANES, LANES), F32)

    def totals(cidx, carry):
        fwd, bwd = step_rows(cidx)
        out = []
        for sl in range(n_slab):
            hf, pf, hb, pb = carry[sl]
            af, ab = af_ref[sl, fwd, :], ab_ref[sl, bwd, :]
            out.append((af * hf + bf_ref[sl, fwd, :], af * pf, ab * hb + bb_ref[sl, bwd, :], ab * pb))
        return tuple(out)
    tot = lax.fori_loop(0, seg_len, totals, tuple((zero, one, zero, one) for _ in range(n_slab)), unroll=8)

    enter = []
    for sl in range(n_slab):
        hf, pf, hb, pb = tot[sl]
        cf, cbk = zero, zero
        for _ in range(SUBLANES - 1):
            cf = jnp.where(row == 0, 0.0, pltpu.roll(hf + pf * cf, 1, 0))
            cbk = jnp.where(row == SUBLANES - 1, 0.0, pltpu.roll(hb + pb * cbk, SUBLANES - 1, 0))
        enter.append((cf, cbk))

    def states(meet, cidx, carry):
        fwd, bwd = step_rows(cidx)
        out = []
        for sl in range(n_slab):
            hf, hb = carry[sl]
            hf = af_ref[sl, fwd, :] * hf + bf_ref[sl, fwd, :]
            hb = ab_ref[sl, bwd, :] * hb + bb_ref[sl, bwd, :]
            if meet:
                sum_ref[sl, fwd, :] = hf + hb_ref[sl, fwd, :]
                sum_ref[sl, bwd, :] = hb + hf_ref[sl, bwd, :]
            else:
                hf_ref[sl, fwd, :] = hf
                hb_ref[sl, bwd, :] = hb
            out.append((hf, hb))
        return tuple(out)
    mid = lax.fori_loop(0, seg_len // 2, functools.partial(states, False), tuple(enter), unroll=8)
    lax.fori_loop(seg_len // 2, seg_len, functools.partial(states, True), mid, unroll=8)

    for g in range(SUBLANES):
        for sl in range(n_slab):
            hs_ref[g * seg_len:(g + 1) * seg_len, sl * LANES:(sl + 1) * LANES] = (
                sum_ref[sl, pl.ds(g, seg_len, stride=SUBLANES), :])


def _lru(xb, conv_w, conv_b, wcat, bcat, lam):
    bsz, seq, _ = xb.shape
    c = RNN_BW
    blk = lambda b, n: (b, 0, n)
    return pl.pallas_call(
        _lru_kernel,
        grid=(bsz, RNN_BLOCKS),
        in_specs=[
            pl.BlockSpec((None, seq, c), blk),
            pl.BlockSpec((CONV_W, c), lambda b, n: (0, n)),
            pl.BlockSpec((1, c), lambda b, n: (0, n)),
            pl.BlockSpec((None, c, 4 * c), lambda b, n: (n, 0, 0)),
            pl.BlockSpec((None, 1, 4 * c), lambda b, n: (n, 0, 0)),
            pl.BlockSpec((2, c), lambda b, n: (0, n)),
        ],
        out_specs=pl.BlockSpec((None, seq, c), blk),
        out_shape=jax.ShapeDtypeStruct((bsz, seq, D_RNN), F32),
        scratch_shapes=[pltpu.VMEM((c // LANES, seq + (CONV_W - 1) * SUBLANES, LANES), F32)]
        + [pltpu.VMEM((c // LANES, seq, LANES), F32)] * 7,
        compiler_params=_params("arbitrary", "arbitrary"),
        name="rglru_scan",
    )(xb, conv_w, conv_b, wcat, bcat, lam)


def _mla_weights(w_in, w_q_b, w_kv_b, q_norm, k_norm):
    half = QK_ROPE // 2

    def slab(t):
        return jnp.pad(t, [(0, 0)] * (t.ndim - 1) + [(0, LANES - QK_HEAD)])

    def rot_slab(t):
        rope = t[..., QK_NOPE:]
        swapped = jnp.concatenate([jnp.zeros_like(t[..., :QK_NOPE]), rope[..., half:], rope[..., :half]], axis=-1)
        return slab(swapped)

    kpe = jnp.pad(w_in[:, Q_LORA + KV_LORA:], ((0, 0), (QK_NOPE, 0)))
    w_in_p = jnp.concatenate([w_in[:, :Q_LORA + KV_LORA], slab(kpe), rot_slab(kpe)], axis=1).astype(BF16)
    wq = w_q_b.reshape(Q_LORA, N_HEADS, QK_HEAD)
    wq_p = slab(wq).reshape(Q_LORA, N_HEADS * LANES).astype(BF16)
    wq_rot = rot_slab(wq).reshape(Q_LORA, N_HEADS * LANES).astype(BF16)
    wkv = w_kv_b.reshape(KV_LORA, N_HEADS, QK_NOPE + V_HEAD)
    wk = jnp.pad(wkv[:, :, :QK_NOPE], ((0, 0), (0, 0), (0, LANES - QK_NOPE))).reshape(KV_LORA, N_HEADS * LANES)
    wv = wkv[:, :, QK_NOPE:].reshape(KV_LORA, N_HEADS * V_HEAD)
    w_kv_p = jnp.concatenate([wk, wv], axis=1).astype(BF16)
    gains = lambda g: jnp.stack([slab(g), rot_slab(g)], axis=0)
    return w_in_p, wq_p, wq_rot, w_kv_p, gains(q_norm), gains(k_norm)


def _rope_tables(positions):
    half = QK_ROPE // 2
    inv_freq = ROPE_THETA ** (-jnp.arange(half, dtype=F32) / half)
    ang = positions.astype(F32)[..., None] * inv_freq
    cos, sin = jnp.cos(ang), jnp.sin(ang)
    lead = positions.shape + (QK_NOPE,)
    tail = positions.shape + (LANES - QK_HEAD,)
    cos_t = jnp.concatenate([jnp.ones(lead, F32), cos, cos, jnp.ones(tail, F32)], axis=-1)
    sin_t = jnp.concatenate([jnp.zeros(lead, F32), -sin, sin, jnp.zeros(tail, F32)], axis=-1)
    return cos_t, sin_t


def _router_weights(w_router, router_bias):
    perm = (jnp.arange(N_EXPERTS) % N_GROUPS) * EXPERTS_PER_GROUP + jnp.arange(N_EXPERTS) // N_GROUPS
    w = w_router[:, perm]
    hi = w.astype(BF16)
    lo = (w - hi.astype(F32)).astype(BF16)
    z = jnp.zeros_like(hi)
    wr1 = jnp.concatenate([hi, lo, z, z], axis=1)
    wr2 = jnp.concatenate([z, z, hi, z], axis=1)
    return wr1, wr2, router_bias[perm].reshape(N_EXPERTS, 1).astype(F32)


def kernel(x, c, positions, norm_mix, norm_ffn, w_ada, b_ada, mla_w_in, mla_q_a_norm, mla_kv_a_norm, mla_w_q_b, mla_w_kv_b, mla_q_norm, mla_k_norm, mla_w_o, rnn_w_in, rnn_conv_w, rnn_conv_b, rnn_lam_f, rnn_w_rf, rnn_b_rf, rnn_w_if, rnn_b_if, rnn_lam_b, rnn_w_rb, rnn_b_rb, rnn_w_ib, rnn_b_ib, rnn_w_o, w_router, router_bias, moe_w_gu, moe_w_dn):
    bsz, seq, d = x.shape
    depth = w_ada.shape[0]
    mod = _ada(c, w_ada, b_ada)
    wr1, wr2, rbias = _router_weights(w_router, router_bias)
    cos_t, sin_t = _rope_tables(positions)
    vec = lambda v: v.reshape(1, -1)
    for i in range(depth):
        sh1, sc1, g1, sh2, sc2, g2 = [mod[i, :, k * d:(k + 1) * d].reshape(bsz, 1, d) for k in range(6)]
        j = i // 2
        if i % 2 == 0:
            w_in_p, wq, wq_rot, wkv, qn, kn = _mla_weights(mla_w_in[j], mla_w_q_b[j], mla_w_kv_b[j],
                                                           mla_q_norm[j], mla_k_norm[j])
            q, k, v = _mla_in(x, vec(norm_mix[i]), sh1, sc1, w_in_p, vec(mla_q_a_norm[j]),
                              vec(mla_kv_a_norm[j]), wq, wq_rot, wkv, qn, kn, cos_t, sin_t)
            a = _attention(q, k, v)
            hs = None
            w_o = mla_w_o[j].astype(BF16)
        else:
            a, xb = _rnn_in(x, vec(norm_mix[i]), sh1, sc1, rnn_w_in[j].astype(BF16))
            wcat = (0.5 * jnp.concatenate([rnn_w_rf[j], rnn_w_if[j], rnn_w_rb[j], rnn_w_ib[j]], axis=-1)).astype(BF16)
            bcat = jnp.stack([b.reshape(RNN_BLOCKS, RNN_BW) for b in
                              (rnn_b_rf[j], rnn_b_if[j], rnn_b_rb[j], rnn_b_ib[j])], axis=1)
            bcat = 0.5 * bcat.reshape(RNN_BLOCKS, 1, 4 * RNN_BW)
            lam = jnp.stack([rnn_lam_f[j], rnn_lam_b[j]], axis=0)
            hs = _lru(xb, rnn_conv_w[j], vec(rnn_conv_b[j]), wcat, bcat, lam)
            w_o = rnn_w_o[j].astype(BF16)
        x, h2, idx, wts = _mix_out(a, hs, x, w_o, g1, vec(norm_ffn[i]), sh2, sc2, wr1, wr2, rbias)
        x = _moe(h2, idx, wts, x, g2, moe_w_gu, moe_w_dn, i)
    return x
```

```python
import functools

import jax
import jax.numpy as jnp
from jax import lax
from jax.experimental import pallas as pl
from jax.experimental.pallas import tpu as pltpu

F32 = jnp.float32
BF16 = jnp.bfloat16

D_MODEL = 1024
N_HEADS = 16
Q_LORA = 384
KV_LORA = 256
QK_NOPE = 64
QK_ROPE = 32
QK_HEAD = QK_NOPE + QK_ROPE
V_HEAD = 64
ROPE_THETA = 10000.0
D_RNN = D_MODEL
RNN_BLOCKS = 4
RNN_BW = D_RNN // RNN_BLOCKS
CONV_W = 4
LRU_C = 8.0
N_EXPERTS = 32
N_GROUPS = 8
EXPERTS_PER_GROUP = N_EXPERTS // N_GROUPS
TOP_K = 2
D_EXPERT = 512
EPS = 1e-6
LOG2_E = 1.4426950408889634

LANES = 128
SUBLANES = 8
VMEM_LIMIT = 52 * 1024 * 1024

ROW_TILE = 512
ROW_SUB = 256
Q_TILE = 2048
Q_SUB = 256
MAX_SAFE_SHIFT = 60.0
MOE_TILE = 512
SCAN_ROWS = 256


def _dot(a, b):
    return jnp.dot(a, b, preferred_element_type=F32)


def _split_bf16(a):
    hi = a.astype(BF16)
    lo = (a - hi.astype(F32)).astype(BF16)
    return hi, lo


def _dot_split(a, b):
    ah, al = _split_bf16(a)
    bh, bl = _split_bf16(b)
    return _dot(ah, bh) + (_dot(ah, bl) + _dot(al, bh))


def _rms(x, gain, n):
    ms = jnp.sum(x * x, axis=-1, keepdims=True) * (1.0 / n)
    return x * lax.rsqrt(ms + EPS) * gain


def _modulate(x, gain, shift, scale):
    return _rms(x, gain, x.shape[-1]) * (1.0 + scale) + shift


def _params(*sem):
    return pltpu.CompilerParams(dimension_semantics=sem, vmem_limit_bytes=VMEM_LIMIT)


def _ada_kernel(c_ref, w_ref, b_ref, o_ref):
    c = c_ref[...]
    o_ref[...] = _dot_split(c * jax.nn.sigmoid(c), w_ref[...]) + b_ref[...]


def _ada(c, w_ada, b_ada):
    depth, d, n = w_ada.shape
    bsz = c.shape[0]
    tn = 1536
    return pl.pallas_call(
        _ada_kernel,
        grid=(depth, n // tn),
        in_specs=[
            pl.BlockSpec((bsz, d), lambda l, j: (0, 0)),
            pl.BlockSpec((None, d, tn), lambda l, j: (l, 0, j)),
            pl.BlockSpec((None, 1, tn), lambda l, j: (l, 0, j)),
        ],
        out_specs=pl.BlockSpec((None, bsz, tn), lambda l, j: (l, 0, j)),
        out_shape=jax.ShapeDtypeStruct((depth, bsz, n), F32),
        compiler_params=_params("arbitrary", "arbitrary"),
        name="adaln_mod",
    )(c, w_ada, b_ada.reshape(depth, 1, n))


def _head_scale(s):
    return lax.rsqrt(jnp.sum(s * s, axis=-1, keepdims=True) * (1.0 / QK_HEAD) + EPS)


def _mla_in_kernel(x_ref, g_ref, sh_ref, sc_ref, win_ref, qan_ref, kvan_ref, wq_ref, wqr_ref, wkv_ref,
                   qn_ref, kn_ref, cos_ref, sin_ref, q_out, k_out, v_out):
    h = _modulate(x_ref[...], g_ref[...], sh_ref[...], sc_ref[...])
    lat = _dot(h.astype(BF16), win_ref[...])
    q_lat = lat[:, :Q_LORA]
    kv_lat = lat[:, Q_LORA:Q_LORA + KV_LORA]
    kpe = lat[:, Q_LORA + KV_LORA:Q_LORA + KV_LORA + LANES]
    kpe_rot = lat[:, Q_LORA + KV_LORA + LANES:]
    qn = _rms(q_lat, qan_ref[...], Q_LORA).astype(BF16)
    q_all = _dot(qn, wq_ref[...])
    q_rot = _dot(qn, wqr_ref[...])
    kv_all = _dot(_rms(kv_lat, kvan_ref[...], KV_LORA).astype(BF16), wkv_ref[...])
    cos_t = cos_ref[...]
    sin_t = sin_ref[...]
    q_scale = LOG2_E * QK_HEAD ** -0.5
    cq = cos_t * (qn_ref[0:1, :] * q_scale)
    sq = sin_t * (qn_ref[1:2, :] * q_scale)
    ck = cos_t * kn_ref[0:1, :]
    k_rot_term = kpe_rot * (sin_t * kn_ref[1:2, :])
    for hh in range(N_HEADS):
        sl = slice(hh * LANES, (hh + 1) * LANES)
        s = q_all[:, sl]
        q_out[hh] = ((s * cq + q_rot[:, sl] * sq) * _head_scale(s)).astype(BF16)
        s = kv_all[:, sl] + kpe
        k_out[hh] = ((s * ck + k_rot_term) * _head_scale(s)).astype(BF16)
    v_out[...] = kv_all[:, N_HEADS * LANES:].astype(BF16)


def _mla_in(x, gain, shift, scale, w_in, q_a_norm, kv_a_norm, w_q, w_q_rot, w_kv, q_norm, k_norm, cos_t, sin_t):
    bsz, seq, d = x.shape
    tm = min(ROW_TILE, seq)
    row = lambda b, i: (b, i, 0)
    per_b = lambda b, i: (b, 0, 0)
    const = lambda b, i: (0, 0)
    full = lambda a: pl.BlockSpec(a.shape, const)
    return pl.pallas_call(
        _mla_in_kernel,
        grid=(bsz, seq // tm),
        in_specs=[
            pl.BlockSpec((None, tm, d), row),
            full(gain),
            pl.BlockSpec((None, 1, d), per_b),
            pl.BlockSpec((None, 1, d), per_b),
            full(w_in), full(q_a_norm), full(kv_a_norm), full(w_q), full(w_q_rot), full(w_kv),
            full(q_norm), full(k_norm),
            pl.BlockSpec((None, tm, LANES), row),
            pl.BlockSpec((None, tm, LANES), row),
        ],
        out_specs=[
            pl.BlockSpec((None, N_HEADS, tm, LANES), lambda b, i: (b, 0, i, 0)),
            pl.BlockSpec((None, N_HEADS, tm, LANES), lambda b, i: (b, 0, i, 0)),
            pl.BlockSpec((None, tm, N_HEADS * V_HEAD), row),
        ],
        out_shape=[
            jax.ShapeDtypeStruct((bsz, N_HEADS, seq, LANES), BF16),
            jax.ShapeDtypeStruct((bsz, N_HEADS, seq, LANES), BF16),
            jax.ShapeDtypeStruct((bsz, seq, N_HEADS * V_HEAD), BF16),
        ],
        compiler_params=_params("arbitrary", "arbitrary"),
        name="mla_in",
    )(x, gain, shift, scale, w_in, q_a_norm, kv_a_norm, w_q, w_q_rot, w_kv, q_norm, k_norm, cos_t, sin_t)


def _attn_kernel(bounded, shift_ref, q_ref, k_ref, v_ref, o_ref):
    v = v_ref[...]
    lane_v = lax.broadcasted_iota(jnp.int32, v.shape, 1)
    v_heads = [jnp.where(lane_v < V_HEAD, v, jnp.ones((), BF16)), jnp.where(lane_v >= V_HEAD, v, jnp.ones((), BF16))]
    lane = lax.broadcasted_iota(jnp.int32, (Q_SUB, LANES), 1)
    for i in range(q_ref.shape[1] // Q_SUB):
        rows = slice(i * Q_SUB, (i + 1) * Q_SUB)
        outs = []
        for j in range(2):
            s = lax.dot_general(q_ref[j, rows, :], k_ref[j], (((1,), (1,)), ((), ())),
                                preferred_element_type=F32)
            m = shift_ref[0] if bounded else jnp.max(s, axis=-1, keepdims=True)
            o = _dot(jnp.exp2(s - m).astype(BF16), v_heads[j])
            denom = o[:, V_HEAD:V_HEAD + 1] if j == 0 else o[:, 0:1]
            outs.append(o / denom)
        o_ref[rows, :] = jnp.where(lane < V_HEAD, outs[0], outs[1]).astype(BF16)


def _attention_call(bounded, shift, q, k, v):
    bsz, _, seq, _ = q.shape
    tq = min(Q_TILE, seq)
    assert tq % Q_SUB == 0
    grid_spec = pltpu.PrefetchScalarGridSpec(
        num_scalar_prefetch=1,
        grid=(bsz, N_HEADS // 2, seq // tq),
        in_specs=[
            pl.BlockSpec((None, 2, tq, LANES), lambda b, h, i, *_: (b, h, i, 0)),
            pl.BlockSpec((None, 2, seq, LANES), lambda b, h, i, *_: (b, h, 0, 0)),
            pl.BlockSpec((None, seq, LANES), lambda b, h, i, *_: (b, 0, h)),
        ],
        out_specs=pl.BlockSpec((None, tq, LANES), lambda b, h, i, *_: (b, i, h)),
    )
    return pl.pallas_call(
        functools.partial(_attn_kernel, bounded),
        grid_spec=grid_spec,
        out_shape=jax.ShapeDtypeStruct((bsz, seq, N_HEADS * V_HEAD), BF16),
        compiler_params=_params("arbitrary", "arbitrary", "arbitrary"),
        name="mla_attention",
    )(shift, q, k, v)


def _attention(q, k, v, q_gain, k_gain):
    score_bound = 1.02 * LOG2_E * QK_HEAD ** 0.5 * jnp.max(jnp.abs(q_gain)) * jnp.max(jnp.abs(k_gain))
    shift = score_bound.reshape(1).astype(F32)
    return lax.cond(score_bound <= MAX_SAFE_SHIFT,
                    functools.partial(_attention_call, True), functools.partial(_attention_call, False),
                    shift, q, k, v)


def _first_index_of_max(vals):
    m = vals[0]
    for v in vals[1:]:
        m = jnp.maximum(m, v)
    idx = jnp.full(m.shape, float(len(vals) - 1), F32)
    for j in range(len(vals) - 2, -1, -1):
        idx = jnp.where(vals[j] == m, float(j), idx)
    return m, idx


def _route(h2, wr1_ref, wr2_ref, rb_ref):
    hh, hl = _split_bf16(h2)
    logits = (_dot(hh, wr1_ref[...]) + _dot(hl, wr2_ref[...])).T
    logit = logits[0:N_EXPERTS] + logits[N_EXPERTS:2 * N_EXPERTS] + logits[2 * N_EXPERTS:3 * N_EXPERTS]
    score = jax.nn.sigmoid(logit)
    biased = score + rb_ref[...]
    a = [biased[j * N_GROUPS:(j + 1) * N_GROUPS] for j in range(EXPERTS_PER_GROUP)]
    sc = [score[j * N_GROUPS:(j + 1) * N_GROUPS] for j in range(EXPERTS_PER_GROUP)]
    hi1, lo1 = jnp.maximum(a[0], a[1]), jnp.minimum(a[0], a[1])
    hi2, lo2 = jnp.maximum(a[2], a[3]), jnp.minimum(a[2], a[3])
    gscore = jnp.maximum(hi1, hi2) + jnp.maximum(jnp.minimum(hi1, hi2), jnp.maximum(lo1, lo2))
    gmax = jnp.max(gscore, axis=0, keepdims=True)
    giota = lax.broadcasted_iota(jnp.int32, gscore.shape, 0).astype(F32)
    gsel = jnp.min(jnp.where(gscore == gmax, giota, float(N_GROUPS)), axis=0, keepdims=True)
    onehot = giota == gsel
    pick = lambda t: jnp.sum(jnp.where(onehot, t, 0.0), axis=0, keepdims=True)
    bj = [pick(t) for t in a]
    sj = [pick(t) for t in sc]
    _, i1 = _first_index_of_max(bj)
    bj2 = [jnp.where(i1 == float(j), -jnp.inf, bj[j]) for j in range(EXPERTS_PER_GROUP)]
    _, i2 = _first_index_of_max(bj2)
    sel = lambda i: jnp.where(i == 0.0, sj[0], jnp.where(i == 1.0, sj[1], jnp.where(i == 2.0, sj[2], sj[3])))
    w1, w2 = sel(i1), sel(i2)
    den = w1 + w2
    base = gsel * float(EXPERTS_PER_GROUP)
    return ((base + i1).astype(jnp.int32), (base + i2).astype(jnp.int32)), (w1 / den, w2 / den)


def _mix_out_kernel(has_gate, *refs):
    if has_gate:
        a_ref, hs_ref, x_ref, wo_ref, g1_ref, g_ref, sh_ref, sc_ref, wr1_ref, wr2_ref, rb_ref, \
            x_out, h_out, idx_out, wts_out = refs
    else:
        a_ref, x_ref, wo_ref, g1_ref, g_ref, sh_ref, sc_ref, wr1_ref, wr2_ref, rb_ref, \
            x_out, h_out, idx_out, wts_out = refs
    tm = x_ref.shape[0]
    sub = min(ROW_SUB, tm)
    for i in range(tm // sub):
        rows = slice(i * sub, (i + 1) * sub)
        if has_gate:
            a = (a_ref[rows, :].astype(F32) * hs_ref[rows, :]).astype(BF16)
        else:
            a = a_ref[rows, :]
        x1 = x_ref[rows, :] + g1_ref[...] * _dot(a, wo_ref[...])
        x_out[rows, :] = x1
        h2 = _modulate(x1, g_ref[...], sh_ref[...], sc_ref[...])
        _to_tiles(h_out, h2, i * sub)
        idx, wts = _route(h2, wr1_ref, wr2_ref, rb_ref)
        for k in range(TOP_K):
            idx_out[k:k + 1, rows] = idx[k]
            wts_out[k:k + 1, rows] = wts[k]


def _mix_out(a, hs, x, w_o, gate1, gain, shift, scale, wr1, wr2, rbias):
    bsz, seq, d = x.shape
    tm = min(ROW_TILE, seq)
    row = lambda b, i: (b, i, 0)
    per_b = lambda b, i: (b, 0, 0)
    const = lambda b, i: (0, 0)
    full = lambda t: pl.BlockSpec(t.shape, const)
    vec = pl.BlockSpec((None, 1, d), per_b)
    acts = [a] if hs is None else [a, hs]
    return pl.pallas_call(
        functools.partial(_mix_out_kernel, hs is not None),
        grid=(bsz, seq // tm),
        in_specs=[pl.BlockSpec((None, tm, t.shape[-1]), row) for t in acts] + [
            pl.BlockSpec((None, tm, d), row), full(w_o), vec, full(gain), vec, vec,
            full(wr1), full(wr2), full(rbias),
        ],
        out_specs=[
            pl.BlockSpec((None, tm, d), row),
            pl.BlockSpec((None, tm * (d // LANES), LANES), row),
            pl.BlockSpec((None, TOP_K, tm), lambda b, i: (b, 0, i)),
            pl.BlockSpec((None, TOP_K, tm), lambda b, i: (b, 0, i)),
        ],
        out_shape=[
            jax.ShapeDtypeStruct((bsz, seq, d), F32),
            jax.ShapeDtypeStruct((bsz, seq * (d // LANES), LANES), F32),
            jax.ShapeDtypeStruct((bsz, TOP_K, seq), jnp.int32),
            jax.ShapeDtypeStruct((bsz, TOP_K, seq), F32),
        ],
        compiler_params=_params("arbitrary", "arbitrary"),
        name="mix_out_route",
    )(*acts, x, w_o, gate1, gain, shift, scale, wr1, wr2, rbias)


N_SUB = D_MODEL // LANES
TABLE_CHUNK = 512


def _to_tiles(ref, val, lo=0):
    n = val.shape[0]
    for s in range(N_SUB):
        ref[pl.ds(lo * N_SUB + s, n, stride=N_SUB), :] = val[:, s * LANES:(s + 1) * LANES]


def _from_tiles(ref, lo, n):
    return jnp.concatenate([ref[pl.ds(lo * N_SUB + s, n, stride=N_SUB), :] for s in range(N_SUB)], axis=1)


def _tables_kernel(idx_ref, rank_ref, cnt_ref, carry):
    @pl.when(pl.program_id(0) == 0)
    def _():
        carry[...] = jnp.zeros_like(carry)

    seq = idx_ref.shape[-1]
    ch = min(TABLE_CHUNK, seq)
    tri = jnp.where(lax.broadcasted_iota(jnp.int32, (ch, ch), 0) <= lax.broadcasted_iota(jnp.int32, (ch, ch), 1),
                    1.0, 0.0).astype(BF16)
    eiota = lax.broadcasted_iota(jnp.int32, (N_EXPERTS, ch), 0)
    cnt = carry[...]
    for k in range(TOP_K):
        for c in range(seq // ch):
            sel = eiota == idx_ref[k:k + 1, c * ch:(c + 1) * ch]
            pref = _dot(jnp.where(sel, 1.0, 0.0).astype(BF16), tri) + cnt
            rank = jnp.sum(jnp.where(sel, pref, 0.0), axis=0, keepdims=True) - 1.0
            rank_ref[k:k + 1, c * ch:(c + 1) * ch] = rank.astype(jnp.int32)
            cnt = pref[:, ch - 1:ch]
    carry[...] = cnt
    cnt_ref[...] = jnp.broadcast_to(cnt, cnt_ref.shape)


def _tables(idx):
    bsz, _, seq = idx.shape
    return pl.pallas_call(
        _tables_kernel,
        grid=(bsz,),
        in_specs=[pl.BlockSpec((None, TOP_K, seq), lambda b: (b, 0, 0))],
        out_specs=[pl.BlockSpec((None, TOP_K, seq), lambda b: (b, 0, 0)),
                   pl.BlockSpec((N_EXPERTS, LANES), lambda b: (0, 0))],
        out_shape=[jax.ShapeDtypeStruct((bsz, TOP_K, seq), jnp.int32),
                   jax.ShapeDtypeStruct((N_EXPERTS, LANES), F32)],
        scratch_shapes=[pltpu.VMEM((N_EXPERTS, 1), F32)],
        compiler_params=_params("arbitrary"),
        name="moe_tables",
    )(idx)


def _zero_runs(step, total, pad_start_ref, pad_len_ref, tail_ref):
    ops = []
    for m in range(-(-2 * N_EXPERTS // total)):
        u = step + m * total
        e = jnp.minimum(u, N_EXPERTS - 1)
        length = jnp.where(u < N_EXPERTS, pad_len_ref[e], 0)
        first = pad_start_ref[e]
        for bit in reversed(range(MOE_TILE.bit_length() - 1)):
            done = lax.shift_left(lax.shift_right_logical(length, bit + 1), bit + 1)
            ops.append((lax.bitwise_and(lax.shift_right_logical(length, bit), 1) == 1, first + done, 1 << bit))
        t = u - N_EXPERTS
        ops.append(((t >= 0) & (t < tail_ref[1]), tail_ref[0] + t * MOE_TILE, MOE_TILE))
    return ops


def _scatter_kernel(total, pad_start_ref, pad_len_ref, tail_ref, dest_ref, src_hbm, dst_hbm,
                    buf, zbuf, sem_in, sem_out, sem_z):
    n = pl.program_id(0) * pl.num_programs(1) + pl.program_id(1)
    tm = buf.shape[1]
    slot = lax.rem(n, 3)

    def load(step, sl):
        return pltpu.make_async_copy(src_hbm.at[pl.ds(step * tm, tm)], buf.at[sl], sem_in.at[sl])

    def drain(sl):
        for _ in range(TOP_K):
            pltpu.make_async_copy(buf.at[sl], dst_hbm.at[pl.ds(0, tm)], sem_out.at[sl]).wait()

    def zero_fill(step, wait):
        for pred, first, rows in _zero_runs(step, total, pad_start_ref, pad_len_ref, tail_ref):
            @pl.when(pred)
            def _(first=first, rows=rows):
                cp = pltpu.make_async_copy(zbuf.at[pl.ds(0, rows)], dst_hbm.at[pl.ds(first, rows)], sem_z)
                cp.wait() if wait else cp.start()

    @pl.when(n == 0)
    def _():
        zbuf[...] = jnp.zeros_like(zbuf)
        load(0, 0).start()
        if total > 1:
            load(1, 1).start()

    load(n, slot).wait()
    for k in range(TOP_K):
        for c in range(tm // LANES):
            def start(j, carry, k=k, c=c):
                row = dest_ref[0, k * tm + c * LANES + j]
                pltpu.make_async_copy(buf.at[slot, c * LANES + j], dst_hbm.at[row], sem_out.at[slot]).start(priority=k)
                return carry
            lax.fori_loop(0, LANES, start, 0, unroll=8)
    zero_fill(n, wait=False)

    @pl.when(n > 0)
    def _():
        drain(lax.rem(n + 2, 3))
        zero_fill(n - 1, wait=True)

    @pl.when(n + 2 < total)
    def _():
        load(n + 2, lax.rem(n + 2, 3)).start()

    @pl.when(n == total - 1)
    def _():
        drain(slot)
        zero_fill(n, wait=True)


def _index_blocks(table, tm):
    bsz, _, seq = table.shape
    nt = seq // tm
    t = table.reshape(bsz, TOP_K, nt, tm).transpose(0, 2, 1, 3)
    return t.reshape(bsz * nt, 1, TOP_K * tm), (None, 1, TOP_K * tm)


def _scatter(dest, pad_start, pad_len, tail, h2t, n_rows):
    bsz, _, seq = dest.shape
    tm = min(ROW_TILE, seq)
    nt = seq // tm
    dest4, dest_block = _index_blocks(dest, tm)
    grid_spec = pltpu.PrefetchScalarGridSpec(
        num_scalar_prefetch=3,
        grid=(bsz, nt),
        in_specs=[
            pl.BlockSpec(dest_block, lambda b, i, *_: (b * nt + i, 0, 0), memory_space=pltpu.SMEM),
            pl.BlockSpec(memory_space=pl.ANY),
        ],
        out_specs=pl.BlockSpec(memory_space=pl.ANY),
        scratch_shapes=[pltpu.VMEM((3, tm, N_SUB, LANES), F32), pltpu.VMEM((MOE_TILE, N_SUB, LANES), F32),
                        pltpu.SemaphoreType.DMA((3,)), pltpu.SemaphoreType.DMA((3,)), pltpu.SemaphoreType.DMA(())],
    )
    return pl.pallas_call(
        functools.partial(_scatter_kernel, bsz * nt),
        grid_spec=grid_spec,
        out_shape=jax.ShapeDtypeStruct((n_rows, N_SUB, LANES), F32),
        compiler_params=_params("arbitrary", "arbitrary"),
        name="moe_scatter",
    )(pad_start, pad_len, tail, dest4, h2t.reshape(bsz * seq, N_SUB, LANES))


def _expert_kernel(blk_exp_ref, blk_first_ref, n_used_ref,
                   xs_ref, wgu_ref, wdn_ref, ys_ref, wgu_bf, wdn_bf):
    i = pl.program_id(0)

    @pl.when(i < n_used_ref[0])
    def _():
        @pl.when(blk_first_ref[i] == 1)
        def _():
            wgu_bf[...] = wgu_ref[...].astype(BF16)
            wdn_bf[...] = wdn_ref[...].astype(BF16)

        x = _from_tiles(xs_ref, 0, MOE_TILE).astype(BF16)
        gu = _dot(x, wgu_bf[...])
        g = gu[:, :D_EXPERT]
        u = gu[:, D_EXPERT:]
        mid = (g * jax.nn.sigmoid(g) * u).astype(BF16)
        _to_tiles(ys_ref, _dot(mid, wdn_bf[...]))

    @pl.when(i >= n_used_ref[0])
    def _():
        ys_ref[...] = jnp.zeros_like(ys_ref)


def _experts(blk_exp, blk_first, n_used, xs, w_gu, w_dn, layer):
    d = D_MODEL
    nb = xs.shape[0] // (MOE_TILE * N_SUB)
    tile = lambda i, *_: (i, 0)
    grid_spec = pltpu.PrefetchScalarGridSpec(
        num_scalar_prefetch=3,
        grid=(nb,),
        in_specs=[
            pl.BlockSpec((MOE_TILE * N_SUB, LANES), tile),
            pl.BlockSpec((None, None, d, 2 * D_EXPERT), lambda i, be, *_: (layer, be[i], 0, 0)),
            pl.BlockSpec((None, None, D_EXPERT, d), lambda i, be, *_: (layer, be[i], 0, 0)),
        ],
        out_specs=pl.BlockSpec((MOE_TILE * N_SUB, LANES), tile),
        scratch_shapes=[pltpu.VMEM((d, 2 * D_EXPERT), BF16), pltpu.VMEM((D_EXPERT, d), BF16)],
    )
    return pl.pallas_call(
        _expert_kernel,
        grid_spec=grid_spec,
        out_shape=jax.ShapeDtypeStruct(xs.shape, F32),
        compiler_params=_params("arbitrary"),
        name="moe_experts",
    )(blk_exp, blk_first, n_used, xs, w_gu, w_dn)


def _combine_kernel(dcur_ref, dnxt_ref, ys_hbm, x_ref, wts_ref, g2_ref, x_out, buf0, buf1, sem):
    nt = pl.num_programs(1)
    n = pl.program_id(0) * nt + pl.program_id(1)
    total = pl.num_programs(0) * nt
    tm = x_ref.shape[0]
    bufs = (buf0, buf1)

    def copy(d_ref, sl, k, r):
        src = ys_hbm.at[pl.ds(pl.multiple_of(d_ref[0, k * tm + r], N_SUB), N_SUB)]
        dst = bufs[sl].at[pl.ds(pl.multiple_of((k * tm + r) * N_SUB, N_SUB), N_SUB)]
        return pltpu.make_async_copy(src, dst, sem.at[sl])

    def drain(sl):
        pltpu.make_async_copy(ys_hbm.at[pl.ds(0, TOP_K * tm * N_SUB)], bufs[sl], sem.at[sl]).wait()

    @pl.when(n == 0)
    def _():
        for k in range(TOP_K):
            def start(r, carry, k=k):
                copy(dcur_ref, 0, k, r).start(priority=k)
                return carry
            lax.fori_loop(0, tm, start, 0, unroll=8)

    def step(sl):
        drain(sl)
        for r in range(tm):
            for k in range(TOP_K):
                copy(dnxt_ref, 1 - sl, k, r).start(priority=k)
        w = wts_ref[...]
        y = w[:, 0:1] * _from_tiles(bufs[sl], 0, tm) + w[:, 1:2] * _from_tiles(bufs[sl], tm, tm)
        x_out[...] = x_ref[...] + g2_ref[...] * y

        @pl.when(n == total - 1)
        def _():
            drain(1 - sl)

    for sl in range(2):
        pl.when(lax.rem(n, 2) == sl)(functools.partial(step, sl))


def _combine(dest8, ys, x, wts_col, gate2):
    bsz, seq, d = x.shape
    tm = min(ROW_TILE, seq)
    nt = seq // tm

    def nxt(b, i):
        return (jnp.minimum(b * nt + i + 1, bsz * nt - 1), 0, 0)

    dest8, dest_block = _index_blocks(dest8, tm)
    return pl.pallas_call(
        _combine_kernel,
        grid=(bsz, nt),
        in_specs=[
            pl.BlockSpec(dest_block, lambda b, i: (b * nt + i, 0, 0), memory_space=pltpu.SMEM),
            pl.BlockSpec(dest_block, nxt, memory_space=pltpu.SMEM),
            pl.BlockSpec(memory_space=pl.ANY),
            pl.BlockSpec((None, tm, d), lambda b, i: (b, i, 0)),
            pl.BlockSpec((None, tm, TOP_K), lambda b, i: (b, i, 0)),
            pl.BlockSpec((None, 1, d), lambda b, i: (b, 0, 0)),
        ],
        out_specs=pl.BlockSpec((None, tm, d), lambda b, i: (b, i, 0)),
        out_shape=jax.ShapeDtypeStruct((bsz, seq, d), F32),
        scratch_shapes=[pltpu.VMEM((TOP_K * tm * N_SUB, LANES), F32), pltpu.VMEM((TOP_K * tm * N_SUB, LANES), F32),
                        pltpu.SemaphoreType.DMA((2,))],
        compiler_params=_params("arbitrary", "arbitrary"),
        name="moe_combine",
    )(dest8, dest8, ys, x, wts_col, gate2)


def _lookup(table, keys):
    hit = keys[..., None] == jnp.arange(table.shape[0], dtype=jnp.int32)
    return jnp.sum(jnp.where(hit, table, 0), axis=-1).astype(jnp.int32)


def _count_le(bounds, q):
    return jnp.sum((bounds <= q[..., None]).astype(jnp.int32), axis=-1)


def _moe(h2t, idx, wts, x, gate2, w_gu, w_dn, layer):
    bsz, seq, _ = x.shape
    n_rows = bsz * seq * TOP_K + N_EXPERTS * MOE_TILE
    nb = n_rows // MOE_TILE
    rank, cnt = _tables(idx)
    counts = cnt[:, 0].astype(jnp.int32)
    padded = ((counts + MOE_TILE - 1) // MOE_TILE) * MOE_TILE
    pend = jnp.cumsum(padded)
    pstart = pend - padded
    dest = _lookup(pstart, idx) + rank
    blk_row = jnp.arange(nb, dtype=jnp.int32) * MOE_TILE
    blk_exp = jnp.minimum(_count_le(pend, blk_row), N_EXPERTS - 1)
    blk_first = (blk_row == _lookup(pstart, blk_exp)).astype(jnp.int32)
    n_used = (pend[-1:] // MOE_TILE).astype(jnp.int32)
    tail = jnp.concatenate([pend[-1:], (n_rows - pend[-1:]) // MOE_TILE]).astype(jnp.int32)
    xs = _scatter(dest, (pstart + counts).astype(jnp.int32), (padded - counts).astype(jnp.int32), tail, h2t, n_rows)
    ys = _experts(blk_exp, blk_first, n_used, xs.reshape(n_rows * N_SUB, LANES), w_gu, w_dn, layer)
    return _combine(dest * N_SUB, ys, x, wts.transpose(0, 2, 1), gate2)


def _rnn_in_kernel(x_ref, g_ref, sh_ref, sc_ref, w_ref, gate_out, xb_out):
    h = _modulate(x_ref[...], g_ref[...], sh_ref[...], sc_ref[...])
    u = _dot(h.astype(BF16), w_ref[...])
    gate_out[...] = jax.nn.gelu(u[:, :D_RNN]).astype(BF16)
    xb_out[...] = u[:, D_RNN:]


def _rnn_in(x, gain, shift, scale, w_in):
    bsz, seq, d = x.shape
    tm = min(ROW_TILE, seq)
    row = lambda b, i: (b, i, 0)
    per_b = lambda b, i: (b, 0, 0)
    const = lambda b, i: (0, 0)
    return pl.pallas_call(
        _rnn_in_kernel,
        grid=(bsz, seq // tm),
        in_specs=[
            pl.BlockSpec((None, tm, d), row),
            pl.BlockSpec(gain.shape, const),
            pl.BlockSpec((None, 1, d), per_b),
            pl.BlockSpec((None, 1, d), per_b),
            pl.BlockSpec(w_in.shape, const),
        ],
        out_specs=[pl.BlockSpec((None, tm, D_RNN), row), pl.BlockSpec((None, tm, D_RNN), row)],
        out_shape=[jax.ShapeDtypeStruct((bsz, seq, D_RNN), BF16),
                   jax.ShapeDtypeStruct((bsz, seq, D_RNN), F32)],
        compiler_params=_params("arbitrary", "arbitrary"),
        name="rnn_in",
    )(x, gain, shift, scale, w_in)


def _lru_kernel(xb_ref, cw_ref, cb_ref, wcat_ref, bcat_ref, lam_ref, hs_ref,
                xi_ref, af_ref, bf_ref, ab_ref, bb_ref, hf_ref, hb_ref, sum_ref):
    seq, c = xb_ref.shape
    seg_len = seq // SUBLANES
    n_slab = c // LANES
    n_rows = seg_len * SUBLANES
    halo = (CONV_W // 2) * SUBLANES
    row = lax.broadcasted_iota(jnp.int32, (SUBLANES, LANES), 0)
    for sl in range(n_slab):
        lanes = slice(sl * LANES, (sl + 1) * LANES)
        for g in range(SUBLANES):
            xi_ref[sl, pl.ds(halo + g, seg_len, stride=SUBLANES), :] = xb_ref[g * seg_len:(g + 1) * seg_len, lanes]
        for back in (1, 2):
            prev = xi_ref[sl, halo + (seg_len - back) * SUBLANES:halo + (seg_len - back + 1) * SUBLANES, :]
            xi_ref[sl, halo - back * SUBLANES:halo - (back - 1) * SUBLANES, :] = jnp.where(
                row == 0, 0.0, pltpu.roll(prev, 1, 0))
        nxt = xi_ref[sl, halo:halo + SUBLANES, :]
        xi_ref[sl, halo + n_rows:halo + n_rows + SUBLANES, :] = jnp.where(
            row == SUBLANES - 1, 0.0, pltpu.roll(nxt, SUBLANES - 1, 0))

    cw = cw_ref[...]
    cb = cb_ref[...]
    lam = lam_ref[...]
    neg = -lam
    softplus = jnp.maximum(neg, 0.0) + jnp.log1p(jnp.exp(-jnp.abs(neg)))
    half_rate = (-0.5 * LRU_C) * softplus
    rows = min(SCAN_ROWS, n_rows)
    n_chunks = n_rows // rows

    for ci in range(n_chunks):
        i0 = ci * rows
        taps = []
        for k in range(CONV_W):
            lo = halo + i0 + (k - CONV_W // 2) * SUBLANES
            taps.append(jnp.concatenate([xi_ref[sl, lo:lo + rows, :] for sl in range(n_slab)], axis=1))
        xc = cb
        for k in range(CONV_W):
            xc = xc + taps[k] * cw[k:k + 1, :]
        xcb = xc.astype(BF16)
        xh = 0.5 * xc
        for dirn, (a_ref, b_ref) in enumerate(((af_ref, bf_ref), (ab_ref, bb_ref))):
            cols = slice(2 * dirn * c, 2 * (dirn + 1) * c)
            th = jnp.tanh(_dot(xcb, wcat_ref[:, cols]) + bcat_ref[:, cols])
            hr = half_rate[dirn:dirn + 1, :]
            log_a = hr * th[:, :c] + hr
            a = jnp.exp(log_a)
            mult = jnp.sqrt(jnp.tanh(-log_a) * (a * a + 1.0))
            if dirn == 0 and ci == 0:
                mult = jnp.where(lax.broadcasted_iota(jnp.int32, mult.shape, 0) == 0, 1.0, mult)
            if dirn == 1 and ci == n_chunks - 1:
                mult = jnp.where(lax.broadcasted_iota(jnp.int32, mult.shape, 0) == rows - 1, 1.0, mult)
            b = mult * (th[:, c:] + 1.0) * xh
            for sl in range(n_slab):
                a_ref[sl, i0:i0 + rows, :] = a[:, sl * LANES:(sl + 1) * LANES]
                b_ref[sl, i0:i0 + rows, :] = b[:, sl * LANES:(sl + 1) * LANES]

    def step_rows(cidx):
        fwd = pl.ds(pl.multiple_of(cidx * SUBLANES, SUBLANES), SUBLANES)
        bwd = pl.ds(pl.multiple_of((seg_len - 1 - cidx) * SUBLANES, SUBLANES), SUBLANES)
        return fwd, bwd

    zero = jnp.zeros((SUBLANES, LANES), F32)
    one = jnp.ones((SUBLANES, LANES), F32)

    def totals(cidx, carry):
        fwd, bwd = step_rows(cidx)
        out = []
        for sl in range(n_slab):
            hf, pf, hb, pb = carry[sl]
            af, ab = af_ref[sl, fwd, :], ab_ref[sl, bwd, :]
            out.append((af * hf + bf_ref[sl, fwd, :], af * pf, ab * hb + bb_ref[sl, bwd, :], ab * pb))
        return tuple(out)
    tot = lax.fori_loop(0, seg_len, totals, tuple((zero, one, zero, one) for _ in range(n_slab)), unroll=8)

    enter = []
    for sl in range(n_slab):
        hf, pf, hb, pb = tot[sl]
        cf, cbk = zero, zero
        for _ in range(SUBLANES - 1):
            cf = jnp.where(row == 0, 0.0, pltpu.roll(hf + pf * cf, 1, 0))
            cbk = jnp.where(row == SUBLANES - 1, 0.0, pltpu.roll(hb + pb * cbk, SUBLANES - 1, 0))
        enter.append((cf, cbk))

    def states(meet, cidx, carry):
        fwd, bwd = step_rows(cidx)
        out = []
        for sl in range(n_slab):
            hf, hb = carry[sl]
            hf = af_ref[sl, fwd, :] * hf + bf_ref[sl, fwd, :]
            hb = ab_ref[sl, bwd, :] * hb + bb_ref[sl, bwd, :]
            if meet:
                sum_ref[sl, fwd, :] = hf + hb_ref[sl, fwd, :]
                sum_ref[sl, bwd, :] = hb + hf_ref[sl, bwd, :]
            else:
                hf_ref[sl, fwd, :] = hf
                hb_ref[sl, bwd, :] = hb
            out.append((hf, hb))
        return tuple(out)
    mid = lax.fori_loop(0, seg_len // 2, functools.partial(states, False), tuple(enter), unroll=8)
    lax.fori_loop(seg_len // 2, seg_len, functools.partial(states, True), mid, unroll=8)

    for g in range(SUBLANES):
        for sl in range(n_slab):
            hs_ref[g * seg_len:(g + 1) * seg_len, sl * LANES:(sl + 1) * LANES] = (
                sum_ref[sl, pl.ds(g, seg_len, stride=SUBLANES), :])


def _lru(xb, conv_w, conv_b, wcat, bcat, lam):
    bsz, seq, _ = xb.shape
    c = RNN_BW
    blk = lambda b, n: (b, 0, n)
    return pl.pallas_call(
        _lru_kernel,
        grid=(bsz, RNN_BLOCKS),
        in_specs=[
            pl.BlockSpec((None, seq, c), blk),
            pl.BlockSpec((CONV_W, c), lambda b, n: (0, n)),
            pl.BlockSpec((1, c), lambda b, n: (0, n)),
            pl.BlockSpec((None, c, 4 * c), lambda b, n: (n, 0, 0)),
            pl.BlockSpec((None, 1, 4 * c), lambda b, n: (n, 0, 0)),
            pl.BlockSpec((2, c), lambda b, n: (0, n)),
        ],
        out_specs=pl.BlockSpec((None, seq, c), blk),
        out_shape=jax.ShapeDtypeStruct((bsz, seq, D_RNN), F32),
        scratch_shapes=[pltpu.VMEM((c // LANES, seq + (CONV_W - 1) * SUBLANES, LANES), F32)]
        + [pltpu.VMEM((c // LANES, seq, LANES), F32)] * 7,
        compiler_params=_params("arbitrary", "arbitrary"),
        name="rglru_scan",
    )(xb, conv_w, conv_b, wcat, bcat, lam)


def _mla_weights(w_in, w_q_b, w_kv_b, q_norm, k_norm):
    half = QK_ROPE // 2

    def slab(t):
        return jnp.pad(t, [(0, 0)] * (t.ndim - 1) + [(0, LANES - QK_HEAD)])

    def rot_slab(t):
        rope = t[..., QK_NOPE:]
        swapped = jnp.concatenate([jnp.zeros_like(t[..., :QK_NOPE]), rope[..., half:], rope[..., :half]], axis=-1)
        return slab(swapped)

    kpe = jnp.pad(w_in[:, Q_LORA + KV_LORA:], ((0, 0), (QK_NOPE, 0)))
    w_in_p = jnp.concatenate([w_in[:, :Q_LORA + KV_LORA], slab(kpe), rot_slab(kpe)], axis=1).astype(BF16)
    wq = w_q_b.reshape(Q_LORA, N_HEADS, QK_HEAD)
    wq_p = slab(wq).reshape(Q_LORA, N_HEADS * LANES).astype(BF16)
    wq_rot = rot_slab(wq).reshape(Q_LORA, N_HEADS * LANES).astype(BF16)
    wkv = w_kv_b.reshape(KV_LORA, N_HEADS, QK_NOPE + V_HEAD)
    wk = jnp.pad(wkv[:, :, :QK_NOPE], ((0, 0), (0, 0), (0, LANES - QK_NOPE))).reshape(KV_LORA, N_HEADS * LANES)
    wv = wkv[:, :, QK_NOPE:].reshape(KV_LORA, N_HEADS * V_HEAD)
    w_kv_p = jnp.concatenate([wk, wv], axis=1).astype(BF16)
    gains = lambda g: jnp.stack([slab(g), rot_slab(g)], axis=0)
    return w_in_p, wq_p, wq_rot, w_kv_p, gains(q_norm), gains(k_norm)


def _rope_tables(positions):
    half = QK_ROPE // 2
    inv_freq = ROPE_THETA ** (-jnp.arange(half, dtype=F32) / half)
    ang = positions.astype(F32)[..., None] * inv_freq
    cos, sin = jnp.cos(ang), jnp.sin(ang)
    lead = positions.shape + (QK_NOPE,)
    tail = positions.shape + (LANES - QK_HEAD,)
    cos_t = jnp.concatenate([jnp.ones(lead, F32), cos, cos, jnp.ones(tail, F32)], axis=-1)
    sin_t = jnp.concatenate([jnp.zeros(lead, F32), -sin, sin, jnp.zeros(tail, F32)], axis=-1)
    return cos_t, sin_t


def _router_weights(w_router, router_bias):
    perm = (jnp.arange(N_EXPERTS) % N_GROUPS) * EXPERTS_PER_GROUP + jnp.arange(N_EXPERTS) // N_GROUPS
    w = w_router[:, perm]
    hi = w.astype(BF16)
    lo = (w - hi.astype(F32)).astype(BF16)
    z = jnp.zeros_like(hi)
    wr1 = jnp.concatenate([hi, lo, z, z], axis=1)
    wr2 = jnp.concatenate([z, z, hi, z], axis=1)
    return wr1, wr2, router_bias[perm].reshape(N_EXPERTS, 1).astype(F32)


def kernel(x, c, positions, norm_mix, norm_ffn, w_ada, b_ada, mla_w_in, mla_q_a_norm, mla_kv_a_norm, mla_w_q_b, mla_w_kv_b, mla_q_norm, mla_k_norm, mla_w_o, rnn_w_in, rnn_conv_w, rnn_conv_b, rnn_lam_f, rnn_w_rf, rnn_b_rf, rnn_w_if, rnn_b_if, rnn_lam_b, rnn_w_rb, rnn_b_rb, rnn_w_ib, rnn_b_ib, rnn_w_o, w_router, router_bias, moe_w_gu, moe_w_dn):
    bsz, seq, d = x.shape
    depth = w_ada.shape[0]
    mod = _ada(c, w_ada, b_ada)
    wr1, wr2, rbias = _router_weights(w_router, router_bias)
    cos_t, sin_t = _rope_tables(positions)
    vec = lambda v: v.reshape(1, -1)
    for i in range(depth):
        sh1, sc1, g1, sh2, sc2, g2 = [mod[i, :, k * d:(k + 1) * d].reshape(bsz, 1, d) for k in range(6)]
        j = i // 2
        if i % 2 == 0:
            w_in_p, wq, wq_rot, wkv, qn, kn = _mla_weights(mla_w_in[j], mla_w_q_b[j], mla_w_kv_b[j],
                                                           mla_q_norm[j], mla_k_norm[j])
            q, k, v = _mla_in(x, vec(norm_mix[i]), sh1, sc1, w_in_p, vec(mla_q_a_norm[j]),
                              vec(mla_kv_a_norm[j]), wq, wq_rot, wkv, qn, kn, cos_t, sin_t)
            a = _attention(q, k, v, mla_q_norm[j], mla_k_norm[j])
            hs = None
            w_o = mla_w_o[j].astype(BF16)
        else:
            a, xb = _rnn_in(x, vec(norm_mix[i]), sh1, sc1, rnn_w_in[j].astype(BF16))
            wcat = (0.5 * jnp.concatenate([rnn_w_rf[j], rnn_w_if[j], rnn_w_rb[j], rnn_w_ib[j]], axis=-1)).astype(BF16)
            bcat = jnp.stack([b.reshape(RNN_BLOCKS, RNN_BW) for b in
                              (rnn_b_rf[j], rnn_b_if[j], rnn_b_rb[j], rnn_b_ib[j])], axis=1)
            bcat = 0.5 * bcat.reshape(RNN_BLOCKS, 1, 4 * RNN_BW)
            lam = jnp.stack([rnn_lam_f[j], rnn_lam_b[j]], axis=0)
            hs = _lru(xb, rnn_conv_w[j], vec(rnn_conv_b[j]), wcat, bcat, lam)
            w_o = rnn_w_o[j].astype(BF16)
        x, h2, idx, wts = _mix_out(a, hs, x, w_o, g1, vec(norm_ffn[i]), sh2, sc2, wr1, wr2, rbias)
        x = _moe(h2, idx, wts, x, g2, moe_w_gu, moe_w_dn, i)
    return x
```

```python
import functools

import jax
import jax.numpy as jnp
from jax import lax
from jax.experimental import pallas as pl
from jax.experimental.pallas import tpu as pltpu

F32 = jnp.float32
BF16 = jnp.bfloat16

D_MODEL = 1024
N_HEADS = 16
Q_LORA = 384
KV_LORA = 256
QK_NOPE = 64
QK_ROPE = 32
QK_HEAD = QK_NOPE + QK_ROPE
V_HEAD = 64
ROPE_THETA = 10000.0
D_RNN = D_MODEL
RNN_BLOCKS = 4
RNN_BW = D_RNN // RNN_BLOCKS
CONV_W = 4
LRU_C = 8.0
N_EXPERTS = 32
N_GROUPS = 8
EXPERTS_PER_GROUP = N_EXPERTS // N_GROUPS
TOP_K = 2
D_EXPERT = 512
EPS = 1e-6
LOG2_E = 1.4426950408889634

LANES = 128
SUBLANES = 8
VMEM_LIMIT = 52 * 1024 * 1024

ROW_TILE = 512
ROW_SUB = 256
Q_TILE = 2048
Q_SUB = 256
MAX_SAFE_SHIFT = 60.0
MOE_TILE = 512
SCAN_ROWS = 256


def _dot(a, b):
    return jnp.dot(a, b, preferred_element_type=F32)


def _split_bf16(a):
    hi = a.astype(BF16)
    lo = (a - hi.astype(F32)).astype(BF16)
    return hi, lo


def _dot_split(a, b):
    ah, al = _split_bf16(a)
    bh, bl = _split_bf16(b)
    return _dot(ah, bh) + (_dot(ah, bl) + _dot(al, bh))


def _rms(x, gain, n):
    ms = jnp.sum(x * x, axis=-1, keepdims=True) * (1.0 / n)
    return x * lax.rsqrt(ms + EPS) * gain


def _modulate(x, gain, shift, scale):
    return _rms(x, gain, x.shape[-1]) * (1.0 + scale) + shift


def _params(*sem):
    return pltpu.CompilerParams(dimension_semantics=sem, vmem_limit_bytes=VMEM_LIMIT)


def _ada_kernel(c_ref, w_ref, b_ref, o_ref):
    c = c_ref[...]
    o_ref[...] = _dot_split(c * jax.nn.sigmoid(c), w_ref[...]) + b_ref[...]


def _ada(c, w_ada, b_ada):
    depth, d, n = w_ada.shape
    bsz = c.shape[0]
    tn = 1536
    return pl.pallas_call(
        _ada_kernel,
        grid=(depth, n // tn),
        in_specs=[
            pl.BlockSpec((bsz, d), lambda l, j: (0, 0)),
            pl.BlockSpec((None, d, tn), lambda l, j: (l, 0, j)),
            pl.BlockSpec((None, 1, tn), lambda l, j: (l, 0, j)),
        ],
        out_specs=pl.BlockSpec((None, bsz, tn), lambda l, j: (l, 0, j)),
        out_shape=jax.ShapeDtypeStruct((depth, bsz, n), F32),
        compiler_params=_params("arbitrary", "arbitrary"),
        name="adaln_mod",
    )(c, w_ada, b_ada.reshape(depth, 1, n))


def _head_scale(s):
    return lax.rsqrt(jnp.sum(s * s, axis=-1, keepdims=True) * (1.0 / QK_HEAD) + EPS)


def _mla_in_kernel(x_ref, g_ref, sh_ref, sc_ref, win_ref, qan_ref, kvan_ref, wq_ref, wqr_ref, wkv_ref,
                   qn_ref, kn_ref, cos_ref, sin_ref, q_out, k_out, v_out):
    h = _modulate(x_ref[...], g_ref[...], sh_ref[...], sc_ref[...])
    lat = _dot(h.astype(BF16), win_ref[...])
    q_lat = lat[:, :Q_LORA]
    kv_lat = lat[:, Q_LORA:Q_LORA + KV_LORA]
    kpe = lat[:, Q_LORA + KV_LORA:Q_LORA + KV_LORA + LANES]
    kpe_rot = lat[:, Q_LORA + KV_LORA + LANES:]
    qn = _rms(q_lat, qan_ref[...], Q_LORA).astype(BF16)
    q_all = _dot(qn, wq_ref[...])
    q_rot = _dot(qn, wqr_ref[...])
    kv_all = _dot(_rms(kv_lat, kvan_ref[...], KV_LORA).astype(BF16), wkv_ref[...])
    cos_t = cos_ref[...]
    sin_t = sin_ref[...]
    q_scale = LOG2_E * QK_HEAD ** -0.5
    cq = cos_t * (qn_ref[0:1, :] * q_scale)
    sq = sin_t * (qn_ref[1:2, :] * q_scale)
    ck = cos_t * kn_ref[0:1, :]
    k_rot_term = kpe_rot * (sin_t * kn_ref[1:2, :])
    for hh in range(N_HEADS):
        sl = slice(hh * LANES, (hh + 1) * LANES)
        s = q_all[:, sl]
        q_out[hh] = ((s * cq + q_rot[:, sl] * sq) * _head_scale(s)).astype(BF16)
        s = kv_all[:, sl] + kpe
        k_out[hh] = ((s * ck + k_rot_term) * _head_scale(s)).astype(BF16)
    v_out[...] = kv_all[:, N_HEADS * LANES:].astype(BF16)


def _mla_in(x, gain, shift, scale, w_in, q_a_norm, kv_a_norm, w_q, w_q_rot, w_kv, q_norm, k_norm, cos_t, sin_t):
    bsz, seq, d = x.shape
    tm = min(ROW_TILE, seq)
    row = lambda b, i: (b, i, 0)
    per_b = lambda b, i: (b, 0, 0)
    const = lambda b, i: (0, 0)
    full = lambda a: pl.BlockSpec(a.shape, const)
    return pl.pallas_call(
        _mla_in_kernel,
        grid=(bsz, seq // tm),
        in_specs=[
            pl.BlockSpec((None, tm, d), row),
            full(gain),
            pl.BlockSpec((None, 1, d), per_b),
            pl.BlockSpec((None, 1, d), per_b),
            full(w_in), full(q_a_norm), full(kv_a_norm), full(w_q), full(w_q_rot), full(w_kv),
            full(q_norm), full(k_norm),
            pl.BlockSpec((None, tm, LANES), row),
            pl.BlockSpec((None, tm, LANES), row),
        ],
        out_specs=[
            pl.BlockSpec((None, N_HEADS, tm, LANES), lambda b, i: (b, 0, i, 0)),
            pl.BlockSpec((None, N_HEADS, tm, LANES), lambda b, i: (b, 0, i, 0)),
            pl.BlockSpec((None, tm, N_HEADS * V_HEAD), row),
        ],
        out_shape=[
            jax.ShapeDtypeStruct((bsz, N_HEADS, seq, LANES), BF16),
            jax.ShapeDtypeStruct((bsz, N_HEADS, seq, LANES), BF16),
            jax.ShapeDtypeStruct((bsz, seq, N_HEADS * V_HEAD), BF16),
        ],
        compiler_params=_params("arbitrary", "arbitrary"),
        name="mla_in",
    )(x, gain, shift, scale, w_in, q_a_norm, kv_a_norm, w_q, w_q_rot, w_kv, q_norm, k_norm, cos_t, sin_t)


def _attn_kernel(bounded, shift_ref, q_ref, k_ref, v_ref, o_ref):
    v = v_ref[...]
    lane_v = lax.broadcasted_iota(jnp.int32, v.shape, 1)
    v_heads = [jnp.where(lane_v < V_HEAD, v, jnp.ones((), BF16)), jnp.where(lane_v >= V_HEAD, v, jnp.ones((), BF16))]
    lane = lax.broadcasted_iota(jnp.int32, (Q_SUB, LANES), 1)
    for i in range(q_ref.shape[1] // Q_SUB):
        rows = slice(i * Q_SUB, (i + 1) * Q_SUB)
        outs = []
        for j in range(2):
            s = lax.dot_general(q_ref[j, rows, :], k_ref[j], (((1,), (1,)), ((), ())),
                                preferred_element_type=F32)
            m = shift_ref[0] if bounded else jnp.max(s, axis=-1, keepdims=True)
            o = _dot(jnp.exp2(s - m).astype(BF16), v_heads[j])
            denom = o[:, V_HEAD:V_HEAD + 1] if j == 0 else o[:, 0:1]
            outs.append(o / denom)
        o_ref[rows, :] = jnp.where(lane < V_HEAD, outs[0], outs[1]).astype(BF16)


def _attention_call(bounded, shift, q, k, v):
    bsz, _, seq, _ = q.shape
    tq = min(Q_TILE, seq)
    assert tq % Q_SUB == 0
    grid_spec = pltpu.PrefetchScalarGridSpec(
        num_scalar_prefetch=1,
        grid=(bsz, N_HEADS // 2, seq // tq),
        in_specs=[
            pl.BlockSpec((None, 2, tq, LANES), lambda b, h, i, *_: (b, h, i, 0)),
            pl.BlockSpec((None, 2, seq, LANES), lambda b, h, i, *_: (b, h, 0, 0)),
            pl.BlockSpec((None, seq, LANES), lambda b, h, i, *_: (b, 0, h)),
        ],
        out_specs=pl.BlockSpec((None, tq, LANES), lambda b, h, i, *_: (b, i, h)),
    )
    return pl.pallas_call(
        functools.partial(_attn_kernel, bounded),
        grid_spec=grid_spec,
        out_shape=jax.ShapeDtypeStruct((bsz, seq, N_HEADS * V_HEAD), BF16),
        compiler_params=_params("arbitrary", "arbitrary", "arbitrary"),
        name="mla_attention",
    )(shift, q, k, v)


def _attention(q, k, v, q_gain, k_gain):
    score_bound = 1.02 * LOG2_E * QK_HEAD ** 0.5 * jnp.max(jnp.abs(q_gain)) * jnp.max(jnp.abs(k_gain))
    shift = score_bound.reshape(1).astype(F32)
    return lax.cond(score_bound <= MAX_SAFE_SHIFT,
                    functools.partial(_attention_call, True), functools.partial(_attention_call, False),
                    shift, q, k, v)


def _first_index_of_max(vals):
    m = vals[0]
    for v in vals[1:]:
        m = jnp.maximum(m, v)
    idx = jnp.full(m.shape, float(len(vals) - 1), F32)
    for j in range(len(vals) - 2, -1, -1):
        idx = jnp.where(vals[j] == m, float(j), idx)
    return m, idx


def _route(h2, wr1_ref, wr2_ref, rb_ref):
    hh, hl = _split_bf16(h2)
    logits = (_dot(hh, wr1_ref[...]) + _dot(hl, wr2_ref[...])).T
    logit = logits[0:N_EXPERTS] + logits[N_EXPERTS:2 * N_EXPERTS] + logits[2 * N_EXPERTS:3 * N_EXPERTS]
    score = jax.nn.sigmoid(logit)
    biased = score + rb_ref[...]
    a = [biased[j * N_GROUPS:(j + 1) * N_GROUPS] for j in range(EXPERTS_PER_GROUP)]
    sc = [score[j * N_GROUPS:(j + 1) * N_GROUPS] for j in range(EXPERTS_PER_GROUP)]
    hi1, lo1 = jnp.maximum(a[0], a[1]), jnp.minimum(a[0], a[1])
    hi2, lo2 = jnp.maximum(a[2], a[3]), jnp.minimum(a[2], a[3])
    gscore = jnp.maximum(hi1, hi2) + jnp.maximum(jnp.minimum(hi1, hi2), jnp.maximum(lo1, lo2))
    gmax = jnp.max(gscore, axis=0, keepdims=True)
    giota = lax.broadcasted_iota(jnp.int32, gscore.shape, 0).astype(F32)
    gsel = jnp.min(jnp.where(gscore == gmax, giota, float(N_GROUPS)), axis=0, keepdims=True)
    onehot = giota == gsel
    pick = lambda t: jnp.sum(jnp.where(onehot, t, 0.0), axis=0, keepdims=True)
    bj = [pick(t) for t in a]
    sj = [pick(t) for t in sc]
    _, i1 = _first_index_of_max(bj)
    bj2 = [jnp.where(i1 == float(j), -jnp.inf, bj[j]) for j in range(EXPERTS_PER_GROUP)]
    _, i2 = _first_index_of_max(bj2)
    sel = lambda i: jnp.where(i == 0.0, sj[0], jnp.where(i == 1.0, sj[1], jnp.where(i == 2.0, sj[2], sj[3])))
    w1, w2 = sel(i1), sel(i2)
    den = w1 + w2
    base = gsel * float(EXPERTS_PER_GROUP)
    return ((base + i1).astype(jnp.int32), (base + i2).astype(jnp.int32)), (w1 / den, w2 / den)


def _mix_out_kernel(has_gate, *refs):
    if has_gate:
        a_ref, hs_ref, x_ref, wo_ref, g1_ref, g_ref, sh_ref, sc_ref, wr1_ref, wr2_ref, rb_ref, \
            x_out, h_out, idx_out, wts_out = refs
    else:
        a_ref, x_ref, wo_ref, g1_ref, g_ref, sh_ref, sc_ref, wr1_ref, wr2_ref, rb_ref, \
            x_out, h_out, idx_out, wts_out = refs
    tm = x_ref.shape[0]
    sub = min(ROW_SUB, tm)
    for i in range(tm // sub):
        rows = slice(i * sub, (i + 1) * sub)
        if has_gate:
            a = (a_ref[rows, :].astype(F32) * hs_ref[rows, :]).astype(BF16)
        else:
            a = a_ref[rows, :]
        x1 = x_ref[rows, :] + g1_ref[...] * _dot(a, wo_ref[...])
        x_out[rows, :] = x1
        h2 = _modulate(x1, g_ref[...], sh_ref[...], sc_ref[...])
        _to_tiles(h_out, h2, i * sub)
        idx, wts = _route(h2, wr1_ref, wr2_ref, rb_ref)
        for k in range(TOP_K):
            idx_out[k:k + 1, rows] = idx[k]
            wts_out[k:k + 1, rows] = wts[k]


def _mix_out(a, hs, x, w_o, gate1, gain, shift, scale, wr1, wr2, rbias):
    bsz, seq, d = x.shape
    tm = min(ROW_TILE, seq)
    row = lambda b, i: (b, i, 0)
    per_b = lambda b, i: (b, 0, 0)
    const = lambda b, i: (0, 0)
    full = lambda t: pl.BlockSpec(t.shape, const)
    vec = pl.BlockSpec((None, 1, d), per_b)
    acts = [a] if hs is None else [a, hs]
    return pl.pallas_call(
        functools.partial(_mix_out_kernel, hs is not None),
        grid=(bsz, seq // tm),
        in_specs=[pl.BlockSpec((None, tm, t.shape[-1]), row) for t in acts] + [
            pl.BlockSpec((None, tm, d), row), full(w_o), vec, full(gain), vec, vec,
            full(wr1), full(wr2), full(rbias),
        ],
        out_specs=[
            pl.BlockSpec((None, tm, d), row),
            pl.BlockSpec((None, tm * N_SUB, LANES), row),
            pl.BlockSpec((None, TOP_K, tm), lambda b, i: (b, 0, i)),
            pl.BlockSpec((None, TOP_K, tm), lambda b, i: (b, 0, i)),
        ],
        out_shape=[
            jax.ShapeDtypeStruct((bsz, seq, d), F32),
            jax.ShapeDtypeStruct((bsz, seq * N_SUB, LANES), U32),
            jax.ShapeDtypeStruct((bsz, TOP_K, seq), jnp.int32),
            jax.ShapeDtypeStruct((bsz, TOP_K, seq), F32),
        ],
        compiler_params=_params("arbitrary", "arbitrary"),
        name="mix_out_route",
    )(*acts, x, w_o, gate1, gain, shift, scale, wr1, wr2, rbias)


N_SUB = D_MODEL // (2 * LANES)
U32 = jnp.uint32
TABLE_CHUNK = 512


def _to_tiles(ref, val, lo=0):
    n, d = val.shape
    bits = lambda t: lax.bitcast_convert_type(t.astype(BF16).astype(F32), U32)
    for s in range(N_SUB):
        hi = bits(val[:, s * LANES:(s + 1) * LANES])
        lo_half = bits(val[:, d // 2 + s * LANES:d // 2 + (s + 1) * LANES])
        ref[pl.ds(lo * N_SUB + s, n, stride=N_SUB), :] = hi | (lo_half >> 16)


def _from_tiles(ref, lo, n):
    words = [ref[pl.ds(lo * N_SUB + s, n, stride=N_SUB), :] for s in range(N_SUB)]
    hi = [lax.bitcast_convert_type(w & jnp.uint32(0xFFFF0000), F32) for w in words]
    lo_half = [lax.bitcast_convert_type(w << 16, F32) for w in words]
    return jnp.concatenate(hi + lo_half, axis=1)


def _tables_kernel(idx_ref, rank_ref, cnt_ref, carry):
    @pl.when(pl.program_id(0) == 0)
    def _():
        carry[...] = jnp.zeros_like(carry)

    seq = idx_ref.shape[-1]
    ch = min(TABLE_CHUNK, seq)
    tri = jnp.where(lax.broadcasted_iota(jnp.int32, (ch, ch), 0) <= lax.broadcasted_iota(jnp.int32, (ch, ch), 1),
                    1.0, 0.0).astype(BF16)
    eiota = lax.broadcasted_iota(jnp.int32, (N_EXPERTS, ch), 0)
    cnt = carry[...]
    for k in range(TOP_K):
        for c in range(seq // ch):
            sel = eiota == idx_ref[k:k + 1, c * ch:(c + 1) * ch]
            pref = _dot(jnp.where(sel, 1.0, 0.0).astype(BF16), tri) + cnt
            rank = jnp.sum(jnp.where(sel, pref, 0.0), axis=0, keepdims=True) - 1.0
            rank_ref[k:k + 1, c * ch:(c + 1) * ch] = rank.astype(jnp.int32)
            cnt = pref[:, ch - 1:ch]
    carry[...] = cnt
    cnt_ref[...] = jnp.broadcast_to(cnt, cnt_ref.shape)


def _tables(idx):
    bsz, _, seq = idx.shape
    return pl.pallas_call(
        _tables_kernel,
        grid=(bsz,),
        in_specs=[pl.BlockSpec((None, TOP_K, seq), lambda b: (b, 0, 0))],
        out_specs=[pl.BlockSpec((None, TOP_K, seq), lambda b: (b, 0, 0)),
                   pl.BlockSpec((N_EXPERTS, LANES), lambda b: (0, 0))],
        out_shape=[jax.ShapeDtypeStruct((bsz, TOP_K, seq), jnp.int32),
                   jax.ShapeDtypeStruct((N_EXPERTS, LANES), F32)],
        scratch_shapes=[pltpu.VMEM((N_EXPERTS, 1), F32)],
        compiler_params=_params("arbitrary"),
        name="moe_tables",
    )(idx)


def _zero_runs(step, total, pad_start_ref, pad_len_ref, tail_ref):
    ops = []
    for m in range(-(-2 * N_EXPERTS // total)):
        u = step + m * total
        e = jnp.minimum(u, N_EXPERTS - 1)
        length = jnp.where(u < N_EXPERTS, pad_len_ref[e], 0)
        first = pad_start_ref[e]
        for bit in reversed(range(MOE_TILE.bit_length() - 1)):
            done = lax.shift_left(lax.shift_right_logical(length, bit + 1), bit + 1)
            ops.append((lax.bitwise_and(lax.shift_right_logical(length, bit), 1) == 1, first + done, 1 << bit))
        t = u - N_EXPERTS
        ops.append(((t >= 0) & (t < tail_ref[1]), tail_ref[0] + t * MOE_TILE, MOE_TILE))
    return ops


def _scatter_kernel(total, pad_start_ref, pad_len_ref, tail_ref, dest_ref, src_hbm, dst_hbm,
                    buf, zbuf, sem_in, sem_out, sem_z):
    n = pl.program_id(0) * pl.num_programs(1) + pl.program_id(1)
    tm = buf.shape[1] // N_SUB
    slot = lax.rem(n, 3)

    def load(step, sl):
        return pltpu.make_async_copy(src_hbm.at[pl.ds(step * tm * N_SUB, tm * N_SUB)], buf.at[sl], sem_in.at[sl])

    def drain(sl):
        for _ in range(TOP_K):
            pltpu.make_async_copy(buf.at[sl], dst_hbm.at[pl.ds(0, tm * N_SUB)], sem_out.at[sl]).wait()

    def zero_fill(step, wait):
        for pred, first, rows in _zero_runs(step, total, pad_start_ref, pad_len_ref, tail_ref):
            @pl.when(pred)
            def _(first=first, rows=rows):
                cp = pltpu.make_async_copy(zbuf.at[pl.ds(0, rows * N_SUB)],
                                           dst_hbm.at[pl.ds(pl.multiple_of(first * N_SUB, N_SUB), rows * N_SUB)], sem_z)
                cp.wait() if wait else cp.start()

    @pl.when(n == 0)
    def _():
        zbuf[...] = jnp.zeros_like(zbuf)
        load(0, 0).start()
        if total > 1:
            load(1, 1).start()

    load(n, slot).wait()
    for k in range(TOP_K):
        for c in range(tm // LANES):
            def start(j, carry, k=k, c=c):
                src = buf.at[slot, pl.ds(pl.multiple_of((c * LANES + j) * N_SUB, N_SUB), N_SUB)]
                dst = dst_hbm.at[pl.ds(pl.multiple_of(dest_ref[0, k * tm + c * LANES + j], N_SUB), N_SUB)]
                pltpu.make_async_copy(src, dst, sem_out.at[slot]).start(priority=k)
                return carry
            lax.fori_loop(0, LANES, start, 0, unroll=8)
    zero_fill(n, wait=False)

    @pl.when(n > 0)
    def _():
        drain(lax.rem(n + 2, 3))
        zero_fill(n - 1, wait=True)

    @pl.when(n + 2 < total)
    def _():
        load(n + 2, lax.rem(n + 2, 3)).start()

    @pl.when(n == total - 1)
    def _():
        drain(slot)
        zero_fill(n, wait=True)


def _index_blocks(table, tm):
    bsz, _, seq = table.shape
    nt = seq // tm
    t = table.reshape(bsz, TOP_K, nt, tm).transpose(0, 2, 1, 3)
    return t.reshape(bsz * nt, 1, TOP_K * tm), (None, 1, TOP_K * tm)


def _scatter(dest, pad_start, pad_len, tail, h2t, n_rows):
    bsz, _, seq = dest.shape
    tm = min(ROW_TILE, seq)
    nt = seq // tm
    dest4, dest_block = _index_blocks(dest, tm)
    grid_spec = pltpu.PrefetchScalarGridSpec(
        num_scalar_prefetch=3,
        grid=(bsz, nt),
        in_specs=[
            pl.BlockSpec(dest_block, lambda b, i, *_: (b * nt + i, 0, 0), memory_space=pltpu.SMEM),
            pl.BlockSpec(memory_space=pl.ANY),
        ],
        out_specs=pl.BlockSpec(memory_space=pl.ANY),
        scratch_shapes=[pltpu.VMEM((3, tm * N_SUB, LANES), U32), pltpu.VMEM((MOE_TILE * N_SUB, LANES), U32),
                        pltpu.SemaphoreType.DMA((3,)), pltpu.SemaphoreType.DMA((3,)), pltpu.SemaphoreType.DMA(())],
    )
    return pl.pallas_call(
        functools.partial(_scatter_kernel, bsz * nt),
        grid_spec=grid_spec,
        out_shape=jax.ShapeDtypeStruct((n_rows * N_SUB, LANES), U32),
        compiler_params=_params("arbitrary", "arbitrary"),
        name="moe_scatter",
    )(pad_start, pad_len, tail, dest4, h2t.reshape(bsz * seq * N_SUB, LANES))


def _expert_kernel(blk_exp_ref, blk_first_ref, n_used_ref,
                   xs_ref, wgu_ref, wdn_ref, ys_ref, wgu_bf, wdn_bf):
    i = pl.program_id(0)

    @pl.when(i < n_used_ref[0])
    def _():
        @pl.when(blk_first_ref[i] == 1)
        def _():
            wgu_bf[...] = wgu_ref[...].astype(BF16)
            wdn_bf[...] = wdn_ref[...].astype(BF16)

        x = _from_tiles(xs_ref, 0, MOE_TILE).astype(BF16)
        gu = _dot(x, wgu_bf[...])
        g = gu[:, :D_EXPERT]
        u = gu[:, D_EXPERT:]
        mid = (g * jax.nn.sigmoid(g) * u).astype(BF16)
        _to_tiles(ys_ref, _dot(mid, wdn_bf[...]))

    @pl.when(i >= n_used_ref[0])
    def _():
        ys_ref[...] = jnp.zeros_like(ys_ref)


def _experts(blk_exp, blk_first, n_used, xs, w_gu, w_dn, layer):
    d = D_MODEL
    nb = xs.shape[0] // (MOE_TILE * N_SUB)
    tile = lambda i, *_: (i, 0)
    grid_spec = pltpu.PrefetchScalarGridSpec(
        num_scalar_prefetch=3,
        grid=(nb,),
        in_specs=[
            pl.BlockSpec((MOE_TILE * N_SUB, LANES), tile),
            pl.BlockSpec((None, None, d, 2 * D_EXPERT), lambda i, be, *_: (layer, be[i], 0, 0)),
            pl.BlockSpec((None, None, D_EXPERT, d), lambda i, be, *_: (layer, be[i], 0, 0)),
        ],
        out_specs=pl.BlockSpec((MOE_TILE * N_SUB, LANES), tile),
        scratch_shapes=[pltpu.VMEM((d, 2 * D_EXPERT), BF16), pltpu.VMEM((D_EXPERT, d), BF16)],
    )
    return pl.pallas_call(
        _expert_kernel,
        grid_spec=grid_spec,
        out_shape=jax.ShapeDtypeStruct(xs.shape, U32),
        compiler_params=_params("arbitrary"),
        name="moe_experts",
    )(blk_exp, blk_first, n_used, xs, w_gu, w_dn)


def _combine_kernel(dcur_ref, dnxt_ref, ys_hbm, x_ref, wts_ref, g2_ref, x_out, buf0, buf1, sem):
    nt = pl.num_programs(1)
    n = pl.program_id(0) * nt + pl.program_id(1)
    total = pl.num_programs(0) * nt
    tm = x_ref.shape[0]
    bufs = (buf0, buf1)

    def copy(d_ref, sl, k, r):
        src = ys_hbm.at[pl.ds(pl.multiple_of(d_ref[0, k * tm + r], N_SUB), N_SUB)]
        dst = bufs[sl].at[pl.ds(pl.multiple_of((k * tm + r) * N_SUB, N_SUB), N_SUB)]
        return pltpu.make_async_copy(src, dst, sem.at[sl])

    def drain(sl):
        pltpu.make_async_copy(ys_hbm.at[pl.ds(0, TOP_K * tm * N_SUB)], bufs[sl], sem.at[sl]).wait()

    @pl.when(n == 0)
    def _():
        for k in range(TOP_K):
            def start(r, carry, k=k):
                copy(dcur_ref, 0, k, r).start(priority=k)
                return carry
            lax.fori_loop(0, tm, start, 0, unroll=8)

    def step(sl):
        drain(sl)
        for r in range(tm):
            for k in range(TOP_K):
                copy(dnxt_ref, 1 - sl, k, r).start(priority=k)
        w = wts_ref[...]
        y = w[:, 0:1] * _from_tiles(bufs[sl], 0, tm) + w[:, 1:2] * _from_tiles(bufs[sl], tm, tm)
        x_out[...] = x_ref[...] + g2_ref[...] * y

        @pl.when(n == total - 1)
        def _():
            drain(1 - sl)

    for sl in range(2):
        pl.when(lax.rem(n, 2) == sl)(functools.partial(step, sl))


def _combine(dest_row, ys, x, wts_col, gate2):
    bsz, seq, d = x.shape
    tm = min(ROW_TILE, seq)
    nt = seq // tm

    def nxt(b, i):
        return (jnp.minimum(b * nt + i + 1, bsz * nt - 1), 0, 0)

    dest_row, dest_block = _index_blocks(dest_row, tm)
    return pl.pallas_call(
        _combine_kernel,
        grid=(bsz, nt),
        in_specs=[
            pl.BlockSpec(dest_block, lambda b, i: (b * nt + i, 0, 0), memory_space=pltpu.SMEM),
            pl.BlockSpec(dest_block, nxt, memory_space=pltpu.SMEM),
            pl.BlockSpec(memory_space=pl.ANY),
            pl.BlockSpec((None, tm, d), lambda b, i: (b, i, 0)),
            pl.BlockSpec((None, tm, TOP_K), lambda b, i: (b, i, 0)),
            pl.BlockSpec((None, 1, d), lambda b, i: (b, 0, 0)),
        ],
        out_specs=pl.BlockSpec((None, tm, d), lambda b, i: (b, i, 0)),
        out_shape=jax.ShapeDtypeStruct((bsz, seq, d), F32),
        scratch_shapes=[pltpu.VMEM((TOP_K * tm * N_SUB, LANES), U32), pltpu.VMEM((TOP_K * tm * N_SUB, LANES), U32),
                        pltpu.SemaphoreType.DMA((2,))],
        compiler_params=_params("arbitrary", "arbitrary"),
        name="moe_combine",
    )(dest_row, dest_row, ys, x, wts_col, gate2)


def _lookup(table, keys):
    hit = keys[..., None] == jnp.arange(table.shape[0], dtype=jnp.int32)
    return jnp.sum(jnp.where(hit, table, 0), axis=-1).astype(jnp.int32)


def _count_le(bounds, q):
    return jnp.sum((bounds <= q[..., None]).astype(jnp.int32), axis=-1)


def _moe(h2t, idx, wts, x, gate2, w_gu, w_dn, layer):
    bsz, seq, _ = x.shape
    n_rows = bsz * seq * TOP_K + N_EXPERTS * MOE_TILE
    nb = n_rows // MOE_TILE
    rank, cnt = _tables(idx)
    counts = cnt[:, 0].astype(jnp.int32)
    padded = ((counts + MOE_TILE - 1) // MOE_TILE) * MOE_TILE
    pend = jnp.cumsum(padded)
    pstart = pend - padded
    dest = _lookup(pstart, idx) + rank
    blk_row = jnp.arange(nb, dtype=jnp.int32) * MOE_TILE
    blk_exp = jnp.minimum(_count_le(pend, blk_row), N_EXPERTS - 1)
    blk_first = (blk_row == _lookup(pstart, blk_exp)).astype(jnp.int32)
    n_used = (pend[-1:] // MOE_TILE).astype(jnp.int32)
    tail = jnp.concatenate([pend[-1:], (n_rows - pend[-1:]) // MOE_TILE]).astype(jnp.int32)
    dest_row = dest * N_SUB
    xs = _scatter(dest_row, (pstart + counts).astype(jnp.int32), (padded - counts).astype(jnp.int32), tail, h2t, n_rows)
    ys = _experts(blk_exp, blk_first, n_used, xs, w_gu, w_dn, layer)
    return _combine(dest_row, ys, x, wts.transpose(0, 2, 1), gate2)


def _rnn_in_kernel(x_ref, g_ref, sh_ref, sc_ref, w_ref, gate_out, xb_out):
    h = _modulate(x_ref[...], g_ref[...], sh_ref[...], sc_ref[...])
    u = _dot(h.astype(BF16), w_ref[...])
    gate_out[...] = jax.nn.gelu(u[:, :D_RNN]).astype(BF16)
    xb_out[...] = u[:, D_RNN:]


def _rnn_in(x, gain, shift, scale, w_in):
    bsz, seq, d = x.shape
    tm = min(ROW_TILE, seq)
    row = lambda b, i: (b, i, 0)
    per_b = lambda b, i: (b, 0, 0)
    const = lambda b, i: (0, 0)
    return pl.pallas_call(
        _rnn_in_kernel,
        grid=(bsz, seq // tm),
        in_specs=[
            pl.BlockSpec((None, tm, d), row),
            pl.BlockSpec(gain.shape, const),
            pl.BlockSpec((None, 1, d), per_b),
            pl.BlockSpec((None, 1, d), per_b),
            pl.BlockSpec(w_in.shape, const),
        ],
        out_specs=[pl.BlockSpec((None, tm, D_RNN), row), pl.BlockSpec((None, tm, D_RNN), row)],
        out_shape=[jax.ShapeDtypeStruct((bsz, seq, D_RNN), BF16),
                   jax.ShapeDtypeStruct((bsz, seq, D_RNN), F32)],
        compiler_params=_params("arbitrary", "arbitrary"),
        name="rnn_in",
    )(x, gain, shift, scale, w_in)


def _lru_kernel(xb_ref, cw_ref, cb_ref, wcat_ref, bcat_ref, lam_ref, hs_ref,
                xi_ref, af_ref, bf_ref, ab_ref, bb_ref, hf_ref, hb_ref, sum_ref):
    seq, c = xb_ref.shape
    seg_len = seq // SUBLANES
    n_slab = c // LANES
    n_rows = seg_len * SUBLANES
    halo = (CONV_W // 2) * SUBLANES
    row = lax.broadcasted_iota(jnp.int32, (SUBLANES, LANES), 0)
    for sl in range(n_slab):
        lanes = slice(sl * LANES, (sl + 1) * LANES)
        for g in range(SUBLANES):
            xi_ref[sl, pl.ds(halo + g, seg_len, stride=SUBLANES), :] = xb_ref[g * seg_len:(g + 1) * seg_len, lanes]
        for back in (1, 2):
            prev = xi_ref[sl, halo + (seg_len - back) * SUBLANES:halo + (seg_len - back + 1) * SUBLANES, :]
            xi_ref[sl, halo - back * SUBLANES:halo - (back - 1) * SUBLANES, :] = jnp.where(
                row == 0, 0.0, pltpu.roll(prev, 1, 0))
        nxt = xi_ref[sl, halo:halo + SUBLANES, :]
        xi_ref[sl, halo + n_rows:halo + n_rows + SUBLANES, :] = jnp.where(
            row == SUBLANES - 1, 0.0, pltpu.roll(nxt, SUBLANES - 1, 0))

    cw = cw_ref[...]
    cb = cb_ref[...]
    lam = lam_ref[...]
    neg = -lam
    softplus = jnp.maximum(neg, 0.0) + jnp.log1p(jnp.exp(-jnp.abs(neg)))
    half_rate = (-0.5 * LRU_C) * softplus
    rows = min(SCAN_ROWS, n_rows)
    n_chunks = n_rows // rows

    for ci in range(n_chunks):
        i0 = ci * rows
        taps = []
        for k in range(CONV_W):
            lo = halo + i0 + (k - CONV_W // 2) * SUBLANES
            taps.append(jnp.concatenate([xi_ref[sl, lo:lo + rows, :] for sl in range(n_slab)], axis=1))
        xc = cb
        for k in range(CONV_W):
            xc = xc + taps[k] * cw[k:k + 1, :]
        xcb = xc.astype(BF16)
        xh = 0.5 * xc
        for dirn, (a_ref, b_ref) in enumerate(((af_ref, bf_ref), (ab_ref, bb_ref))):
            cols = slice(2 * dirn * c, 2 * (dirn + 1) * c)
            th = jnp.tanh(_dot(xcb, wcat_ref[:, cols]) + bcat_ref[:, cols])
            hr = half_rate[dirn:dirn + 1, :]
            log_a = hr * th[:, :c] + hr
            a = jnp.exp(log_a)
            mult = jnp.sqrt(jnp.tanh(-log_a) * (a * a + 1.0))
            if dirn == 0 and ci == 0:
                mult = jnp.where(lax.broadcasted_iota(jnp.int32, mult.shape, 0) == 0, 1.0, mult)
            if dirn == 1 and ci == n_chunks - 1:
                mult = jnp.where(lax.broadcasted_iota(jnp.int32, mult.shape, 0) == rows - 1, 1.0, mult)
            b = mult * (th[:, c:] + 1.0) * xh
            for sl in range(n_slab):
                a_ref[sl, i0:i0 + rows, :] = a[:, sl * LANES:(sl + 1) * LANES]
                b_ref[sl, i0:i0 + rows, :] = b[:, sl * LANES:(sl + 1) * LANES]

    def step_rows(cidx):
        fwd = pl.ds(pl.multiple_of(cidx * SUBLANES, SUBLANES), SUBLANES)
        bwd = pl.ds(pl.multiple_of((seg_len - 1 - cidx) * SUBLANES, SUBLANES), SUBLANES)
        return fwd, bwd

    zero = jnp.zeros((SUBLANES, LANES), F32)
    one = jnp.ones((SUBLANES, LANES), F32)

    def totals(cidx, carry):
        fwd, bwd = step_rows(cidx)
        out = []
        for sl in range(n_slab):
            hf, pf, hb, pb = carry[sl]
            af, ab = af_ref[sl, fwd, :], ab_ref[sl, bwd, :]
            out.append((af * hf + bf_ref[sl, fwd, :], af * pf, ab * hb + bb_ref[sl, bwd, :], ab * pb))
        return tuple(out)
    tot = lax.fori_loop(0, seg_len, totals, tuple((zero, one, zero, one) for _ in range(n_slab)), unroll=8)

    enter = []
    for sl in range(n_slab):
        hf, pf, hb, pb = tot[sl]
        cf, cbk = zero, zero
        for _ in range(SUBLANES - 1):
            cf = jnp.where(row == 0, 0.0, pltpu.roll(hf + pf * cf, 1, 0))
            cbk = jnp.where(row == SUBLANES - 1, 0.0, pltpu.roll(hb + pb * cbk, SUBLANES - 1, 0))
        enter.append((cf, cbk))

    def states(meet, cidx, carry):
        fwd, bwd = step_rows(cidx)
        out = []
        for sl in range(n_slab):
            hf, hb = carry[sl]
            hf = af_ref[sl, fwd, :] * hf + bf_ref[sl, fwd, :]
            hb = ab_ref[sl, bwd, :] * hb + bb_ref[sl, bwd, :]
            if meet:
                sum_ref[sl, fwd, :] = hf + hb_ref[sl, fwd, :]
                sum_ref[sl, bwd, :] = hb + hf_ref[sl, bwd, :]
            else:
                hf_ref[sl, fwd, :] = hf
                hb_ref[sl, bwd, :] = hb
            out.append((hf, hb))
        return tuple(out)
    mid = lax.fori_loop(0, seg_len // 2, functools.partial(states, False), tuple(enter), unroll=8)
    lax.fori_loop(seg_len // 2, seg_len, functools.partial(states, True), mid, unroll=8)

    for g in range(SUBLANES):
        for sl in range(n_slab):
            hs_ref[g * seg_len:(g + 1) * seg_len, sl * LANES:(sl + 1) * LANES] = (
                sum_ref[sl, pl.ds(g, seg_len, stride=SUBLANES), :])


def _lru(xb, conv_w, conv_b, wcat, bcat, lam):
    bsz, seq, _ = xb.shape
    c = RNN_BW
    blk = lambda b, n: (b, 0, n)
    return pl.pallas_call(
        _lru_kernel,
        grid=(bsz, RNN_BLOCKS),
        in_specs=[
            pl.BlockSpec((None, seq, c), blk),
            pl.BlockSpec((CONV_W, c), lambda b, n: (0, n)),
            pl.BlockSpec((1, c), lambda b, n: (0, n)),
            pl.BlockSpec((None, c, 4 * c), lambda b, n: (n, 0, 0)),
            pl.BlockSpec((None, 1, 4 * c), lambda b, n: (n, 0, 0)),
            pl.BlockSpec((2, c), lambda b, n: (0, n)),
        ],
        out_specs=pl.BlockSpec((None, seq, c), blk),
        out_shape=jax.ShapeDtypeStruct((bsz, seq, D_RNN), F32),
        scratch_shapes=[pltpu.VMEM((c // LANES, seq + (CONV_W - 1) * SUBLANES, LANES), F32)]
        + [pltpu.VMEM((c // LANES, seq, LANES), F32)] * 7,
        compiler_params=_params("arbitrary", "arbitrary"),
        name="rglru_scan",
    )(xb, conv_w, conv_b, wcat, bcat, lam)


def _mla_weights(w_in, w_q_b, w_kv_b, q_norm, k_norm):
    half = QK_ROPE // 2

    def slab(t):
        return jnp.pad(t, [(0, 0)] * (t.ndim - 1) + [(0, LANES - QK_HEAD)])

    def rot_slab(t):
        rope = t[..., QK_NOPE:]
        swapped = jnp.concatenate([jnp.zeros_like(t[..., :QK_NOPE]), rope[..., half:], rope[..., :half]], axis=-1)
        return slab(swapped)

    kpe = jnp.pad(w_in[:, Q_LORA + KV_LORA:], ((0, 0), (QK_NOPE, 0)))
    w_in_p = jnp.concatenate([w_in[:, :Q_LORA + KV_LORA], slab(kpe), rot_slab(kpe)], axis=1).astype(BF16)
    wq = w_q_b.reshape(Q_LORA, N_HEADS, QK_HEAD)
    wq_p = slab(wq).reshape(Q_LORA, N_HEADS * LANES).astype(BF16)
    wq_rot = rot_slab(wq).reshape(Q_LORA, N_HEADS * LANES).astype(BF16)
    wkv = w_kv_b.reshape(KV_LORA, N_HEADS, QK_NOPE + V_HEAD)
    wk = jnp.pad(wkv[:, :, :QK_NOPE], ((0, 0), (0, 0), (0, LANES - QK_NOPE))).reshape(KV_LORA, N_HEADS * LANES)
    wv = wkv[:, :, QK_NOPE:].reshape(KV_LORA, N_HEADS * V_HEAD)
    w_kv_p = jnp.concatenate([wk, wv], axis=1).astype(BF16)
    gains = lambda g: jnp.stack([slab(g), rot_slab(g)], axis=0)
    return w_in_p, wq_p, wq_rot, w_kv_p, gains(q_norm), gains(k_norm)


def _rope_tables(positions):
    half = QK_ROPE // 2
    inv_freq = ROPE_THETA ** (-jnp.arange(half, dtype=F32) / half)
    ang = positions.astype(F32)[..., None] * inv_freq
    cos, sin = jnp.cos(ang), jnp.sin(ang)
    lead = positions.shape + (QK_NOPE,)
    tail = positions.shape + (LANES - QK_HEAD,)
    cos_t = jnp.concatenate([jnp.ones(lead, F32), cos, cos, jnp.ones(tail, F32)], axis=-1)
    sin_t = jnp.concatenate([jnp.zeros(lead, F32), -sin, sin, jnp.zeros(tail, F32)], axis=-1)
    return cos_t, sin_t


def _router_weights(w_router, router_bias):
    perm = (jnp.arange(N_EXPERTS) % N_GROUPS) * EXPERTS_PER_GROUP + jnp.arange(N_EXPERTS) // N_GROUPS
    w = w_router[:, perm]
    hi = w.astype(BF16)
    lo = (w - hi.astype(F32)).astype(BF16)
    z = jnp.zeros_like(hi)
    wr1 = jnp.concatenate([hi, lo, z, z], axis=1)
    wr2 = jnp.concatenate([z, z, hi, z], axis=1)
    return wr1, wr2, router_bias[perm].reshape(N_EXPERTS, 1).astype(F32)


def kernel(x, c, positions, norm_mix, norm_ffn, w_ada, b_ada, mla_w_in, mla_q_a_norm, mla_kv_a_norm, mla_w_q_b, mla_w_kv_b, mla_q_norm, mla_k_norm, mla_w_o, rnn_w_in, rnn_conv_w, rnn_conv_b, rnn_lam_f, rnn_w_rf, rnn_b_rf, rnn_w_if, rnn_b_if, rnn_lam_b, rnn_w_rb, rnn_b_rb, rnn_w_ib, rnn_b_ib, rnn_w_o, w_router, router_bias, moe_w_gu, moe_w_dn):
    bsz, seq, d = x.shape
    depth = w_ada.shape[0]
    mod = _ada(c, w_ada, b_ada)
    wr1, wr2, rbias = _router_weights(w_router, router_bias)
    cos_t, sin_t = _rope_tables(positions)
    vec = lambda v: v.reshape(1, -1)
    for i in range(depth):
        sh1, sc1, g1, sh2, sc2, g2 = [mod[i, :, k * d:(k + 1) * d].reshape(bsz, 1, d) for k in range(6)]
        j = i // 2
        if i % 2 == 0:
            w_in_p, wq, wq_rot, wkv, qn, kn = _mla_weights(mla_w_in[j], mla_w_q_b[j], mla_w_kv_b[j],
                                                           mla_q_norm[j], mla_k_norm[j])
            q, k, v = _mla_in(x, vec(norm_mix[i]), sh1, sc1, w_in_p, vec(mla_q_a_norm[j]),
                              vec(mla_kv_a_norm[j]), wq, wq_rot, wkv, qn, kn, cos_t, sin_t)
            a = _attention(q, k, v, mla_q_norm[j], mla_k_norm[j])
            hs = None
            w_o = mla_w_o[j].astype(BF16)
        else:
            a, xb = _rnn_in(x, vec(norm_mix[i]), sh1, sc1, rnn_w_in[j].astype(BF16))
            wcat = (0.5 * jnp.concatenate([rnn_w_rf[j], rnn_w_if[j], rnn_w_rb[j], rnn_w_ib[j]], axis=-1)).astype(BF16)
            bcat = jnp.stack([b.reshape(RNN_BLOCKS, RNN_BW) for b in
                              (rnn_b_rf[j], rnn_b_if[j], rnn_b_rb[j], rnn_b_ib[j])], axis=1)
            bcat = 0.5 * bcat.reshape(RNN_BLOCKS, 1, 4 * RNN_BW)
            lam = jnp.stack([rnn_lam_f[j], rnn_lam_b[j]], axis=0)
            hs = _lru(xb, rnn_conv_w[j], vec(rnn_conv_b[j]), wcat, bcat, lam)
            w_o = rnn_w_o[j].astype(BF16)
        x, h2, idx, wts = _mix_out(a, hs, x, w_o, g1, vec(norm_ffn[i]), sh2, sc2, wr1, wr2, rbias)
        x = _moe(h2, idx, wts, x, g2, moe_w_gu, moe_w_dn, i)
    return x
```

```python
import functools

import jax
import jax.numpy as jnp
from jax import lax
from jax.experimental import pallas as pl
from jax.experimental.pallas import tpu as pltpu

F32 = jnp.float32
BF16 = jnp.bfloat16

D_MODEL = 1024
N_HEADS = 16
Q_LORA = 384
KV_LORA = 256
QK_NOPE = 64
QK_ROPE = 32
QK_HEAD = QK_NOPE + QK_ROPE
V_HEAD = 64
ROPE_THETA = 10000.0
D_RNN = D_MODEL
RNN_BLOCKS = 4
RNN_BW = D_RNN // RNN_BLOCKS
CONV_W = 4
LRU_C = 8.0
N_EXPERTS = 32
N_GROUPS = 8
EXPERTS_PER_GROUP = N_EXPERTS // N_GROUPS
TOP_K = 2
D_EXPERT = 512
EPS = 1e-6
LOG2_E = 1.4426950408889634

LANES = 128
SUBLANES = 8
VMEM_LIMIT = 52 * 1024 * 1024

ROW_TILE = 512
ROW_SUB = 256
Q_TILE = 2048
Q_SUB = 256
MAX_SAFE_SHIFT = 60.0
MOE_TILE = 512
SCAN_ROWS = 256


def _dot(a, b):
    return jnp.dot(a, b, preferred_element_type=F32)


def _split_bf16(a):
    hi = a.astype(BF16)
    lo = (a - hi.astype(F32)).astype(BF16)
    return hi, lo


def _dot_split(a, b):
    ah, al = _split_bf16(a)
    bh, bl = _split_bf16(b)
    return _dot(ah, bh) + (_dot(ah, bl) + _dot(al, bh))


def _rms(x, gain, n):
    ms = jnp.sum(x * x, axis=-1, keepdims=True) * (1.0 / n)
    return x * lax.rsqrt(ms + EPS) * gain


def _modulate(x, gain, shift, scale):
    return _rms(x, gain, x.shape[-1]) * (1.0 + scale) + shift


def _params(*sem):
    return pltpu.CompilerParams(dimension_semantics=sem, vmem_limit_bytes=VMEM_LIMIT)


def _ada_kernel(c_ref, w_ref, b_ref, o_ref):
    c = c_ref[...]
    o_ref[...] = _dot_split(c * jax.nn.sigmoid(c), w_ref[...]) + b_ref[...]


def _ada(c, w_ada, b_ada):
    depth, d, n = w_ada.shape
    bsz = c.shape[0]
    tn = 1536
    return pl.pallas_call(
        _ada_kernel,
        grid=(depth, n // tn),
        in_specs=[
            pl.BlockSpec((bsz, d), lambda l, j: (0, 0)),
            pl.BlockSpec((None, d, tn), lambda l, j: (l, 0, j)),
            pl.BlockSpec((None, 1, tn), lambda l, j: (l, 0, j)),
        ],
        out_specs=pl.BlockSpec((None, bsz, tn), lambda l, j: (l, 0, j)),
        out_shape=jax.ShapeDtypeStruct((depth, bsz, n), F32),
        compiler_params=_params("arbitrary", "arbitrary"),
        name="adaln_mod",
    )(c, w_ada, b_ada.reshape(depth, 1, n))


def _head_scale(s):
    return lax.rsqrt(jnp.sum(s * s, axis=-1, keepdims=True) * (1.0 / QK_HEAD) + EPS)


def _mla_in_kernel(x_ref, g_ref, sh_ref, sc_ref, win_ref, qan_ref, kvan_ref, wq_ref, wqr_ref, wkv_ref,
                   qn_ref, kn_ref, cos_ref, sin_ref, q_out, k_out, v_out):
    h = _modulate(x_ref[...], g_ref[...], sh_ref[...], sc_ref[...])
    lat = _dot(h.astype(BF16), win_ref[...])
    q_lat = lat[:, :Q_LORA]
    kv_lat = lat[:, Q_LORA:Q_LORA + KV_LORA]
    kpe = lat[:, Q_LORA + KV_LORA:Q_LORA + KV_LORA + LANES]
    kpe_rot = lat[:, Q_LORA + KV_LORA + LANES:]
    qn = _rms(q_lat, qan_ref[...], Q_LORA).astype(BF16)
    q_all = _dot(qn, wq_ref[...])
    q_rot = _dot(qn, wqr_ref[...])
    kv_all = _dot(_rms(kv_lat, kvan_ref[...], KV_LORA).astype(BF16), wkv_ref[...])
    cos_t = cos_ref[...]
    sin_t = sin_ref[...]
    q_scale = LOG2_E * QK_HEAD ** -0.5
    cq = cos_t * (qn_ref[0:1, :] * q_scale)
    sq = sin_t * (qn_ref[1:2, :] * q_scale)
    ck = cos_t * kn_ref[0:1, :]
    k_rot_term = kpe_rot * (sin_t * kn_ref[1:2, :])
    for hh in range(N_HEADS):
        sl = slice(hh * LANES, (hh + 1) * LANES)
        s = q_all[:, sl]
        q_out[hh] = ((s * cq + q_rot[:, sl] * sq) * _head_scale(s)).astype(BF16)
        s = kv_all[:, sl] + kpe
        k_out[hh] = ((s * ck + k_rot_term) * _head_scale(s)).astype(BF16)
    v_out[...] = kv_all[:, N_HEADS * LANES:].astype(BF16)


def _mla_in(x, gain, shift, scale, w_in, q_a_norm, kv_a_norm, w_q, w_q_rot, w_kv, q_norm, k_norm, cos_t, sin_t):
    bsz, seq, d = x.shape
    tm = min(ROW_TILE, seq)
    row = lambda b, i: (b, i, 0)
    per_b = lambda b, i: (b, 0, 0)
    const = lambda b, i: (0, 0)
    full = lambda a: pl.BlockSpec(a.shape, const)
    return pl.pallas_call(
        _mla_in_kernel,
        grid=(bsz, seq // tm),
        in_specs=[
            pl.BlockSpec((None, tm, d), row),
            full(gain),
            pl.BlockSpec((None, 1, d), per_b),
            pl.BlockSpec((None, 1, d), per_b),
            full(w_in), full(q_a_norm), full(kv_a_norm), full(w_q), full(w_q_rot), full(w_kv),
            full(q_norm), full(k_norm),
            pl.BlockSpec((None, tm, LANES), row),
            pl.BlockSpec((None, tm, LANES), row),
        ],
        out_specs=[
            pl.BlockSpec((None, N_HEADS, tm, LANES), lambda b, i: (b, 0, i, 0)),
            pl.BlockSpec((None, N_HEADS, tm, LANES), lambda b, i: (b, 0, i, 0)),
            pl.BlockSpec((None, tm, N_HEADS * V_HEAD), row),
        ],
        out_shape=[
            jax.ShapeDtypeStruct((bsz, N_HEADS, seq, LANES), BF16),
            jax.ShapeDtypeStruct((bsz, N_HEADS, seq, LANES), BF16),
            jax.ShapeDtypeStruct((bsz, seq, N_HEADS * V_HEAD), BF16),
        ],
        compiler_params=_params("arbitrary", "arbitrary"),
        name="mla_in",
    )(x, gain, shift, scale, w_in, q_a_norm, kv_a_norm, w_q, w_q_rot, w_kv, q_norm, k_norm, cos_t, sin_t)


def _attn_kernel(bounded, shift_ref, q_ref, k_ref, v_ref, o_ref):
    v = v_ref[...]
    lane_v = lax.broadcasted_iota(jnp.int32, v.shape, 1)
    v_heads = [jnp.where(lane_v < V_HEAD, v, jnp.ones((), BF16)), jnp.where(lane_v >= V_HEAD, v, jnp.ones((), BF16))]
    lane = lax.broadcasted_iota(jnp.int32, (Q_SUB, LANES), 1)
    for i in range(q_ref.shape[1] // Q_SUB):
        rows = slice(i * Q_SUB, (i + 1) * Q_SUB)
        outs = []
        for j in range(2):
            s = lax.dot_general(q_ref[j, rows, :], k_ref[j], (((1,), (1,)), ((), ())),
                                preferred_element_type=F32)
            m = shift_ref[0] if bounded else jnp.max(s, axis=-1, keepdims=True)
            o = _dot(jnp.exp2(s - m).astype(BF16), v_heads[j])
            denom = o[:, V_HEAD:V_HEAD + 1] if j == 0 else o[:, 0:1]
            outs.append(o / denom)
        o_ref[rows, :] = jnp.where(lane < V_HEAD, outs[0], outs[1]).astype(BF16)


def _attention_call(bounded, shift, q, k, v):
    bsz, _, seq, _ = q.shape
    tq = min(Q_TILE, seq)
    assert tq % Q_SUB == 0
    grid_spec = pltpu.PrefetchScalarGridSpec(
        num_scalar_prefetch=1,
        grid=(bsz, N_HEADS // 2, seq // tq),
        in_specs=[
            pl.BlockSpec((None, 2, tq, LANES), lambda b, h, i, *_: (b, h, i, 0)),
            pl.BlockSpec((None, 2, seq, LANES), lambda b, h, i, *_: (b, h, 0, 0)),
            pl.BlockSpec((None, seq, LANES), lambda b, h, i, *_: (b, 0, h)),
        ],
        out_specs=pl.BlockSpec((None, tq, LANES), lambda b, h, i, *_: (b, i, h)),
    )
    return pl.pallas_call(
        functools.partial(_attn_kernel, bounded),
        grid_spec=grid_spec,
        out_shape=jax.ShapeDtypeStruct((bsz, seq, N_HEADS * V_HEAD), BF16),
        compiler_params=_params("arbitrary", "arbitrary", "arbitrary"),
        name="mla_attention",
    )(shift, q, k, v)


def _attention(q, k, v, q_gain, k_gain):
    score_bound = 1.02 * LOG2_E * QK_HEAD ** 0.5 * jnp.max(jnp.abs(q_gain)) * jnp.max(jnp.abs(k_gain))
    shift = score_bound.reshape(1).astype(F32)
    return lax.cond(score_bound <= MAX_SAFE_SHIFT,
                    functools.partial(_attention_call, True), functools.partial(_attention_call, False),
                    shift, q, k, v)


def _first_index_of_max(vals):
    m = vals[0]
    for v in vals[1:]:
        m = jnp.maximum(m, v)
    idx = jnp.full(m.shape, float(len(vals) - 1), F32)
    for j in range(len(vals) - 2, -1, -1):
        idx = jnp.where(vals[j] == m, float(j), idx)
    return m, idx


def _route(h2, wr1_ref, wr2_ref, rb_ref):
    hh, hl = _split_bf16(h2)
    logits = (_dot(hh, wr1_ref[...]) + _dot(hl, wr2_ref[...])).T
    logit = logits[0:N_EXPERTS] + logits[N_EXPERTS:2 * N_EXPERTS] + logits[2 * N_EXPERTS:3 * N_EXPERTS]
    score = jax.nn.sigmoid(logit)
    biased = score + rb_ref[...]
    a = [biased[j * N_GROUPS:(j + 1) * N_GROUPS] for j in range(EXPERTS_PER_GROUP)]
    sc = [score[j * N_GROUPS:(j + 1) * N_GROUPS] for j in range(EXPERTS_PER_GROUP)]
    hi1, lo1 = jnp.maximum(a[0], a[1]), jnp.minimum(a[0], a[1])
    hi2, lo2 = jnp.maximum(a[2], a[3]), jnp.minimum(a[2], a[3])
    gscore = jnp.maximum(hi1, hi2) + jnp.maximum(jnp.minimum(hi1, hi2), jnp.maximum(lo1, lo2))
    gmax = jnp.max(gscore, axis=0, keepdims=True)
    giota = lax.broadcasted_iota(jnp.int32, gscore.shape, 0).astype(F32)
    gsel = jnp.min(jnp.where(gscore == gmax, giota, float(N_GROUPS)), axis=0, keepdims=True)
    onehot = giota == gsel
    pick = lambda t: jnp.sum(jnp.where(onehot, t, 0.0), axis=0, keepdims=True)
    bj = [pick(t) for t in a]
    sj = [pick(t) for t in sc]
    _, i1 = _first_index_of_max(bj)
    bj2 = [jnp.where(i1 == float(j), -jnp.inf, bj[j]) for j in range(EXPERTS_PER_GROUP)]
    _, i2 = _first_index_of_max(bj2)
    sel = lambda i: jnp.where(i == 0.0, sj[0], jnp.where(i == 1.0, sj[1], jnp.where(i == 2.0, sj[2], sj[3])))
    w1, w2 = sel(i1), sel(i2)
    den = w1 + w2
    base = gsel * float(EXPERTS_PER_GROUP)
    return ((base + i1).astype(jnp.int32), (base + i2).astype(jnp.int32)), (w1 / den, w2 / den)


def _mix_out_kernel(has_gate, *refs):
    if has_gate:
        a_ref, hs_ref, x_ref, wo_ref, g1_ref, g_ref, sh_ref, sc_ref, wr1_ref, wr2_ref, rb_ref, \
            x_out, h_out, idx_out, wts_out = refs
    else:
        a_ref, x_ref, wo_ref, g1_ref, g_ref, sh_ref, sc_ref, wr1_ref, wr2_ref, rb_ref, \
            x_out, h_out, idx_out, wts_out = refs
    tm = x_ref.shape[0]
    sub = min(ROW_SUB, tm)
    for i in range(tm // sub):
        rows = slice(i * sub, (i + 1) * sub)
        if has_gate:
            a = (a_ref[rows, :].astype(F32) * hs_ref[rows, :]).astype(BF16)
        else:
            a = a_ref[rows, :]
        x1 = x_ref[rows, :] + g1_ref[...] * _dot(a, wo_ref[...])
        x_out[rows, :] = x1
        h2 = _modulate(x1, g_ref[...], sh_ref[...], sc_ref[...])
        _to_tiles(h_out, h2, i * sub)
        idx, wts = _route(h2, wr1_ref, wr2_ref, rb_ref)
        for k in range(TOP_K):
            idx_out[k:k + 1, rows] = idx[k]
            wts_out[k:k + 1, rows] = wts[k]


def _mix_out(a, hs, x, w_o, gate1, gain, shift, scale, wr1, wr2, rbias):
    bsz, seq, d = x.shape
    tm = min(ROW_TILE, seq)
    row = lambda b, i: (b, i, 0)
    per_b = lambda b, i: (b, 0, 0)
    const = lambda b, i: (0, 0)
    full = lambda t: pl.BlockSpec(t.shape, const)
    vec = pl.BlockSpec((None, 1, d), per_b)
    acts = [a] if hs is None else [a, hs]
    return pl.pallas_call(
        functools.partial(_mix_out_kernel, hs is not None),
        grid=(bsz, seq // tm),
        in_specs=[pl.BlockSpec((None, tm, t.shape[-1]), row) for t in acts] + [
            pl.BlockSpec((None, tm, d), row), full(w_o), vec, full(gain), vec, vec,
            full(wr1), full(wr2), full(rbias),
        ],
        out_specs=[
            pl.BlockSpec((None, tm, d), row),
            pl.BlockSpec((None, tm * N_SUB, LANES), row),
            pl.BlockSpec((None, TOP_K, tm), lambda b, i: (b, 0, i)),
            pl.BlockSpec((None, TOP_K, tm), lambda b, i: (b, 0, i)),
        ],
        out_shape=[
            jax.ShapeDtypeStruct((bsz, seq, d), F32),
            jax.ShapeDtypeStruct((bsz, seq * N_SUB, LANES), U32),
            jax.ShapeDtypeStruct((bsz, TOP_K, seq), jnp.int32),
            jax.ShapeDtypeStruct((bsz, TOP_K, seq), F32),
        ],
        compiler_params=_params("arbitrary", "arbitrary"),
        name="mix_out_route",
    )(*acts, x, w_o, gate1, gain, shift, scale, wr1, wr2, rbias)


N_SUB = D_MODEL // (2 * LANES)
U32 = jnp.uint32
TABLE_CHUNK = 512


def _to_tiles(ref, val, lo=0):
    n, d = val.shape
    bits = lambda t: lax.bitcast_convert_type(t.astype(BF16).astype(F32), U32)
    for s in range(N_SUB):
        hi = bits(val[:, s * LANES:(s + 1) * LANES])
        lo_half = bits(val[:, d // 2 + s * LANES:d // 2 + (s + 1) * LANES])
        ref[pl.ds(lo * N_SUB + s, n, stride=N_SUB), :] = hi | (lo_half >> 16)


def _from_tiles(ref, lo, n):
    words = [ref[pl.ds(lo * N_SUB + s, n, stride=N_SUB), :] for s in range(N_SUB)]
    hi = [lax.bitcast_convert_type(w & jnp.uint32(0xFFFF0000), F32) for w in words]
    lo_half = [lax.bitcast_convert_type(w << 16, F32) for w in words]
    return jnp.concatenate(hi + lo_half, axis=1)


def _tables_kernel(idx_ref, rank_ref, cnt_ref, carry):
    @pl.when(pl.program_id(0) == 0)
    def _():
        carry[...] = jnp.zeros_like(carry)

    seq = idx_ref.shape[-1]
    ch = min(TABLE_CHUNK, seq)
    tri = jnp.where(lax.broadcasted_iota(jnp.int32, (ch, ch), 0) <= lax.broadcasted_iota(jnp.int32, (ch, ch), 1),
                    1.0, 0.0).astype(BF16)
    eiota = lax.broadcasted_iota(jnp.int32, (N_EXPERTS, ch), 0)
    cnt = carry[...]
    for k in range(TOP_K):
        for c in range(seq // ch):
            sel = eiota == idx_ref[k:k + 1, c * ch:(c + 1) * ch]
            pref = _dot(jnp.where(sel, 1.0, 0.0).astype(BF16), tri) + cnt
            rank = jnp.sum(jnp.where(sel, pref, 0.0), axis=0, keepdims=True) - 1.0
            rank_ref[k:k + 1, c * ch:(c + 1) * ch] = rank.astype(jnp.int32)
            cnt = pref[:, ch - 1:ch]
    carry[...] = cnt
    cnt_ref[...] = jnp.broadcast_to(cnt, cnt_ref.shape)


def _tables(idx):
    bsz, _, seq = idx.shape
    return pl.pallas_call(
        _tables_kernel,
        grid=(bsz,),
        in_specs=[pl.BlockSpec((None, TOP_K, seq), lambda b: (b, 0, 0))],
        out_specs=[pl.BlockSpec((None, TOP_K, seq), lambda b: (b, 0, 0)),
                   pl.BlockSpec((N_EXPERTS, LANES), lambda b: (0, 0))],
        out_shape=[jax.ShapeDtypeStruct((bsz, TOP_K, seq), jnp.int32),
                   jax.ShapeDtypeStruct((N_EXPERTS, LANES), F32)],
        scratch_shapes=[pltpu.VMEM((N_EXPERTS, 1), F32)],
        compiler_params=_params("arbitrary"),
        name="moe_tables",
    )(idx)


def _zero_runs(step, total, pad_start_ref, pad_len_ref, tail_ref):
    ops = []
    for m in range(-(-2 * N_EXPERTS // total)):
        u = step + m * total
        e = jnp.minimum(u, N_EXPERTS - 1)
        length = jnp.where(u < N_EXPERTS, pad_len_ref[e], 0)
        first = pad_start_ref[e]
        for bit in reversed(range(MOE_TILE.bit_length() - 1)):
            done = lax.shift_left(lax.shift_right_logical(length, bit + 1), bit + 1)
            ops.append((lax.bitwise_and(lax.shift_right_logical(length, bit), 1) == 1, first + done, 1 << bit))
        t = u - N_EXPERTS
        ops.append(((t >= 0) & (t < tail_ref[1]), tail_ref[0] + t * MOE_TILE, MOE_TILE))
    return ops


def _scatter_kernel(total, pad_start_ref, pad_len_ref, tail_ref, dest_ref, src_hbm, dst_hbm,
                    buf, zbuf, sem_in, sem_out, sem_z):
    n = pl.program_id(0) * pl.num_programs(1) + pl.program_id(1)
    tm = buf.shape[1] // N_SUB
    slot = lax.rem(n, 3)

    def load(step, sl):
        return pltpu.make_async_copy(src_hbm.at[pl.ds(step * tm * N_SUB, tm * N_SUB)], buf.at[sl], sem_in.at[sl])

    def drain(sl):
        for _ in range(TOP_K):
            pltpu.make_async_copy(buf.at[sl], dst_hbm.at[pl.ds(0, tm * N_SUB)], sem_out.at[sl]).wait()

    def zero_fill(step, wait):
        for pred, first, rows in _zero_runs(step, total, pad_start_ref, pad_len_ref, tail_ref):
            @pl.when(pred)
            def _(first=first, rows=rows):
                cp = pltpu.make_async_copy(zbuf.at[pl.ds(0, rows * N_SUB)],
                                           dst_hbm.at[pl.ds(pl.multiple_of(first * N_SUB, N_SUB), rows * N_SUB)], sem_z)
                cp.wait() if wait else cp.start()

    @pl.when(n == 0)
    def _():
        zbuf[...] = jnp.zeros_like(zbuf)
        load(0, 0).start()
        if total > 1:
            load(1, 1).start()

    load(n, slot).wait()
    for k in range(TOP_K):
        for c in range(tm // LANES):
            def start(j, carry, k=k, c=c):
                src = buf.at[slot, pl.ds(pl.multiple_of((c * LANES + j) * N_SUB, N_SUB), N_SUB)]
                dst = dst_hbm.at[pl.ds(pl.multiple_of(dest_ref[0, k * tm + c * LANES + j], N_SUB), N_SUB)]
                pltpu.make_async_copy(src, dst, sem_out.at[slot]).start(priority=k)
                return carry
            lax.fori_loop(0, LANES, start, 0, unroll=8)
    zero_fill(n, wait=False)

    @pl.when(n > 0)
    def _():
        drain(lax.rem(n + 2, 3))
        zero_fill(n - 1, wait=True)

    @pl.when(n + 2 < total)
    def _():
        load(n + 2, lax.rem(n + 2, 3)).start()

    @pl.when(n == total - 1)
    def _():
        drain(slot)
        zero_fill(n, wait=True)


def _index_blocks(table, tm):
    bsz, _, seq = table.shape
    nt = seq // tm
    t = table.reshape(bsz, TOP_K, nt, tm).transpose(0, 2, 1, 3)
    return t.reshape(bsz * nt, 1, TOP_K * tm), (None, 1, TOP_K * tm)


def _scatter(dest, pad_start, pad_len, tail, h2t, n_rows):
    bsz, _, seq = dest.shape
    tm = min(ROW_TILE, seq)
    nt = seq // tm
    dest4, dest_block = _index_blocks(dest, tm)
    grid_spec = pltpu.PrefetchScalarGridSpec(
        num_scalar_prefetch=3,
        grid=(bsz, nt),
        in_specs=[
            pl.BlockSpec(dest_block, lambda b, i, *_: (b * nt + i, 0, 0), memory_space=pltpu.SMEM),
            pl.BlockSpec(memory_space=pl.ANY),
        ],
        out_specs=pl.BlockSpec(memory_space=pl.ANY),
        scratch_shapes=[pltpu.VMEM((3, tm * N_SUB, LANES), U32), pltpu.VMEM((MOE_TILE * N_SUB, LANES), U32),
                        pltpu.SemaphoreType.DMA((3,)), pltpu.SemaphoreType.DMA((3,)), pltpu.SemaphoreType.DMA(())],
    )
    return pl.pallas_call(
        functools.partial(_scatter_kernel, bsz * nt),
        grid_spec=grid_spec,
        out_shape=jax.ShapeDtypeStruct((n_rows * N_SUB, LANES), U32),
        compiler_params=_params("arbitrary", "arbitrary"),
        name="moe_scatter",
    )(pad_start, pad_len, tail, dest4, h2t.reshape(bsz * seq * N_SUB, LANES))


def _expert_kernel(blk_exp_ref, blk_first_ref, n_used_ref,
                   xs_ref, wgu_ref, wdn_ref, ys_ref, wgu_bf, wdn_bf):
    i = pl.program_id(0)

    @pl.when(i < n_used_ref[0])
    def _():
        @pl.when(blk_first_ref[i] == 1)
        def _():
            wgu_bf[...] = wgu_ref[...].astype(BF16)
            wdn_bf[...] = wdn_ref[...].astype(BF16)

        x = _from_tiles(xs_ref, 0, MOE_TILE).astype(BF16)
        gu = _dot(x, wgu_bf[...])
        g = gu[:, :D_EXPERT]
        u = gu[:, D_EXPERT:]
        mid = (g * jax.nn.sigmoid(g) * u).astype(BF16)
        _to_tiles(ys_ref, _dot(mid, wdn_bf[...]))

    @pl.when(i >= n_used_ref[0])
    def _():
        ys_ref[...] = jnp.zeros_like(ys_ref)


def _experts(blk_exp, blk_first, n_used, xs, w_gu, w_dn, layer):
    d = D_MODEL
    nb = xs.shape[0] // (MOE_TILE * N_SUB)
    tile = lambda i, *_: (i, 0)
    grid_spec = pltpu.PrefetchScalarGridSpec(
        num_scalar_prefetch=3,
        grid=(nb,),
        in_specs=[
            pl.BlockSpec((MOE_TILE * N_SUB, LANES), tile),
            pl.BlockSpec((None, None, d, 2 * D_EXPERT), lambda i, be, *_: (layer, be[i], 0, 0)),
            pl.BlockSpec((None, None, D_EXPERT, d), lambda i, be, *_: (layer, be[i], 0, 0)),
        ],
        out_specs=pl.BlockSpec((MOE_TILE * N_SUB, LANES), tile),
        scratch_shapes=[pltpu.VMEM((d, 2 * D_EXPERT), BF16), pltpu.VMEM((D_EXPERT, d), BF16)],
    )
    return pl.pallas_call(
        _expert_kernel,
        grid_spec=grid_spec,
        out_shape=jax.ShapeDtypeStruct(xs.shape, U32),
        compiler_params=_params("arbitrary"),
        name="moe_experts",
    )(blk_exp, blk_first, n_used, xs, w_gu, w_dn)


def _combine_kernel(dcur_ref, dnxt_ref, ys_hbm, x_ref, wts_ref, g2_ref, x_out, buf0, buf1, sem):
    nt = pl.num_programs(1)
    n = pl.program_id(0) * nt + pl.program_id(1)
    total = pl.num_programs(0) * nt
    tm = x_ref.shape[0]
    bufs = (buf0, buf1)

    def copy(d_ref, sl, k, r):
        src = ys_hbm.at[pl.ds(pl.multiple_of(d_ref[0, k * tm + r], N_SUB), N_SUB)]
        dst = bufs[sl].at[pl.ds(pl.multiple_of((k * tm + r) * N_SUB, N_SUB), N_SUB)]
        return pltpu.make_async_copy(src, dst, sem.at[sl])

    def drain(sl):
        pltpu.make_async_copy(ys_hbm.at[pl.ds(0, TOP_K * tm * N_SUB)], bufs[sl], sem.at[sl]).wait()

    @pl.when(n == 0)
    def _():
        for k in range(TOP_K):
            def start(r, carry, k=k):
                copy(dcur_ref, 0, k, r).start(priority=k)
                return carry
            lax.fori_loop(0, tm, start, 0, unroll=8)

    def step(sl):
        drain(sl)
        for r in range(tm):
            for k in range(TOP_K):
                copy(dnxt_ref, 1 - sl, k, r).start(priority=k)
        w = wts_ref[...]
        y = w[:, 0:1] * _from_tiles(bufs[sl], 0, tm) + w[:, 1:2] * _from_tiles(bufs[sl], tm, tm)
        x_out[...] = x_ref[...] + g2_ref[...] * y

        @pl.when(n == total - 1)
        def _():
            drain(1 - sl)

    for sl in range(2):
        pl.when(lax.rem(n, 2) == sl)(functools.partial(step, sl))


def _combine(dest_row, ys, x, wts_col, gate2):
    bsz, seq, d = x.shape
    tm = min(ROW_TILE, seq)
    nt = seq // tm

    def nxt(b, i):
        return (jnp.minimum(b * nt + i + 1, bsz * nt - 1), 0, 0)

    dest_row, dest_block = _index_blocks(dest_row, tm)
    return pl.pallas_call(
        _combine_kernel,
        grid=(bsz, nt),
        in_specs=[
            pl.BlockSpec(dest_block, lambda b, i: (b * nt + i, 0, 0), memory_space=pltpu.SMEM),
            pl.BlockSpec(dest_block, nxt, memory_space=pltpu.SMEM),
            pl.BlockSpec(memory_space=pl.ANY),
            pl.BlockSpec((None, tm, d), lambda b, i: (b, i, 0)),
            pl.BlockSpec((None, tm, TOP_K), lambda b, i: (b, i, 0)),
            pl.BlockSpec((None, 1, d), lambda b, i: (b, 0, 0)),
        ],
        out_specs=pl.BlockSpec((None, tm, d), lambda b, i: (b, i, 0)),
        out_shape=jax.ShapeDtypeStruct((bsz, seq, d), F32),
        scratch_shapes=[pltpu.VMEM((TOP_K * tm * N_SUB, LANES), U32), pltpu.VMEM((TOP_K * tm * N_SUB, LANES), U32),
                        pltpu.SemaphoreType.DMA((2,))],
        compiler_params=_params("arbitrary", "arbitrary"),
        name="moe_combine",
    )(dest_row, dest_row, ys, x, wts_col, gate2)


def _lookup(table, keys):
    hit = keys[..., None] == jnp.arange(table.shape[0], dtype=jnp.int32)
    return jnp.sum(jnp.where(hit, table, 0), axis=-1).astype(jnp.int32)


def _count_le(bounds, q):
    return jnp.sum((bounds <= q[..., None]).astype(jnp.int32), axis=-1)


def _moe(h2t, idx, wts, x, gate2, w_gu, w_dn, layer):
    bsz, seq, _ = x.shape
    n_rows = bsz * seq * TOP_K + N_EXPERTS * MOE_TILE
    nb = n_rows // MOE_TILE
    rank, cnt = _tables(idx)
    counts = cnt[:, 0].astype(jnp.int32)
    padded = ((counts + MOE_TILE - 1) // MOE_TILE) * MOE_TILE
    pend = jnp.cumsum(padded)
    pstart = pend - padded
    dest = _lookup(pstart, idx) + rank
    blk_row = jnp.arange(nb, dtype=jnp.int32) * MOE_TILE
    blk_exp = jnp.minimum(_count_le(pend, blk_row), N_EXPERTS - 1)
    blk_first = (blk_row == _lookup(pstart, blk_exp)).astype(jnp.int32)
    n_used = (pend[-1:] // MOE_TILE).astype(jnp.int32)
    tail = jnp.concatenate([pend[-1:], (n_rows - pend[-1:]) // MOE_TILE]).astype(jnp.int32)
    dest_row = dest * N_SUB
    xs = _scatter(dest_row, (pstart + counts).astype(jnp.int32), (padded - counts).astype(jnp.int32), tail, h2t, n_rows)
    ys = _experts(blk_exp, blk_first, n_used, xs, w_gu, w_dn, layer)
    return _combine(dest_row, ys, x, wts.transpose(0, 2, 1), gate2)


def _rnn_in_kernel(x_ref, g_ref, sh_ref, sc_ref, w_ref, gate_out, xb_out):
    h = _modulate(x_ref[...], g_ref[...], sh_ref[...], sc_ref[...])
    u = _dot(h.astype(BF16), w_ref[...])
    gate_out[...] = jax.nn.gelu(u[:, :D_RNN]).astype(BF16)
    xb_out[...] = u[:, D_RNN:]


def _rnn_in(x, gain, shift, scale, w_in):
    bsz, seq, d = x.shape
    tm = min(ROW_TILE, seq)
    row = lambda b, i: (b, i, 0)
    per_b = lambda b, i: (b, 0, 0)
    const = lambda b, i: (0, 0)
    return pl.pallas_call(
        _rnn_in_kernel,
        grid=(bsz, seq // tm),
        in_specs=[
            pl.BlockSpec((None, tm, d), row),
            pl.BlockSpec(gain.shape, const),
            pl.BlockSpec((None, 1, d), per_b),
            pl.BlockSpec((None, 1, d), per_b),
            pl.BlockSpec(w_in.shape, const),
        ],
        out_specs=[pl.BlockSpec((None, tm, D_RNN), row), pl.BlockSpec((None, tm, D_RNN), row)],
        out_shape=[jax.ShapeDtypeStruct((bsz, seq, D_RNN), BF16),
                   jax.ShapeDtypeStruct((bsz, seq, D_RNN), F32)],
        compiler_params=_params("arbitrary", "arbitrary"),
        name="rnn_in",
    )(x, gain, shift, scale, w_in)


def _lru_kernel(xb_ref, cw_ref, cb_ref, wcat_ref, bcat_ref, lam_ref, hs_ref,
                xi_ref, af_ref, bf_ref, ab_ref, bb_ref, hf_ref, hb_ref, sum_ref):
    seq, c = xb_ref.shape
    seg_len = seq // SUBLANES
    n_slab = c // LANES
    n_rows = seg_len * SUBLANES
    halo = (CONV_W // 2) * SUBLANES
    row = lax.broadcasted_iota(jnp.int32, (SUBLANES, LANES), 0)
    for sl in range(n_slab):
        lanes = slice(sl * LANES, (sl + 1) * LANES)
        for g in range(SUBLANES):
            xi_ref[sl, pl.ds(halo + g, seg_len, stride=SUBLANES), :] = xb_ref[g * seg_len:(g + 1) * seg_len, lanes]
        for back in (1, 2):
            prev = xi_ref[sl, halo + (seg_len - back) * SUBLANES:halo + (seg_len - back + 1) * SUBLANES, :]
            xi_ref[sl, halo - back * SUBLANES:halo - (back - 1) * SUBLANES, :] = jnp.where(
                row == 0, 0.0, pltpu.roll(prev, 1, 0))
        nxt = xi_ref[sl, halo:halo + SUBLANES, :]
        xi_ref[sl, halo + n_rows:halo + n_rows + SUBLANES, :] = jnp.where(
            row == SUBLANES - 1, 0.0, pltpu.roll(nxt, SUBLANES - 1, 0))

    cw = cw_ref[...]
    cb = cb_ref[...]
    lam = lam_ref[...]
    neg = -lam
    softplus = jnp.maximum(neg, 0.0) + jnp.log1p(jnp.exp(-jnp.abs(neg)))
    half_rate = (-0.5 * LRU_C) * softplus
    rows = min(SCAN_ROWS, n_rows)
    n_chunks = n_rows // rows

    for ci in range(n_chunks):
        i0 = ci * rows
        taps = []
        for k in range(CONV_W):
            lo = halo + i0 + (k - CONV_W // 2) * SUBLANES
            taps.append(jnp.concatenate([xi_ref[sl, lo:lo + rows, :] for sl in range(n_slab)], axis=1))
        xc = cb
        for k in range(CONV_W):
            xc = xc + taps[k] * cw[k:k + 1, :]
        xcb = xc.astype(BF16)
        xh = 0.5 * xc
        for dirn, (a_ref, b_ref) in enumerate(((af_ref, bf_ref), (ab_ref, bb_ref))):
            cols = slice(2 * dirn * c, 2 * (dirn + 1) * c)
            th = jnp.tanh(_dot(xcb, wcat_ref[:, cols]) + bcat_ref[:, cols])
            hr = half_rate[dirn:dirn + 1, :]
            log_a = hr * th[:, :c] + hr
            a = jnp.exp(log_a)
            m2 = jnp.tanh(-log_a) * (a * a + 1.0)
            mult = jnp.where(m2 > 0.0, m2 * lax.rsqrt(m2), 0.0)
            if dirn == 0 and ci == 0:
                mult = jnp.where(lax.broadcasted_iota(jnp.int32, mult.shape, 0) == 0, 1.0, mult)
            if dirn == 1 and ci == n_chunks - 1:
                mult = jnp.where(lax.broadcasted_iota(jnp.int32, mult.shape, 0) == rows - 1, 1.0, mult)
            b = mult * (th[:, c:] + 1.0) * xh
            for sl in range(n_slab):
                a_ref[sl, i0:i0 + rows, :] = a[:, sl * LANES:(sl + 1) * LANES]
                b_ref[sl, i0:i0 + rows, :] = b[:, sl * LANES:(sl + 1) * LANES]

    def step_rows(cidx):
        fwd = pl.ds(pl.multiple_of(cidx * SUBLANES, SUBLANES), SUBLANES)
        bwd = pl.ds(pl.multiple_of((seg_len - 1 - cidx) * SUBLANES, SUBLANES), SUBLANES)
        return fwd, bwd

    zero = jnp.zeros((SUBLANES, LANES), F32)
    one = jnp.ones((SUBLANES, LANES), F32)

    def totals(cidx, carry):
        fwd, bwd = step_rows(cidx)
        out = []
        for sl in range(n_slab):
            hf, pf, hb, pb = carry[sl]
            af, ab = af_ref[sl, fwd, :], ab_ref[sl, bwd, :]
            out.append((af * hf + bf_ref[sl, fwd, :], af * pf, ab * hb + bb_ref[sl, bwd, :], ab * pb))
        return tuple(out)
    tot = lax.fori_loop(0, seg_len, totals, tuple((zero, one, zero, one) for _ in range(n_slab)), unroll=8)

    enter = []
    for sl in range(n_slab):
        hf, pf, hb, pb = tot[sl]
        cf, cbk = zero, zero
        for _ in range(SUBLANES - 1):
            cf = jnp.where(row == 0, 0.0, pltpu.roll(hf + pf * cf, 1, 0))
            cbk = jnp.where(row == SUBLANES - 1, 0.0, pltpu.roll(hb + pb * cbk, SUBLANES - 1, 0))
        enter.append((cf, cbk))

    def states(meet, cidx, carry):
        fwd, bwd = step_rows(cidx)
        out = []
        for sl in range(n_slab):
            hf, hb = carry[sl]
            hf = af_ref[sl, fwd, :] * hf + bf_ref[sl, fwd, :]
            hb = ab_ref[sl, bwd, :] * hb + bb_ref[sl, bwd, :]
            if meet:
                sum_ref[sl, fwd, :] = hf + hb_ref[sl, fwd, :]
                sum_ref[sl, bwd, :] = hb + hf_ref[sl, bwd, :]
            else:
                hf_ref[sl, fwd, :] = hf
                hb_ref[sl, bwd, :] = hb
            out.append((hf, hb))
        return tuple(out)
    mid = lax.fori_loop(0, seg_len // 2, functools.partial(states, False), tuple(enter), unroll=8)
    lax.fori_loop(seg_len // 2, seg_len, functools.partial(states, True), mid, unroll=8)

    for g in range(SUBLANES):
        for sl in range(n_slab):
            hs_ref[g * seg_len:(g + 1) * seg_len, sl * LANES:(sl + 1) * LANES] = (
                sum_ref[sl, pl.ds(g, seg_len, stride=SUBLANES), :]).astype(BF16)


def _lru(xb, conv_w, conv_b, wcat, bcat, lam):
    bsz, seq, _ = xb.shape
    c = RNN_BW
    blk = lambda b, n: (b, 0, n)
    return pl.pallas_call(
        _lru_kernel,
        grid=(bsz, RNN_BLOCKS),
        in_specs=[
            pl.BlockSpec((None, seq, c), blk),
            pl.BlockSpec((CONV_W, c), lambda b, n: (0, n)),
            pl.BlockSpec((1, c), lambda b, n: (0, n)),
            pl.BlockSpec((None, c, 4 * c), lambda b, n: (n, 0, 0)),
            pl.BlockSpec((None, 1, 4 * c), lambda b, n: (n, 0, 0)),
            pl.BlockSpec((2, c), lambda b, n: (0, n)),
        ],
        out_specs=pl.BlockSpec((None, seq, c), blk),
        out_shape=jax.ShapeDtypeStruct((bsz, seq, D_RNN), BF16),
        scratch_shapes=[pltpu.VMEM((c // LANES, seq + (CONV_W - 1) * SUBLANES, LANES), F32)]
        + [pltpu.VMEM((c // LANES, seq, LANES), F32)] * 7,
        compiler_params=_params("arbitrary", "arbitrary"),
        name="rglru_scan",
    )(xb, conv_w, conv_b, wcat, bcat, lam)


def _mla_weights(w_in, w_q_b, w_kv_b, q_norm, k_norm):
    half = QK_ROPE // 2

    def slab(t):
        return jnp.pad(t, [(0, 0)] * (t.ndim - 1) + [(0, LANES - QK_HEAD)])

    def rot_slab(t):
        rope = t[..., QK_NOPE:]
        swapped = jnp.concatenate([jnp.zeros_like(t[..., :QK_NOPE]), rope[..., half:], rope[..., :half]], axis=-1)
        return slab(swapped)

    kpe = jnp.pad(w_in[:, Q_LORA + KV_LORA:], ((0, 0), (QK_NOPE, 0)))
    w_in_p = jnp.concatenate([w_in[:, :Q_LORA + KV_LORA], slab(kpe), rot_slab(kpe)], axis=1).astype(BF16)
    wq = w_q_b.reshape(Q_LORA, N_HEADS, QK_HEAD)
    wq_p = slab(wq).reshape(Q_LORA, N_HEADS * LANES).astype(BF16)
    wq_rot = rot_slab(wq).reshape(Q_LORA, N_HEADS * LANES).astype(BF16)
    wkv = w_kv_b.reshape(KV_LORA, N_HEADS, QK_NOPE + V_HEAD)
    wk = jnp.pad(wkv[:, :, :QK_NOPE], ((0, 0), (0, 0), (0, LANES - QK_NOPE))).reshape(KV_LORA, N_HEADS * LANES)
    wv = wkv[:, :, QK_NOPE:].reshape(KV_LORA, N_HEADS * V_HEAD)
    w_kv_p = jnp.concatenate([wk, wv], axis=1).astype(BF16)
    gains = lambda g: jnp.stack([slab(g), rot_slab(g)], axis=0)
    return w_in_p, wq_p, wq_rot, w_kv_p, gains(q_norm), gains(k_norm)


def _rope_tables(positions):
    half = QK_ROPE // 2
    inv_freq = ROPE_THETA ** (-jnp.arange(half, dtype=F32) / half)
    ang = positions.astype(F32)[:, None, :] * inv_freq[None, :, None]
    cos, sin = jnp.cos(ang), jnp.sin(ang)
    bsz, seq = positions.shape
    fill = lambda value, n: jnp.full((bsz, n, seq), value, F32)
    cos_t = jnp.concatenate([fill(1.0, QK_NOPE), cos, cos, fill(1.0, LANES - QK_HEAD)], axis=1)
    sin_t = jnp.concatenate([fill(0.0, QK_NOPE), -sin, sin, fill(0.0, LANES - QK_HEAD)], axis=1)
    return cos_t.transpose(0, 2, 1), sin_t.transpose(0, 2, 1)


def _router_weights(w_router, router_bias):
    perm = (jnp.arange(N_EXPERTS) % N_GROUPS) * EXPERTS_PER_GROUP + jnp.arange(N_EXPERTS) // N_GROUPS
    w = w_router[:, perm]
    hi = w.astype(BF16)
    lo = (w - hi.astype(F32)).astype(BF16)
    z = jnp.zeros_like(hi)
    wr1 = jnp.concatenate([hi, lo, z, z], axis=1)
    wr2 = jnp.concatenate([z, z, hi, z], axis=1)
    return wr1, wr2, router_bias[perm].reshape(N_EXPERTS, 1).astype(F32)


def kernel(x, c, positions, norm_mix, norm_ffn, w_ada, b_ada, mla_w_in, mla_q_a_norm, mla_kv_a_norm, mla_w_q_b, mla_w_kv_b, mla_q_norm, mla_k_norm, mla_w_o, rnn_w_in, rnn_conv_w, rnn_conv_b, rnn_lam_f, rnn_w_rf, rnn_b_rf, rnn_w_if, rnn_b_if, rnn_lam_b, rnn_w_rb, rnn_b_rb, rnn_w_ib, rnn_b_ib, rnn_w_o, w_router, router_bias, moe_w_gu, moe_w_dn):
    bsz, seq, d = x.shape
    depth = w_ada.shape[0]
    mod = _ada(c, w_ada, b_ada)
    wr1, wr2, rbias = _router_weights(w_router, router_bias)
    cos_t, sin_t = _rope_tables(positions)
    vec = lambda v: v.reshape(1, -1)
    for i in range(depth):
        sh1, sc1, g1, sh2, sc2, g2 = [mod[i, :, k * d:(k + 1) * d].reshape(bsz, 1, d) for k in range(6)]
        j = i // 2
        if i % 2 == 0:
            w_in_p, wq, wq_rot, wkv, qn, kn = _mla_weights(mla_w_in[j], mla_w_q_b[j], mla_w_kv_b[j],
                                                           mla_q_norm[j], mla_k_norm[j])
            q, k, v = _mla_in(x, vec(norm_mix[i]), sh1, sc1, w_in_p, vec(mla_q_a_norm[j]),
                              vec(mla_kv_a_norm[j]), wq, wq_rot, wkv, qn, kn, cos_t, sin_t)
            a = _attention(q, k, v, mla_q_norm[j], mla_k_norm[j])
            hs = None
            w_o = mla_w_o[j].astype(BF16)
        else:
            a, xb = _rnn_in(x, vec(norm_mix[i]), sh1, sc1, rnn_w_in[j].astype(BF16))
            wcat = (0.5 * jnp.concatenate([rnn_w_rf[j], rnn_w_if[j], rnn_w_rb[j], rnn_w_ib[j]], axis=-1)).astype(BF16)
            bcat = jnp.stack([b.reshape(RNN_BLOCKS, RNN_BW) for b in
                              (rnn_b_rf[j], rnn_b_if[j], rnn_b_rb[j], rnn_b_ib[j])], axis=1)
            bcat = 0.5 * bcat.reshape(RNN_BLOCKS, 1, 4 * RNN_BW)
            lam = jnp.stack([rnn_lam_f[j], rnn_lam_b[j]], axis=0)
            hs = _lru(xb, rnn_conv_w[j], vec(rnn_conv_b[j]), wcat, bcat, lam)
            w_o = rnn_w_o[j].astype(BF16)
        x, h2, idx, wts = _mix_out(a, hs, x, w_o, g1, vec(norm_ffn[i]), sh2, sc2, wr1, wr2, rbias)
        x = _moe(h2, idx, wts, x, g2, moe_w_gu, moe_w_dn, i)
    return x
```

```python
import functools

import jax
import jax.numpy as jnp
from jax import lax
from jax.experimental import pallas as pl
from jax.experimental.pallas import tpu as pltpu

F32 = jnp.float32
BF16 = jnp.bfloat16

D_MODEL = 1024
N_HEADS = 16
Q_LORA = 384
KV_LORA = 256
QK_NOPE = 64
QK_ROPE = 32
QK_HEAD = QK_NOPE + QK_ROPE
V_HEAD = 64
ROPE_THETA = 10000.0
D_RNN = D_MODEL
RNN_BLOCKS = 4
RNN_BW = D_RNN // RNN_BLOCKS
CONV_W = 4
LRU_C = 8.0
N_EXPERTS = 32
N_GROUPS = 8
EXPERTS_PER_GROUP = N_EXPERTS // N_GROUPS
TOP_K = 2
D_EXPERT = 512
EPS = 1e-6
LOG2_E = 1.4426950408889634

LANES = 128
SUBLANES = 8
VMEM_LIMIT = 52 * 1024 * 1024

ROW_TILE = 512
ROW_SUB = 256
Q_TILE = 2048
Q_SUB = 256
MAX_SAFE_SHIFT = 60.0
MOE_TILE = 512
SCAN_ROWS = 256


def _dot(a, b):
    return jnp.dot(a, b, preferred_element_type=F32)


def _split_bf16(a):
    hi = a.astype(BF16)
    lo = (a - hi.astype(F32)).astype(BF16)
    return hi, lo


def _dot_split(a, b):
    ah, al = _split_bf16(a)
    bh, bl = _split_bf16(b)
    return _dot(ah, bh) + (_dot(ah, bl) + _dot(al, bh))


def _rms(x, gain, n):
    ms = jnp.sum(x * x, axis=-1, keepdims=True) * (1.0 / n)
    return x * lax.rsqrt(ms + EPS) * gain


def _modulate(x, gain, shift, scale):
    return _rms(x, gain, x.shape[-1]) * (1.0 + scale) + shift


def _params(*sem):
    return pltpu.CompilerParams(dimension_semantics=sem, vmem_limit_bytes=VMEM_LIMIT)


def _ada_kernel(c_ref, w_ref, b_ref, o_ref):
    c = c_ref[...]
    o_ref[...] = _dot_split(c * jax.nn.sigmoid(c), w_ref[...]) + b_ref[...]


def _ada(c, w_ada, b_ada):
    depth, d, n = w_ada.shape
    bsz = c.shape[0]
    tn = 1536
    return pl.pallas_call(
        _ada_kernel,
        grid=(depth, n // tn),
        in_specs=[
            pl.BlockSpec((bsz, d), lambda l, j: (0, 0)),
            pl.BlockSpec((None, d, tn), lambda l, j: (l, 0, j)),
            pl.BlockSpec((None, 1, tn), lambda l, j: (l, 0, j)),
        ],
        out_specs=pl.BlockSpec((None, bsz, tn), lambda l, j: (l, 0, j)),
        out_shape=jax.ShapeDtypeStruct((depth, bsz, n), F32),
        compiler_params=_params("arbitrary", "arbitrary"),
        name="adaln_mod",
    )(c, w_ada, b_ada.reshape(depth, 1, n))


def _head_scale(s):
    return lax.rsqrt(jnp.sum(s * s, axis=-1, keepdims=True) * (1.0 / QK_HEAD) + EPS)


def _mla_in_kernel(x_ref, g_ref, sh_ref, sc_ref, win_ref, qan_ref, kvan_ref, wq_ref, wqr_ref, wkv_ref,
                   qn_ref, kn_ref, cos_ref, sin_ref, q_out, k_out, v_out):
    h = _modulate(x_ref[...], g_ref[...], sh_ref[...], sc_ref[...])
    lat = _dot(h.astype(BF16), win_ref[...])
    q_lat = lat[:, :Q_LORA]
    kv_lat = lat[:, Q_LORA:Q_LORA + KV_LORA]
    kpe = lat[:, Q_LORA + KV_LORA:Q_LORA + KV_LORA + LANES]
    kpe_rot = lat[:, Q_LORA + KV_LORA + LANES:]
    qn = _rms(q_lat, qan_ref[...], Q_LORA).astype(BF16)
    q_all = _dot(qn, wq_ref[...])
    q_rot = _dot(qn, wqr_ref[...])
    kv_all = _dot(_rms(kv_lat, kvan_ref[...], KV_LORA).astype(BF16), wkv_ref[...])
    cos_t = cos_ref[...]
    sin_t = sin_ref[...]
    q_scale = LOG2_E * QK_HEAD ** -0.5
    cq = cos_t * (qn_ref[0:1, :] * q_scale)
    sq = sin_t * (qn_ref[1:2, :] * q_scale)
    ck = cos_t * kn_ref[0:1, :]
    k_rot_term = kpe_rot * (sin_t * kn_ref[1:2, :])
    for hh in range(N_HEADS):
        sl = slice(hh * LANES, (hh + 1) * LANES)
        s = q_all[:, sl]
        q_out[hh] = ((s * cq + q_rot[:, sl] * sq) * _head_scale(s)).astype(BF16)
        s = kv_all[:, sl] + kpe
        k_out[hh] = ((s * ck + k_rot_term) * _head_scale(s)).astype(BF16)
    v_out[...] = kv_all[:, N_HEADS * LANES:].astype(BF16)


def _mla_in(x, gain, shift, scale, w_in, q_a_norm, kv_a_norm, w_q, w_q_rot, w_kv, q_norm, k_norm, cos_t, sin_t):
    bsz, seq, d = x.shape
    tm = min(ROW_TILE, seq)
    row = lambda b, i: (b, i, 0)
    per_b = lambda b, i: (b, 0, 0)
    const = lambda b, i: (0, 0)
    full = lambda a: pl.BlockSpec(a.shape, const)
    return pl.pallas_call(
        _mla_in_kernel,
        grid=(bsz, seq // tm),
        in_specs=[
            pl.BlockSpec((None, tm, d), row),
            full(gain),
            pl.BlockSpec((None, 1, d), per_b),
            pl.BlockSpec((None, 1, d), per_b),
            full(w_in), full(q_a_norm), full(kv_a_norm), full(w_q), full(w_q_rot), full(w_kv),
            full(q_norm), full(k_norm),
            pl.BlockSpec((None, tm, LANES), row),
            pl.BlockSpec((None, tm, LANES), row),
        ],
        out_specs=[
            pl.BlockSpec((None, N_HEADS, tm, LANES), lambda b, i: (b, 0, i, 0)),
            pl.BlockSpec((None, N_HEADS, tm, LANES), lambda b, i: (b, 0, i, 0)),
            pl.BlockSpec((None, tm, N_HEADS * V_HEAD), row),
        ],
        out_shape=[
            jax.ShapeDtypeStruct((bsz, N_HEADS, seq, LANES), BF16),
            jax.ShapeDtypeStruct((bsz, N_HEADS, seq, LANES), BF16),
            jax.ShapeDtypeStruct((bsz, seq, N_HEADS * V_HEAD), BF16),
        ],
        compiler_params=_params("arbitrary", "arbitrary"),
        name="mla_in",
    )(x, gain, shift, scale, w_in, q_a_norm, kv_a_norm, w_q, w_q_rot, w_kv, q_norm, k_norm, cos_t, sin_t)


def _attn_kernel(bounded, shift_ref, q_ref, k_ref, v_ref, o_ref):
    v = v_ref[...]
    lane_v = lax.broadcasted_iota(jnp.int32, v.shape, 1)
    v_heads = [jnp.where(lane_v < V_HEAD, v, jnp.ones((), BF16)), jnp.where(lane_v >= V_HEAD, v, jnp.ones((), BF16))]
    lane = lax.broadcasted_iota(jnp.int32, (Q_SUB, LANES), 1)
    for i in range(q_ref.shape[1] // Q_SUB):
        rows = slice(i * Q_SUB, (i + 1) * Q_SUB)
        outs = []
        for j in range(2):
            s = lax.dot_general(q_ref[j, rows, :], k_ref[j], (((1,), (1,)), ((), ())),
                                preferred_element_type=F32)
            m = shift_ref[0] if bounded else jnp.max(s, axis=-1, keepdims=True)
            o = _dot(jnp.exp2(s - m).astype(BF16), v_heads[j])
            denom = o[:, V_HEAD:V_HEAD + 1] if j == 0 else o[:, 0:1]
            outs.append(o / denom)
        o_ref[rows, :] = jnp.where(lane < V_HEAD, outs[0], outs[1]).astype(BF16)


def _attention_call(bounded, shift, q, k, v):
    bsz, _, seq, _ = q.shape
    tq = min(Q_TILE, seq)
    assert tq % Q_SUB == 0
    grid_spec = pltpu.PrefetchScalarGridSpec(
        num_scalar_prefetch=1,
        grid=(bsz, N_HEADS // 2, seq // tq),
        in_specs=[
            pl.BlockSpec((None, 2, tq, LANES), lambda b, h, i, *_: (b, h, i, 0)),
            pl.BlockSpec((None, 2, seq, LANES), lambda b, h, i, *_: (b, h, 0, 0)),
            pl.BlockSpec((None, seq, LANES), lambda b, h, i, *_: (b, 0, h)),
        ],
        out_specs=pl.BlockSpec((None, tq, LANES), lambda b, h, i, *_: (b, i, h)),
    )
    return pl.pallas_call(
        functools.partial(_attn_kernel, bounded),
        grid_spec=grid_spec,
        out_shape=jax.ShapeDtypeStruct((bsz, seq, N_HEADS * V_HEAD), BF16),
        compiler_params=_params("arbitrary", "arbitrary", "arbitrary"),
        name="mla_attention",
    )(shift, q, k, v)


def _attention(q, k, v, q_gain, k_gain):
    score_bound = 1.02 * LOG2_E * QK_HEAD ** 0.5 * jnp.max(jnp.abs(q_gain)) * jnp.max(jnp.abs(k_gain))
    shift = score_bound.reshape(1).astype(F32)
    return lax.cond(score_bound <= MAX_SAFE_SHIFT,
                    functools.partial(_attention_call, True), functools.partial(_attention_call, False),
                    shift, q, k, v)


def _first_index_of_max(vals):
    m = vals[0]
    for v in vals[1:]:
        m = jnp.maximum(m, v)
    idx = jnp.full(m.shape, float(len(vals) - 1), F32)
    for j in range(len(vals) - 2, -1, -1):
        idx = jnp.where(vals[j] == m, float(j), idx)
    return m, idx


def _route(h2, wr1_ref, wr2_ref, rb_ref):
    hh, hl = _split_bf16(h2)
    logits = (_dot(hh, wr1_ref[...]) + _dot(hl, wr2_ref[...])).T
    logit = logits[0:N_EXPERTS] + logits[N_EXPERTS:2 * N_EXPERTS] + logits[2 * N_EXPERTS:3 * N_EXPERTS]
    score = jax.nn.sigmoid(logit)
    biased = score + rb_ref[...]
    a = [biased[j * N_GROUPS:(j + 1) * N_GROUPS] for j in range(EXPERTS_PER_GROUP)]
    sc = [score[j * N_GROUPS:(j + 1) * N_GROUPS] for j in range(EXPERTS_PER_GROUP)]
    hi1, lo1 = jnp.maximum(a[0], a[1]), jnp.minimum(a[0], a[1])
    hi2, lo2 = jnp.maximum(a[2], a[3]), jnp.minimum(a[2], a[3])
    gscore = jnp.maximum(hi1, hi2) + jnp.maximum(jnp.minimum(hi1, hi2), jnp.maximum(lo1, lo2))
    gmax = jnp.max(gscore, axis=0, keepdims=True)
    giota = lax.broadcasted_iota(jnp.int32, gscore.shape, 0).astype(F32)
    gsel = jnp.min(jnp.where(gscore == gmax, giota, float(N_GROUPS)), axis=0, keepdims=True)
    onehot = giota == gsel
    pick = lambda t: jnp.sum(jnp.where(onehot, t, 0.0), axis=0, keepdims=True)
    bj = [pick(t) for t in a]
    sj = [pick(t) for t in sc]
    _, i1 = _first_index_of_max(bj)
    bj2 = [jnp.where(i1 == float(j), -jnp.inf, bj[j]) for j in range(EXPERTS_PER_GROUP)]
    _, i2 = _first_index_of_max(bj2)
    sel = lambda i: jnp.where(i == 0.0, sj[0], jnp.where(i == 1.0, sj[1], jnp.where(i == 2.0, sj[2], sj[3])))
    w1, w2 = sel(i1), sel(i2)
    den = w1 + w2
    base = gsel * float(EXPERTS_PER_GROUP)
    return ((base + i1).astype(jnp.int32), (base + i2).astype(jnp.int32)), (w1 / den, w2 / den)


def _mix_out_kernel(has_gate, *refs):
    if has_gate:
        a_ref, hs_ref, x_ref, wo_ref, g1_ref, g_ref, sh_ref, sc_ref, wr1_ref, wr2_ref, rb_ref, \
            x_out, h_out, idx_out, wts_out = refs
    else:
        a_ref, x_ref, wo_ref, g1_ref, g_ref, sh_ref, sc_ref, wr1_ref, wr2_ref, rb_ref, \
            x_out, h_out, idx_out, wts_out = refs
    tm = x_ref.shape[0]
    sub = min(ROW_SUB, tm)
    for i in range(tm // sub):
        rows = slice(i * sub, (i + 1) * sub)
        if has_gate:
            a = (a_ref[rows, :].astype(F32) * hs_ref[rows, :]).astype(BF16)
        else:
            a = a_ref[rows, :]
        x1 = x_ref[rows, :] + g1_ref[...] * _dot(a, wo_ref[...])
        x_out[rows, :] = x1
        h2 = _modulate(x1, g_ref[...], sh_ref[...], sc_ref[...])
        _to_tiles(h_out, h2, i * sub)
        idx, wts = _route(h2, wr1_ref, wr2_ref, rb_ref)
        for k in range(TOP_K):
            idx_out[k:k + 1, rows] = idx[k]
            wts_out[k:k + 1, rows] = wts[k]


def _mix_out(a, hs, x, w_o, gate1, gain, shift, scale, wr1, wr2, rbias):
    bsz, seq, d = x.shape
    tm = min(ROW_TILE, seq)
    row = lambda b, i: (b, i, 0)
    per_b = lambda b, i: (b, 0, 0)
    const = lambda b, i: (0, 0)
    full = lambda t: pl.BlockSpec(t.shape, const)
    vec = pl.BlockSpec((None, 1, d), per_b)
    acts = [a] if hs is None else [a, hs]
    return pl.pallas_call(
        functools.partial(_mix_out_kernel, hs is not None),
        grid=(bsz, seq // tm),
        in_specs=[pl.BlockSpec((None, tm, t.shape[-1]), row) for t in acts] + [
            pl.BlockSpec((None, tm, d), row), full(w_o), vec, full(gain), vec, vec,
            full(wr1), full(wr2), full(rbias),
        ],
        out_specs=[
            pl.BlockSpec((None, tm, d), row),
            pl.BlockSpec((None, tm * N_SUB, LANES), row),
            pl.BlockSpec((None, TOP_K, tm), lambda b, i: (b, 0, i)),
            pl.BlockSpec((None, TOP_K, tm), lambda b, i: (b, 0, i)),
        ],
        out_shape=[
            jax.ShapeDtypeStruct((bsz, seq, d), F32),
            jax.ShapeDtypeStruct((bsz, seq * N_SUB, LANES), U32),
            jax.ShapeDtypeStruct((bsz, TOP_K, seq), jnp.int32),
            jax.ShapeDtypeStruct((bsz, TOP_K, seq), F32),
        ],
        compiler_params=_params("arbitrary", "arbitrary"),
        name="mix_out_route",
    )(*acts, x, w_o, gate1, gain, shift, scale, wr1, wr2, rbias)


N_SUB = D_MODEL // (2 * LANES)
U32 = jnp.uint32
TABLE_CHUNK = 512


def _to_tiles(ref, val, lo=0):
    n, d = val.shape
    bits = lambda t: lax.bitcast_convert_type(t.astype(BF16).astype(F32), U32)
    for s in range(N_SUB):
        hi = bits(val[:, s * LANES:(s + 1) * LANES])
        lo_half = bits(val[:, d // 2 + s * LANES:d // 2 + (s + 1) * LANES])
        ref[pl.ds(lo * N_SUB + s, n, stride=N_SUB), :] = hi | (lo_half >> 16)


def _from_tiles(ref, lo, n):
    words = [ref[pl.ds(lo * N_SUB + s, n, stride=N_SUB), :] for s in range(N_SUB)]
    hi = [lax.bitcast_convert_type(w & jnp.uint32(0xFFFF0000), F32) for w in words]
    lo_half = [lax.bitcast_convert_type(w << 16, F32) for w in words]
    return jnp.concatenate(hi + lo_half, axis=1)


def _tables_kernel(idx_ref, rank_ref, cnt_ref, carry):
    @pl.when(pl.program_id(0) == 0)
    def _():
        carry[...] = jnp.zeros_like(carry)

    seq = idx_ref.shape[-1]
    ch = min(TABLE_CHUNK, seq)
    tri = jnp.where(lax.broadcasted_iota(jnp.int32, (ch, ch), 0) <= lax.broadcasted_iota(jnp.int32, (ch, ch), 1),
                    1.0, 0.0).astype(BF16)
    eiota = lax.broadcasted_iota(jnp.int32, (N_EXPERTS, ch), 0)
    cnt = carry[...]
    for k in range(TOP_K):
        for c in range(seq // ch):
            sel = eiota == idx_ref[k:k + 1, c * ch:(c + 1) * ch]
            pref = _dot(jnp.where(sel, 1.0, 0.0).astype(BF16), tri) + cnt
            rank = jnp.sum(jnp.where(sel, pref, 0.0), axis=0, keepdims=True) - 1.0
            rank_ref[k:k + 1, c * ch:(c + 1) * ch] = rank.astype(jnp.int32)
            cnt = pref[:, ch - 1:ch]
    carry[...] = cnt
    cnt_ref[...] = jnp.broadcast_to(cnt, cnt_ref.shape)


def _tables(idx):
    bsz, _, seq = idx.shape
    return pl.pallas_call(
        _tables_kernel,
        grid=(bsz,),
        in_specs=[pl.BlockSpec((None, TOP_K, seq), lambda b: (b, 0, 0))],
        out_specs=[pl.BlockSpec((None, TOP_K, seq), lambda b: (b, 0, 0)),
                   pl.BlockSpec((N_EXPERTS, LANES), lambda b: (0, 0))],
        out_shape=[jax.ShapeDtypeStruct((bsz, TOP_K, seq), jnp.int32),
                   jax.ShapeDtypeStruct((N_EXPERTS, LANES), F32)],
        scratch_shapes=[pltpu.VMEM((N_EXPERTS, 1), F32)],
        compiler_params=_params("arbitrary"),
        name="moe_tables",
    )(idx)


def _zero_runs(step, total, pad_start_ref, pad_len_ref, tail_ref):
    ops = []
    for m in range(-(-2 * N_EXPERTS // total)):
        u = step + m * total
        e = jnp.minimum(u, N_EXPERTS - 1)
        length = jnp.where(u < N_EXPERTS, pad_len_ref[e], 0)
        first = pad_start_ref[e]
        for bit in reversed(range(MOE_TILE.bit_length() - 1)):
            done = lax.shift_left(lax.shift_right_logical(length, bit + 1), bit + 1)
            ops.append((lax.bitwise_and(lax.shift_right_logical(length, bit), 1) == 1, first + done, 1 << bit))
        t = u - N_EXPERTS
        ops.append(((t >= 0) & (t < tail_ref[1]), tail_ref[0] + t * MOE_TILE, MOE_TILE))
    return ops


def _scatter_kernel(total, pad_start_ref, pad_len_ref, tail_ref, dest_ref, src_hbm, dst_hbm,
                    buf, zbuf, sem_in, sem_out, sem_z):
    n = pl.program_id(0) * pl.num_programs(1) + pl.program_id(1)
    tm = buf.shape[1] // N_SUB
    slot = lax.rem(n, 3)

    def load(step, sl):
        return pltpu.make_async_copy(src_hbm.at[pl.ds(step * tm * N_SUB, tm * N_SUB)], buf.at[sl], sem_in.at[sl])

    def drain(sl):
        for _ in range(TOP_K):
            pltpu.make_async_copy(buf.at[sl], dst_hbm.at[pl.ds(0, tm * N_SUB)], sem_out.at[sl]).wait()

    def zero_fill(step, wait):
        for pred, first, rows in _zero_runs(step, total, pad_start_ref, pad_len_ref, tail_ref):
            @pl.when(pred)
            def _(first=first, rows=rows):
                cp = pltpu.make_async_copy(zbuf.at[pl.ds(0, rows * N_SUB)],
                                           dst_hbm.at[pl.ds(pl.multiple_of(first * N_SUB, N_SUB), rows * N_SUB)], sem_z)
                cp.wait() if wait else cp.start()

    @pl.when(n == 0)
    def _():
        zbuf[...] = jnp.zeros_like(zbuf)
        load(0, 0).start()
        if total > 1:
            load(1, 1).start()

    load(n, slot).wait()
    for k in range(TOP_K):
        for c in range(tm // LANES):
            def start(j, carry, k=k, c=c):
                src = buf.at[slot, pl.ds(pl.multiple_of((c * LANES + j) * N_SUB, N_SUB), N_SUB)]
                dst = dst_hbm.at[pl.ds(pl.multiple_of(dest_ref[0, k * tm + c * LANES + j], N_SUB), N_SUB)]
                pltpu.make_async_copy(src, dst, sem_out.at[slot]).start(priority=k)
                return carry
            lax.fori_loop(0, LANES, start, 0, unroll=8)
    zero_fill(n, wait=False)

    @pl.when(n > 0)
    def _():
        drain(lax.rem(n + 2, 3))
        zero_fill(n - 1, wait=True)

    @pl.when(n + 2 < total)
    def _():
        load(n + 2, lax.rem(n + 2, 3)).start()

    @pl.when(n == total - 1)
    def _():
        drain(slot)
        zero_fill(n, wait=True)


def _index_blocks(table, tm):
    bsz, _, seq = table.shape
    nt = seq // tm
    t = table.reshape(bsz, TOP_K, nt, tm).transpose(0, 2, 1, 3)
    return t.reshape(bsz * nt, 1, TOP_K * tm), (None, 1, TOP_K * tm)


def _scatter(dest, pad_start, pad_len, tail, h2t, n_rows):
    bsz, _, seq = dest.shape
    tm = min(ROW_TILE, seq)
    nt = seq // tm
    dest4, dest_block = _index_blocks(dest, tm)
    grid_spec = pltpu.PrefetchScalarGridSpec(
        num_scalar_prefetch=3,
        grid=(bsz, nt),
        in_specs=[
            pl.BlockSpec(dest_block, lambda b, i, *_: (b * nt + i, 0, 0), memory_space=pltpu.SMEM),
            pl.BlockSpec(memory_space=pl.ANY),
        ],
        out_specs=pl.BlockSpec(memory_space=pl.ANY),
        scratch_shapes=[pltpu.VMEM((3, tm * N_SUB, LANES), U32), pltpu.VMEM((MOE_TILE * N_SUB, LANES), U32),
                        pltpu.SemaphoreType.DMA((3,)), pltpu.SemaphoreType.DMA((3,)), pltpu.SemaphoreType.DMA(())],
    )
    return pl.pallas_call(
        functools.partial(_scatter_kernel, bsz * nt),
        grid_spec=grid_spec,
        out_shape=jax.ShapeDtypeStruct((n_rows * N_SUB, LANES), U32),
        compiler_params=_params("arbitrary", "arbitrary"),
        name="moe_scatter",
    )(pad_start, pad_len, tail, dest4, h2t.reshape(bsz * seq * N_SUB, LANES))


def _expert_kernel(blk_exp_ref, blk_first_ref, n_used_ref,
                   xs_ref, wgu_ref, wdn_ref, ys_ref, wgu_bf, wdn_bf):
    i = pl.program_id(0)

    @pl.when(i < n_used_ref[0])
    def _():
        @pl.when(blk_first_ref[i] == 1)
        def _():
            wgu_bf[...] = wgu_ref[...].astype(BF16)
            wdn_bf[...] = wdn_ref[...].astype(BF16)

        x = _from_tiles(xs_ref, 0, MOE_TILE).astype(BF16)
        gu = _dot(x, wgu_bf[...])
        g = gu[:, :D_EXPERT]
        u = gu[:, D_EXPERT:]
        mid = (g * jax.nn.sigmoid(g) * u).astype(BF16)
        _to_tiles(ys_ref, _dot(mid, wdn_bf[...]))

    @pl.when(i >= n_used_ref[0])
    def _():
        ys_ref[...] = jnp.zeros_like(ys_ref)


def _experts(blk_exp, blk_first, n_used, xs, w_gu, w_dn, layer):
    d = D_MODEL
    nb = xs.shape[0] // (MOE_TILE * N_SUB)
    tile = lambda i, *_: (i, 0)
    grid_spec = pltpu.PrefetchScalarGridSpec(
        num_scalar_prefetch=3,
        grid=(nb,),
        in_specs=[
            pl.BlockSpec((MOE_TILE * N_SUB, LANES), tile),
            pl.BlockSpec((None, None, d, 2 * D_EXPERT), lambda i, be, *_: (layer, be[i], 0, 0)),
            pl.BlockSpec((None, None, D_EXPERT, d), lambda i, be, *_: (layer, be[i], 0, 0)),
        ],
        out_specs=pl.BlockSpec((MOE_TILE * N_SUB, LANES), tile),
        scratch_shapes=[pltpu.VMEM((d, 2 * D_EXPERT), BF16), pltpu.VMEM((D_EXPERT, d), BF16)],
    )
    return pl.pallas_call(
        _expert_kernel,
        grid_spec=grid_spec,
        out_shape=jax.ShapeDtypeStruct(xs.shape, U32),
        compiler_params=_params("arbitrary"),
        name="moe_experts",
    )(blk_exp, blk_first, n_used, xs, w_gu, w_dn)


def _combine_kernel(dcur_ref, dnxt_ref, ys_hbm, x_ref, wts_ref, g2_ref, x_out, buf0, buf1, sem):
    nt = pl.num_programs(1)
    n = pl.program_id(0) * nt + pl.program_id(1)
    total = pl.num_programs(0) * nt
    tm = x_ref.shape[0]
    bufs = (buf0, buf1)

    def copy(d_ref, sl, k, r):
        src = ys_hbm.at[pl.ds(pl.multiple_of(d_ref[0, k * tm + r], N_SUB), N_SUB)]
        dst = bufs[sl].at[pl.ds(pl.multiple_of((k * tm + r) * N_SUB, N_SUB), N_SUB)]
        return pltpu.make_async_copy(src, dst, sem.at[sl])

    def drain(sl):
        pltpu.make_async_copy(ys_hbm.at[pl.ds(0, TOP_K * tm * N_SUB)], bufs[sl], sem.at[sl]).wait()

    @pl.when(n == 0)
    def _():
        for k in range(TOP_K):
            def start(r, carry, k=k):
                copy(dcur_ref, 0, k, r).start(priority=k)
                return carry
            lax.fori_loop(0, tm, start, 0, unroll=8)

    def step(sl):
        drain(sl)
        for r in range(tm):
            for k in range(TOP_K):
                copy(dnxt_ref, 1 - sl, k, r).start(priority=k)
        w = wts_ref[...]
        y = w[:, 0:1] * _from_tiles(bufs[sl], 0, tm) + w[:, 1:2] * _from_tiles(bufs[sl], tm, tm)
        x_out[...] = x_ref[...] + g2_ref[...] * y

        @pl.when(n == total - 1)
        def _():
            drain(1 - sl)

    for sl in range(2):
        pl.when(lax.rem(n, 2) == sl)(functools.partial(step, sl))


def _combine(dest_row, ys, x, wts_col, gate2):
    bsz, seq, d = x.shape
    tm = min(ROW_TILE, seq)
    nt = seq // tm

    def nxt(b, i):
        return (jnp.minimum(b * nt + i + 1, bsz * nt - 1), 0, 0)

    dest_row, dest_block = _index_blocks(dest_row, tm)
    return pl.pallas_call(
        _combine_kernel,
        grid=(bsz, nt),
        in_specs=[
            pl.BlockSpec(dest_block, lambda b, i: (b * nt + i, 0, 0), memory_space=pltpu.SMEM),
            pl.BlockSpec(dest_block, nxt, memory_space=pltpu.SMEM),
            pl.BlockSpec(memory_space=pl.ANY),
            pl.BlockSpec((None, tm, d), lambda b, i: (b, i, 0)),
            pl.BlockSpec((None, tm, TOP_K), lambda b, i: (b, i, 0)),
            pl.BlockSpec((None, 1, d), lambda b, i: (b, 0, 0)),
        ],
        out_specs=pl.BlockSpec((None, tm, d), lambda b, i: (b, i, 0)),
        out_shape=jax.ShapeDtypeStruct((bsz, seq, d), F32),
        scratch_shapes=[pltpu.VMEM((TOP_K * tm * N_SUB, LANES), U32), pltpu.VMEM((TOP_K * tm * N_SUB, LANES), U32),
                        pltpu.SemaphoreType.DMA((2,))],
        compiler_params=_params("arbitrary", "arbitrary"),
        name="moe_combine",
    )(dest_row, dest_row, ys, x, wts_col, gate2)


def _lookup(table, keys):
    hit = keys[..., None] == jnp.arange(table.shape[0], dtype=jnp.int32)
    return jnp.sum(jnp.where(hit, table, 0), axis=-1).astype(jnp.int32)


def _count_le(bounds, q):
    return jnp.sum((bounds <= q[..., None]).astype(jnp.int32), axis=-1)


def _moe(h2t, idx, wts, x, gate2, w_gu, w_dn, layer):
    bsz, seq, _ = x.shape
    n_rows = bsz * seq * TOP_K + N_EXPERTS * MOE_TILE
    nb = n_rows // MOE_TILE
    rank, cnt = _tables(idx)
    counts = cnt[:, 0].astype(jnp.int32)
    padded = ((counts + MOE_TILE - 1) // MOE_TILE) * MOE_TILE
    pend = jnp.cumsum(padded)
    pstart = pend - padded
    dest = _lookup(pstart, idx) + rank
    blk_row = jnp.arange(nb, dtype=jnp.int32) * MOE_TILE
    blk_exp = jnp.minimum(_count_le(pend, blk_row), N_EXPERTS - 1)
    blk_first = (blk_row == _lookup(pstart, blk_exp)).astype(jnp.int32)
    n_used = (pend[-1:] // MOE_TILE).astype(jnp.int32)
    tail = jnp.concatenate([pend[-1:], (n_rows - pend[-1:]) // MOE_TILE]).astype(jnp.int32)
    dest_row = dest * N_SUB
    xs = _scatter(dest_row, (pstart + counts).astype(jnp.int32), (padded - counts).astype(jnp.int32), tail, h2t, n_rows)
    ys = _experts(blk_exp, blk_first, n_used, xs, w_gu, w_dn, layer)
    return _combine(dest_row, ys, x, wts.transpose(0, 2, 1), gate2)


def _rnn_in_kernel(x_ref, g_ref, sh_ref, sc_ref, w_ref, gate_out, xb_out):
    h = _modulate(x_ref[...], g_ref[...], sh_ref[...], sc_ref[...])
    u = _dot(h.astype(BF16), w_ref[...])
    gate_out[...] = jax.nn.gelu(u[:, :D_RNN]).astype(BF16)
    xb_out[...] = u[:, D_RNN:]


def _rnn_in(x, gain, shift, scale, w_in):
    bsz, seq, d = x.shape
    tm = min(ROW_TILE, seq)
    row = lambda b, i: (b, i, 0)
    per_b = lambda b, i: (b, 0, 0)
    const = lambda b, i: (0, 0)
    return pl.pallas_call(
        _rnn_in_kernel,
        grid=(bsz, seq // tm),
        in_specs=[
            pl.BlockSpec((None, tm, d), row),
            pl.BlockSpec(gain.shape, const),
            pl.BlockSpec((None, 1, d), per_b),
            pl.BlockSpec((None, 1, d), per_b),
            pl.BlockSpec(w_in.shape, const),
        ],
        out_specs=[pl.BlockSpec((None, tm, D_RNN), row), pl.BlockSpec((None, tm, D_RNN), row)],
        out_shape=[jax.ShapeDtypeStruct((bsz, seq, D_RNN), BF16),
                   jax.ShapeDtypeStruct((bsz, seq, D_RNN), F32)],
        compiler_params=_params("arbitrary", "arbitrary"),
        name="rnn_in",
    )(x, gain, shift, scale, w_in)


def _lru_kernel(xb_ref, cw_ref, cb_ref, wcat_ref, bcat_ref, lam_ref, hs_ref,
                xi_ref, af_ref, bf_ref, ab_ref, bb_ref, hf_ref, hb_ref, sum_ref):
    seq, c = xb_ref.shape
    seg_len = seq // SUBLANES
    n_slab = c // LANES
    n_rows = seg_len * SUBLANES
    halo = (CONV_W // 2) * SUBLANES
    row = lax.broadcasted_iota(jnp.int32, (SUBLANES, LANES), 0)
    for sl in range(n_slab):
        lanes = slice(sl * LANES, (sl + 1) * LANES)
        for g in range(SUBLANES):
            xi_ref[sl, pl.ds(halo + g, seg_len, stride=SUBLANES), :] = xb_ref[g * seg_len:(g + 1) * seg_len, lanes]
        for back in (1, 2):
            prev = xi_ref[sl, halo + (seg_len - back) * SUBLANES:halo + (seg_len - back + 1) * SUBLANES, :]
            xi_ref[sl, halo - back * SUBLANES:halo - (back - 1) * SUBLANES, :] = jnp.where(
                row == 0, 0.0, pltpu.roll(prev, 1, 0))
        nxt = xi_ref[sl, halo:halo + SUBLANES, :]
        xi_ref[sl, halo + n_rows:halo + n_rows + SUBLANES, :] = jnp.where(
            row == SUBLANES - 1, 0.0, pltpu.roll(nxt, SUBLANES - 1, 0))

    cw = cw_ref[...]
    cb = cb_ref[...]
    lam = lam_ref[...]
    neg = -lam
    softplus = jnp.maximum(neg, 0.0) + jnp.log1p(jnp.exp(-jnp.abs(neg)))
    half_rate = (-0.5 * LRU_C) * softplus
    rows = min(SCAN_ROWS, n_rows)
    n_chunks = n_rows // rows

    for ci in range(n_chunks):
        i0 = ci * rows
        taps = []
        for k in range(CONV_W):
            lo = halo + i0 + (k - CONV_W // 2) * SUBLANES
            taps.append(jnp.concatenate([xi_ref[sl, lo:lo + rows, :] for sl in range(n_slab)], axis=1))
        xc = cb
        for k in range(CONV_W):
            xc = xc + taps[k] * cw[k:k + 1, :]
        xcb = xc.astype(BF16)
        xh = 0.5 * xc
        for dirn, (a_ref, b_ref) in enumerate(((af_ref, bf_ref), (ab_ref, bb_ref))):
            cols = slice(2 * dirn * c, 2 * (dirn + 1) * c)
            th = jnp.tanh(_dot(xcb, wcat_ref[:, cols]) + bcat_ref[:, cols])
            hr = half_rate[dirn:dirn + 1, :]
            log_a = hr * th[:, :c] + hr
            a = jnp.exp(log_a)
            m2 = jnp.tanh(-log_a) * (a * a + 1.0)
            mult = jnp.where(m2 > 0.0, m2 * lax.rsqrt(m2), 0.0)
            if dirn == 0 and ci == 0:
                mult = jnp.where(lax.broadcasted_iota(jnp.int32, mult.shape, 0) == 0, 1.0, mult)
            if dirn == 1 and ci == n_chunks - 1:
                mult = jnp.where(lax.broadcasted_iota(jnp.int32, mult.shape, 0) == rows - 1, 1.0, mult)
            b = mult * (th[:, c:] + 1.0) * xh
            for sl in range(n_slab):
                a_ref[sl, i0:i0 + rows, :] = a[:, sl * LANES:(sl + 1) * LANES]
                b_ref[sl, i0:i0 + rows, :] = b[:, sl * LANES:(sl + 1) * LANES]

    def step_rows(cidx):
        fwd = pl.ds(pl.multiple_of(cidx * SUBLANES, SUBLANES), SUBLANES)
        bwd = pl.ds(pl.multiple_of((seg_len - 1 - cidx) * SUBLANES, SUBLANES), SUBLANES)
        return fwd, bwd

    zero = jnp.zeros((SUBLANES, LANES), F32)
    one = jnp.ones((SUBLANES, LANES), F32)

    def totals(cidx, carry):
        fwd, bwd = step_rows(cidx)
        out = []
        for sl in range(n_slab):
            hf, pf, hb, pb = carry[sl]
            af, ab = af_ref[sl, fwd, :], ab_ref[sl, bwd, :]
            out.append((af * hf + bf_ref[sl, fwd, :], af * pf, ab * hb + bb_ref[sl, bwd, :], ab * pb))
        return tuple(out)
    tot = lax.fori_loop(0, seg_len, totals, tuple((zero, one, zero, one) for _ in range(n_slab)), unroll=8)

    enter = []
    for sl in range(n_slab):
        hf, pf, hb, pb = tot[sl]
        cf, cbk = zero, zero
        for _ in range(SUBLANES - 1):
            cf = jnp.where(row == 0, 0.0, pltpu.roll(hf + pf * cf, 1, 0))
            cbk = jnp.where(row == SUBLANES - 1, 0.0, pltpu.roll(hb + pb * cbk, SUBLANES - 1, 0))
        enter.append((cf, cbk))

    def states(meet, cidx, carry):
        fwd, bwd = step_rows(cidx)
        out = []
        for sl in range(n_slab):
            hf, hb = carry[sl]
            hf = af_ref[sl, fwd, :] * hf + bf_ref[sl, fwd, :]
            hb = ab_ref[sl, bwd, :] * hb + bb_ref[sl, bwd, :]
            if meet:
                sum_ref[sl, fwd, :] = hf + hb_ref[sl, fwd, :]
                sum_ref[sl, bwd, :] = hb + hf_ref[sl, bwd, :]
            else:
                hf_ref[sl, fwd, :] = hf
                hb_ref[sl, bwd, :] = hb
            out.append((hf, hb))
        return tuple(out)
    mid = lax.fori_loop(0, seg_len // 2, functools.partial(states, False), tuple(enter), unroll=8)
    lax.fori_loop(seg_len // 2, seg_len, functools.partial(states, True), mid, unroll=8)

    for g in range(SUBLANES):
        for sl in range(n_slab):
            hs_ref[g * seg_len:(g + 1) * seg_len, sl * LANES:(sl + 1) * LANES] = (
                sum_ref[sl, pl.ds(g, seg_len, stride=SUBLANES), :]).astype(BF16)


def _lru(xb, conv_w, conv_b, wcat, bcat, lam):
    bsz, seq, _ = xb.shape
    c = RNN_BW
    blk = lambda b, n: (b, 0, n)
    return pl.pallas_call(
        _lru_kernel,
        grid=(bsz, RNN_BLOCKS),
        in_specs=[
            pl.BlockSpec((None, seq, c), blk),
            pl.BlockSpec((CONV_W, c), lambda b, n: (0, n)),
            pl.BlockSpec((1, c), lambda b, n: (0, n)),
            pl.BlockSpec((None, c, 4 * c), lambda b, n: (n, 0, 0)),
            pl.BlockSpec((None, 1, 4 * c), lambda b, n: (n, 0, 0)),
            pl.BlockSpec((2, c), lambda b, n: (0, n)),
        ],
        out_specs=pl.BlockSpec((None, seq, c), blk),
        out_shape=jax.ShapeDtypeStruct((bsz, seq, D_RNN), BF16),
        scratch_shapes=[pltpu.VMEM((c // LANES, seq + (CONV_W - 1) * SUBLANES, LANES), F32)]
        + [pltpu.VMEM((c // LANES, seq, LANES), F32)] * 7,
        compiler_params=_params("arbitrary", "arbitrary"),
        name="rglru_scan",
    )(xb, conv_w, conv_b, wcat, bcat, lam)


def _mla_weights(w_in, w_q_b, w_kv_b, q_norm, k_norm):
    half = QK_ROPE // 2

    def slab(t):
        return jnp.pad(t, [(0, 0)] * (t.ndim - 1) + [(0, LANES - QK_HEAD)])

    def rot_slab(t):
        rope = t[..., QK_NOPE:]
        swapped = jnp.concatenate([jnp.zeros_like(t[..., :QK_NOPE]), rope[..., half:], rope[..., :half]], axis=-1)
        return slab(swapped)

    kpe = jnp.pad(w_in[:, Q_LORA + KV_LORA:], ((0, 0), (QK_NOPE, 0)))
    w_in_p = jnp.concatenate([w_in[:, :Q_LORA + KV_LORA], slab(kpe), rot_slab(kpe)], axis=1).astype(BF16)
    wq = w_q_b.reshape(Q_LORA, N_HEADS, QK_HEAD)
    wq_p = slab(wq).reshape(Q_LORA, N_HEADS * LANES).astype(BF16)
    wq_rot = rot_slab(wq).reshape(Q_LORA, N_HEADS * LANES).astype(BF16)
    wkv = w_kv_b.reshape(KV_LORA, N_HEADS, QK_NOPE + V_HEAD)
    wk = jnp.pad(wkv[:, :, :QK_NOPE], ((0, 0), (0, 0), (0, LANES - QK_NOPE))).reshape(KV_LORA, N_HEADS * LANES)
    wv = wkv[:, :, QK_NOPE:].reshape(KV_LORA, N_HEADS * V_HEAD)
    w_kv_p = jnp.concatenate([wk, wv], axis=1).astype(BF16)
    gains = lambda g: jnp.stack([slab(g), rot_slab(g)], axis=0)
    return w_in_p, wq_p, wq_rot, w_kv_p, gains(q_norm), gains(k_norm)


def _rope_kernel(pos_ref, freq_ref, cos_out, sin_out):
    ang = freq_ref[...] * pos_ref[...].astype(F32)
    cos, sin = jnp.cos(ang), jnp.sin(ang)
    seq = ang.shape[1]
    fill = lambda value, n: jnp.full((n, seq), value, F32)
    cos_out[...] = jnp.concatenate([fill(1.0, QK_NOPE), cos, cos, fill(1.0, LANES - QK_HEAD)], axis=0).T
    sin_out[...] = jnp.concatenate([fill(0.0, QK_NOPE), -sin, sin, fill(0.0, LANES - QK_HEAD)], axis=0).T


def _rope_tables(positions):
    half = QK_ROPE // 2
    inv_freq = ROPE_THETA ** (-jnp.arange(half, dtype=F32) / half)
    bsz, seq = positions.shape
    table = jax.ShapeDtypeStruct((bsz, seq, LANES), F32)
    return pl.pallas_call(
        _rope_kernel,
        grid=(bsz,),
        in_specs=[pl.BlockSpec((None, 1, seq), lambda b: (b, 0, 0)), pl.BlockSpec((half, 1), lambda b: (0, 0))],
        out_specs=[pl.BlockSpec((None, seq, LANES), lambda b: (b, 0, 0))] * 2,
        out_shape=[table, table],
        compiler_params=_params("arbitrary"),
        name="rope_tables",
    )(positions.reshape(bsz, 1, seq), inv_freq.reshape(half, 1))


def _router_weights(w_router, router_bias):
    perm = (jnp.arange(N_EXPERTS) % N_GROUPS) * EXPERTS_PER_GROUP + jnp.arange(N_EXPERTS) // N_GROUPS
    w = w_router[:, perm]
    hi = w.astype(BF16)
    lo = (w - hi.astype(F32)).astype(BF16)
    z = jnp.zeros_like(hi)
    wr1 = jnp.concatenate([hi, lo, z, z], axis=1)
    wr2 = jnp.concatenate([z, z, hi, z], axis=1)
    return wr1, wr2, router_bias[perm].reshape(N_EXPERTS, 1).astype(F32)


def kernel(x, c, positions, norm_mix, norm_ffn, w_ada, b_ada, mla_w_in, mla_q_a_norm, mla_kv_a_norm, mla_w_q_b, mla_w_kv_b, mla_q_norm, mla_k_norm, mla_w_o, rnn_w_in, rnn_conv_w, rnn_conv_b, rnn_lam_f, rnn_w_rf, rnn_b_rf, rnn_w_if, rnn_b_if, rnn_lam_b, rnn_w_rb, rnn_b_rb, rnn_w_ib, rnn_b_ib, rnn_w_o, w_router, router_bias, moe_w_gu, moe_w_dn):
    bsz, seq, d = x.shape
    depth = w_ada.shape[0]
    mod = _ada(c, w_ada, b_ada)
    wr1, wr2, rbias = _router_weights(w_router, router_bias)
    cos_t, sin_t = _rope_tables(positions)
    vec = lambda v: v.reshape(1, -1)
    for i in range(depth):
        sh1, sc1, g1, sh2, sc2, g2 = [mod[i, :, k * d:(k + 1) * d].reshape(bsz, 1, d) for k in range(6)]
        j = i // 2
        if i % 2 == 0:
            w_in_p, wq, wq_rot, wkv, qn, kn = _mla_weights(mla_w_in[j], mla_w_q_b[j], mla_w_kv_b[j],
                                                           mla_q_norm[j], mla_k_norm[j])
            q, k, v = _mla_in(x, vec(norm_mix[i]), sh1, sc1, w_in_p, vec(mla_q_a_norm[j]),
                              vec(mla_kv_a_norm[j]), wq, wq_rot, wkv, qn, kn, cos_t, sin_t)
            a = _attention(q, k, v, mla_q_norm[j], mla_k_norm[j])
            hs = None
            w_o = mla_w_o[j].astype(BF16)
        else:
            a, xb = _rnn_in(x, vec(norm_mix[i]), sh1, sc1, rnn_w_in[j].astype(BF16))
            wcat = (0.5 * jnp.concatenate([rnn_w_rf[j], rnn_w_if[j], rnn_w_rb[j], rnn_w_ib[j]], axis=-1)).astype(BF16)
            bcat = jnp.stack([b.reshape(RNN_BLOCKS, RNN_BW) for b in
                              (rnn_b_rf[j], rnn_b_if[j], rnn_b_rb[j], rnn_b_ib[j])], axis=1)
            bcat = 0.5 * bcat.reshape(RNN_BLOCKS, 1, 4 * RNN_BW)
            lam = jnp.stack([rnn_lam_f[j], rnn_lam_b[j]], axis=0)
            hs = _lru(xb, rnn_conv_w[j], vec(rnn_conv_b[j]), wcat, bcat, lam)
            w_o = rnn_w_o[j].astype(BF16)
        x, h2, idx, wts = _mix_out(a, hs, x, w_o, g1, vec(norm_ffn[i]), sh2, sc2, wr1, wr2, rbias)
        x = _moe(h2, idx, wts, x, g2, moe_w_gu, moe_w_dn, i)
    return x
```

```python
import functools

import jax
import jax.numpy as jnp
from jax import lax
from jax.experimental import pallas as pl
from jax.experimental.pallas import tpu as pltpu

F32 = jnp.float32
BF16 = jnp.bfloat16

D_MODEL = 1024
N_HEADS = 16
Q_LORA = 384
KV_LORA = 256
QK_NOPE = 64
QK_ROPE = 32
QK_HEAD = QK_NOPE + QK_ROPE
V_HEAD = 64
ROPE_THETA = 10000.0
D_RNN = D_MODEL
RNN_BLOCKS = 4
RNN_BW = D_RNN // RNN_BLOCKS
CONV_W = 4
LRU_C = 8.0
N_EXPERTS = 32
N_GROUPS = 8
EXPERTS_PER_GROUP = N_EXPERTS // N_GROUPS
TOP_K = 2
D_EXPERT = 512
EPS = 1e-6
LOG2_E = 1.4426950408889634

LANES = 128
SUBLANES = 8
VMEM_LIMIT = 52 * 1024 * 1024

ROW_TILE = 512
ROW_SUB = 256
Q_TILE = 2048
Q_SUB = 256
MAX_SAFE_SHIFT = 60.0
MOE_TILE = 512
SCAN_ROWS = 256
ADA_TILE = 1536
assert MOE_TILE & (MOE_TILE - 1) == 0


def _dot(a, b):
    return jnp.dot(a, b, preferred_element_type=F32)


def _split_bf16(a):
    hi = a.astype(BF16)
    lo = (a - hi.astype(F32)).astype(BF16)
    return hi, lo


def _dot_split(a, b):
    ah, al = _split_bf16(a)
    bh, bl = _split_bf16(b)
    return _dot(ah, bh) + (_dot(ah, bl) + _dot(al, bh))


def _rms(x, gain, n):
    ms = jnp.sum(x * x, axis=-1, keepdims=True) * (1.0 / n)
    return x * lax.rsqrt(ms + EPS) * gain


def _modulate(x, gain, shift, scale):
    return _rms(x, gain, x.shape[-1]) * (1.0 + scale) + shift


def _params(*sem):
    return pltpu.CompilerParams(dimension_semantics=sem, vmem_limit_bytes=VMEM_LIMIT)


def _ada_kernel(c_ref, w_ref, b_ref, o_ref):
    c = c_ref[...]
    o_ref[...] = _dot_split(c * jax.nn.sigmoid(c), w_ref[...]) + b_ref[...]


def _ada(c, w_ada, b_ada):
    depth, d, n = w_ada.shape
    bsz = c.shape[0]
    tn = min(ADA_TILE, n)
    return pl.pallas_call(
        _ada_kernel,
        grid=(depth, n // tn),
        in_specs=[
            pl.BlockSpec((bsz, d), lambda l, j: (0, 0)),
            pl.BlockSpec((None, d, tn), lambda l, j: (l, 0, j)),
            pl.BlockSpec((None, 1, tn), lambda l, j: (l, 0, j)),
        ],
        out_specs=pl.BlockSpec((None, bsz, tn), lambda l, j: (l, 0, j)),
        out_shape=jax.ShapeDtypeStruct((depth, bsz, n), F32),
        compiler_params=_params("arbitrary", "arbitrary"),
        name="adaln_mod",
    )(c, w_ada, b_ada.reshape(depth, 1, n))


def _head_scale(s):
    return lax.rsqrt(jnp.sum(s * s, axis=-1, keepdims=True) * (1.0 / QK_HEAD) + EPS)


def _mla_in_kernel(x_ref, g_ref, sh_ref, sc_ref, win_ref, qan_ref, kvan_ref, wq_ref, wqr_ref, wkv_ref,
                   qn_ref, kn_ref, cos_ref, sin_ref, q_out, k_out, v_out):
    h = _modulate(x_ref[...], g_ref[...], sh_ref[...], sc_ref[...])
    lat = _dot(h.astype(BF16), win_ref[...])
    q_lat = lat[:, :Q_LORA]
    kv_lat = lat[:, Q_LORA:Q_LORA + KV_LORA]
    kpe = lat[:, Q_LORA + KV_LORA:Q_LORA + KV_LORA + LANES]
    kpe_rot = lat[:, Q_LORA + KV_LORA + LANES:]
    qn = _rms(q_lat, qan_ref[...], Q_LORA).astype(BF16)
    q_all = _dot(qn, wq_ref[...])
    q_rot = _dot(qn, wqr_ref[...])
    kv_all = _dot(_rms(kv_lat, kvan_ref[...], KV_LORA).astype(BF16), wkv_ref[...])
    cos_t = cos_ref[...]
    sin_t = sin_ref[...]
    q_scale = LOG2_E * QK_HEAD ** -0.5
    cq = cos_t * (qn_ref[0:1, :] * q_scale)
    sq = sin_t * (qn_ref[1:2, :] * q_scale)
    ck = cos_t * kn_ref[0:1, :]
    k_rot_term = kpe_rot * (sin_t * kn_ref[1:2, :])
    for hh in range(N_HEADS):
        sl = slice(hh * LANES, (hh + 1) * LANES)
        s = q_all[:, sl]
        q_out[hh] = ((s * cq + q_rot[:, sl] * sq) * _head_scale(s)).astype(BF16)
        s = kv_all[:, sl] + kpe
        k_out[hh] = ((s * ck + k_rot_term) * _head_scale(s)).astype(BF16)
    v_out[...] = kv_all[:, N_HEADS * LANES:].astype(BF16)


def _mla_in(x, gain, shift, scale, w_in, q_a_norm, kv_a_norm, w_q, w_q_rot, w_kv, q_norm, k_norm, cos_t, sin_t):
    bsz, seq, d = x.shape
    tm = min(ROW_TILE, seq)
    row = lambda b, i: (b, i, 0)
    per_b = lambda b, i: (b, 0, 0)
    const = lambda b, i: (0, 0)
    full = lambda a: pl.BlockSpec(a.shape, const)
    return pl.pallas_call(
        _mla_in_kernel,
        grid=(bsz, seq // tm),
        in_specs=[
            pl.BlockSpec((None, tm, d), row),
            full(gain),
            pl.BlockSpec((None, 1, d), per_b),
            pl.BlockSpec((None, 1, d), per_b),
            full(w_in), full(q_a_norm), full(kv_a_norm), full(w_q), full(w_q_rot), full(w_kv),
            full(q_norm), full(k_norm),
            pl.BlockSpec((None, tm, LANES), row),
            pl.BlockSpec((None, tm, LANES), row),
        ],
        out_specs=[
            pl.BlockSpec((None, N_HEADS, tm, LANES), lambda b, i: (b, 0, i, 0)),
            pl.BlockSpec((None, N_HEADS, tm, LANES), lambda b, i: (b, 0, i, 0)),
            pl.BlockSpec((None, tm, N_HEADS * V_HEAD), row),
        ],
        out_shape=[
            jax.ShapeDtypeStruct((bsz, N_HEADS, seq, LANES), BF16),
            jax.ShapeDtypeStruct((bsz, N_HEADS, seq, LANES), BF16),
            jax.ShapeDtypeStruct((bsz, seq, N_HEADS * V_HEAD), BF16),
        ],
        compiler_params=_params("arbitrary", "arbitrary"),
        name="mla_in",
    )(x, gain, shift, scale, w_in, q_a_norm, kv_a_norm, w_q, w_q_rot, w_kv, q_norm, k_norm, cos_t, sin_t)


def _attn_kernel(bounded, shift_ref, q_ref, k_ref, v_ref, o_ref):
    v = v_ref[...]
    lane_v = lax.broadcasted_iota(jnp.int32, v.shape, 1)
    v_heads = [jnp.where(lane_v < V_HEAD, v, jnp.ones((), BF16)), jnp.where(lane_v >= V_HEAD, v, jnp.ones((), BF16))]
    lane = lax.broadcasted_iota(jnp.int32, (Q_SUB, LANES), 1)
    for i in range(q_ref.shape[1] // Q_SUB):
        rows = slice(i * Q_SUB, (i + 1) * Q_SUB)
        outs = []
        for j in range(2):
            s = lax.dot_general(q_ref[j, rows, :], k_ref[j], (((1,), (1,)), ((), ())),
                                preferred_element_type=F32)
            m = shift_ref[0] if bounded else jnp.max(s, axis=-1, keepdims=True)
            o = _dot(jnp.exp2(s - m).astype(BF16), v_heads[j])
            denom = o[:, V_HEAD:V_HEAD + 1] if j == 0 else o[:, 0:1]
            outs.append(o / denom)
        o_ref[rows, :] = jnp.where(lane < V_HEAD, outs[0], outs[1]).astype(BF16)


def _attention_call(bounded, shift, q, k, v):
    bsz, _, seq, _ = q.shape
    tq = min(Q_TILE, seq)
    assert tq % Q_SUB == 0
    grid_spec = pltpu.PrefetchScalarGridSpec(
        num_scalar_prefetch=1,
        grid=(bsz, N_HEADS // 2, seq // tq),
        in_specs=[
            pl.BlockSpec((None, 2, tq, LANES), lambda b, h, i, *_: (b, h, i, 0)),
            pl.BlockSpec((None, 2, seq, LANES), lambda b, h, i, *_: (b, h, 0, 0)),
            pl.BlockSpec((None, seq, LANES), lambda b, h, i, *_: (b, 0, h)),
        ],
        out_specs=pl.BlockSpec((None, tq, LANES), lambda b, h, i, *_: (b, i, h)),
    )
    return pl.pallas_call(
        functools.partial(_attn_kernel, bounded),
        grid_spec=grid_spec,
        out_shape=jax.ShapeDtypeStruct((bsz, seq, N_HEADS * V_HEAD), BF16),
        compiler_params=_params("arbitrary", "arbitrary", "arbitrary"),
        name="mla_attention",
    )(shift, q, k, v)


def _attention(q, k, v, q_gain, k_gain):
    score_bound = 1.02 * LOG2_E * QK_HEAD ** 0.5 * jnp.max(jnp.abs(q_gain)) * jnp.max(jnp.abs(k_gain))
    shift = score_bound.reshape(1).astype(F32)
    return lax.cond(score_bound <= MAX_SAFE_SHIFT,
                    functools.partial(_attention_call, True), functools.partial(_attention_call, False),
                    shift, q, k, v)


def _first_index_of_max(vals):
    m = vals[0]
    for v in vals[1:]:
        m = jnp.maximum(m, v)
    idx = jnp.full(m.shape, float(len(vals) - 1), F32)
    for j in range(len(vals) - 2, -1, -1):
        idx = jnp.where(vals[j] == m, float(j), idx)
    return m, idx


def _route(h2, wr1_ref, wr2_ref, rb_ref):
    hh, hl = _split_bf16(h2)
    logits = (_dot(hh, wr1_ref[...]) + _dot(hl, wr2_ref[...])).T
    logit = logits[0:N_EXPERTS] + logits[N_EXPERTS:2 * N_EXPERTS] + logits[2 * N_EXPERTS:3 * N_EXPERTS]
    score = jax.nn.sigmoid(logit)
    biased = score + rb_ref[...]
    a = [biased[j * N_GROUPS:(j + 1) * N_GROUPS] for j in range(EXPERTS_PER_GROUP)]
    sc = [score[j * N_GROUPS:(j + 1) * N_GROUPS] for j in range(EXPERTS_PER_GROUP)]
    hi1, lo1 = jnp.maximum(a[0], a[1]), jnp.minimum(a[0], a[1])
    hi2, lo2 = jnp.maximum(a[2], a[3]), jnp.minimum(a[2], a[3])
    gscore = jnp.maximum(hi1, hi2) + jnp.maximum(jnp.minimum(hi1, hi2), jnp.maximum(lo1, lo2))
    gmax = jnp.max(gscore, axis=0, keepdims=True)
    giota = lax.broadcasted_iota(jnp.int32, gscore.shape, 0).astype(F32)
    gsel = jnp.min(jnp.where(gscore == gmax, giota, float(N_GROUPS)), axis=0, keepdims=True)
    onehot = giota == gsel
    pick = lambda t: jnp.sum(jnp.where(onehot, t, 0.0), axis=0, keepdims=True)
    bj = [pick(t) for t in a]
    sj = [pick(t) for t in sc]
    _, i1 = _first_index_of_max(bj)
    bj2 = [jnp.where(i1 == float(j), -jnp.inf, bj[j]) for j in range(EXPERTS_PER_GROUP)]
    _, i2 = _first_index_of_max(bj2)
    sel = lambda i: jnp.where(i == 0.0, sj[0], jnp.where(i == 1.0, sj[1], jnp.where(i == 2.0, sj[2], sj[3])))
    w1, w2 = sel(i1), sel(i2)
    den = w1 + w2
    base = gsel * float(EXPERTS_PER_GROUP)
    return ((base + i1).astype(jnp.int32), (base + i2).astype(jnp.int32)), (w1 / den, w2 / den)


def _mix_out_kernel(has_gate, *refs):
    if has_gate:
        a_ref, hs_ref, x_ref, wo_ref, g1_ref, g_ref, sh_ref, sc_ref, wr1_ref, wr2_ref, rb_ref, \
            x_out, h_out, idx_out, wts_out = refs
    else:
        a_ref, x_ref, wo_ref, g1_ref, g_ref, sh_ref, sc_ref, wr1_ref, wr2_ref, rb_ref, \
            x_out, h_out, idx_out, wts_out = refs
    tm = x_ref.shape[0]
    sub = min(ROW_SUB, tm)
    for i in range(tm // sub):
        rows = slice(i * sub, (i + 1) * sub)
        if has_gate:
            a = (a_ref[rows, :].astype(F32) * hs_ref[rows, :]).astype(BF16)
        else:
            a = a_ref[rows, :]
        x1 = x_ref[rows, :] + g1_ref[...] * _dot(a, wo_ref[...])
        x_out[rows, :] = x1
        h2 = _modulate(x1, g_ref[...], sh_ref[...], sc_ref[...])
        _to_tiles(h_out, h2, i * sub)
        idx, wts = _route(h2, wr1_ref, wr2_ref, rb_ref)
        for k in range(TOP_K):
            idx_out[k:k + 1, rows] = idx[k]
            wts_out[k:k + 1, rows] = wts[k]


def _mix_out(a, hs, x, w_o, gate1, gain, shift, scale, wr1, wr2, rbias):
    bsz, seq, d = x.shape
    tm = min(ROW_TILE, seq)
    row = lambda b, i: (b, i, 0)
    per_b = lambda b, i: (b, 0, 0)
    const = lambda b, i: (0, 0)
    full = lambda t: pl.BlockSpec(t.shape, const)
    vec = pl.BlockSpec((None, 1, d), per_b)
    acts = [a] if hs is None else [a, hs]
    return pl.pallas_call(
        functools.partial(_mix_out_kernel, hs is not None),
        grid=(bsz, seq // tm),
        in_specs=[pl.BlockSpec((None, tm, t.shape[-1]), row) for t in acts] + [
            pl.BlockSpec((None, tm, d), row), full(w_o), vec, full(gain), vec, vec,
            full(wr1), full(wr2), full(rbias),
        ],
        out_specs=[
            pl.BlockSpec((None, tm, d), row),
            pl.BlockSpec((None, tm * N_SUB, LANES), row),
            pl.BlockSpec((None, TOP_K, tm), lambda b, i: (b, 0, i)),
            pl.BlockSpec((None, TOP_K, tm), lambda b, i: (b, 0, i)),
        ],
        out_shape=[
            jax.ShapeDtypeStruct((bsz, seq, d), F32),
            jax.ShapeDtypeStruct((bsz, seq * N_SUB, LANES), U32),
            jax.ShapeDtypeStruct((bsz, TOP_K, seq), jnp.int32),
            jax.ShapeDtypeStruct((bsz, TOP_K, seq), F32),
        ],
        compiler_params=_params("arbitrary", "arbitrary"),
        name="mix_out_route",
    )(*acts, x, w_o, gate1, gain, shift, scale, wr1, wr2, rbias)


N_SUB = D_MODEL // (2 * LANES)
U32 = jnp.uint32
TABLE_CHUNK = 512


def _to_tiles(ref, val, lo=0):
    n, d = val.shape
    bits = lambda t: lax.bitcast_convert_type(t.astype(BF16).astype(F32), U32)
    for s in range(N_SUB):
        hi = bits(val[:, s * LANES:(s + 1) * LANES])
        lo_half = bits(val[:, d // 2 + s * LANES:d // 2 + (s + 1) * LANES])
        ref[pl.ds(lo * N_SUB + s, n, stride=N_SUB), :] = hi | (lo_half >> 16)


def _from_tiles(ref, lo, n):
    words = [ref[pl.ds(lo * N_SUB + s, n, stride=N_SUB), :] for s in range(N_SUB)]
    hi = [lax.bitcast_convert_type(w & jnp.uint32(0xFFFF0000), F32) for w in words]
    lo_half = [lax.bitcast_convert_type(w << 16, F32) for w in words]
    return jnp.concatenate(hi + lo_half, axis=1)


def _tables_kernel(idx_ref, rank_ref, cnt_ref, carry):
    @pl.when(pl.program_id(0) == 0)
    def _():
        carry[...] = jnp.zeros_like(carry)

    seq = idx_ref.shape[-1]
    ch = min(TABLE_CHUNK, seq)
    tri = jnp.where(lax.broadcasted_iota(jnp.int32, (ch, ch), 0) <= lax.broadcasted_iota(jnp.int32, (ch, ch), 1),
                    1.0, 0.0).astype(BF16)
    eiota = lax.broadcasted_iota(jnp.int32, (N_EXPERTS, ch), 0)
    cnt = carry[...]
    for k in range(TOP_K):
        for c in range(seq // ch):
            sel = eiota == idx_ref[k:k + 1, c * ch:(c + 1) * ch]
            pref = _dot(jnp.where(sel, 1.0, 0.0).astype(BF16), tri) + cnt
            rank = jnp.sum(jnp.where(sel, pref, 0.0), axis=0, keepdims=True) - 1.0
            rank_ref[k:k + 1, c * ch:(c + 1) * ch] = rank.astype(jnp.int32)
            cnt = pref[:, ch - 1:ch]
    carry[...] = cnt
    cnt_ref[...] = jnp.broadcast_to(cnt, cnt_ref.shape)


def _tables(idx):
    bsz, _, seq = idx.shape
    return pl.pallas_call(
        _tables_kernel,
        grid=(bsz,),
        in_specs=[pl.BlockSpec((None, TOP_K, seq), lambda b: (b, 0, 0))],
        out_specs=[pl.BlockSpec((None, TOP_K, seq), lambda b: (b, 0, 0)),
                   pl.BlockSpec((N_EXPERTS, LANES), lambda b: (0, 0))],
        out_shape=[jax.ShapeDtypeStruct((bsz, TOP_K, seq), jnp.int32),
                   jax.ShapeDtypeStruct((N_EXPERTS, LANES), F32)],
        scratch_shapes=[pltpu.VMEM((N_EXPERTS, 1), F32)],
        compiler_params=_params("arbitrary"),
        name="moe_tables",
    )(idx)


def _zero_runs(step, total, pad_start_ref, pad_len_ref, tail_ref):
    ops = []
    for m in range(-(-2 * N_EXPERTS // total)):
        u = step + m * total
        e = jnp.minimum(u, N_EXPERTS - 1)
        length = jnp.where(u < N_EXPERTS, pad_len_ref[e], 0)
        first = pad_start_ref[e]
        for bit in reversed(range(MOE_TILE.bit_length() - 1)):
            done = lax.shift_left(lax.shift_right_logical(length, bit + 1), bit + 1)
            ops.append((lax.bitwise_and(lax.shift_right_logical(length, bit), 1) == 1, first + done, 1 << bit))
        t = u - N_EXPERTS
        ops.append(((t >= 0) & (t < tail_ref[1]), tail_ref[0] + t * MOE_TILE, MOE_TILE))
    return ops


def _scatter_kernel(total, pad_start_ref, pad_len_ref, tail_ref, dest_ref, src_hbm, dst_hbm,
                    buf, zbuf, sem_in, sem_out, sem_z):
    n = pl.program_id(0) * pl.num_programs(1) + pl.program_id(1)
    tm = buf.shape[1] // N_SUB
    slot = lax.rem(n, 3)

    def load(step, sl):
        return pltpu.make_async_copy(src_hbm.at[pl.ds(step * tm * N_SUB, tm * N_SUB)], buf.at[sl], sem_in.at[sl])

    def drain(sl):
        for _ in range(TOP_K):
            pltpu.make_async_copy(buf.at[sl], dst_hbm.at[pl.ds(0, tm * N_SUB)], sem_out.at[sl]).wait()

    def zero_fill(step, wait):
        for pred, first, rows in _zero_runs(step, total, pad_start_ref, pad_len_ref, tail_ref):
            @pl.when(pred)
            def _(first=first, rows=rows):
                cp = pltpu.make_async_copy(zbuf.at[pl.ds(0, rows * N_SUB)],
                                           dst_hbm.at[pl.ds(pl.multiple_of(first * N_SUB, N_SUB), rows * N_SUB)], sem_z)
                cp.wait() if wait else cp.start()

    @pl.when(n == 0)
    def _():
        zbuf[...] = jnp.zeros_like(zbuf)
        load(0, 0).start()
        if total > 1:
            load(1, 1).start()

    load(n, slot).wait()
    for k in range(TOP_K):
        for c in range(tm // LANES):
            def start(j, carry, k=k, c=c):
                src = buf.at[slot, pl.ds(pl.multiple_of((c * LANES + j) * N_SUB, N_SUB), N_SUB)]
                dst = dst_hbm.at[pl.ds(pl.multiple_of(dest_ref[0, k * tm + c * LANES + j], N_SUB), N_SUB)]
                pltpu.make_async_copy(src, dst, sem_out.at[slot]).start(priority=k)
                return carry
            lax.fori_loop(0, LANES, start, 0, unroll=8)
    zero_fill(n, wait=False)

    @pl.when(n > 0)
    def _():
        drain(lax.rem(n + 2, 3))
        zero_fill(n - 1, wait=True)

    @pl.when(n + 2 < total)
    def _():
        load(n + 2, lax.rem(n + 2, 3)).start()

    @pl.when(n == total - 1)
    def _():
        drain(slot)
        zero_fill(n, wait=True)


def _index_blocks(table, tm):
    bsz, _, seq = table.shape
    nt = seq // tm
    t = table.reshape(bsz, TOP_K, nt, tm).transpose(0, 2, 1, 3)
    return t.reshape(bsz * nt, 1, TOP_K * tm), (None, 1, TOP_K * tm)


def _scatter(dest, pad_start, pad_len, tail, h2t, n_rows):
    bsz, _, seq = dest.shape
    tm = min(ROW_TILE, seq)
    nt = seq // tm
    dest4, dest_block = _index_blocks(dest, tm)
    grid_spec = pltpu.PrefetchScalarGridSpec(
        num_scalar_prefetch=3,
        grid=(bsz, nt),
        in_specs=[
            pl.BlockSpec(dest_block, lambda b, i, *_: (b * nt + i, 0, 0), memory_space=pltpu.SMEM),
            pl.BlockSpec(memory_space=pl.ANY),
        ],
        out_specs=pl.BlockSpec(memory_space=pl.ANY),
        scratch_shapes=[pltpu.VMEM((3, tm * N_SUB, LANES), U32), pltpu.VMEM((MOE_TILE * N_SUB, LANES), U32),
                        pltpu.SemaphoreType.DMA((3,)), pltpu.SemaphoreType.DMA((3,)), pltpu.SemaphoreType.DMA(())],
    )
    return pl.pallas_call(
        functools.partial(_scatter_kernel, bsz * nt),
        grid_spec=grid_spec,
        out_shape=jax.ShapeDtypeStruct((n_rows * N_SUB, LANES), U32),
        compiler_params=_params("arbitrary", "arbitrary"),
        name="moe_scatter",
    )(pad_start, pad_len, tail, dest4, h2t.reshape(bsz * seq * N_SUB, LANES))


def _expert_kernel(blk_exp_ref, blk_first_ref, n_used_ref,
                   xs_ref, wgu_ref, wdn_ref, ys_ref, wgu_bf, wdn_bf):
    i = pl.program_id(0)

    @pl.when(i < n_used_ref[0])
    def _():
        @pl.when(blk_first_ref[i] == 1)
        def _():
            wgu_bf[...] = wgu_ref[...].astype(BF16)
            wdn_bf[...] = wdn_ref[...].astype(BF16)

        x = _from_tiles(xs_ref, 0, MOE_TILE).astype(BF16)
        gu = _dot(x, wgu_bf[...])
        g = gu[:, :D_EXPERT]
        u = gu[:, D_EXPERT:]
        mid = (g * jax.nn.sigmoid(g) * u).astype(BF16)
        _to_tiles(ys_ref, _dot(mid, wdn_bf[...]))

    @pl.when(i >= n_used_ref[0])
    def _():
        ys_ref[...] = jnp.zeros_like(ys_ref)


def _experts(blk_exp, blk_first, n_used, xs, w_gu, w_dn, layer):
    d = D_MODEL
    nb = xs.shape[0] // (MOE_TILE * N_SUB)
    tile = lambda i, *_: (i, 0)
    grid_spec = pltpu.PrefetchScalarGridSpec(
        num_scalar_prefetch=3,
        grid=(nb,),
        in_specs=[
            pl.BlockSpec((MOE_TILE * N_SUB, LANES), tile),
            pl.BlockSpec((None, None, d, 2 * D_EXPERT), lambda i, be, *_: (layer, be[i], 0, 0)),
            pl.BlockSpec((None, None, D_EXPERT, d), lambda i, be, *_: (layer, be[i], 0, 0)),
        ],
        out_specs=pl.BlockSpec((MOE_TILE * N_SUB, LANES), tile),
        scratch_shapes=[pltpu.VMEM((d, 2 * D_EXPERT), BF16), pltpu.VMEM((D_EXPERT, d), BF16)],
    )
    return pl.pallas_call(
        _expert_kernel,
        grid_spec=grid_spec,
        out_shape=jax.ShapeDtypeStruct(xs.shape, U32),
        compiler_params=_params("arbitrary"),
        name="moe_experts",
    )(blk_exp, blk_first, n_used, xs, w_gu, w_dn)


def _combine_kernel(dcur_ref, dnxt_ref, ys_hbm, x_ref, wts_ref, g2_ref, x_out, buf0, buf1, sem):
    nt = pl.num_programs(1)
    n = pl.program_id(0) * nt + pl.program_id(1)
    total = pl.num_programs(0) * nt
    tm = x_ref.shape[0]
    bufs = (buf0, buf1)

    def copy(d_ref, sl, k, r):
        src = ys_hbm.at[pl.ds(pl.multiple_of(d_ref[0, k * tm + r], N_SUB), N_SUB)]
        dst = bufs[sl].at[pl.ds(pl.multiple_of((k * tm + r) * N_SUB, N_SUB), N_SUB)]
        return pltpu.make_async_copy(src, dst, sem.at[sl])

    def drain(sl):
        pltpu.make_async_copy(ys_hbm.at[pl.ds(0, TOP_K * tm * N_SUB)], bufs[sl], sem.at[sl]).wait()

    @pl.when(n == 0)
    def _():
        for k in range(TOP_K):
            def start(r, carry, k=k):
                copy(dcur_ref, 0, k, r).start(priority=k)
                return carry
            lax.fori_loop(0, tm, start, 0, unroll=8)

    def step(sl):
        drain(sl)
        for r in range(tm):
            for k in range(TOP_K):
                copy(dnxt_ref, 1 - sl, k, r).start(priority=k)
        w = wts_ref[...]
        y = w[:, 0:1] * _from_tiles(bufs[sl], 0, tm) + w[:, 1:2] * _from_tiles(bufs[sl], tm, tm)
        x_out[...] = x_ref[...] + g2_ref[...] * y

        @pl.when(n == total - 1)
        def _():
            drain(1 - sl)

    for sl in range(2):
        pl.when(lax.rem(n, 2) == sl)(functools.partial(step, sl))


def _combine(dest_row, ys, x, wts_col, gate2):
    bsz, seq, d = x.shape
    tm = min(ROW_TILE, seq)
    nt = seq // tm

    def nxt(b, i):
        return (jnp.minimum(b * nt + i + 1, bsz * nt - 1), 0, 0)

    dest_row, dest_block = _index_blocks(dest_row, tm)
    return pl.pallas_call(
        _combine_kernel,
        grid=(bsz, nt),
        in_specs=[
            pl.BlockSpec(dest_block, lambda b, i: (b * nt + i, 0, 0), memory_space=pltpu.SMEM),
            pl.BlockSpec(dest_block, nxt, memory_space=pltpu.SMEM),
            pl.BlockSpec(memory_space=pl.ANY),
            pl.BlockSpec((None, tm, d), lambda b, i: (b, i, 0)),
            pl.BlockSpec((None, tm, TOP_K), lambda b, i: (b, i, 0)),
            pl.BlockSpec((None, 1, d), lambda b, i: (b, 0, 0)),
        ],
        out_specs=pl.BlockSpec((None, tm, d), lambda b, i: (b, i, 0)),
        out_shape=jax.ShapeDtypeStruct((bsz, seq, d), F32),
        scratch_shapes=[pltpu.VMEM((TOP_K * tm * N_SUB, LANES), U32), pltpu.VMEM((TOP_K * tm * N_SUB, LANES), U32),
                        pltpu.SemaphoreType.DMA((2,))],
        compiler_params=_params("arbitrary", "arbitrary"),
        name="moe_combine",
    )(dest_row, dest_row, ys, x, wts_col, gate2)


def _lookup(table, keys):
    hit = keys[..., None] == jnp.arange(table.shape[0], dtype=jnp.int32)
    return jnp.sum(jnp.where(hit, table, 0), axis=-1).astype(jnp.int32)


def _count_le(bounds, q):
    return jnp.sum((bounds <= q[..., None]).astype(jnp.int32), axis=-1)


def _moe(h2t, idx, wts, x, gate2, w_gu, w_dn, layer):
    bsz, seq, _ = x.shape
    n_rows = bsz * seq * TOP_K + N_EXPERTS * MOE_TILE
    nb = n_rows // MOE_TILE
    rank, cnt = _tables(idx)
    counts = cnt[:, 0].astype(jnp.int32)
    padded = ((counts + MOE_TILE - 1) // MOE_TILE) * MOE_TILE
    pend = jnp.cumsum(padded)
    pstart = pend - padded
    dest = _lookup(pstart, idx) + rank
    blk_row = jnp.arange(nb, dtype=jnp.int32) * MOE_TILE
    blk_exp = jnp.minimum(_count_le(pend, blk_row), N_EXPERTS - 1)
    blk_first = (blk_row == _lookup(pstart, blk_exp)).astype(jnp.int32)
    n_used = (pend[-1:] // MOE_TILE).astype(jnp.int32)
    tail = jnp.concatenate([pend[-1:], (n_rows - pend[-1:]) // MOE_TILE]).astype(jnp.int32)
    dest_row = dest * N_SUB
    xs = _scatter(dest_row, (pstart + counts).astype(jnp.int32), (padded - counts).astype(jnp.int32), tail, h2t, n_rows)
    ys = _experts(blk_exp, blk_first, n_used, xs, w_gu, w_dn, layer)
    return _combine(dest_row, ys, x, wts.transpose(0, 2, 1), gate2)


def _rnn_in_kernel(x_ref, g_ref, sh_ref, sc_ref, w_ref, gate_out, xb_out):
    h = _modulate(x_ref[...], g_ref[...], sh_ref[...], sc_ref[...])
    u = _dot(h.astype(BF16), w_ref[...])
    gate_out[...] = jax.nn.gelu(u[:, :D_RNN]).astype(BF16)
    xb_out[...] = u[:, D_RNN:]


def _rnn_in(x, gain, shift, scale, w_in):
    bsz, seq, d = x.shape
    tm = min(ROW_TILE, seq)
    row = lambda b, i: (b, i, 0)
    per_b = lambda b, i: (b, 0, 0)
    const = lambda b, i: (0, 0)
    return pl.pallas_call(
        _rnn_in_kernel,
        grid=(bsz, seq // tm),
        in_specs=[
            pl.BlockSpec((None, tm, d), row),
            pl.BlockSpec(gain.shape, const),
            pl.BlockSpec((None, 1, d), per_b),
            pl.BlockSpec((None, 1, d), per_b),
            pl.BlockSpec(w_in.shape, const),
        ],
        out_specs=[pl.BlockSpec((None, tm, D_RNN), row), pl.BlockSpec((None, tm, D_RNN), row)],
        out_shape=[jax.ShapeDtypeStruct((bsz, seq, D_RNN), BF16),
                   jax.ShapeDtypeStruct((bsz, seq, D_RNN), F32)],
        compiler_params=_params("arbitrary", "arbitrary"),
        name="rnn_in",
    )(x, gain, shift, scale, w_in)


def _lru_kernel(xb_ref, cw_ref, cb_ref, wcat_ref, bcat_ref, lam_ref, hs_ref,
                xi_ref, af_ref, bf_ref, ab_ref, bb_ref, hf_ref, hb_ref, sum_ref):
    seq, c = xb_ref.shape
    seg_len = seq // SUBLANES
    n_slab = c // LANES
    n_rows = seg_len * SUBLANES
    halo = (CONV_W // 2) * SUBLANES
    row = lax.broadcasted_iota(jnp.int32, (SUBLANES, LANES), 0)
    for sl in range(n_slab):
        lanes = slice(sl * LANES, (sl + 1) * LANES)
        for g in range(SUBLANES):
            xi_ref[sl, pl.ds(halo + g, seg_len, stride=SUBLANES), :] = xb_ref[g * seg_len:(g + 1) * seg_len, lanes]
        for back in (1, 2):
            prev = xi_ref[sl, halo + (seg_len - back) * SUBLANES:halo + (seg_len - back + 1) * SUBLANES, :]
            xi_ref[sl, halo - back * SUBLANES:halo - (back - 1) * SUBLANES, :] = jnp.where(
                row == 0, 0.0, pltpu.roll(prev, 1, 0))
        nxt = xi_ref[sl, halo:halo + SUBLANES, :]
        xi_ref[sl, halo + n_rows:halo + n_rows + SUBLANES, :] = jnp.where(
            row == SUBLANES - 1, 0.0, pltpu.roll(nxt, SUBLANES - 1, 0))

    cw = cw_ref[...]
    cb = cb_ref[...]
    lam = lam_ref[...]
    neg = -lam
    softplus = jnp.maximum(neg, 0.0) + jnp.log1p(jnp.exp(-jnp.abs(neg)))
    half_rate = (-0.5 * LRU_C) * softplus
    rows = min(SCAN_ROWS, n_rows)
    n_chunks = n_rows // rows

    for ci in range(n_chunks):
        i0 = ci * rows
        taps = []
        for k in range(CONV_W):
            lo = halo + i0 + (k - CONV_W // 2) * SUBLANES
            taps.append(jnp.concatenate([xi_ref[sl, lo:lo + rows, :] for sl in range(n_slab)], axis=1))
        xc = cb
        for k in range(CONV_W):
            xc = xc + taps[k] * cw[k:k + 1, :]
        xcb = xc.astype(BF16)
        xh = 0.5 * xc
        for dirn, (a_ref, b_ref) in enumerate(((af_ref, bf_ref), (ab_ref, bb_ref))):
            cols = slice(2 * dirn * c, 2 * (dirn + 1) * c)
            th = jnp.tanh(_dot(xcb, wcat_ref[:, cols]) + bcat_ref[:, cols])
            hr = half_rate[dirn:dirn + 1, :]
            log_a = hr * th[:, :c] + hr
            a = jnp.exp(log_a)
            m2 = jnp.tanh(log_a) * (-1.0 - a * a)
            mult = jnp.where(m2 > 0.0, m2 * lax.rsqrt(m2), 0.0)
            if dirn == 0 and ci == 0:
                mult = jnp.where(lax.broadcasted_iota(jnp.int32, mult.shape, 0) == 0, 1.0, mult)
            if dirn == 1 and ci == n_chunks - 1:
                mult = jnp.where(lax.broadcasted_iota(jnp.int32, mult.shape, 0) == rows - 1, 1.0, mult)
            b = mult * (th[:, c:] + 1.0) * xh
            for sl in range(n_slab):
                a_ref[sl, i0:i0 + rows, :] = a[:, sl * LANES:(sl + 1) * LANES]
                b_ref[sl, i0:i0 + rows, :] = b[:, sl * LANES:(sl + 1) * LANES]

    def step_rows(cidx):
        fwd = pl.ds(pl.multiple_of(cidx * SUBLANES, SUBLANES), SUBLANES)
        bwd = pl.ds(pl.multiple_of((seg_len - 1 - cidx) * SUBLANES, SUBLANES), SUBLANES)
        return fwd, bwd

    zero = jnp.zeros((SUBLANES, LANES), F32)
    one = jnp.ones((SUBLANES, LANES), F32)

    def totals(cidx, carry):
        fwd, bwd = step_rows(cidx)
        out = []
        for sl in range(n_slab):
            hf, pf, hb, pb = carry[sl]
            af, ab = af_ref[sl, fwd, :], ab_ref[sl, bwd, :]
            out.append((af * hf + bf_ref[sl, fwd, :], af * pf, ab * hb + bb_ref[sl, bwd, :], ab * pb))
        return tuple(out)
    tot = lax.fori_loop(0, seg_len, totals, tuple((zero, one, zero, one) for _ in range(n_slab)), unroll=8)

    enter = []
    for sl in range(n_slab):
        hf, pf, hb, pb = tot[sl]
        cf, cbk = zero, zero
        for _ in range(SUBLANES - 1):
            cf = jnp.where(row == 0, 0.0, pltpu.roll(hf + pf * cf, 1, 0))
            cbk = jnp.where(row == SUBLANES - 1, 0.0, pltpu.roll(hb + pb * cbk, SUBLANES - 1, 0))
        enter.append((cf, cbk))

    def states(meet, cidx, carry):
        fwd, bwd = step_rows(cidx)
        out = []
        for sl in range(n_slab):
            hf, hb = carry[sl]
            hf = af_ref[sl, fwd, :] * hf + bf_ref[sl, fwd, :]
            hb = ab_ref[sl, bwd, :] * hb + bb_ref[sl, bwd, :]
            if meet:
                sum_ref[sl, fwd, :] = hf + hb_ref[sl, fwd, :]
                sum_ref[sl, bwd, :] = hb + hf_ref[sl, bwd, :]
            else:
                hf_ref[sl, fwd, :] = hf
                hb_ref[sl, bwd, :] = hb
            out.append((hf, hb))
        return tuple(out)
    mid = lax.fori_loop(0, seg_len // 2, functools.partial(states, False), tuple(enter), unroll=8)
    lax.fori_loop(seg_len // 2, seg_len, functools.partial(states, True), mid, unroll=8)

    for g in range(SUBLANES):
        for sl in range(n_slab):
            hs_ref[g * seg_len:(g + 1) * seg_len, sl * LANES:(sl + 1) * LANES] = (
                sum_ref[sl, pl.ds(g, seg_len, stride=SUBLANES), :]).astype(BF16)


def _lru(xb, conv_w, conv_b, wcat, bcat, lam):
    bsz, seq, _ = xb.shape
    c = RNN_BW
    blk = lambda b, n: (b, 0, n)
    return pl.pallas_call(
        _lru_kernel,
        grid=(bsz, RNN_BLOCKS),
        in_specs=[
            pl.BlockSpec((None, seq, c), blk),
            pl.BlockSpec((CONV_W, c), lambda b, n: (0, n)),
            pl.BlockSpec((1, c), lambda b, n: (0, n)),
            pl.BlockSpec((None, c, 4 * c), lambda b, n: (n, 0, 0)),
            pl.BlockSpec((None, 1, 4 * c), lambda b, n: (n, 0, 0)),
            pl.BlockSpec((2, c), lambda b, n: (0, n)),
        ],
        out_specs=pl.BlockSpec((None, seq, c), blk),
        out_shape=jax.ShapeDtypeStruct((bsz, seq, D_RNN), BF16),
        scratch_shapes=[pltpu.VMEM((c // LANES, seq + (CONV_W - 1) * SUBLANES, LANES), F32)]
        + [pltpu.VMEM((c // LANES, seq, LANES), F32)] * 7,
        compiler_params=_params("arbitrary", "arbitrary"),
        name="rglru_scan",
    )(xb, conv_w, conv_b, wcat, bcat, lam)


def _mla_weights(w_in, w_q_b, w_kv_b, q_norm, k_norm):
    half = QK_ROPE // 2

    def slab(t):
        return jnp.pad(t, [(0, 0)] * (t.ndim - 1) + [(0, LANES - QK_HEAD)])

    def rot_slab(t):
        rope = t[..., QK_NOPE:]
        swapped = jnp.concatenate([jnp.zeros_like(t[..., :QK_NOPE]), rope[..., half:], rope[..., :half]], axis=-1)
        return slab(swapped)

    kpe = jnp.pad(w_in[:, Q_LORA + KV_LORA:], ((0, 0), (QK_NOPE, 0)))
    w_in_p = jnp.concatenate([w_in[:, :Q_LORA + KV_LORA], slab(kpe), rot_slab(kpe)], axis=1).astype(BF16)
    wq = w_q_b.reshape(Q_LORA, N_HEADS, QK_HEAD)
    wq_p = slab(wq).reshape(Q_LORA, N_HEADS * LANES).astype(BF16)
    wq_rot = rot_slab(wq).reshape(Q_LORA, N_HEADS * LANES).astype(BF16)
    wkv = w_kv_b.reshape(KV_LORA, N_HEADS, QK_NOPE + V_HEAD)
    wk = jnp.pad(wkv[:, :, :QK_NOPE], ((0, 0), (0, 0), (0, LANES - QK_NOPE))).reshape(KV_LORA, N_HEADS * LANES)
    wv = wkv[:, :, QK_NOPE:].reshape(KV_LORA, N_HEADS * V_HEAD)
    w_kv_p = jnp.concatenate([wk, wv], axis=1).astype(BF16)
    gains = lambda g: jnp.stack([slab(g), rot_slab(g)], axis=0)
    return w_in_p, wq_p, wq_rot, w_kv_p, gains(q_norm), gains(k_norm)


def _rope_kernel(pos_ref, freq_ref, cos_out, sin_out):
    ang = freq_ref[...] * pos_ref[...].astype(F32)
    cos, sin = jnp.cos(ang), jnp.sin(ang)
    seq = ang.shape[1]
    fill = lambda value, n: jnp.full((n, seq), value, F32)
    cos_out[...] = jnp.concatenate([fill(1.0, QK_NOPE), cos, cos, fill(1.0, LANES - QK_HEAD)], axis=0).T
    sin_out[...] = jnp.concatenate([fill(0.0, QK_NOPE), -sin, sin, fill(0.0, LANES - QK_HEAD)], axis=0).T


def _rope_tables(positions):
    half = QK_ROPE // 2
    inv_freq = ROPE_THETA ** (-jnp.arange(half, dtype=F32) / half)
    bsz, seq = positions.shape
    table = jax.ShapeDtypeStruct((bsz, seq, LANES), F32)
    return pl.pallas_call(
        _rope_kernel,
        grid=(bsz,),
        in_specs=[pl.BlockSpec((None, 1, seq), lambda b: (b, 0, 0)), pl.BlockSpec((half, 1), lambda b: (0, 0))],
        out_specs=[pl.BlockSpec((None, seq, LANES), lambda b: (b, 0, 0))] * 2,
        out_shape=[table, table],
        compiler_params=_params("arbitrary"),
        name="rope_tables",
    )(positions.reshape(bsz, 1, seq), inv_freq.reshape(half, 1))


def _router_weights(w_router, router_bias):
    perm = (jnp.arange(N_EXPERTS) % N_GROUPS) * EXPERTS_PER_GROUP + jnp.arange(N_EXPERTS) // N_GROUPS
    w = w_router[:, perm]
    hi = w.astype(BF16)
    lo = (w - hi.astype(F32)).astype(BF16)
    z = jnp.zeros_like(hi)
    wr1 = jnp.concatenate([hi, lo, z, z], axis=1)
    wr2 = jnp.concatenate([z, z, hi, z], axis=1)
    return wr1, wr2, router_bias[perm].reshape(N_EXPERTS, 1).astype(F32)


def kernel(x, c, positions, norm_mix, norm_ffn, w_ada, b_ada, mla_w_in, mla_q_a_norm, mla_kv_a_norm, mla_w_q_b, mla_w_kv_b, mla_q_norm, mla_k_norm, mla_w_o, rnn_w_in, rnn_conv_w, rnn_conv_b, rnn_lam_f, rnn_w_rf, rnn_b_rf, rnn_w_if, rnn_b_if, rnn_lam_b, rnn_w_rb, rnn_b_rb, rnn_w_ib, rnn_b_ib, rnn_w_o, w_router, router_bias, moe_w_gu, moe_w_dn):
    bsz, seq, d = x.shape
    depth = w_ada.shape[0]
    mod = _ada(c, w_ada, b_ada)
    wr1, wr2, rbias = _router_weights(w_router, router_bias)
    cos_t, sin_t = _rope_tables(positions)
    vec = lambda v: v.reshape(1, -1)
    for i in range(depth):
        sh1, sc1, g1, sh2, sc2, g2 = [mod[i, :, k * d:(k + 1) * d].reshape(bsz, 1, d) for k in range(6)]
        j = i // 2
        if i % 2 == 0:
            w_in_p, wq, wq_rot, wkv, qn, kn = _mla_weights(mla_w_in[j], mla_w_q_b[j], mla_w_kv_b[j],
                                                           mla_q_norm[j], mla_k_norm[j])
            q, k, v = _mla_in(x, vec(norm_mix[i]), sh1, sc1, w_in_p, vec(mla_q_a_norm[j]),
                              vec(mla_kv_a_norm[j]), wq, wq_rot, wkv, qn, kn, cos_t, sin_t)
            a = _attention(q, k, v, mla_q_norm[j], mla_k_norm[j])
            hs = None
            w_o = mla_w_o[j].astype(BF16)
        else:
            a, xb = _rnn_in(x, vec(norm_mix[i]), sh1, sc1, rnn_w_in[j].astype(BF16))
            wcat = (0.5 * jnp.concatenate([rnn_w_rf[j], rnn_w_if[j], rnn_w_rb[j], rnn_w_ib[j]], axis=-1)).astype(BF16)
            bcat = jnp.stack([b.reshape(RNN_BLOCKS, RNN_BW) for b in
                              (rnn_b_rf[j], rnn_b_if[j], rnn_b_rb[j], rnn_b_ib[j])], axis=1)
            bcat = 0.5 * bcat.reshape(RNN_BLOCKS, 1, 4 * RNN_BW)
            lam = jnp.stack([rnn_lam_f[j], rnn_lam_b[j]], axis=0)
            hs = _lru(xb, rnn_conv_w[j], vec(rnn_conv_b[j]), wcat, bcat, lam)
            w_o = rnn_w_o[j].astype(BF16)
        x, h2, idx, wts = _mix_out(a, hs, x, w_o, g1, vec(norm_ffn[i]), sh2, sc2, wr1, wr2, rbias)
        x = _moe(h2, idx, wts, x, g2, moe_w_gu, moe_w_dn, i)
    return x
```

```python
import functools

import jax
import jax.numpy as jnp
from jax import lax
from jax.experimental import pallas as pl
from jax.experimental.pallas import tpu as pltpu

F32 = jnp.float32
BF16 = jnp.bfloat16

D_MODEL = 1024
N_HEADS = 16
Q_LORA = 384
KV_LORA = 256
QK_NOPE = 64
QK_ROPE = 32
QK_HEAD = QK_NOPE + QK_ROPE
V_HEAD = 64
ROPE_THETA = 10000.0
D_RNN = D_MODEL
RNN_BLOCKS = 4
RNN_BW = D_RNN // RNN_BLOCKS
CONV_W = 4
LRU_C = 8.0
N_EXPERTS = 32
N_GROUPS = 8
EXPERTS_PER_GROUP = N_EXPERTS // N_GROUPS
TOP_K = 2
D_EXPERT = 512
EPS = 1e-6
LOG2_E = 1.4426950408889634

LANES = 128
SUBLANES = 8
VMEM_LIMIT = 52 * 1024 * 1024

ROW_TILE = 512
ROW_SUB = 256
Q_TILE = 2048
Q_SUB = 256
MAX_SAFE_SHIFT = 60.0
MOE_TILE = 512
SCAN_ROWS = 256
ADA_TILE = 1536
assert MOE_TILE & (MOE_TILE - 1) == 0


def _dot(a, b):
    return jnp.dot(a, b, preferred_element_type=F32)


def _split_bf16(a):
    hi = a.astype(BF16)
    lo = (a - hi.astype(F32)).astype(BF16)
    return hi, lo


def _dot_split(a, b):
    ah, al = _split_bf16(a)
    bh, bl = _split_bf16(b)
    return _dot(ah, bh) + (_dot(ah, bl) + _dot(al, bh))


def _rms(x, gain, n):
    ms = jnp.sum(x * x, axis=-1, keepdims=True) * (1.0 / n)
    return x * lax.rsqrt(ms + EPS) * gain


def _modulate(x, gain, shift, scale):
    return _rms(x, gain, x.shape[-1]) * (1.0 + scale) + shift


def _params(*sem):
    return pltpu.CompilerParams(dimension_semantics=sem, vmem_limit_bytes=VMEM_LIMIT)


def _ada_kernel(c_ref, w_ref, b_ref, o_ref):
    c = c_ref[...]
    o_ref[...] = _dot_split(c * jax.nn.sigmoid(c), w_ref[...]) + b_ref[...]


def _ada(c, w_ada, b_ada):
    depth, d, n = w_ada.shape
    bsz = c.shape[0]
    tn = min(ADA_TILE, n)
    return pl.pallas_call(
        _ada_kernel,
        grid=(depth, n // tn),
        in_specs=[
            pl.BlockSpec((bsz, d), lambda l, j: (0, 0)),
            pl.BlockSpec((None, d, tn), lambda l, j: (l, 0, j)),
            pl.BlockSpec((None, 1, tn), lambda l, j: (l, 0, j)),
        ],
        out_specs=pl.BlockSpec((None, bsz, tn), lambda l, j: (l, 0, j)),
        out_shape=jax.ShapeDtypeStruct((depth, bsz, n), F32),
        compiler_params=_params("arbitrary", "arbitrary"),
        name="adaln_mod",
    )(c, w_ada, b_ada.reshape(depth, 1, n))


def _head_scale(s):
    return lax.rsqrt(jnp.sum(s * s, axis=-1, keepdims=True) * (1.0 / QK_HEAD) + EPS)


def _mla_in_kernel(x_ref, g_ref, sh_ref, sc_ref, win_ref, qan_ref, kvan_ref, wq_ref, wqr_ref, wkv_ref,
                   qn_ref, kn_ref, cos_ref, sin_ref, q_out, k_out, v_out):
    h = _modulate(x_ref[...], g_ref[...], sh_ref[...], sc_ref[...])
    lat = _dot(h.astype(BF16), win_ref[...])
    q_lat = lat[:, :Q_LORA]
    kv_lat = lat[:, Q_LORA:Q_LORA + KV_LORA]
    kpe = lat[:, Q_LORA + KV_LORA:Q_LORA + KV_LORA + LANES]
    kpe_rot = lat[:, Q_LORA + KV_LORA + LANES:]
    qn = _rms(q_lat, qan_ref[...], Q_LORA).astype(BF16)
    q_all = _dot(qn, wq_ref[...])
    q_rot = _dot(qn, wqr_ref[...])
    kv_all = _dot(_rms(kv_lat, kvan_ref[...], KV_LORA).astype(BF16), wkv_ref[...])
    cos_t = cos_ref[...]
    sin_t = sin_ref[...]
    q_scale = LOG2_E * QK_HEAD ** -0.5
    cq = cos_t * (qn_ref[0:1, :] * q_scale)
    sq = sin_t * (qn_ref[1:2, :] * q_scale)
    ck = cos_t * kn_ref[0:1, :]
    k_rot_term = kpe_rot * (sin_t * kn_ref[1:2, :])
    for hh in range(N_HEADS):
        sl = slice(hh * LANES, (hh + 1) * LANES)
        s = q_all[:, sl]
        q_out[hh] = ((s * cq + q_rot[:, sl] * sq) * _head_scale(s)).astype(BF16)
        s = kv_all[:, sl] + kpe
        k_out[hh] = ((s * ck + k_rot_term) * _head_scale(s)).astype(BF16)
    v_out[...] = kv_all[:, N_HEADS * LANES:].astype(BF16)


def _mla_in(x, gain, shift, scale, w_in, q_a_norm, kv_a_norm, w_q, w_q_rot, w_kv, q_norm, k_norm, cos_t, sin_t):
    bsz, seq, d = x.shape
    tm = min(ROW_TILE, seq)
    row = lambda b, i: (b, i, 0)
    per_b = lambda b, i: (b, 0, 0)
    const = lambda b, i: (0, 0)
    full = lambda a: pl.BlockSpec(a.shape, const)
    return pl.pallas_call(
        _mla_in_kernel,
        grid=(bsz, seq // tm),
        in_specs=[
            pl.BlockSpec((None, tm, d), row),
            full(gain),
            pl.BlockSpec((None, 1, d), per_b),
            pl.BlockSpec((None, 1, d), per_b),
            full(w_in), full(q_a_norm), full(kv_a_norm), full(w_q), full(w_q_rot), full(w_kv),
            full(q_norm), full(k_norm),
            pl.BlockSpec((None, tm, LANES), row),
            pl.BlockSpec((None, tm, LANES), row),
        ],
        out_specs=[
            pl.BlockSpec((None, N_HEADS, tm, LANES), lambda b, i: (b, 0, i, 0)),
            pl.BlockSpec((None, N_HEADS, tm, LANES), lambda b, i: (b, 0, i, 0)),
            pl.BlockSpec((None, tm, N_HEADS * V_HEAD), row),
        ],
        out_shape=[
            jax.ShapeDtypeStruct((bsz, N_HEADS, seq, LANES), BF16),
            jax.ShapeDtypeStruct((bsz, N_HEADS, seq, LANES), BF16),
            jax.ShapeDtypeStruct((bsz, seq, N_HEADS * V_HEAD), BF16),
        ],
        compiler_params=_params("arbitrary", "arbitrary"),
        name="mla_in",
    )(x, gain, shift, scale, w_in, q_a_norm, kv_a_norm, w_q, w_q_rot, w_kv, q_norm, k_norm, cos_t, sin_t)


def _attn_kernel(bounded, shift_ref, q_ref, k_ref, v_ref, o_ref):
    v = v_ref[...]
    lane_v = lax.broadcasted_iota(jnp.int32, v.shape, 1)
    v_heads = [jnp.where(lane_v < V_HEAD, v, jnp.ones((), BF16)), jnp.where(lane_v >= V_HEAD, v, jnp.ones((), BF16))]
    lane = lax.broadcasted_iota(jnp.int32, (Q_SUB, LANES), 1)
    for i in range(q_ref.shape[1] // Q_SUB):
        rows = slice(i * Q_SUB, (i + 1) * Q_SUB)
        outs = []
        for j in range(2):
            s = lax.dot_general(q_ref[j, rows, :], k_ref[j], (((1,), (1,)), ((), ())),
                                preferred_element_type=F32)
            m = shift_ref[0] if bounded else jnp.max(s, axis=-1, keepdims=True)
            o = _dot(jnp.exp2(s - m).astype(BF16), v_heads[j])
            denom = o[:, V_HEAD:V_HEAD + 1] if j == 0 else o[:, 0:1]
            outs.append(o / denom)
        o_ref[rows, :] = jnp.where(lane < V_HEAD, outs[0], outs[1]).astype(BF16)


def _attention_call(bounded, shift, q, k, v):
    bsz, _, seq, _ = q.shape
    tq = min(Q_TILE, seq)
    assert tq % Q_SUB == 0
    grid_spec = pltpu.PrefetchScalarGridSpec(
        num_scalar_prefetch=1,
        grid=(bsz, N_HEADS // 2, seq // tq),
        in_specs=[
            pl.BlockSpec((None, 2, tq, LANES), lambda b, h, i, *_: (b, h, i, 0)),
            pl.BlockSpec((None, 2, seq, LANES), lambda b, h, i, *_: (b, h, 0, 0)),
            pl.BlockSpec((None, seq, LANES), lambda b, h, i, *_: (b, 0, h)),
        ],
        out_specs=pl.BlockSpec((None, tq, LANES), lambda b, h, i, *_: (b, i, h)),
    )
    return pl.pallas_call(
        functools.partial(_attn_kernel, bounded),
        grid_spec=grid_spec,
        out_shape=jax.ShapeDtypeStruct((bsz, seq, N_HEADS * V_HEAD), BF16),
        compiler_params=_params("arbitrary", "arbitrary", "arbitrary"),
        name="mla_attention",
    )(shift, q, k, v)


def _attention(q, k, v, q_gain, k_gain):
    score_bound = 1.02 * LOG2_E * QK_HEAD ** 0.5 * jnp.max(jnp.abs(q_gain)) * jnp.max(jnp.abs(k_gain))
    shift = score_bound.reshape(1).astype(F32)
    return lax.cond(score_bound <= MAX_SAFE_SHIFT,
                    functools.partial(_attention_call, True), functools.partial(_attention_call, False),
                    shift, q, k, v)


def _first_index_of_max(vals):
    m = vals[0]
    for v in vals[1:]:
        m = jnp.maximum(m, v)
    idx = jnp.full(m.shape, float(len(vals) - 1), F32)
    for j in range(len(vals) - 2, -1, -1):
        idx = jnp.where(vals[j] == m, float(j), idx)
    return m, idx


def _route(h2, wr1_ref, wr2_ref, rb_ref):
    hh, hl = _split_bf16(h2)
    logits = (_dot(hh, wr1_ref[...]) + _dot(hl, wr2_ref[...])).T
    logit = logits[0:N_EXPERTS] + logits[N_EXPERTS:2 * N_EXPERTS] + logits[2 * N_EXPERTS:3 * N_EXPERTS]
    score = jax.nn.sigmoid(logit)
    biased = score + rb_ref[...]
    a = [biased[j * N_GROUPS:(j + 1) * N_GROUPS] for j in range(EXPERTS_PER_GROUP)]
    sc = [score[j * N_GROUPS:(j + 1) * N_GROUPS] for j in range(EXPERTS_PER_GROUP)]
    hi1, lo1 = jnp.maximum(a[0], a[1]), jnp.minimum(a[0], a[1])
    hi2, lo2 = jnp.maximum(a[2], a[3]), jnp.minimum(a[2], a[3])
    gscore = jnp.maximum(hi1, hi2) + jnp.maximum(jnp.minimum(hi1, hi2), jnp.maximum(lo1, lo2))
    gmax = jnp.max(gscore, axis=0, keepdims=True)
    giota = lax.broadcasted_iota(jnp.int32, gscore.shape, 0).astype(F32)
    gsel = jnp.min(jnp.where(gscore == gmax, giota, float(N_GROUPS)), axis=0, keepdims=True)
    onehot = giota == gsel
    pick = lambda t: jnp.sum(jnp.where(onehot, t, 0.0), axis=0, keepdims=True)
    bj = [pick(t) for t in a]
    sj = [pick(t) for t in sc]
    _, i1 = _first_index_of_max(bj)
    bj2 = [jnp.where(i1 == float(j), -jnp.inf, bj[j]) for j in range(EXPERTS_PER_GROUP)]
    _, i2 = _first_index_of_max(bj2)
    sel = lambda i: jnp.where(i == 0.0, sj[0], jnp.where(i == 1.0, sj[1], jnp.where(i == 2.0, sj[2], sj[3])))
    w1, w2 = sel(i1), sel(i2)
    den = w1 + w2
    base = gsel * float(EXPERTS_PER_GROUP)
    return ((base + i1).astype(jnp.int32), (base + i2).astype(jnp.int32)), (w1 / den, w2 / den)


def _mix_out_kernel(has_gate, *refs):
    if has_gate:
        a_ref, hs_ref, x_ref, wo_ref, g1_ref, g_ref, sh_ref, sc_ref, wr1_ref, wr2_ref, rb_ref, \
            x_out, h_out, idx_out, wts_out = refs
    else:
        a_ref, x_ref, wo_ref, g1_ref, g_ref, sh_ref, sc_ref, wr1_ref, wr2_ref, rb_ref, \
            x_out, h_out, idx_out, wts_out = refs
    tm = x_ref.shape[0]
    sub = min(ROW_SUB, tm)
    for i in range(tm // sub):
        rows = slice(i * sub, (i + 1) * sub)
        if has_gate:
            a = (a_ref[rows, :].astype(F32) * hs_ref[rows, :]).astype(BF16)
        else:
            a = a_ref[rows, :]
        x1 = x_ref[rows, :] + g1_ref[...] * _dot(a, wo_ref[...])
        x_out[rows, :] = x1
        h2 = _modulate(x1, g_ref[...], sh_ref[...], sc_ref[...])
        _to_tiles(h_out, h2, i * sub)
        idx, wts = _route(h2, wr1_ref, wr2_ref, rb_ref)
        for k in range(TOP_K):
            idx_out[k:k + 1, rows] = idx[k]
            wts_out[k:k + 1, rows] = wts[k]


def _mix_out(a, hs, x, w_o, gate1, gain, shift, scale, wr1, wr2, rbias):
    bsz, seq, d = x.shape
    tm = min(ROW_TILE, seq)
    row = lambda b, i: (b, i, 0)
    per_b = lambda b, i: (b, 0, 0)
    const = lambda b, i: (0, 0)
    full = lambda t: pl.BlockSpec(t.shape, const)
    vec = pl.BlockSpec((None, 1, d), per_b)
    acts = [a] if hs is None else [a, hs]
    return pl.pallas_call(
        functools.partial(_mix_out_kernel, hs is not None),
        grid=(bsz, seq // tm),
        in_specs=[pl.BlockSpec((None, tm, t.shape[-1]), row) for t in acts] + [
            pl.BlockSpec((None, tm, d), row), full(w_o), vec, full(gain), vec, vec,
            full(wr1), full(wr2), full(rbias),
        ],
        out_specs=[
            pl.BlockSpec((None, tm, d), row),
            pl.BlockSpec((None, tm * N_SUB, LANES), row),
            pl.BlockSpec((None, TOP_K, tm), lambda b, i: (b, 0, i)),
            pl.BlockSpec((None, TOP_K, tm), lambda b, i: (b, 0, i)),
        ],
        out_shape=[
            jax.ShapeDtypeStruct((bsz, seq, d), F32),
            jax.ShapeDtypeStruct((bsz, seq * N_SUB, LANES), U32),
            jax.ShapeDtypeStruct((bsz, TOP_K, seq), jnp.int32),
            jax.ShapeDtypeStruct((bsz, TOP_K, seq), F32),
        ],
        compiler_params=_params("arbitrary", "arbitrary"),
        name="mix_out_route",
    )(*acts, x, w_o, gate1, gain, shift, scale, wr1, wr2, rbias)


N_SUB = D_MODEL // (2 * LANES)
U32 = jnp.uint32
TABLE_CHUNK = 512


def _to_tiles(ref, val, lo=0):
    n, d = val.shape
    bits = lambda t: lax.bitcast_convert_type(t.astype(BF16).astype(F32), U32)
    for s in range(N_SUB):
        hi = bits(val[:, s * LANES:(s + 1) * LANES])
        lo_half = bits(val[:, d // 2 + s * LANES:d // 2 + (s + 1) * LANES])
        ref[pl.ds(lo * N_SUB + s, n, stride=N_SUB), :] = hi | (lo_half >> 16)


def _from_tiles(ref, lo, n):
    words = [ref[pl.ds(lo * N_SUB + s, n, stride=N_SUB), :] for s in range(N_SUB)]
    hi = [lax.bitcast_convert_type(w & jnp.uint32(0xFFFF0000), F32) for w in words]
    lo_half = [lax.bitcast_convert_type(w << 16, F32) for w in words]
    return jnp.concatenate(hi + lo_half, axis=1)


def _tables_kernel(idx_ref, rank_ref, cnt_ref, carry):
    @pl.when(pl.program_id(0) == 0)
    def _():
        carry[...] = jnp.zeros_like(carry)

    seq = idx_ref.shape[-1]
    ch = min(TABLE_CHUNK, seq)
    tri = jnp.where(lax.broadcasted_iota(jnp.int32, (ch, ch), 0) <= lax.broadcasted_iota(jnp.int32, (ch, ch), 1),
                    1.0, 0.0).astype(BF16)
    eiota = lax.broadcasted_iota(jnp.int32, (N_EXPERTS, ch), 0)
    cnt = carry[...]
    for k in range(TOP_K):
        for c in range(seq // ch):
            sel = eiota == idx_ref[k:k + 1, c * ch:(c + 1) * ch]
            pref = _dot(jnp.where(sel, 1.0, 0.0).astype(BF16), tri) + cnt
            rank = jnp.sum(jnp.where(sel, pref, 0.0), axis=0, keepdims=True) - 1.0
            rank_ref[k:k + 1, c * ch:(c + 1) * ch] = rank.astype(jnp.int32)
            cnt = pref[:, ch - 1:ch]
    carry[...] = cnt
    cnt_ref[...] = jnp.broadcast_to(cnt, cnt_ref.shape)


def _tables(idx):
    bsz, _, seq = idx.shape
    return pl.pallas_call(
        _tables_kernel,
        grid=(bsz,),
        in_specs=[pl.BlockSpec((None, TOP_K, seq), lambda b: (b, 0, 0))],
        out_specs=[pl.BlockSpec((None, TOP_K, seq), lambda b: (b, 0, 0)),
                   pl.BlockSpec((N_EXPERTS, LANES), lambda b: (0, 0))],
        out_shape=[jax.ShapeDtypeStruct((bsz, TOP_K, seq), jnp.int32),
                   jax.ShapeDtypeStruct((N_EXPERTS, LANES), F32)],
        scratch_shapes=[pltpu.VMEM((N_EXPERTS, 1), F32)],
        compiler_params=_params("arbitrary"),
        name="moe_tables",
    )(idx)


def _zero_runs(step, total, pad_start_ref, pad_len_ref, tail_ref):
    ops = []
    for m in range(-(-2 * N_EXPERTS // total)):
        u = step + m * total
        e = jnp.minimum(u, N_EXPERTS - 1)
        length = jnp.where(u < N_EXPERTS, pad_len_ref[e], 0)
        first = pad_start_ref[e]
        for bit in reversed(range(MOE_TILE.bit_length() - 1)):
            done = lax.shift_left(lax.shift_right_logical(length, bit + 1), bit + 1)
            ops.append((lax.bitwise_and(lax.shift_right_logical(length, bit), 1) == 1, first + done, 1 << bit))
        t = u - N_EXPERTS
        ops.append(((t >= 0) & (t < tail_ref[1]), tail_ref[0] + t * MOE_TILE, MOE_TILE))
    return ops


def _scatter_kernel(total, pad_start_ref, pad_len_ref, tail_ref, dest_ref, src_hbm, dst_hbm,
                    buf, zbuf, sem_in, sem_out, sem_z):
    n = pl.program_id(0) * pl.num_programs(1) + pl.program_id(1)
    tm = buf.shape[1] // N_SUB
    slot = lax.rem(n, 3)

    def load(step, sl):
        return pltpu.make_async_copy(src_hbm.at[pl.ds(step * tm * N_SUB, tm * N_SUB)], buf.at[sl], sem_in.at[sl])

    def drain(sl):
        for _ in range(TOP_K):
            pltpu.make_async_copy(buf.at[sl], dst_hbm.at[pl.ds(0, tm * N_SUB)], sem_out.at[sl]).wait()

    def zero_fill(step, wait):
        for pred, first, rows in _zero_runs(step, total, pad_start_ref, pad_len_ref, tail_ref):
            @pl.when(pred)
            def _(first=first, rows=rows):
                cp = pltpu.make_async_copy(zbuf.at[pl.ds(0, rows * N_SUB)],
                                           dst_hbm.at[pl.ds(pl.multiple_of(first * N_SUB, N_SUB), rows * N_SUB)], sem_z)
                cp.wait() if wait else cp.start()

    @pl.when(n == 0)
    def _():
        zbuf[...] = jnp.zeros_like(zbuf)
        load(0, 0).start()
        if total > 1:
            load(1, 1).start()

    load(n, slot).wait()
    for k in range(TOP_K):
        for c in range(tm // LANES):
            def start(j, carry, k=k, c=c):
                src = buf.at[slot, pl.ds(pl.multiple_of((c * LANES + j) * N_SUB, N_SUB), N_SUB)]
                dst = dst_hbm.at[pl.ds(pl.multiple_of(dest_ref[0, k * tm + c * LANES + j], N_SUB), N_SUB)]
                pltpu.make_async_copy(src, dst, sem_out.at[slot]).start(priority=k)
                return carry
            lax.fori_loop(0, LANES, start, 0, unroll=8)
    zero_fill(n, wait=False)

    @pl.when(n > 0)
    def _():
        drain(lax.rem(n + 2, 3))
        zero_fill(n - 1, wait=True)

    @pl.when(n + 2 < total)
    def _():
        load(n + 2, lax.rem(n + 2, 3)).start()

    @pl.when(n == total - 1)
    def _():
        drain(slot)
        zero_fill(n, wait=True)


def _index_blocks(table, tm):
    bsz, _, seq = table.shape
    nt = seq // tm
    t = table.reshape(bsz, TOP_K, nt, tm).transpose(0, 2, 1, 3)
    return t.reshape(bsz * nt, 1, TOP_K * tm), (None, 1, TOP_K * tm)


def _scatter(dest, pad_start, pad_len, tail, h2t, n_rows):
    bsz, _, seq = dest.shape
    tm = min(ROW_TILE, seq)
    nt = seq // tm
    dest4, dest_block = _index_blocks(dest, tm)
    grid_spec = pltpu.PrefetchScalarGridSpec(
        num_scalar_prefetch=3,
        grid=(bsz, nt),
        in_specs=[
            pl.BlockSpec(dest_block, lambda b, i, *_: (b * nt + i, 0, 0), memory_space=pltpu.SMEM),
            pl.BlockSpec(memory_space=pl.ANY),
        ],
        out_specs=pl.BlockSpec(memory_space=pl.ANY),
        scratch_shapes=[pltpu.VMEM((3, tm * N_SUB, LANES), U32), pltpu.VMEM((MOE_TILE * N_SUB, LANES), U32),
                        pltpu.SemaphoreType.DMA((3,)), pltpu.SemaphoreType.DMA((3,)), pltpu.SemaphoreType.DMA(())],
    )
    return pl.pallas_call(
        functools.partial(_scatter_kernel, bsz * nt),
        grid_spec=grid_spec,
        out_shape=jax.ShapeDtypeStruct((n_rows * N_SUB, LANES), U32),
        compiler_params=_params("arbitrary", "arbitrary"),
        name="moe_scatter",
    )(pad_start, pad_len, tail, dest4, h2t.reshape(bsz * seq * N_SUB, LANES))


def _expert_kernel(layer, blk_exp_ref, blk_first_ref, blk_next_ref, blk_slot_ref, n_used_ref,
                   xs_ref, wgu_hbm, wdn_hbm, ys_ref, wgu_f32, wdn_f32, wgu_bf, wdn_bf, sem):
    i = pl.program_id(0)

    def fetch(e, sl):
        return (pltpu.make_async_copy(wgu_hbm.at[layer, e], wgu_f32.at[sl], sem.at[0, sl]),
                pltpu.make_async_copy(wdn_hbm.at[layer, e], wdn_f32.at[sl], sem.at[1, sl]))

    @pl.when(i < n_used_ref[0])
    def _():
        @pl.when(blk_first_ref[i] == 1)
        def _():
            e, sl, nxt = blk_exp_ref[i], blk_slot_ref[i], blk_next_ref[i]

            @pl.when(i == 0)
            def _():
                for cp in fetch(e, sl):
                    cp.start()

            for cp in fetch(e, sl):
                cp.wait()
            wgu_bf[...] = wgu_f32[sl].astype(BF16)
            wdn_bf[...] = wdn_f32[sl].astype(BF16)

            @pl.when(nxt >= 0)
            def _():
                for cp in fetch(nxt, 1 - sl):
                    cp.start()

        x = _from_tiles(xs_ref, 0, MOE_TILE).astype(BF16)
        gu = _dot(x, wgu_bf[...])
        g = gu[:, :D_EXPERT]
        u = gu[:, D_EXPERT:]
        mid = (g * jax.nn.sigmoid(g) * u).astype(BF16)
        _to_tiles(ys_ref, _dot(mid, wdn_bf[...]))

    @pl.when(i >= n_used_ref[0])
    def _():
        ys_ref[...] = jnp.zeros_like(ys_ref)


def _experts(blk_exp, blk_first, blk_next, blk_slot, n_used, xs, w_gu, w_dn, layer):
    d = D_MODEL
    nb = xs.shape[0] // (MOE_TILE * N_SUB)
    tile = lambda i, *_: (i, 0)
    grid_spec = pltpu.PrefetchScalarGridSpec(
        num_scalar_prefetch=5,
        grid=(nb,),
        in_specs=[
            pl.BlockSpec((MOE_TILE * N_SUB, LANES), tile),
            pl.BlockSpec(memory_space=pl.ANY),
            pl.BlockSpec(memory_space=pl.ANY),
        ],
        out_specs=pl.BlockSpec((MOE_TILE * N_SUB, LANES), tile),
        scratch_shapes=[pltpu.VMEM((2, d, 2 * D_EXPERT), F32), pltpu.VMEM((2, D_EXPERT, d), F32),
                        pltpu.VMEM((d, 2 * D_EXPERT), BF16), pltpu.VMEM((D_EXPERT, d), BF16),
                        pltpu.SemaphoreType.DMA((2, 2))],
    )
    return pl.pallas_call(
        functools.partial(_expert_kernel, layer),
        grid_spec=grid_spec,
        out_shape=jax.ShapeDtypeStruct(xs.shape, U32),
        compiler_params=_params("arbitrary"),
        name="moe_experts",
    )(blk_exp, blk_first, blk_next, blk_slot, n_used, xs, w_gu, w_dn)


def _combine_kernel(dcur_ref, dnxt_ref, ys_hbm, x_ref, wts_ref, g2_ref, x_out, buf0, buf1, sem):
    nt = pl.num_programs(1)
    n = pl.program_id(0) * nt + pl.program_id(1)
    total = pl.num_programs(0) * nt
    tm = x_ref.shape[0]
    bufs = (buf0, buf1)

    def copy(d_ref, sl, k, r):
        src = ys_hbm.at[pl.ds(pl.multiple_of(d_ref[0, k * tm + r], N_SUB), N_SUB)]
        dst = bufs[sl].at[pl.ds(pl.multiple_of((k * tm + r) * N_SUB, N_SUB), N_SUB)]
        return pltpu.make_async_copy(src, dst, sem.at[sl])

    def drain(sl):
        pltpu.make_async_copy(ys_hbm.at[pl.ds(0, TOP_K * tm * N_SUB)], bufs[sl], sem.at[sl]).wait()

    @pl.when(n == 0)
    def _():
        for k in range(TOP_K):
            def start(r, carry, k=k):
                copy(dcur_ref, 0, k, r).start(priority=k)
                return carry
            lax.fori_loop(0, tm, start, 0, unroll=8)

    def step(sl):
        drain(sl)
        for r in range(tm):
            for k in range(TOP_K):
                copy(dnxt_ref, 1 - sl, k, r).start(priority=k)
        w = wts_ref[...]
        y = w[:, 0:1] * _from_tiles(bufs[sl], 0, tm) + w[:, 1:2] * _from_tiles(bufs[sl], tm, tm)
        x_out[...] = x_ref[...] + g2_ref[...] * y

        @pl.when(n == total - 1)
        def _():
            drain(1 - sl)

    for sl in range(2):
        pl.when(lax.rem(n, 2) == sl)(functools.partial(step, sl))


def _combine(dest_row, ys, x, wts_col, gate2):
    bsz, seq, d = x.shape
    tm = min(ROW_TILE, seq)
    nt = seq // tm

    def nxt(b, i):
        return (jnp.minimum(b * nt + i + 1, bsz * nt - 1), 0, 0)

    dest_row, dest_block = _index_blocks(dest_row, tm)
    return pl.pallas_call(
        _combine_kernel,
        grid=(bsz, nt),
        in_specs=[
            pl.BlockSpec(dest_block, lambda b, i: (b * nt + i, 0, 0), memory_space=pltpu.SMEM),
            pl.BlockSpec(dest_block, nxt, memory_space=pltpu.SMEM),
            pl.BlockSpec(memory_space=pl.ANY),
            pl.BlockSpec((None, tm, d), lambda b, i: (b, i, 0)),
            pl.BlockSpec((None, tm, TOP_K), lambda b, i: (b, i, 0)),
            pl.BlockSpec((None, 1, d), lambda b, i: (b, 0, 0)),
        ],
        out_specs=pl.BlockSpec((None, tm, d), lambda b, i: (b, i, 0)),
        out_shape=jax.ShapeDtypeStruct((bsz, seq, d), F32),
        scratch_shapes=[pltpu.VMEM((TOP_K * tm * N_SUB, LANES), U32), pltpu.VMEM((TOP_K * tm * N_SUB, LANES), U32),
                        pltpu.SemaphoreType.DMA((2,))],
        compiler_params=_params("arbitrary", "arbitrary"),
        name="moe_combine",
    )(dest_row, dest_row, ys, x, wts_col, gate2)


def _lookup(table, keys):
    hit = keys[..., None] == jnp.arange(table.shape[0], dtype=jnp.int32)
    return jnp.sum(jnp.where(hit, table, 0), axis=-1).astype(jnp.int32)


def _count_le(bounds, q):
    return jnp.sum((bounds <= q[..., None]).astype(jnp.int32), axis=-1)


def _moe(h2t, idx, wts, x, gate2, w_gu, w_dn, layer):
    bsz, seq, _ = x.shape
    n_rows = bsz * seq * TOP_K + N_EXPERTS * MOE_TILE
    nb = n_rows // MOE_TILE
    rank, cnt = _tables(idx)
    counts = cnt[:, 0].astype(jnp.int32)
    padded = ((counts + MOE_TILE - 1) // MOE_TILE) * MOE_TILE
    pend = jnp.cumsum(padded)
    pstart = pend - padded
    dest = _lookup(pstart, idx) + rank
    blk_row = jnp.arange(nb, dtype=jnp.int32) * MOE_TILE
    blk_exp = jnp.minimum(_count_le(pend, blk_row), N_EXPERTS - 1)
    blk_first = (blk_row == _lookup(pstart, blk_exp)).astype(jnp.int32)
    n_used = (pend[-1:] // MOE_TILE).astype(jnp.int32)
    owns = counts > 0
    eid = jnp.arange(N_EXPERTS, dtype=jnp.int32)
    later = lax.cummin(jnp.where(owns, eid, N_EXPERTS), axis=0, reverse=True)
    nxt = jnp.concatenate([later[1:], jnp.full((1,), N_EXPERTS, jnp.int32)])
    blk_next = _lookup(jnp.where(nxt < N_EXPERTS, nxt, -1), blk_exp)
    blk_slot = _lookup((jnp.cumsum(owns.astype(jnp.int32)) - 1) % 2, blk_exp)
    tail = jnp.concatenate([pend[-1:], (n_rows - pend[-1:]) // MOE_TILE]).astype(jnp.int32)
    dest_row = dest * N_SUB
    xs = _scatter(dest_row, (pstart + counts).astype(jnp.int32), (padded - counts).astype(jnp.int32), tail, h2t, n_rows)
    ys = _experts(blk_exp, blk_first, blk_next, blk_slot, n_used, xs, w_gu, w_dn, layer)
    return _combine(dest_row, ys, x, wts.transpose(0, 2, 1), gate2)


def _rnn_in_kernel(x_ref, g_ref, sh_ref, sc_ref, w_ref, gate_out, xb_out):
    h = _modulate(x_ref[...], g_ref[...], sh_ref[...], sc_ref[...])
    u = _dot(h.astype(BF16), w_ref[...])
    gate_out[...] = jax.nn.gelu(u[:, :D_RNN]).astype(BF16)
    xb_out[...] = u[:, D_RNN:]


def _rnn_in(x, gain, shift, scale, w_in):
    bsz, seq, d = x.shape
    tm = min(ROW_TILE, seq)
    row = lambda b, i: (b, i, 0)
    per_b = lambda b, i: (b, 0, 0)
    const = lambda b, i: (0, 0)
    return pl.pallas_call(
        _rnn_in_kernel,
        grid=(bsz, seq // tm),
        in_specs=[
            pl.BlockSpec((None, tm, d), row),
            pl.BlockSpec(gain.shape, const),
            pl.BlockSpec((None, 1, d), per_b),
            pl.BlockSpec((None, 1, d), per_b),
            pl.BlockSpec(w_in.shape, const),
        ],
        out_specs=[pl.BlockSpec((None, tm, D_RNN), row), pl.BlockSpec((None, tm, D_RNN), row)],
        out_shape=[jax.ShapeDtypeStruct((bsz, seq, D_RNN), BF16),
                   jax.ShapeDtypeStruct((bsz, seq, D_RNN), F32)],
        compiler_params=_params("arbitrary", "arbitrary"),
        name="rnn_in",
    )(x, gain, shift, scale, w_in)


def _lru_kernel(xb_ref, cw_ref, cb_ref, wcat_ref, bcat_ref, lam_ref, hs_ref,
                xi_ref, af_ref, bf_ref, ab_ref, bb_ref, hf_ref, hb_ref, sum_ref):
    seq, c = xb_ref.shape
    seg_len = seq // SUBLANES
    n_slab = c // LANES
    n_rows = seg_len * SUBLANES
    halo = (CONV_W // 2) * SUBLANES
    row = lax.broadcasted_iota(jnp.int32, (SUBLANES, LANES), 0)
    for sl in range(n_slab):
        lanes = slice(sl * LANES, (sl + 1) * LANES)
        for g in range(SUBLANES):
            xi_ref[sl, pl.ds(halo + g, seg_len, stride=SUBLANES), :] = xb_ref[g * seg_len:(g + 1) * seg_len, lanes]
        for back in (1, 2):
            prev = xi_ref[sl, halo + (seg_len - back) * SUBLANES:halo + (seg_len - back + 1) * SUBLANES, :]
            xi_ref[sl, halo - back * SUBLANES:halo - (back - 1) * SUBLANES, :] = jnp.where(
                row == 0, 0.0, pltpu.roll(prev, 1, 0))
        nxt = xi_ref[sl, halo:halo + SUBLANES, :]
        xi_ref[sl, halo + n_rows:halo + n_rows + SUBLANES, :] = jnp.where(
            row == SUBLANES - 1, 0.0, pltpu.roll(nxt, SUBLANES - 1, 0))

    cw = cw_ref[...]
    cb = cb_ref[...]
    lam = lam_ref[...]
    neg = -lam
    softplus = jnp.maximum(neg, 0.0) + jnp.log1p(jnp.exp(-jnp.abs(neg)))
    half_rate = (-0.5 * LRU_C) * softplus
    rows = min(SCAN_ROWS, n_rows)
    n_chunks = n_rows // rows

    for ci in range(n_chunks):
        i0 = ci * rows
        taps = []
        for k in range(CONV_W):
            lo = halo + i0 + (k - CONV_W // 2) * SUBLANES
            taps.append(jnp.concatenate([xi_ref[sl, lo:lo + rows, :] for sl in range(n_slab)], axis=1))
        xc = cb
        for k in range(CONV_W):
            xc = xc + taps[k] * cw[k:k + 1, :]
        xcb = xc.astype(BF16)
        xh = 0.5 * xc
        for dirn, (a_ref, b_ref) in enumerate(((af_ref, bf_ref), (ab_ref, bb_ref))):
            cols = slice(2 * dirn * c, 2 * (dirn + 1) * c)
            th = jnp.tanh(_dot(xcb, wcat_ref[:, cols]) + bcat_ref[:, cols])
            hr = half_rate[dirn:dirn + 1, :]
            log_a = hr * th[:, :c] + hr
            a = jnp.exp(log_a)
            m2 = jnp.tanh(log_a) * (-1.0 - a * a)
            mult = jnp.where(m2 > 0.0, m2 * lax.rsqrt(m2), 0.0)
            if dirn == 0 and ci == 0:
                mult = jnp.where(lax.broadcasted_iota(jnp.int32, mult.shape, 0) == 0, 1.0, mult)
            if dirn == 1 and ci == n_chunks - 1:
                mult = jnp.where(lax.broadcasted_iota(jnp.int32, mult.shape, 0) == rows - 1, 1.0, mult)
            b = mult * (th[:, c:] + 1.0) * xh
            for sl in range(n_slab):
                a_ref[sl, i0:i0 + rows, :] = a[:, sl * LANES:(sl + 1) * LANES]
                b_ref[sl, i0:i0 + rows, :] = b[:, sl * LANES:(sl + 1) * LANES]

    def step_rows(cidx):
        fwd = pl.ds(pl.multiple_of(cidx * SUBLANES, SUBLANES), SUBLANES)
        bwd = pl.ds(pl.multiple_of((seg_len - 1 - cidx) * SUBLANES, SUBLANES), SUBLANES)
        return fwd, bwd

    zero = jnp.zeros((SUBLANES, LANES), F32)
    one = jnp.ones((SUBLANES, LANES), F32)

    def totals(cidx, carry):
        fwd, bwd = step_rows(cidx)
        out = []
        for sl in range(n_slab):
            hf, pf, hb, pb = carry[sl]
            af, ab = af_ref[sl, fwd, :], ab_ref[sl, bwd, :]
            out.append((af * hf + bf_ref[sl, fwd, :], af * pf, ab * hb + bb_ref[sl, bwd, :], ab * pb))
        return tuple(out)
    tot = lax.fori_loop(0, seg_len, totals, tuple((zero, one, zero, one) for _ in range(n_slab)), unroll=8)

    enter = []
    for sl in range(n_slab):
        hf, pf, hb, pb = tot[sl]
        cf, cbk = zero, zero
        for _ in range(SUBLANES - 1):
            cf = jnp.where(row == 0, 0.0, pltpu.roll(hf + pf * cf, 1, 0))
            cbk = jnp.where(row == SUBLANES - 1, 0.0, pltpu.roll(hb + pb * cbk, SUBLANES - 1, 0))
        enter.append((cf, cbk))

    def states(meet, cidx, carry):
        fwd, bwd = step_rows(cidx)
        out = []
        for sl in range(n_slab):
            hf, hb = carry[sl]
            hf = af_ref[sl, fwd, :] * hf + bf_ref[sl, fwd, :]
            hb = ab_ref[sl, bwd, :] * hb + bb_ref[sl, bwd, :]
            if meet:
                sum_ref[sl, fwd, :] = hf + hb_ref[sl, fwd, :]
                sum_ref[sl, bwd, :] = hb + hf_ref[sl, bwd, :]
            else:
                hf_ref[sl, fwd, :] = hf
                hb_ref[sl, bwd, :] = hb
            out.append((hf, hb))
        return tuple(out)
    mid = lax.fori_loop(0, seg_len // 2, functools.partial(states, False), tuple(enter), unroll=8)
    lax.fori_loop(seg_len // 2, seg_len, functools.partial(states, True), mid, unroll=8)

    for g in range(SUBLANES):
        for sl in range(n_slab):
            hs_ref[g * seg_len:(g + 1) * seg_len, sl * LANES:(sl + 1) * LANES] = (
                sum_ref[sl, pl.ds(g, seg_len, stride=SUBLANES), :]).astype(BF16)


def _lru(xb, conv_w, conv_b, wcat, bcat, lam):
    bsz, seq, _ = xb.shape
    c = RNN_BW
    blk = lambda b, n: (b, 0, n)
    return pl.pallas_call(
        _lru_kernel,
        grid=(bsz, RNN_BLOCKS),
        in_specs=[
            pl.BlockSpec((None, seq, c), blk),
            pl.BlockSpec((CONV_W, c), lambda b, n: (0, n)),
            pl.BlockSpec((1, c), lambda b, n: (0, n)),
            pl.BlockSpec((None, c, 4 * c), lambda b, n: (n, 0, 0)),
            pl.BlockSpec((None, 1, 4 * c), lambda b, n: (n, 0, 0)),
            pl.BlockSpec((2, c), lambda b, n: (0, n)),
        ],
        out_specs=pl.BlockSpec((None, seq, c), blk),
        out_shape=jax.ShapeDtypeStruct((bsz, seq, D_RNN), BF16),
        scratch_shapes=[pltpu.VMEM((c // LANES, seq + (CONV_W - 1) * SUBLANES, LANES), F32)]
        + [pltpu.VMEM((c // LANES, seq, LANES), F32)] * 7,
        compiler_params=_params("arbitrary", "arbitrary"),
        name="rglru_scan",
    )(xb, conv_w, conv_b, wcat, bcat, lam)


def _mla_weights(w_in, w_q_b, w_kv_b, q_norm, k_norm):
    half = QK_ROPE // 2

    def slab(t):
        return jnp.pad(t, [(0, 0)] * (t.ndim - 1) + [(0, LANES - QK_HEAD)])

    def rot_slab(t):
        rope = t[..., QK_NOPE:]
        swapped = jnp.concatenate([jnp.zeros_like(t[..., :QK_NOPE]), rope[..., half:], rope[..., :half]], axis=-1)
        return slab(swapped)

    kpe = jnp.pad(w_in[:, Q_LORA + KV_LORA:], ((0, 0), (QK_NOPE, 0)))
    w_in_p = jnp.concatenate([w_in[:, :Q_LORA + KV_LORA], slab(kpe), rot_slab(kpe)], axis=1).astype(BF16)
    wq = w_q_b.reshape(Q_LORA, N_HEADS, QK_HEAD)
    wq_p = slab(wq).reshape(Q_LORA, N_HEADS * LANES).astype(BF16)
    wq_rot = rot_slab(wq).reshape(Q_LORA, N_HEADS * LANES).astype(BF16)
    wkv = w_kv_b.reshape(KV_LORA, N_HEADS, QK_NOPE + V_HEAD)
    wk = jnp.pad(wkv[:, :, :QK_NOPE], ((0, 0), (0, 0), (0, LANES - QK_NOPE))).reshape(KV_LORA, N_HEADS * LANES)
    wv = wkv[:, :, QK_NOPE:].reshape(KV_LORA, N_HEADS * V_HEAD)
    w_kv_p = jnp.concatenate([wk, wv], axis=1).astype(BF16)
    gains = lambda g: jnp.stack([slab(g), rot_slab(g)], axis=0)
    return w_in_p, wq_p, wq_rot, w_kv_p, gains(q_norm), gains(k_norm)


def _rope_kernel(pos_ref, freq_ref, cos_out, sin_out):
    ang = freq_ref[...] * pos_ref[...].astype(F32)
    cos, sin = jnp.cos(ang), jnp.sin(ang)
    seq = ang.shape[1]
    fill = lambda value, n: jnp.full((n, seq), value, F32)
    cos_out[...] = jnp.concatenate([fill(1.0, QK_NOPE), cos, cos, fill(1.0, LANES - QK_HEAD)], axis=0).T
    sin_out[...] = jnp.concatenate([fill(0.0, QK_NOPE), -sin, sin, fill(0.0, LANES - QK_HEAD)], axis=0).T


def _rope_tables(positions):
    half = QK_ROPE // 2
    inv_freq = ROPE_THETA ** (-jnp.arange(half, dtype=F32) / half)
    bsz, seq = positions.shape
    table = jax.ShapeDtypeStruct((bsz, seq, LANES), F32)
    return pl.pallas_call(
        _rope_kernel,
        grid=(bsz,),
        in_specs=[pl.BlockSpec((None, 1, seq), lambda b: (b, 0, 0)), pl.BlockSpec((half, 1), lambda b: (0, 0))],
        out_specs=[pl.BlockSpec((None, seq, LANES), lambda b: (b, 0, 0))] * 2,
        out_shape=[table, table],
        compiler_params=_params("arbitrary"),
        name="rope_tables",
    )(positions.reshape(bsz, 1, seq), inv_freq.reshape(half, 1))


def _router_weights(w_router, router_bias):
    perm = (jnp.arange(N_EXPERTS) % N_GROUPS) * EXPERTS_PER_GROUP + jnp.arange(N_EXPERTS) // N_GROUPS
    w = w_router[:, perm]
    hi = w.astype(BF16)
    lo = (w - hi.astype(F32)).astype(BF16)
    z = jnp.zeros_like(hi)
    wr1 = jnp.concatenate([hi, lo, z, z], axis=1)
    wr2 = jnp.concatenate([z, z, hi, z], axis=1)
    return wr1, wr2, router_bias[perm].reshape(N_EXPERTS, 1).astype(F32)


def kernel(x, c, positions, norm_mix, norm_ffn, w_ada, b_ada, mla_w_in, mla_q_a_norm, mla_kv_a_norm, mla_w_q_b, mla_w_kv_b, mla_q_norm, mla_k_norm, mla_w_o, rnn_w_in, rnn_conv_w, rnn_conv_b, rnn_lam_f, rnn_w_rf, rnn_b_rf, rnn_w_if, rnn_b_if, rnn_lam_b, rnn_w_rb, rnn_b_rb, rnn_w_ib, rnn_b_ib, rnn_w_o, w_router, router_bias, moe_w_gu, moe_w_dn):
    bsz, seq, d = x.shape
    depth = w_ada.shape[0]
    mod = _ada(c, w_ada, b_ada)
    wr1, wr2, rbias = _router_weights(w_router, router_bias)
    cos_t, sin_t = _rope_tables(positions)
    vec = lambda v: v.reshape(1, -1)
    for i in range(depth):
        sh1, sc1, g1, sh2, sc2, g2 = [mod[i, :, k * d:(k + 1) * d].reshape(bsz, 1, d) for k in range(6)]
        j = i // 2
        if i % 2 == 0:
            w_in_p, wq, wq_rot, wkv, qn, kn = _mla_weights(mla_w_in[j], mla_w_q_b[j], mla_w_kv_b[j],
                                                           mla_q_norm[j], mla_k_norm[j])
            q, k, v = _mla_in(x, vec(norm_mix[i]), sh1, sc1, w_in_p, vec(mla_q_a_norm[j]),
                              vec(mla_kv_a_norm[j]), wq, wq_rot, wkv, qn, kn, cos_t, sin_t)
            a = _attention(q, k, v, mla_q_norm[j], mla_k_norm[j])
            hs = None
            w_o = mla_w_o[j].astype(BF16)
        else:
            a, xb = _rnn_in(x, vec(norm_mix[i]), sh1, sc1, rnn_w_in[j].astype(BF16))
            wcat = (0.5 * jnp.concatenate([rnn_w_rf[j], rnn_w_if[j], rnn_w_rb[j], rnn_w_ib[j]], axis=-1)).astype(BF16)
            bcat = jnp.stack([b.reshape(RNN_BLOCKS, RNN_BW) for b in
                              (rnn_b_rf[j], rnn_b_if[j], rnn_b_rb[j], rnn_b_ib[j])], axis=1)
            bcat = 0.5 * bcat.reshape(RNN_BLOCKS, 1, 4 * RNN_BW)
            lam = jnp.stack([rnn_lam_f[j], rnn_lam_b[j]], axis=0)
            hs = _lru(xb, rnn_conv_w[j], vec(rnn_conv_b[j]), wcat, bcat, lam)
            w_o = rnn_w_o[j].astype(BF16)
        x, h2, idx, wts = _mix_out(a, hs, x, w_o, g1, vec(norm_ffn[i]), sh2, sc2, wr1, wr2, rbias)
        x = _moe(h2, idx, wts, x, g2, moe_w_gu, moe_w_dn, i)
    return x
```

```python
import functools

import jax
import jax.numpy as jnp
from jax import lax
from jax.experimental import pallas as pl
from jax.experimental.pallas import tpu as pltpu

F32 = jnp.float32
BF16 = jnp.bfloat16

D_MODEL = 1024
N_HEADS = 16
Q_LORA = 384
KV_LORA = 256
QK_NOPE = 64
QK_ROPE = 32
QK_HEAD = QK_NOPE + QK_ROPE
V_HEAD = 64
ROPE_THETA = 10000.0
D_RNN = D_MODEL
RNN_BLOCKS = 4
RNN_BW = D_RNN // RNN_BLOCKS
CONV_W = 4
LRU_C = 8.0
N_EXPERTS = 32
N_GROUPS = 8
EXPERTS_PER_GROUP = N_EXPERTS // N_GROUPS
TOP_K = 2
D_EXPERT = 512
EPS = 1e-6
LOG2_E = 1.4426950408889634

LANES = 128
SUBLANES = 8
VMEM_LIMIT = 52 * 1024 * 1024

ROW_TILE = 512
ROW_SUB = 256
MIX_TILE = 1024
Q_TILE = 2048
Q_SUB = 256
MAX_SAFE_SHIFT = 60.0
MOE_TILE = 512
SCAN_ROWS = 256
ADA_TILE = 1536
assert MOE_TILE & (MOE_TILE - 1) == 0


def _dot(a, b):
    return jnp.dot(a, b, preferred_element_type=F32)


def _split_bf16(a):
    hi = a.astype(BF16)
    lo = (a - hi.astype(F32)).astype(BF16)
    return hi, lo


def _dot_split(a, b):
    ah, al = _split_bf16(a)
    bh, bl = _split_bf16(b)
    return _dot(ah, bh) + (_dot(ah, bl) + _dot(al, bh))


def _rms(x, gain, n):
    ms = jnp.sum(x * x, axis=-1, keepdims=True) * (1.0 / n)
    return x * lax.rsqrt(ms + EPS) * gain


def _modulate(x, gain, shift, scale):
    return _rms(x, gain, x.shape[-1]) * (1.0 + scale) + shift


def _params(*sem):
    return pltpu.CompilerParams(dimension_semantics=sem, vmem_limit_bytes=VMEM_LIMIT)


def _ada_kernel(c_ref, w_ref, b_ref, o_ref):
    c = c_ref[...]
    o_ref[...] = _dot_split(c * jax.nn.sigmoid(c), w_ref[...]) + b_ref[...]


def _ada(c, w_ada, b_ada):
    depth, d, n = w_ada.shape
    bsz = c.shape[0]
    tn = min(ADA_TILE, n)
    return pl.pallas_call(
        _ada_kernel,
        grid=(depth, n // tn),
        in_specs=[
            pl.BlockSpec((bsz, d), lambda l, j: (0, 0)),
            pl.BlockSpec((None, d, tn), lambda l, j: (l, 0, j)),
            pl.BlockSpec((None, 1, tn), lambda l, j: (l, 0, j)),
        ],
        out_specs=pl.BlockSpec((None, bsz, tn), lambda l, j: (l, 0, j)),
        out_shape=jax.ShapeDtypeStruct((depth, bsz, n), F32),
        compiler_params=_params("arbitrary", "arbitrary"),
        name="adaln_mod",
    )(c, w_ada, b_ada.reshape(depth, 1, n))


def _head_scale(s):
    return lax.rsqrt(jnp.sum(s * s, axis=-1, keepdims=True) * (1.0 / QK_HEAD) + EPS)


def _mla_in_kernel(x_ref, g_ref, sh_ref, sc_ref, win_ref, qan_ref, kvan_ref, wq_ref, wqr_ref, wkv_ref,
                   qn_ref, kn_ref, cos_ref, sin_ref, q_out, k_out, v_out):
    h = _modulate(x_ref[...], g_ref[...], sh_ref[...], sc_ref[...])
    lat = _dot(h.astype(BF16), win_ref[...])
    q_lat = lat[:, :Q_LORA]
    kv_lat = lat[:, Q_LORA:Q_LORA + KV_LORA]
    kpe = lat[:, Q_LORA + KV_LORA:Q_LORA + KV_LORA + LANES]
    kpe_rot = lat[:, Q_LORA + KV_LORA + LANES:]
    qn = _rms(q_lat, qan_ref[...], Q_LORA).astype(BF16)
    q_all = _dot(qn, wq_ref[...])
    q_rot = _dot(qn, wqr_ref[...])
    kv_all = _dot(_rms(kv_lat, kvan_ref[...], KV_LORA).astype(BF16), wkv_ref[...])
    cos_t = cos_ref[...]
    sin_t = sin_ref[...]
    q_scale = LOG2_E * QK_HEAD ** -0.5
    cq = cos_t * (qn_ref[0:1, :] * q_scale)
    sq = sin_t * (qn_ref[1:2, :] * q_scale)
    ck = cos_t * kn_ref[0:1, :]
    k_rot_term = kpe_rot * (sin_t * kn_ref[1:2, :])
    for hh in range(N_HEADS):
        sl = slice(hh * LANES, (hh + 1) * LANES)
        s = q_all[:, sl]
        q_out[hh] = ((s * cq + q_rot[:, sl] * sq) * _head_scale(s)).astype(BF16)
        s = kv_all[:, sl] + kpe
        k_out[hh] = ((s * ck + k_rot_term) * _head_scale(s)).astype(BF16)
    v_out[...] = kv_all[:, N_HEADS * LANES:].astype(BF16)


def _mla_in(x, gain, shift, scale, w_in, q_a_norm, kv_a_norm, w_q, w_q_rot, w_kv, q_norm, k_norm, cos_t, sin_t):
    bsz, seq, d = x.shape
    tm = min(ROW_TILE, seq)
    row = lambda b, i: (b, i, 0)
    per_b = lambda b, i: (b, 0, 0)
    const = lambda b, i: (0, 0)
    full = lambda a: pl.BlockSpec(a.shape, const)
    return pl.pallas_call(
        _mla_in_kernel,
        grid=(bsz, seq // tm),
        in_specs=[
            pl.BlockSpec((None, tm, d), row),
            full(gain),
            pl.BlockSpec((None, 1, d), per_b),
            pl.BlockSpec((None, 1, d), per_b),
            full(w_in), full(q_a_norm), full(kv_a_norm), full(w_q), full(w_q_rot), full(w_kv),
            full(q_norm), full(k_norm),
            pl.BlockSpec((None, tm, LANES), row),
            pl.BlockSpec((None, tm, LANES), row),
        ],
        out_specs=[
            pl.BlockSpec((None, N_HEADS, tm, LANES), lambda b, i: (b, 0, i, 0)),
            pl.BlockSpec((None, N_HEADS, tm, LANES), lambda b, i: (b, 0, i, 0)),
            pl.BlockSpec((None, tm, N_HEADS * V_HEAD), row),
        ],
        out_shape=[
            jax.ShapeDtypeStruct((bsz, N_HEADS, seq, LANES), BF16),
            jax.ShapeDtypeStruct((bsz, N_HEADS, seq, LANES), BF16),
            jax.ShapeDtypeStruct((bsz, seq, N_HEADS * V_HEAD), BF16),
        ],
        compiler_params=_params("arbitrary", "arbitrary"),
        name="mla_in",
    )(x, gain, shift, scale, w_in, q_a_norm, kv_a_norm, w_q, w_q_rot, w_kv, q_norm, k_norm, cos_t, sin_t)


def _attn_kernel(bounded, shift_ref, q_ref, k_ref, v_ref, o_ref):
    v = v_ref[...]
    lane_v = lax.broadcasted_iota(jnp.int32, v.shape, 1)
    v_heads = [jnp.where(lane_v < V_HEAD, v, jnp.ones((), BF16)), jnp.where(lane_v >= V_HEAD, v, jnp.ones((), BF16))]
    lane = lax.broadcasted_iota(jnp.int32, (Q_SUB, LANES), 1)
    for i in range(q_ref.shape[1] // Q_SUB):
        rows = slice(i * Q_SUB, (i + 1) * Q_SUB)
        outs = []
        for j in range(2):
            s = lax.dot_general(q_ref[j, rows, :], k_ref[j], (((1,), (1,)), ((), ())),
                                preferred_element_type=F32)
            m = shift_ref[0] if bounded else jnp.max(s, axis=-1, keepdims=True)
            o = _dot(jnp.exp2(s - m).astype(BF16), v_heads[j])
            denom = o[:, V_HEAD:V_HEAD + 1] if j == 0 else o[:, 0:1]
            outs.append(o / denom)
        o_ref[rows, :] = jnp.where(lane < V_HEAD, outs[0], outs[1]).astype(BF16)


def _attention_call(bounded, shift, q, k, v):
    bsz, _, seq, _ = q.shape
    tq = min(Q_TILE, seq)
    assert tq % Q_SUB == 0
    grid_spec = pltpu.PrefetchScalarGridSpec(
        num_scalar_prefetch=1,
        grid=(bsz, N_HEADS // 2, seq // tq),
        in_specs=[
            pl.BlockSpec((None, 2, tq, LANES), lambda b, h, i, *_: (b, h, i, 0)),
            pl.BlockSpec((None, 2, seq, LANES), lambda b, h, i, *_: (b, h, 0, 0)),
            pl.BlockSpec((None, seq, LANES), lambda b, h, i, *_: (b, 0, h)),
        ],
        out_specs=pl.BlockSpec((None, tq, LANES), lambda b, h, i, *_: (b, i, h)),
    )
    return pl.pallas_call(
        functools.partial(_attn_kernel, bounded),
        grid_spec=grid_spec,
        out_shape=jax.ShapeDtypeStruct((bsz, seq, N_HEADS * V_HEAD), BF16),
        compiler_params=_params("arbitrary", "arbitrary", "arbitrary"),
        name="mla_attention",
    )(shift, q, k, v)


def _attention(q, k, v, q_gain, k_gain):
    score_bound = 1.02 * LOG2_E * QK_HEAD ** 0.5 * jnp.max(jnp.abs(q_gain)) * jnp.max(jnp.abs(k_gain))
    shift = score_bound.reshape(1).astype(F32)
    return lax.cond(score_bound <= MAX_SAFE_SHIFT,
                    functools.partial(_attention_call, True), functools.partial(_attention_call, False),
                    shift, q, k, v)


def _first_index_of_max(vals):
    m = vals[0]
    for v in vals[1:]:
        m = jnp.maximum(m, v)
    idx = jnp.full(m.shape, float(len(vals) - 1), F32)
    for j in range(len(vals) - 2, -1, -1):
        idx = jnp.where(vals[j] == m, float(j), idx)
    return m, idx


def _route(h2, wr1_ref, wr2_ref, rb_ref):
    hh, hl = _split_bf16(h2)
    logits = (_dot(hh, wr1_ref[...]) + _dot(hl, wr2_ref[...])).T
    logit = logits[0:N_EXPERTS] + logits[N_EXPERTS:2 * N_EXPERTS] + logits[2 * N_EXPERTS:3 * N_EXPERTS]
    score = jax.nn.sigmoid(logit)
    biased = score + rb_ref[...]
    a = [biased[j * N_GROUPS:(j + 1) * N_GROUPS] for j in range(EXPERTS_PER_GROUP)]
    sc = [score[j * N_GROUPS:(j + 1) * N_GROUPS] for j in range(EXPERTS_PER_GROUP)]
    hi1, lo1 = jnp.maximum(a[0], a[1]), jnp.minimum(a[0], a[1])
    hi2, lo2 = jnp.maximum(a[2], a[3]), jnp.minimum(a[2], a[3])
    gscore = jnp.maximum(hi1, hi2) + jnp.maximum(jnp.minimum(hi1, hi2), jnp.maximum(lo1, lo2))
    gmax = jnp.max(gscore, axis=0, keepdims=True)
    giota = lax.broadcasted_iota(jnp.int32, gscore.shape, 0).astype(F32)
    gsel = jnp.min(jnp.where(gscore == gmax, giota, float(N_GROUPS)), axis=0, keepdims=True)
    onehot = giota == gsel
    pick = lambda t: jnp.sum(jnp.where(onehot, t, 0.0), axis=0, keepdims=True)
    bj = [pick(t) for t in a]
    sj = [pick(t) for t in sc]
    _, i1 = _first_index_of_max(bj)
    bj2 = [jnp.where(i1 == float(j), -jnp.inf, bj[j]) for j in range(EXPERTS_PER_GROUP)]
    _, i2 = _first_index_of_max(bj2)
    sel = lambda i: jnp.where(i == 0.0, sj[0], jnp.where(i == 1.0, sj[1], jnp.where(i == 2.0, sj[2], sj[3])))
    w1, w2 = sel(i1), sel(i2)
    den = w1 + w2
    base = gsel * float(EXPERTS_PER_GROUP)
    return ((base + i1).astype(jnp.int32), (base + i2).astype(jnp.int32)), (w1 / den, w2 / den)


def _mix_out_kernel(has_gate, *refs):
    if has_gate:
        a_ref, hs_ref, x_ref, wo_ref, g1_ref, g_ref, sh_ref, sc_ref, wr1_ref, wr2_ref, rb_ref, \
            x_out, h_out, idx_out, wts_out = refs
    else:
        a_ref, x_ref, wo_ref, g1_ref, g_ref, sh_ref, sc_ref, wr1_ref, wr2_ref, rb_ref, \
            x_out, h_out, idx_out, wts_out = refs
    tm = x_ref.shape[0]
    sub = min(ROW_SUB, tm)
    for i in range(tm // sub):
        rows = slice(i * sub, (i + 1) * sub)
        if has_gate:
            a = (a_ref[rows, :].astype(F32) * hs_ref[rows, :]).astype(BF16)
        else:
            a = a_ref[rows, :]
        x1 = x_ref[rows, :] + g1_ref[...] * _dot(a, wo_ref[...])
        x_out[rows, :] = x1
        h2 = _modulate(x1, g_ref[...], sh_ref[...], sc_ref[...])
        _to_tiles(h_out, h2, i * sub)
        idx, wts = _route(h2, wr1_ref, wr2_ref, rb_ref)
        for k in range(TOP_K):
            idx_out[k:k + 1, rows] = idx[k]
            wts_out[k:k + 1, rows] = wts[k]


def _mix_out(a, hs, x, w_o, gate1, gain, shift, scale, wr1, wr2, rbias):
    bsz, seq, d = x.shape
    tm = min(MIX_TILE, seq)
    row = lambda b, i: (b, i, 0)
    per_b = lambda b, i: (b, 0, 0)
    const = lambda b, i: (0, 0)
    full = lambda t: pl.BlockSpec(t.shape, const)
    vec = pl.BlockSpec((None, 1, d), per_b)
    acts = [a] if hs is None else [a, hs]
    return pl.pallas_call(
        functools.partial(_mix_out_kernel, hs is not None),
        grid=(bsz, seq // tm),
        in_specs=[pl.BlockSpec((None, tm, t.shape[-1]), row) for t in acts] + [
            pl.BlockSpec((None, tm, d), row), full(w_o), vec, full(gain), vec, vec,
            full(wr1), full(wr2), full(rbias),
        ],
        out_specs=[
            pl.BlockSpec((None, tm, d), row),
            pl.BlockSpec((None, tm * N_SUB, LANES), row),
            pl.BlockSpec((None, TOP_K, tm), lambda b, i: (b, 0, i)),
            pl.BlockSpec((None, TOP_K, tm), lambda b, i: (b, 0, i)),
        ],
        out_shape=[
            jax.ShapeDtypeStruct((bsz, seq, d), F32),
            jax.ShapeDtypeStruct((bsz, seq * N_SUB, LANES), U32),
            jax.ShapeDtypeStruct((bsz, TOP_K, seq), jnp.int32),
            jax.ShapeDtypeStruct((bsz, TOP_K, seq), F32),
        ],
        compiler_params=_params("arbitrary", "arbitrary"),
        name="mix_out_route",
    )(*acts, x, w_o, gate1, gain, shift, scale, wr1, wr2, rbias)


N_SUB = D_MODEL // (2 * LANES)
U32 = jnp.uint32
TABLE_CHUNK = 512


def _to_tiles(ref, val, lo=0):
    n, d = val.shape
    bits = lambda t: lax.bitcast_convert_type(t.astype(BF16).astype(F32), U32)
    for s in range(N_SUB):
        hi = bits(val[:, s * LANES:(s + 1) * LANES])
        lo_half = bits(val[:, d // 2 + s * LANES:d // 2 + (s + 1) * LANES])
        ref[pl.ds(lo * N_SUB + s, n, stride=N_SUB), :] = hi | (lo_half >> 16)


def _from_tiles(ref, lo, n):
    words = [ref[pl.ds(lo * N_SUB + s, n, stride=N_SUB), :] for s in range(N_SUB)]
    hi = [lax.bitcast_convert_type(w & jnp.uint32(0xFFFF0000), F32) for w in words]
    lo_half = [lax.bitcast_convert_type(w << 16, F32) for w in words]
    return jnp.concatenate(hi + lo_half, axis=1)


def _tables_kernel(idx_ref, rank_ref, cnt_ref, carry):
    @pl.when(pl.program_id(0) == 0)
    def _():
        carry[...] = jnp.zeros_like(carry)

    seq = idx_ref.shape[-1]
    ch = min(TABLE_CHUNK, seq)
    tri = jnp.where(lax.broadcasted_iota(jnp.int32, (ch, ch), 0) <= lax.broadcasted_iota(jnp.int32, (ch, ch), 1),
                    1.0, 0.0).astype(BF16)
    eiota = lax.broadcasted_iota(jnp.int32, (N_EXPERTS, ch), 0)
    cnt = carry[...]
    for k in range(TOP_K):
        for c in range(seq // ch):
            sel = eiota == idx_ref[k:k + 1, c * ch:(c + 1) * ch]
            pref = _dot(jnp.where(sel, 1.0, 0.0).astype(BF16), tri) + cnt
            rank = jnp.sum(jnp.where(sel, pref, 0.0), axis=0, keepdims=True) - 1.0
            rank_ref[k:k + 1, c * ch:(c + 1) * ch] = rank.astype(jnp.int32)
            cnt = pref[:, ch - 1:ch]
    carry[...] = cnt
    cnt_ref[...] = jnp.broadcast_to(cnt, cnt_ref.shape)


def _tables(idx):
    bsz, _, seq = idx.shape
    return pl.pallas_call(
        _tables_kernel,
        grid=(bsz,),
        in_specs=[pl.BlockSpec((None, TOP_K, seq), lambda b: (b, 0, 0))],
        out_specs=[pl.BlockSpec((None, TOP_K, seq), lambda b: (b, 0, 0)),
                   pl.BlockSpec((N_EXPERTS, LANES), lambda b: (0, 0))],
        out_shape=[jax.ShapeDtypeStruct((bsz, TOP_K, seq), jnp.int32),
                   jax.ShapeDtypeStruct((N_EXPERTS, LANES), F32)],
        scratch_shapes=[pltpu.VMEM((N_EXPERTS, 1), F32)],
        compiler_params=_params("arbitrary"),
        name="moe_tables",
    )(idx)


def _zero_runs(step, total, pad_start_ref, pad_len_ref, tail_ref):
    ops = []
    for m in range(-(-2 * N_EXPERTS // total)):
        u = step + m * total
        e = jnp.minimum(u, N_EXPERTS - 1)
        length = jnp.where(u < N_EXPERTS, pad_len_ref[e], 0)
        first = pad_start_ref[e]
        for bit in reversed(range(MOE_TILE.bit_length() - 1)):
            done = lax.shift_left(lax.shift_right_logical(length, bit + 1), bit + 1)
            ops.append((lax.bitwise_and(lax.shift_right_logical(length, bit), 1) == 1, first + done, 1 << bit))
        t = u - N_EXPERTS
        ops.append(((t >= 0) & (t < tail_ref[1]), tail_ref[0] + t * MOE_TILE, MOE_TILE))
    return ops


def _scatter_kernel(total, pad_start_ref, pad_len_ref, tail_ref, dest_ref, src_hbm, dst_hbm,
                    buf, zbuf, sem_in, sem_out, sem_z):
    n = pl.program_id(0) * pl.num_programs(1) + pl.program_id(1)
    tm = buf.shape[1] // N_SUB
    slot = lax.rem(n, 3)

    def load(step, sl):
        return pltpu.make_async_copy(src_hbm.at[pl.ds(step * tm * N_SUB, tm * N_SUB)], buf.at[sl], sem_in.at[sl])

    def drain(sl):
        for _ in range(TOP_K):
            pltpu.make_async_copy(buf.at[sl], dst_hbm.at[pl.ds(0, tm * N_SUB)], sem_out.at[sl]).wait()

    def zero_fill(step, wait):
        for pred, first, rows in _zero_runs(step, total, pad_start_ref, pad_len_ref, tail_ref):
            @pl.when(pred)
            def _(first=first, rows=rows):
                cp = pltpu.make_async_copy(zbuf.at[pl.ds(0, rows * N_SUB)],
                                           dst_hbm.at[pl.ds(pl.multiple_of(first * N_SUB, N_SUB), rows * N_SUB)], sem_z)
                cp.wait() if wait else cp.start()

    @pl.when(n == 0)
    def _():
        zbuf[...] = jnp.zeros_like(zbuf)
        load(0, 0).start()
        if total > 1:
            load(1, 1).start()

    load(n, slot).wait()
    for k in range(TOP_K):
        for c in range(tm // LANES):
            def start(j, carry, k=k, c=c):
                src = buf.at[slot, pl.ds(pl.multiple_of((c * LANES + j) * N_SUB, N_SUB), N_SUB)]
                dst = dst_hbm.at[pl.ds(pl.multiple_of(dest_ref[0, k * tm + c * LANES + j], N_SUB), N_SUB)]
                pltpu.make_async_copy(src, dst, sem_out.at[slot]).start(priority=k)
                return carry
            lax.fori_loop(0, LANES, start, 0, unroll=8)
    zero_fill(n, wait=False)

    @pl.when(n > 0)
    def _():
        drain(lax.rem(n + 2, 3))
        zero_fill(n - 1, wait=True)

    @pl.when(n + 2 < total)
    def _():
        load(n + 2, lax.rem(n + 2, 3)).start()

    @pl.when(n == total - 1)
    def _():
        drain(slot)
        zero_fill(n, wait=True)


def _index_blocks(table, tm):
    bsz, _, seq = table.shape
    nt = seq // tm
    t = table.reshape(bsz, TOP_K, nt, tm).transpose(0, 2, 1, 3)
    return t.reshape(bsz * nt, 1, TOP_K * tm), (None, 1, TOP_K * tm)


def _scatter(dest, pad_start, pad_len, tail, h2t, n_rows):
    bsz, _, seq = dest.shape
    tm = min(ROW_TILE, seq)
    nt = seq // tm
    dest4, dest_block = _index_blocks(dest, tm)
    grid_spec = pltpu.PrefetchScalarGridSpec(
        num_scalar_prefetch=3,
        grid=(bsz, nt),
        in_specs=[
            pl.BlockSpec(dest_block, lambda b, i, *_: (b * nt + i, 0, 0), memory_space=pltpu.SMEM),
            pl.BlockSpec(memory_space=pl.ANY),
        ],
        out_specs=pl.BlockSpec(memory_space=pl.ANY),
        scratch_shapes=[pltpu.VMEM((3, tm * N_SUB, LANES), U32), pltpu.VMEM((MOE_TILE * N_SUB, LANES), U32),
                        pltpu.SemaphoreType.DMA((3,)), pltpu.SemaphoreType.DMA((3,)), pltpu.SemaphoreType.DMA(())],
    )
    return pl.pallas_call(
        functools.partial(_scatter_kernel, bsz * nt),
        grid_spec=grid_spec,
        out_shape=jax.ShapeDtypeStruct((n_rows * N_SUB, LANES), U32),
        compiler_params=_params("arbitrary", "arbitrary"),
        name="moe_scatter",
    )(pad_start, pad_len, tail, dest4, h2t.reshape(bsz * seq * N_SUB, LANES))


def _expert_kernel(layer, blk_exp_ref, blk_first_ref, blk_next_ref, blk_slot_ref, n_used_ref,
                   xs_ref, wgu_hbm, wdn_hbm, ys_ref, wgu_f32, wdn_f32, wgu_bf, wdn_bf, sem):
    i = pl.program_id(0)

    def fetch(e, sl):
        return (pltpu.make_async_copy(wgu_hbm.at[layer, e], wgu_f32.at[sl], sem.at[0, sl]),
                pltpu.make_async_copy(wdn_hbm.at[layer, e], wdn_f32.at[sl], sem.at[1, sl]))

    @pl.when(i < n_used_ref[0])
    def _():
        @pl.when(blk_first_ref[i] == 1)
        def _():
            e, sl, nxt = blk_exp_ref[i], blk_slot_ref[i], blk_next_ref[i]

            @pl.when(i == 0)
            def _():
                for cp in fetch(e, sl):
                    cp.start()

            for cp in fetch(e, sl):
                cp.wait()
            wgu_bf[...] = wgu_f32[sl].astype(BF16)
            wdn_bf[...] = wdn_f32[sl].astype(BF16)

            @pl.when(nxt >= 0)
            def _():
                for cp in fetch(nxt, 1 - sl):
                    cp.start()

        x = _from_tiles(xs_ref, 0, MOE_TILE).astype(BF16)
        gu = _dot(x, wgu_bf[...])
        g = gu[:, :D_EXPERT]
        u = gu[:, D_EXPERT:]
        mid = (g * jax.nn.sigmoid(g) * u).astype(BF16)
        _to_tiles(ys_ref, _dot(mid, wdn_bf[...]))

    @pl.when(i >= n_used_ref[0])
    def _():
        ys_ref[...] = jnp.zeros_like(ys_ref)


def _experts(blk_exp, blk_first, blk_next, blk_slot, n_used, xs, w_gu, w_dn, layer):
    d = D_MODEL
    nb = xs.shape[0] // (MOE_TILE * N_SUB)
    tile = lambda i, *_: (i, 0)
    grid_spec = pltpu.PrefetchScalarGridSpec(
        num_scalar_prefetch=5,
        grid=(nb,),
        in_specs=[
            pl.BlockSpec((MOE_TILE * N_SUB, LANES), tile),
            pl.BlockSpec(memory_space=pl.ANY),
            pl.BlockSpec(memory_space=pl.ANY),
        ],
        out_specs=pl.BlockSpec((MOE_TILE * N_SUB, LANES), tile),
        scratch_shapes=[pltpu.VMEM((2, d, 2 * D_EXPERT), F32), pltpu.VMEM((2, D_EXPERT, d), F32),
                        pltpu.VMEM((d, 2 * D_EXPERT), BF16), pltpu.VMEM((D_EXPERT, d), BF16),
                        pltpu.SemaphoreType.DMA((2, 2))],
    )
    return pl.pallas_call(
        functools.partial(_expert_kernel, layer),
        grid_spec=grid_spec,
        out_shape=jax.ShapeDtypeStruct(xs.shape, U32),
        compiler_params=_params("arbitrary"),
        name="moe_experts",
    )(blk_exp, blk_first, blk_next, blk_slot, n_used, xs, w_gu, w_dn)


def _combine_kernel(dcur_ref, dnxt_ref, ys_hbm, x_ref, wts_ref, g2_ref, x_out, buf0, buf1, sem):
    nt = pl.num_programs(1)
    n = pl.program_id(0) * nt + pl.program_id(1)
    total = pl.num_programs(0) * nt
    tm = x_ref.shape[0]
    bufs = (buf0, buf1)

    def copy(d_ref, sl, k, r):
        src = ys_hbm.at[pl.ds(pl.multiple_of(d_ref[0, k * tm + r], N_SUB), N_SUB)]
        dst = bufs[sl].at[pl.ds(pl.multiple_of((k * tm + r) * N_SUB, N_SUB), N_SUB)]
        return pltpu.make_async_copy(src, dst, sem.at[sl])

    def drain(sl):
        pltpu.make_async_copy(ys_hbm.at[pl.ds(0, TOP_K * tm * N_SUB)], bufs[sl], sem.at[sl]).wait()

    @pl.when(n == 0)
    def _():
        for k in range(TOP_K):
            def start(r, carry, k=k):
                copy(dcur_ref, 0, k, r).start(priority=k)
                return carry
            lax.fori_loop(0, tm, start, 0, unroll=8)

    def step(sl):
        drain(sl)
        for r in range(tm):
            for k in range(TOP_K):
                copy(dnxt_ref, 1 - sl, k, r).start(priority=k)
        w = wts_ref[...]
        y = w[:, 0:1] * _from_tiles(bufs[sl], 0, tm) + w[:, 1:2] * _from_tiles(bufs[sl], tm, tm)
        x_out[...] = x_ref[...] + g2_ref[...] * y

        @pl.when(n == total - 1)
        def _():
            drain(1 - sl)

    for sl in range(2):
        pl.when(lax.rem(n, 2) == sl)(functools.partial(step, sl))


def _combine(dest_row, ys, x, wts_col, gate2):
    bsz, seq, d = x.shape
    tm = min(ROW_TILE, seq)
    nt = seq // tm

    def nxt(b, i):
        return (jnp.minimum(b * nt + i + 1, bsz * nt - 1), 0, 0)

    dest_row, dest_block = _index_blocks(dest_row, tm)
    return pl.pallas_call(
        _combine_kernel,
        grid=(bsz, nt),
        in_specs=[
            pl.BlockSpec(dest_block, lambda b, i: (b * nt + i, 0, 0), memory_space=pltpu.SMEM),
            pl.BlockSpec(dest_block, nxt, memory_space=pltpu.SMEM),
            pl.BlockSpec(memory_space=pl.ANY),
            pl.BlockSpec((None, tm, d), lambda b, i: (b, i, 0)),
            pl.BlockSpec((None, tm, TOP_K), lambda b, i: (b, i, 0)),
            pl.BlockSpec((None, 1, d), lambda b, i: (b, 0, 0)),
        ],
        out_specs=pl.BlockSpec((None, tm, d), lambda b, i: (b, i, 0)),
        out_shape=jax.ShapeDtypeStruct((bsz, seq, d), F32),
        scratch_shapes=[pltpu.VMEM((TOP_K * tm * N_SUB, LANES), U32), pltpu.VMEM((TOP_K * tm * N_SUB, LANES), U32),
                        pltpu.SemaphoreType.DMA((2,))],
        compiler_params=_params("arbitrary", "arbitrary"),
        name="moe_combine",
    )(dest_row, dest_row, ys, x, wts_col, gate2)


def _lookup(table, keys):
    hit = keys[..., None] == jnp.arange(table.shape[0], dtype=jnp.int32)
    return jnp.sum(jnp.where(hit, table, 0), axis=-1).astype(jnp.int32)


def _count_le(bounds, q):
    return jnp.sum((bounds <= q[..., None]).astype(jnp.int32), axis=-1)


def _moe(h2t, idx, wts, x, gate2, w_gu, w_dn, layer):
    bsz, seq, _ = x.shape
    n_rows = bsz * seq * TOP_K + N_EXPERTS * MOE_TILE
    nb = n_rows // MOE_TILE
    rank, cnt = _tables(idx)
    counts = cnt[:, 0].astype(jnp.int32)
    padded = ((counts + MOE_TILE - 1) // MOE_TILE) * MOE_TILE
    pend = jnp.cumsum(padded)
    pstart = pend - padded
    dest = _lookup(pstart, idx) + rank
    blk_row = jnp.arange(nb, dtype=jnp.int32) * MOE_TILE
    blk_exp = jnp.minimum(_count_le(pend, blk_row), N_EXPERTS - 1)
    blk_first = (blk_row == _lookup(pstart, blk_exp)).astype(jnp.int32)
    n_used = (pend[-1:] // MOE_TILE).astype(jnp.int32)
    owns = counts > 0
    eid = jnp.arange(N_EXPERTS, dtype=jnp.int32)
    later = lax.cummin(jnp.where(owns, eid, N_EXPERTS), axis=0, reverse=True)
    nxt = jnp.concatenate([later[1:], jnp.full((1,), N_EXPERTS, jnp.int32)])
    blk_next = _lookup(jnp.where(nxt < N_EXPERTS, nxt, -1), blk_exp)
    blk_slot = _lookup((jnp.cumsum(owns.astype(jnp.int32)) - 1) % 2, blk_exp)
    tail = jnp.concatenate([pend[-1:], (n_rows - pend[-1:]) // MOE_TILE]).astype(jnp.int32)
    dest_row = dest * N_SUB
    xs = _scatter(dest_row, (pstart + counts).astype(jnp.int32), (padded - counts).astype(jnp.int32), tail, h2t, n_rows)
    ys = _experts(blk_exp, blk_first, blk_next, blk_slot, n_used, xs, w_gu, w_dn, layer)
    return _combine(dest_row, ys, x, wts.transpose(0, 2, 1), gate2)


def _rnn_in_kernel(x_ref, g_ref, sh_ref, sc_ref, w_ref, gate_out, xb_out):
    h = _modulate(x_ref[...], g_ref[...], sh_ref[...], sc_ref[...])
    u = _dot(h.astype(BF16), w_ref[...])
    gate_out[...] = jax.nn.gelu(u[:, :D_RNN]).astype(BF16)
    xb_out[...] = u[:, D_RNN:]


def _rnn_in(x, gain, shift, scale, w_in):
    bsz, seq, d = x.shape
    tm = min(MIX_TILE, seq)
    row = lambda b, i: (b, i, 0)
    per_b = lambda b, i: (b, 0, 0)
    const = lambda b, i: (0, 0)
    return pl.pallas_call(
        _rnn_in_kernel,
        grid=(bsz, seq // tm),
        in_specs=[
            pl.BlockSpec((None, tm, d), row),
            pl.BlockSpec(gain.shape, const),
            pl.BlockSpec((None, 1, d), per_b),
            pl.BlockSpec((None, 1, d), per_b),
            pl.BlockSpec(w_in.shape, const),
        ],
        out_specs=[pl.BlockSpec((None, tm, D_RNN), row), pl.BlockSpec((None, tm, D_RNN), row)],
        out_shape=[jax.ShapeDtypeStruct((bsz, seq, D_RNN), BF16),
                   jax.ShapeDtypeStruct((bsz, seq, D_RNN), F32)],
        compiler_params=_params("arbitrary", "arbitrary"),
        name="rnn_in",
    )(x, gain, shift, scale, w_in)


def _lru_kernel(xb_ref, cw_ref, cb_ref, wcat_ref, bcat_ref, lam_ref, hs_ref,
                xi_ref, af_ref, bf_ref, ab_ref, bb_ref, hf_ref, hb_ref, sum_ref):
    seq, c = xb_ref.shape
    seg_len = seq // SUBLANES
    n_slab = c // LANES
    n_rows = seg_len * SUBLANES
    halo = (CONV_W // 2) * SUBLANES
    row = lax.broadcasted_iota(jnp.int32, (SUBLANES, LANES), 0)
    for sl in range(n_slab):
        lanes = slice(sl * LANES, (sl + 1) * LANES)
        for g in range(SUBLANES):
            xi_ref[sl, pl.ds(halo + g, seg_len, stride=SUBLANES), :] = xb_ref[g * seg_len:(g + 1) * seg_len, lanes]
        for back in (1, 2):
            prev = xi_ref[sl, halo + (seg_len - back) * SUBLANES:halo + (seg_len - back + 1) * SUBLANES, :]
            xi_ref[sl, halo - back * SUBLANES:halo - (back - 1) * SUBLANES, :] = jnp.where(
                row == 0, 0.0, pltpu.roll(prev, 1, 0))
        nxt = xi_ref[sl, halo:halo + SUBLANES, :]
        xi_ref[sl, halo + n_rows:halo + n_rows + SUBLANES, :] = jnp.where(
            row == SUBLANES - 1, 0.0, pltpu.roll(nxt, SUBLANES - 1, 0))

    cw = cw_ref[...]
    cb = cb_ref[...]
    lam = lam_ref[...]
    neg = -lam
    softplus = jnp.maximum(neg, 0.0) + jnp.log1p(jnp.exp(-jnp.abs(neg)))
    half_rate = (-0.5 * LRU_C) * softplus
    rows = min(SCAN_ROWS, n_rows)
    n_chunks = n_rows // rows

    for ci in range(n_chunks):
        i0 = ci * rows
        taps = []
        for k in range(CONV_W):
            lo = halo + i0 + (k - CONV_W // 2) * SUBLANES
            taps.append(jnp.concatenate([xi_ref[sl, lo:lo + rows, :] for sl in range(n_slab)], axis=1))
        xc = cb
        for k in range(CONV_W):
            xc = xc + taps[k] * cw[k:k + 1, :]
        xcb = xc.astype(BF16)
        xh = 0.5 * xc
        for dirn, (a_ref, b_ref) in enumerate(((af_ref, bf_ref), (ab_ref, bb_ref))):
            cols = slice(2 * dirn * c, 2 * (dirn + 1) * c)
            th = jnp.tanh(_dot(xcb, wcat_ref[:, cols]) + bcat_ref[:, cols])
            hr = half_rate[dirn:dirn + 1, :]
            log_a = hr * th[:, :c] + hr
            a = jnp.exp(log_a)
            m2 = jnp.tanh(log_a) * (-1.0 - a * a)
            mult = jnp.where(m2 > 0.0, m2 * lax.rsqrt(m2), 0.0)
            if dirn == 0 and ci == 0:
                mult = jnp.where(lax.broadcasted_iota(jnp.int32, mult.shape, 0) == 0, 1.0, mult)
            if dirn == 1 and ci == n_chunks - 1:
                mult = jnp.where(lax.broadcasted_iota(jnp.int32, mult.shape, 0) == rows - 1, 1.0, mult)
            b = mult * (th[:, c:] + 1.0) * xh
            for sl in range(n_slab):
                a_ref[sl, i0:i0 + rows, :] = a[:, sl * LANES:(sl + 1) * LANES]
                b_ref[sl, i0:i0 + rows, :] = b[:, sl * LANES:(sl + 1) * LANES]

    def step_rows(cidx):
        fwd = pl.ds(pl.multiple_of(cidx * SUBLANES, SUBLANES), SUBLANES)
        bwd = pl.ds(pl.multiple_of((seg_len - 1 - cidx) * SUBLANES, SUBLANES), SUBLANES)
        return fwd, bwd

    zero = jnp.zeros((SUBLANES, LANES), F32)
    one = jnp.ones((SUBLANES, LANES), F32)

    def totals(cidx, carry):
        fwd, bwd = step_rows(cidx)
        out = []
        for sl in range(n_slab):
            hf, pf, hb, pb = carry[sl]
            af, ab = af_ref[sl, fwd, :], ab_ref[sl, bwd, :]
            out.append((af * hf + bf_ref[sl, fwd, :], af * pf, ab * hb + bb_ref[sl, bwd, :], ab * pb))
        return tuple(out)
    tot = lax.fori_loop(0, seg_len, totals, tuple((zero, one, zero, one) for _ in range(n_slab)), unroll=8)

    enter = []
    for sl in range(n_slab):
        hf, pf, hb, pb = tot[sl]
        cf, cbk = zero, zero
        for _ in range(SUBLANES - 1):
            cf = jnp.where(row == 0, 0.0, pltpu.roll(hf + pf * cf, 1, 0))
            cbk = jnp.where(row == SUBLANES - 1, 0.0, pltpu.roll(hb + pb * cbk, SUBLANES - 1, 0))
        enter.append((cf, cbk))

    def states(meet, cidx, carry):
        fwd, bwd = step_rows(cidx)
        out = []
        for sl in range(n_slab):
            hf, hb = carry[sl]
            hf = af_ref[sl, fwd, :] * hf + bf_ref[sl, fwd, :]
            hb = ab_ref[sl, bwd, :] * hb + bb_ref[sl, bwd, :]
            if meet:
                sum_ref[sl, fwd, :] = hf + hb_ref[sl, fwd, :]
                sum_ref[sl, bwd, :] = hb + hf_ref[sl, bwd, :]
            else:
                hf_ref[sl, fwd, :] = hf
                hb_ref[sl, bwd, :] = hb
            out.append((hf, hb))
        return tuple(out)
    mid = lax.fori_loop(0, seg_len // 2, functools.partial(states, False), tuple(enter), unroll=8)
    lax.fori_loop(seg_len // 2, seg_len, functools.partial(states, True), mid, unroll=8)

    for g in range(SUBLANES):
        for sl in range(n_slab):
            hs_ref[g * seg_len:(g + 1) * seg_len, sl * LANES:(sl + 1) * LANES] = (
                sum_ref[sl, pl.ds(g, seg_len, stride=SUBLANES), :]).astype(BF16)


def _lru(xb, conv_w, conv_b, wcat, bcat, lam):
    bsz, seq, _ = xb.shape
    c = RNN_BW
    blk = lambda b, n: (b, 0, n)
    return pl.pallas_call(
        _lru_kernel,
        grid=(bsz, RNN_BLOCKS),
        in_specs=[
            pl.BlockSpec((None, seq, c), blk),
            pl.BlockSpec((CONV_W, c), lambda b, n: (0, n)),
            pl.BlockSpec((1, c), lambda b, n: (0, n)),
            pl.BlockSpec((None, c, 4 * c), lambda b, n: (n, 0, 0)),
            pl.BlockSpec((None, 1, 4 * c), lambda b, n: (n, 0, 0)),
            pl.BlockSpec((2, c), lambda b, n: (0, n)),
        ],
        out_specs=pl.BlockSpec((None, seq, c), blk),
        out_shape=jax.ShapeDtypeStruct((bsz, seq, D_RNN), BF16),
        scratch_shapes=[pltpu.VMEM((c // LANES, seq + (CONV_W - 1) * SUBLANES, LANES), F32)]
        + [pltpu.VMEM((c // LANES, seq, LANES), F32)] * 7,
        compiler_params=_params("arbitrary", "arbitrary"),
        name="rglru_scan",
    )(xb, conv_w, conv_b, wcat, bcat, lam)


def _mla_weights(w_in, w_q_b, w_kv_b, q_norm, k_norm):
    half = QK_ROPE // 2

    def slab(t):
        return jnp.pad(t, [(0, 0)] * (t.ndim - 1) + [(0, LANES - QK_HEAD)])

    def rot_slab(t):
        rope = t[..., QK_NOPE:]
        swapped = jnp.concatenate([jnp.zeros_like(t[..., :QK_NOPE]), rope[..., half:], rope[..., :half]], axis=-1)
        return slab(swapped)

    kpe = jnp.pad(w_in[:, Q_LORA + KV_LORA:], ((0, 0), (QK_NOPE, 0)))
    w_in_p = jnp.concatenate([w_in[:, :Q_LORA + KV_LORA], slab(kpe), rot_slab(kpe)], axis=1).astype(BF16)
    wq = w_q_b.reshape(Q_LORA, N_HEADS, QK_HEAD)
    wq_p = slab(wq).reshape(Q_LORA, N_HEADS * LANES).astype(BF16)
    wq_rot = rot_slab(wq).reshape(Q_LORA, N_HEADS * LANES).astype(BF16)
    wkv = w_kv_b.reshape(KV_LORA, N_HEADS, QK_NOPE + V_HEAD)
    wk = jnp.pad(wkv[:, :, :QK_NOPE], ((0, 0), (0, 0), (0, LANES - QK_NOPE))).reshape(KV_LORA, N_HEADS * LANES)
    wv = wkv[:, :, QK_NOPE:].reshape(KV_LORA, N_HEADS * V_HEAD)
    w_kv_p = jnp.concatenate([wk, wv], axis=1).astype(BF16)
    gains = lambda g: jnp.stack([slab(g), rot_slab(g)], axis=0)
    return w_in_p, wq_p, wq_rot, w_kv_p, gains(q_norm), gains(k_norm)


def _rope_kernel(pos_ref, freq_ref, cos_out, sin_out):
    ang = freq_ref[...] * pos_ref[...].astype(F32)
    cos, sin = jnp.cos(ang), jnp.sin(ang)
    seq = ang.shape[1]
    fill = lambda value, n: jnp.full((n, seq), value, F32)
    cos_out[...] = jnp.concatenate([fill(1.0, QK_NOPE), cos, cos, fill(1.0, LANES - QK_HEAD)], axis=0).T
    sin_out[...] = jnp.concatenate([fill(0.0, QK_NOPE), -sin, sin, fill(0.0, LANES - QK_HEAD)], axis=0).T


def _rope_tables(positions):
    half = QK_ROPE // 2
    inv_freq = ROPE_THETA ** (-jnp.arange(half, dtype=F32) / half)
    bsz, seq = positions.shape
    table = jax.ShapeDtypeStruct((bsz, seq, LANES), F32)
    return pl.pallas_call(
        _rope_kernel,
        grid=(bsz,),
        in_specs=[pl.BlockSpec((None, 1, seq), lambda b: (b, 0, 0)), pl.BlockSpec((half, 1), lambda b: (0, 0))],
        out_specs=[pl.BlockSpec((None, seq, LANES), lambda b: (b, 0, 0))] * 2,
        out_shape=[table, table],
        compiler_params=_params("arbitrary"),
        name="rope_tables",
    )(positions.reshape(bsz, 1, seq), inv_freq.reshape(half, 1))


def _router_weights(w_router, router_bias):
    perm = (jnp.arange(N_EXPERTS) % N_GROUPS) * EXPERTS_PER_GROUP + jnp.arange(N_EXPERTS) // N_GROUPS
    w = w_router[:, perm]
    hi = w.astype(BF16)
    lo = (w - hi.astype(F32)).astype(BF16)
    z = jnp.zeros_like(hi)
    wr1 = jnp.concatenate([hi, lo, z, z], axis=1)
    wr2 = jnp.concatenate([z, z, hi, z], axis=1)
    return wr1, wr2, router_bias[perm].reshape(N_EXPERTS, 1).astype(F32)


def kernel(x, c, positions, norm_mix, norm_ffn, w_ada, b_ada, mla_w_in, mla_q_a_norm, mla_kv_a_norm, mla_w_q_b, mla_w_kv_b, mla_q_norm, mla_k_norm, mla_w_o, rnn_w_in, rnn_conv_w, rnn_conv_b, rnn_lam_f, rnn_w_rf, rnn_b_rf, rnn_w_if, rnn_b_if, rnn_lam_b, rnn_w_rb, rnn_b_rb, rnn_w_ib, rnn_b_ib, rnn_w_o, w_router, router_bias, moe_w_gu, moe_w_dn):
    bsz, seq, d = x.shape
    depth = w_ada.shape[0]
    mod = _ada(c, w_ada, b_ada)
    wr1, wr2, rbias = _router_weights(w_router, router_bias)
    cos_t, sin_t = _rope_tables(positions)
    vec = lambda v: v.reshape(1, -1)
    for i in range(depth):
        sh1, sc1, g1, sh2, sc2, g2 = [mod[i, :, k * d:(k + 1) * d].reshape(bsz, 1, d) for k in range(6)]
        j = i // 2
        if i % 2 == 0:
            w_in_p, wq, wq_rot, wkv, qn, kn = _mla_weights(mla_w_in[j], mla_w_q_b[j], mla_w_kv_b[j],
                                                           mla_q_norm[j], mla_k_norm[j])
            q, k, v = _mla_in(x, vec(norm_mix[i]), sh1, sc1, w_in_p, vec(mla_q_a_norm[j]),
                              vec(mla_kv_a_norm[j]), wq, wq_rot, wkv, qn, kn, cos_t, sin_t)
            a = _attention(q, k, v, mla_q_norm[j], mla_k_norm[j])
            hs = None
            w_o = mla_w_o[j].astype(BF16)
        else:
            a, xb = _rnn_in(x, vec(norm_mix[i]), sh1, sc1, rnn_w_in[j].astype(BF16))
            wcat = (0.5 * jnp.concatenate([rnn_w_rf[j], rnn_w_if[j], rnn_w_rb[j], rnn_w_ib[j]], axis=-1)).astype(BF16)
            bcat = jnp.stack([b.reshape(RNN_BLOCKS, RNN_BW) for b in
                              (rnn_b_rf[j], rnn_b_if[j], rnn_b_rb[j], rnn_b_ib[j])], axis=1)
            bcat = 0.5 * bcat.reshape(RNN_BLOCKS, 1, 4 * RNN_BW)
            lam = jnp.stack([rnn_lam_f[j], rnn_lam_b[j]], axis=0)
            hs = _lru(xb, rnn_conv_w[j], vec(rnn_conv_b[j]), wcat, bcat, lam)
            w_o = rnn_w_o[j].astype(BF16)
        x, h2, idx, wts = _mix_out(a, hs, x, w_o, g1, vec(norm_ffn[i]), sh2, sc2, wr1, wr2, rbias)
        x = _moe(h2, idx, wts, x, g2, moe_w_gu, moe_w_dn, i)
    return x
```

```python
import functools

import jax
import jax.numpy as jnp
from jax import lax
from jax.experimental import pallas as pl
from jax.experimental.pallas import tpu as pltpu

F32 = jnp.float32
BF16 = jnp.bfloat16

D_MODEL = 1024
N_HEADS = 16
Q_LORA = 384
KV_LORA = 256
QK_NOPE = 64
QK_ROPE = 32
QK_HEAD = QK_NOPE + QK_ROPE
V_HEAD = 64
ROPE_THETA = 10000.0
D_RNN = D_MODEL
RNN_BLOCKS = 4
RNN_BW = D_RNN // RNN_BLOCKS
CONV_W = 4
LRU_C = 8.0
N_EXPERTS = 32
N_GROUPS = 8
EXPERTS_PER_GROUP = N_EXPERTS // N_GROUPS
TOP_K = 2
D_EXPERT = 512
EPS = 1e-6
LOG2_E = 1.4426950408889634

LANES = 128
SUBLANES = 8
VMEM_LIMIT = 52 * 1024 * 1024

ROW_TILE = 512
ROW_SUB = 256
MIX_TILE = 1024
Q_TILE = 2048
Q_SUB = 256
MAX_SAFE_SHIFT = 60.0
MOE_TILE = 512
SCAN_ROWS = 512
ADA_TILE = 1536
assert MOE_TILE & (MOE_TILE - 1) == 0


def _dot(a, b):
    return jnp.dot(a, b, preferred_element_type=F32)


def _split_bf16(a):
    hi = a.astype(BF16)
    lo = (a - hi.astype(F32)).astype(BF16)
    return hi, lo


def _dot_split(a, b):
    ah, al = _split_bf16(a)
    bh, bl = _split_bf16(b)
    return _dot(ah, bh) + (_dot(ah, bl) + _dot(al, bh))


def _rms(x, gain, n):
    ms = jnp.sum(x * x, axis=-1, keepdims=True) * (1.0 / n)
    return x * lax.rsqrt(ms + EPS) * gain


def _modulate(x, gain, shift, scale):
    return _rms(x, gain, x.shape[-1]) * (1.0 + scale) + shift


def _params(*sem):
    return pltpu.CompilerParams(dimension_semantics=sem, vmem_limit_bytes=VMEM_LIMIT)


def _ada_kernel(c_ref, w_ref, b_ref, o_ref):
    c = c_ref[...]
    o_ref[...] = _dot_split(c * jax.nn.sigmoid(c), w_ref[...]) + b_ref[...]


def _ada(c, w_ada, b_ada):
    depth, d, n = w_ada.shape
    bsz = c.shape[0]
    tn = min(ADA_TILE, n)
    return pl.pallas_call(
        _ada_kernel,
        grid=(depth, n // tn),
        in_specs=[
            pl.BlockSpec((bsz, d), lambda l, j: (0, 0)),
            pl.BlockSpec((None, d, tn), lambda l, j: (l, 0, j)),
            pl.BlockSpec((None, 1, tn), lambda l, j: (l, 0, j)),
        ],
        out_specs=pl.BlockSpec((None, bsz, tn), lambda l, j: (l, 0, j)),
        out_shape=jax.ShapeDtypeStruct((depth, bsz, n), F32),
        compiler_params=_params("arbitrary", "arbitrary"),
        name="adaln_mod",
    )(c, w_ada, b_ada.reshape(depth, 1, n))


def _head_scale(s):
    return lax.rsqrt(jnp.sum(s * s, axis=-1, keepdims=True) * (1.0 / QK_HEAD) + EPS)


def _mla_in_kernel(x_ref, g_ref, sh_ref, sc_ref, win_ref, qan_ref, kvan_ref, wq_ref, wqr_ref, wkv_ref,
                   qn_ref, kn_ref, cos_ref, sin_ref, q_out, k_out, v_out):
    h = _modulate(x_ref[...], g_ref[...], sh_ref[...], sc_ref[...])
    lat = _dot(h.astype(BF16), win_ref[...])
    q_lat = lat[:, :Q_LORA]
    kv_lat = lat[:, Q_LORA:Q_LORA + KV_LORA]
    kpe = lat[:, Q_LORA + KV_LORA:Q_LORA + KV_LORA + LANES]
    kpe_rot = lat[:, Q_LORA + KV_LORA + LANES:]
    qn = _rms(q_lat, qan_ref[...], Q_LORA).astype(BF16)
    q_all = _dot(qn, wq_ref[...])
    q_rot = _dot(qn, wqr_ref[...])
    kv_all = _dot(_rms(kv_lat, kvan_ref[...], KV_LORA).astype(BF16), wkv_ref[...])
    cos_t = cos_ref[...]
    sin_t = sin_ref[...]
    q_scale = LOG2_E * QK_HEAD ** -0.5
    cq = cos_t * (qn_ref[0:1, :] * q_scale)
    sq = sin_t * (qn_ref[1:2, :] * q_scale)
    ck = cos_t * kn_ref[0:1, :]
    k_rot_term = kpe_rot * (sin_t * kn_ref[1:2, :])
    for hh in range(N_HEADS):
        sl = slice(hh * LANES, (hh + 1) * LANES)
        s = q_all[:, sl]
        q_out[hh] = ((s * cq + q_rot[:, sl] * sq) * _head_scale(s)).astype(BF16)
        s = kv_all[:, sl] + kpe
        k_out[hh] = ((s * ck + k_rot_term) * _head_scale(s)).astype(BF16)
    v_out[...] = kv_all[:, N_HEADS * LANES:].astype(BF16)


def _mla_in(x, gain, shift, scale, w_in, q_a_norm, kv_a_norm, w_q, w_q_rot, w_kv, q_norm, k_norm, cos_t, sin_t):
    bsz, seq, d = x.shape
    tm = min(ROW_TILE, seq)
    row = lambda b, i: (b, i, 0)
    per_b = lambda b, i: (b, 0, 0)
    const = lambda b, i: (0, 0)
    full = lambda a: pl.BlockSpec(a.shape, const)
    return pl.pallas_call(
        _mla_in_kernel,
        grid=(bsz, seq // tm),
        in_specs=[
            pl.BlockSpec((None, tm, d), row),
            full(gain),
            pl.BlockSpec((None, 1, d), per_b),
            pl.BlockSpec((None, 1, d), per_b),
            full(w_in), full(q_a_norm), full(kv_a_norm), full(w_q), full(w_q_rot), full(w_kv),
            full(q_norm), full(k_norm),
            pl.BlockSpec((None, tm, LANES), row),
            pl.BlockSpec((None, tm, LANES), row),
        ],
        out_specs=[
            pl.BlockSpec((None, N_HEADS, tm, LANES), lambda b, i: (b, 0, i, 0)),
            pl.BlockSpec((None, N_HEADS, tm, LANES), lambda b, i: (b, 0, i, 0)),
            pl.BlockSpec((None, tm, N_HEADS * V_HEAD), row),
        ],
        out_shape=[
            jax.ShapeDtypeStruct((bsz, N_HEADS, seq, LANES), BF16),
            jax.ShapeDtypeStruct((bsz, N_HEADS, seq, LANES), BF16),
            jax.ShapeDtypeStruct((bsz, seq, N_HEADS * V_HEAD), BF16),
        ],
        compiler_params=_params("arbitrary", "arbitrary"),
        name="mla_in",
    )(x, gain, shift, scale, w_in, q_a_norm, kv_a_norm, w_q, w_q_rot, w_kv, q_norm, k_norm, cos_t, sin_t)


def _attn_kernel(bounded, shift_ref, q_ref, k_ref, v_ref, o_ref):
    v = v_ref[...]
    lane_v = lax.broadcasted_iota(jnp.int32, v.shape, 1)
    v_heads = [jnp.where(lane_v < V_HEAD, v, jnp.ones((), BF16)), jnp.where(lane_v >= V_HEAD, v, jnp.ones((), BF16))]
    lane = lax.broadcasted_iota(jnp.int32, (Q_SUB, LANES), 1)
    for i in range(q_ref.shape[1] // Q_SUB):
        rows = slice(i * Q_SUB, (i + 1) * Q_SUB)
        outs = []
        for j in range(2):
            s = lax.dot_general(q_ref[j, rows, :], k_ref[j], (((1,), (1,)), ((), ())),
                                preferred_element_type=F32)
            m = shift_ref[0] if bounded else jnp.max(s, axis=-1, keepdims=True)
            o = _dot(jnp.exp2(s - m).astype(BF16), v_heads[j])
            denom = o[:, V_HEAD:V_HEAD + 1] if j == 0 else o[:, 0:1]
            outs.append(o / denom)
        o_ref[rows, :] = jnp.where(lane < V_HEAD, outs[0], outs[1]).astype(BF16)


def _attention_call(bounded, shift, q, k, v):
    bsz, _, seq, _ = q.shape
    tq = min(Q_TILE, seq)
    assert tq % Q_SUB == 0
    grid_spec = pltpu.PrefetchScalarGridSpec(
        num_scalar_prefetch=1,
        grid=(bsz, N_HEADS // 2, seq // tq),
        in_specs=[
            pl.BlockSpec((None, 2, tq, LANES), lambda b, h, i, *_: (b, h, i, 0)),
            pl.BlockSpec((None, 2, seq, LANES), lambda b, h, i, *_: (b, h, 0, 0)),
            pl.BlockSpec((None, seq, LANES), lambda b, h, i, *_: (b, 0, h)),
        ],
        out_specs=pl.BlockSpec((None, tq, LANES), lambda b, h, i, *_: (b, i, h)),
    )
    return pl.pallas_call(
        functools.partial(_attn_kernel, bounded),
        grid_spec=grid_spec,
        out_shape=jax.ShapeDtypeStruct((bsz, seq, N_HEADS * V_HEAD), BF16),
        compiler_params=_params("arbitrary", "arbitrary", "arbitrary"),
        name="mla_attention",
    )(shift, q, k, v)


def _attention(q, k, v, q_gain, k_gain):
    score_bound = 1.02 * LOG2_E * QK_HEAD ** 0.5 * jnp.max(jnp.abs(q_gain)) * jnp.max(jnp.abs(k_gain))
    shift = score_bound.reshape(1).astype(F32)
    return lax.cond(score_bound <= MAX_SAFE_SHIFT,
                    functools.partial(_attention_call, True), functools.partial(_attention_call, False),
                    shift, q, k, v)


def _first_index_of_max(vals):
    m = vals[0]
    for v in vals[1:]:
        m = jnp.maximum(m, v)
    idx = jnp.full(m.shape, float(len(vals) - 1), F32)
    for j in range(len(vals) - 2, -1, -1):
        idx = jnp.where(vals[j] == m, float(j), idx)
    return m, idx


def _route(h2, wr1_ref, wr2_ref, rb_ref):
    hh, hl = _split_bf16(h2)
    logits = (_dot(hh, wr1_ref[...]) + _dot(hl, wr2_ref[...])).T
    logit = logits[0:N_EXPERTS] + logits[N_EXPERTS:2 * N_EXPERTS] + logits[2 * N_EXPERTS:3 * N_EXPERTS]
    score = jax.nn.sigmoid(logit)
    biased = score + rb_ref[...]
    a = [biased[j * N_GROUPS:(j + 1) * N_GROUPS] for j in range(EXPERTS_PER_GROUP)]
    sc = [score[j * N_GROUPS:(j + 1) * N_GROUPS] for j in range(EXPERTS_PER_GROUP)]
    hi1, lo1 = jnp.maximum(a[0], a[1]), jnp.minimum(a[0], a[1])
    hi2, lo2 = jnp.maximum(a[2], a[3]), jnp.minimum(a[2], a[3])
    gscore = jnp.maximum(hi1, hi2) + jnp.maximum(jnp.minimum(hi1, hi2), jnp.maximum(lo1, lo2))
    gmax = jnp.max(gscore, axis=0, keepdims=True)
    giota = lax.broadcasted_iota(jnp.int32, gscore.shape, 0).astype(F32)
    gsel = jnp.min(jnp.where(gscore == gmax, giota, float(N_GROUPS)), axis=0, keepdims=True)
    onehot = giota == gsel
    pick = lambda t: jnp.sum(jnp.where(onehot, t, 0.0), axis=0, keepdims=True)
    bj = [pick(t) for t in a]
    sj = [pick(t) for t in sc]
    _, i1 = _first_index_of_max(bj)
    bj2 = [jnp.where(i1 == float(j), -jnp.inf, bj[j]) for j in range(EXPERTS_PER_GROUP)]
    _, i2 = _first_index_of_max(bj2)
    sel = lambda i: jnp.where(i == 0.0, sj[0], jnp.where(i == 1.0, sj[1], jnp.where(i == 2.0, sj[2], sj[3])))
    w1, w2 = sel(i1), sel(i2)
    den = w1 + w2
    base = gsel * float(EXPERTS_PER_GROUP)
    return ((base + i1).astype(jnp.int32), (base + i2).astype(jnp.int32)), (w1 / den, w2 / den)


def _mix_out_kernel(has_gate, *refs):
    if has_gate:
        a_ref, hs_ref, x_ref, wo_ref, g1_ref, g_ref, sh_ref, sc_ref, wr1_ref, wr2_ref, rb_ref, \
            x_out, h_out, idx_out, wts_out = refs
    else:
        a_ref, x_ref, wo_ref, g1_ref, g_ref, sh_ref, sc_ref, wr1_ref, wr2_ref, rb_ref, \
            x_out, h_out, idx_out, wts_out = refs
    tm = x_ref.shape[0]
    sub = min(ROW_SUB, tm)
    for i in range(tm // sub):
        rows = slice(i * sub, (i + 1) * sub)
        if has_gate:
            a = (a_ref[rows, :].astype(F32) * hs_ref[rows, :]).astype(BF16)
        else:
            a = a_ref[rows, :]
        x1 = x_ref[rows, :] + g1_ref[...] * _dot(a, wo_ref[...])
        x_out[rows, :] = x1
        h2 = _modulate(x1, g_ref[...], sh_ref[...], sc_ref[...])
        _to_tiles(h_out, h2, i * sub)
        idx, wts = _route(h2, wr1_ref, wr2_ref, rb_ref)
        for k in range(TOP_K):
            idx_out[k:k + 1, rows] = idx[k]
            wts_out[k:k + 1, rows] = wts[k]


def _mix_out(a, hs, x, w_o, gate1, gain, shift, scale, wr1, wr2, rbias):
    bsz, seq, d = x.shape
    tm = min(MIX_TILE, seq)
    row = lambda b, i: (b, i, 0)
    per_b = lambda b, i: (b, 0, 0)
    const = lambda b, i: (0, 0)
    full = lambda t: pl.BlockSpec(t.shape, const)
    vec = pl.BlockSpec((None, 1, d), per_b)
    acts = [a] if hs is None else [a, hs]
    return pl.pallas_call(
        functools.partial(_mix_out_kernel, hs is not None),
        grid=(bsz, seq // tm),
        in_specs=[pl.BlockSpec((None, tm, t.shape[-1]), row) for t in acts] + [
            pl.BlockSpec((None, tm, d), row), full(w_o), vec, full(gain), vec, vec,
            full(wr1), full(wr2), full(rbias),
        ],
        out_specs=[
            pl.BlockSpec((None, tm, d), row),
            pl.BlockSpec((None, tm * N_SUB, LANES), row),
            pl.BlockSpec((None, TOP_K, tm), lambda b, i: (b, 0, i)),
            pl.BlockSpec((None, TOP_K, tm), lambda b, i: (b, 0, i)),
        ],
        out_shape=[
            jax.ShapeDtypeStruct((bsz, seq, d), F32),
            jax.ShapeDtypeStruct((bsz, seq * N_SUB, LANES), U32),
            jax.ShapeDtypeStruct((bsz, TOP_K, seq), jnp.int32),
            jax.ShapeDtypeStruct((bsz, TOP_K, seq), F32),
        ],
        compiler_params=_params("arbitrary", "arbitrary"),
        name="mix_out_route",
    )(*acts, x, w_o, gate1, gain, shift, scale, wr1, wr2, rbias)


N_SUB = D_MODEL // (2 * LANES)
U32 = jnp.uint32
TABLE_CHUNK = 512


def _to_tiles(ref, val, lo=0):
    n, d = val.shape
    bits = lambda t: lax.bitcast_convert_type(t.astype(BF16).astype(F32), U32)
    for s in range(N_SUB):
        hi = bits(val[:, s * LANES:(s + 1) * LANES])
        lo_half = bits(val[:, d // 2 + s * LANES:d // 2 + (s + 1) * LANES])
        ref[pl.ds(lo * N_SUB + s, n, stride=N_SUB), :] = hi | (lo_half >> 16)


def _from_tiles(ref, lo, n):
    words = [ref[pl.ds(lo * N_SUB + s, n, stride=N_SUB), :] for s in range(N_SUB)]
    hi = [lax.bitcast_convert_type(w & jnp.uint32(0xFFFF0000), F32) for w in words]
    lo_half = [lax.bitcast_convert_type(w << 16, F32) for w in words]
    return jnp.concatenate(hi + lo_half, axis=1)


def _tables_kernel(idx_ref, rank_ref, cnt_ref, carry):
    @pl.when(pl.program_id(0) == 0)
    def _():
        carry[...] = jnp.zeros_like(carry)

    seq = idx_ref.shape[-1]
    ch = min(TABLE_CHUNK, seq)
    tri = jnp.where(lax.broadcasted_iota(jnp.int32, (ch, ch), 0) <= lax.broadcasted_iota(jnp.int32, (ch, ch), 1),
                    1.0, 0.0).astype(BF16)
    eiota = lax.broadcasted_iota(jnp.int32, (N_EXPERTS, ch), 0)
    cnt = carry[...]
    for k in range(TOP_K):
        for c in range(seq // ch):
            sel = eiota == idx_ref[k:k + 1, c * ch:(c + 1) * ch]
            pref = _dot(jnp.where(sel, 1.0, 0.0).astype(BF16), tri) + cnt
            rank = jnp.sum(jnp.where(sel, pref, 0.0), axis=0, keepdims=True) - 1.0
            rank_ref[k:k + 1, c * ch:(c + 1) * ch] = rank.astype(jnp.int32)
            cnt = pref[:, ch - 1:ch]
    carry[...] = cnt
    cnt_ref[...] = jnp.broadcast_to(cnt, cnt_ref.shape)


def _tables(idx):
    bsz, _, seq = idx.shape
    return pl.pallas_call(
        _tables_kernel,
        grid=(bsz,),
        in_specs=[pl.BlockSpec((None, TOP_K, seq), lambda b: (b, 0, 0))],
        out_specs=[pl.BlockSpec((None, TOP_K, seq), lambda b: (b, 0, 0)),
                   pl.BlockSpec((N_EXPERTS, LANES), lambda b: (0, 0))],
        out_shape=[jax.ShapeDtypeStruct((bsz, TOP_K, seq), jnp.int32),
                   jax.ShapeDtypeStruct((N_EXPERTS, LANES), F32)],
        scratch_shapes=[pltpu.VMEM((N_EXPERTS, 1), F32)],
        compiler_params=_params("arbitrary"),
        name="moe_tables",
    )(idx)


def _zero_runs(step, total, pad_start_ref, pad_len_ref, tail_ref):
    ops = []
    for m in range(-(-2 * N_EXPERTS // total)):
        u = step + m * total
        e = jnp.minimum(u, N_EXPERTS - 1)
        length = jnp.where(u < N_EXPERTS, pad_len_ref[e], 0)
        first = pad_start_ref[e]
        for bit in reversed(range(MOE_TILE.bit_length() - 1)):
            done = lax.shift_left(lax.shift_right_logical(length, bit + 1), bit + 1)
            ops.append((lax.bitwise_and(lax.shift_right_logical(length, bit), 1) == 1, first + done, 1 << bit))
        t = u - N_EXPERTS
        ops.append(((t >= 0) & (t < tail_ref[1]), tail_ref[0] + t * MOE_TILE, MOE_TILE))
    return ops


def _scatter_kernel(total, pad_start_ref, pad_len_ref, tail_ref, dest_ref, src_hbm, dst_hbm,
                    buf, zbuf, sem_in, sem_out, sem_z):
    n = pl.program_id(0) * pl.num_programs(1) + pl.program_id(1)
    tm = buf.shape[1] // N_SUB
    slot = lax.rem(n, 3)

    def load(step, sl):
        return pltpu.make_async_copy(src_hbm.at[pl.ds(step * tm * N_SUB, tm * N_SUB)], buf.at[sl], sem_in.at[sl])

    def drain(sl):
        for _ in range(TOP_K):
            pltpu.make_async_copy(buf.at[sl], dst_hbm.at[pl.ds(0, tm * N_SUB)], sem_out.at[sl]).wait()

    def zero_fill(step, wait):
        for pred, first, rows in _zero_runs(step, total, pad_start_ref, pad_len_ref, tail_ref):
            @pl.when(pred)
            def _(first=first, rows=rows):
                cp = pltpu.make_async_copy(zbuf.at[pl.ds(0, rows * N_SUB)],
                                           dst_hbm.at[pl.ds(pl.multiple_of(first * N_SUB, N_SUB), rows * N_SUB)], sem_z)
                cp.wait() if wait else cp.start()

    @pl.when(n == 0)
    def _():
        zbuf[...] = jnp.zeros_like(zbuf)
        load(0, 0).start()
        if total > 1:
            load(1, 1).start()

    load(n, slot).wait()
    for k in range(TOP_K):
        for c in range(tm // LANES):
            def start(j, carry, k=k, c=c):
                src = buf.at[slot, pl.ds(pl.multiple_of((c * LANES + j) * N_SUB, N_SUB), N_SUB)]
                dst = dst_hbm.at[pl.ds(pl.multiple_of(dest_ref[0, k * tm + c * LANES + j], N_SUB), N_SUB)]
                pltpu.make_async_copy(src, dst, sem_out.at[slot]).start(priority=k)
                return carry
            lax.fori_loop(0, LANES, start, 0, unroll=8)
    zero_fill(n, wait=False)

    @pl.when(n > 0)
    def _():
        drain(lax.rem(n + 2, 3))
        zero_fill(n - 1, wait=True)

    @pl.when(n + 2 < total)
    def _():
        load(n + 2, lax.rem(n + 2, 3)).start()

    @pl.when(n == total - 1)
    def _():
        drain(slot)
        zero_fill(n, wait=True)


def _index_blocks(table, tm):
    bsz, _, seq = table.shape
    nt = seq // tm
    t = table.reshape(bsz, TOP_K, nt, tm).transpose(0, 2, 1, 3)
    return t.reshape(bsz * nt, 1, TOP_K * tm), (None, 1, TOP_K * tm)


def _scatter(dest, pad_start, pad_len, tail, h2t, n_rows):
    bsz, _, seq = dest.shape
    tm = min(ROW_TILE, seq)
    nt = seq // tm
    dest4, dest_block = _index_blocks(dest, tm)
    grid_spec = pltpu.PrefetchScalarGridSpec(
        num_scalar_prefetch=3,
        grid=(bsz, nt),
        in_specs=[
            pl.BlockSpec(dest_block, lambda b, i, *_: (b * nt + i, 0, 0), memory_space=pltpu.SMEM),
            pl.BlockSpec(memory_space=pl.ANY),
        ],
        out_specs=pl.BlockSpec(memory_space=pl.ANY),
        scratch_shapes=[pltpu.VMEM((3, tm * N_SUB, LANES), U32), pltpu.VMEM((MOE_TILE * N_SUB, LANES), U32),
                        pltpu.SemaphoreType.DMA((3,)), pltpu.SemaphoreType.DMA((3,)), pltpu.SemaphoreType.DMA(())],
    )
    return pl.pallas_call(
        functools.partial(_scatter_kernel, bsz * nt),
        grid_spec=grid_spec,
        out_shape=jax.ShapeDtypeStruct((n_rows * N_SUB, LANES), U32),
        compiler_params=_params("arbitrary", "arbitrary"),
        name="moe_scatter",
    )(pad_start, pad_len, tail, dest4, h2t.reshape(bsz * seq * N_SUB, LANES))


def _expert_kernel(layer, blk_exp_ref, blk_first_ref, blk_next_ref, blk_slot_ref, n_used_ref,
                   xs_ref, wgu_hbm, wdn_hbm, ys_ref, wgu_f32, wdn_f32, wgu_bf, wdn_bf, sem):
    i = pl.program_id(0)

    def fetch(e, sl):
        return (pltpu.make_async_copy(wgu_hbm.at[layer, e], wgu_f32.at[sl], sem.at[0, sl]),
                pltpu.make_async_copy(wdn_hbm.at[layer, e], wdn_f32.at[sl], sem.at[1, sl]))

    @pl.when(i < n_used_ref[0])
    def _():
        @pl.when(blk_first_ref[i] == 1)
        def _():
            e, sl, nxt = blk_exp_ref[i], blk_slot_ref[i], blk_next_ref[i]

            @pl.when(i == 0)
            def _():
                for cp in fetch(e, sl):
                    cp.start()

            for cp in fetch(e, sl):
                cp.wait()
            wgu_bf[...] = wgu_f32[sl].astype(BF16)
            wdn_bf[...] = wdn_f32[sl].astype(BF16)

            @pl.when(nxt >= 0)
            def _():
                for cp in fetch(nxt, 1 - sl):
                    cp.start()

        x = _from_tiles(xs_ref, 0, MOE_TILE).astype(BF16)
        gu = _dot(x, wgu_bf[...])
        g = gu[:, :D_EXPERT]
        u = gu[:, D_EXPERT:]
        mid = (g * jax.nn.sigmoid(g) * u).astype(BF16)
        _to_tiles(ys_ref, _dot(mid, wdn_bf[...]))

    @pl.when(i >= n_used_ref[0])
    def _():
        ys_ref[...] = jnp.zeros_like(ys_ref)


def _experts(blk_exp, blk_first, blk_next, blk_slot, n_used, xs, w_gu, w_dn, layer):
    d = D_MODEL
    nb = xs.shape[0] // (MOE_TILE * N_SUB)
    tile = lambda i, *_: (i, 0)
    grid_spec = pltpu.PrefetchScalarGridSpec(
        num_scalar_prefetch=5,
        grid=(nb,),
        in_specs=[
            pl.BlockSpec((MOE_TILE * N_SUB, LANES), tile),
            pl.BlockSpec(memory_space=pl.ANY),
            pl.BlockSpec(memory_space=pl.ANY),
        ],
        out_specs=pl.BlockSpec((MOE_TILE * N_SUB, LANES), tile),
        scratch_shapes=[pltpu.VMEM((2, d, 2 * D_EXPERT), F32), pltpu.VMEM((2, D_EXPERT, d), F32),
                        pltpu.VMEM((d, 2 * D_EXPERT), BF16), pltpu.VMEM((D_EXPERT, d), BF16),
                        pltpu.SemaphoreType.DMA((2, 2))],
    )
    return pl.pallas_call(
        functools.partial(_expert_kernel, layer),
        grid_spec=grid_spec,
        out_shape=jax.ShapeDtypeStruct(xs.shape, U32),
        compiler_params=_params("arbitrary"),
        name="moe_experts",
    )(blk_exp, blk_first, blk_next, blk_slot, n_used, xs, w_gu, w_dn)


def _combine_kernel(dcur_ref, dnxt_ref, ys_hbm, x_ref, wts_ref, g2_ref, x_out, buf0, buf1, sem):
    nt = pl.num_programs(1)
    n = pl.program_id(0) * nt + pl.program_id(1)
    total = pl.num_programs(0) * nt
    tm = x_ref.shape[0]
    bufs = (buf0, buf1)

    def copy(d_ref, sl, k, r):
        src = ys_hbm.at[pl.ds(pl.multiple_of(d_ref[0, k * tm + r], N_SUB), N_SUB)]
        dst = bufs[sl].at[pl.ds(pl.multiple_of((k * tm + r) * N_SUB, N_SUB), N_SUB)]
        return pltpu.make_async_copy(src, dst, sem.at[sl])

    def drain(sl):
        pltpu.make_async_copy(ys_hbm.at[pl.ds(0, TOP_K * tm * N_SUB)], bufs[sl], sem.at[sl]).wait()

    @pl.when(n == 0)
    def _():
        for k in range(TOP_K):
            def start(r, carry, k=k):
                copy(dcur_ref, 0, k, r).start(priority=k)
                return carry
            lax.fori_loop(0, tm, start, 0, unroll=8)

    def step(sl):
        drain(sl)
        for r in range(tm):
            for k in range(TOP_K):
                copy(dnxt_ref, 1 - sl, k, r).start(priority=k)
        w = wts_ref[...]
        y = w[:, 0:1] * _from_tiles(bufs[sl], 0, tm) + w[:, 1:2] * _from_tiles(bufs[sl], tm, tm)
        x_out[...] = x_ref[...] + g2_ref[...] * y

        @pl.when(n == total - 1)
        def _():
            drain(1 - sl)

    for sl in range(2):
        pl.when(lax.rem(n, 2) == sl)(functools.partial(step, sl))


def _combine(dest_row, ys, x, wts_col, gate2):
    bsz, seq, d = x.shape
    tm = min(ROW_TILE, seq)
    nt = seq // tm

    def nxt(b, i):
        return (jnp.minimum(b * nt + i + 1, bsz * nt - 1), 0, 0)

    dest_row, dest_block = _index_blocks(dest_row, tm)
    return pl.pallas_call(
        _combine_kernel,
        grid=(bsz, nt),
        in_specs=[
            pl.BlockSpec(dest_block, lambda b, i: (b * nt + i, 0, 0), memory_space=pltpu.SMEM),
            pl.BlockSpec(dest_block, nxt, memory_space=pltpu.SMEM),
            pl.BlockSpec(memory_space=pl.ANY),
            pl.BlockSpec((None, tm, d), lambda b, i: (b, i, 0)),
            pl.BlockSpec((None, tm, TOP_K), lambda b, i: (b, i, 0)),
            pl.BlockSpec((None, 1, d), lambda b, i: (b, 0, 0)),
        ],
        out_specs=pl.BlockSpec((None, tm, d), lambda b, i: (b, i, 0)),
        out_shape=jax.ShapeDtypeStruct((bsz, seq, d), F32),
        scratch_shapes=[pltpu.VMEM((TOP_K * tm * N_SUB, LANES), U32), pltpu.VMEM((TOP_K * tm * N_SUB, LANES), U32),
                        pltpu.SemaphoreType.DMA((2,))],
        compiler_params=_params("arbitrary", "arbitrary"),
        name="moe_combine",
    )(dest_row, dest_row, ys, x, wts_col, gate2)


def _lookup(table, keys):
    hit = keys[..., None] == jnp.arange(table.shape[0], dtype=jnp.int32)
    return jnp.sum(jnp.where(hit, table, 0), axis=-1).astype(jnp.int32)


def _count_le(bounds, q):
    return jnp.sum((bounds <= q[..., None]).astype(jnp.int32), axis=-1)


def _moe(h2t, idx, wts, x, gate2, w_gu, w_dn, layer):
    bsz, seq, _ = x.shape
    n_rows = bsz * seq * TOP_K + N_EXPERTS * MOE_TILE
    nb = n_rows // MOE_TILE
    rank, cnt = _tables(idx)
    counts = cnt[:, 0].astype(jnp.int32)
    padded = ((counts + MOE_TILE - 1) // MOE_TILE) * MOE_TILE
    pend = jnp.cumsum(padded)
    pstart = pend - padded
    dest = _lookup(pstart, idx) + rank
    blk_row = jnp.arange(nb, dtype=jnp.int32) * MOE_TILE
    blk_exp = jnp.minimum(_count_le(pend, blk_row), N_EXPERTS - 1)
    blk_first = (blk_row == _lookup(pstart, blk_exp)).astype(jnp.int32)
    n_used = (pend[-1:] // MOE_TILE).astype(jnp.int32)
    owns = counts > 0
    eid = jnp.arange(N_EXPERTS, dtype=jnp.int32)
    later = lax.cummin(jnp.where(owns, eid, N_EXPERTS), axis=0, reverse=True)
    nxt = jnp.concatenate([later[1:], jnp.full((1,), N_EXPERTS, jnp.int32)])
    blk_next = _lookup(jnp.where(nxt < N_EXPERTS, nxt, -1), blk_exp)
    blk_slot = _lookup((jnp.cumsum(owns.astype(jnp.int32)) - 1) % 2, blk_exp)
    tail = jnp.concatenate([pend[-1:], (n_rows - pend[-1:]) // MOE_TILE]).astype(jnp.int32)
    dest_row = dest * N_SUB
    xs = _scatter(dest_row, (pstart + counts).astype(jnp.int32), (padded - counts).astype(jnp.int32), tail, h2t, n_rows)
    ys = _experts(blk_exp, blk_first, blk_next, blk_slot, n_used, xs, w_gu, w_dn, layer)
    return _combine(dest_row, ys, x, wts.transpose(0, 2, 1), gate2)


def _rnn_in_kernel(x_ref, g_ref, sh_ref, sc_ref, w_ref, gate_out, xb_out):
    h = _modulate(x_ref[...], g_ref[...], sh_ref[...], sc_ref[...])
    u = _dot(h.astype(BF16), w_ref[...])
    gate_out[...] = jax.nn.gelu(u[:, :D_RNN]).astype(BF16)
    xb_out[...] = u[:, D_RNN:]


def _rnn_in(x, gain, shift, scale, w_in):
    bsz, seq, d = x.shape
    tm = min(MIX_TILE, seq)
    row = lambda b, i: (b, i, 0)
    per_b = lambda b, i: (b, 0, 0)
    const = lambda b, i: (0, 0)
    return pl.pallas_call(
        _rnn_in_kernel,
        grid=(bsz, seq // tm),
        in_specs=[
            pl.BlockSpec((None, tm, d), row),
            pl.BlockSpec(gain.shape, const),
            pl.BlockSpec((None, 1, d), per_b),
            pl.BlockSpec((None, 1, d), per_b),
            pl.BlockSpec(w_in.shape, const),
        ],
        out_specs=[pl.BlockSpec((None, tm, D_RNN), row), pl.BlockSpec((None, tm, D_RNN), row)],
        out_shape=[jax.ShapeDtypeStruct((bsz, seq, D_RNN), BF16),
                   jax.ShapeDtypeStruct((bsz, seq, D_RNN), F32)],
        compiler_params=_params("arbitrary", "arbitrary"),
        name="rnn_in",
    )(x, gain, shift, scale, w_in)


def _lru_kernel(xb_ref, cw_ref, cb_ref, wcat_ref, bcat_ref, lam_ref, hs_ref,
                xi_ref, af_ref, bf_ref, ab_ref, bb_ref, hf_ref, hb_ref, sum_ref):
    seq, c = xb_ref.shape
    seg_len = seq // SUBLANES
    n_slab = c // LANES
    n_rows = seg_len * SUBLANES
    halo = (CONV_W // 2) * SUBLANES
    row = lax.broadcasted_iota(jnp.int32, (SUBLANES, LANES), 0)
    for sl in range(n_slab):
        lanes = slice(sl * LANES, (sl + 1) * LANES)
        for g in range(SUBLANES):
            xi_ref[sl, pl.ds(halo + g, seg_len, stride=SUBLANES), :] = xb_ref[g * seg_len:(g + 1) * seg_len, lanes]
        for back in (1, 2):
            prev = xi_ref[sl, halo + (seg_len - back) * SUBLANES:halo + (seg_len - back + 1) * SUBLANES, :]
            xi_ref[sl, halo - back * SUBLANES:halo - (back - 1) * SUBLANES, :] = jnp.where(
                row == 0, 0.0, pltpu.roll(prev, 1, 0))
        nxt = xi_ref[sl, halo:halo + SUBLANES, :]
        xi_ref[sl, halo + n_rows:halo + n_rows + SUBLANES, :] = jnp.where(
            row == SUBLANES - 1, 0.0, pltpu.roll(nxt, SUBLANES - 1, 0))

    cw = cw_ref[...]
    cb = cb_ref[...]
    lam = lam_ref[...]
    neg = -lam
    softplus = jnp.maximum(neg, 0.0) + jnp.log1p(jnp.exp(-jnp.abs(neg)))
    half_rate = (-0.5 * LRU_C) * softplus
    rows = min(SCAN_ROWS, n_rows)
    n_chunks = n_rows // rows

    for ci in range(n_chunks):
        i0 = ci * rows
        taps = []
        for k in range(CONV_W):
            lo = halo + i0 + (k - CONV_W // 2) * SUBLANES
            taps.append(jnp.concatenate([xi_ref[sl, lo:lo + rows, :] for sl in range(n_slab)], axis=1))
        xc = cb
        for k in range(CONV_W):
            xc = xc + taps[k] * cw[k:k + 1, :]
        xcb = xc.astype(BF16)
        xh = 0.5 * xc
        for dirn, (a_ref, b_ref) in enumerate(((af_ref, bf_ref), (ab_ref, bb_ref))):
            cols = slice(2 * dirn * c, 2 * (dirn + 1) * c)
            th = jnp.tanh(_dot(xcb, wcat_ref[:, cols]) + bcat_ref[:, cols])
            hr = half_rate[dirn:dirn + 1, :]
            log_a = hr * th[:, :c] + hr
            a = jnp.exp(log_a)
            m2 = jnp.tanh(log_a) * (-1.0 - a * a)
            mult = jnp.where(m2 > 0.0, m2 * lax.rsqrt(m2), 0.0)
            if dirn == 0 and ci == 0:
                mult = jnp.where(lax.broadcasted_iota(jnp.int32, mult.shape, 0) == 0, 1.0, mult)
            if dirn == 1 and ci == n_chunks - 1:
                mult = jnp.where(lax.broadcasted_iota(jnp.int32, mult.shape, 0) == rows - 1, 1.0, mult)
            b = mult * (th[:, c:] + 1.0) * xh
            for sl in range(n_slab):
                a_ref[sl, i0:i0 + rows, :] = a[:, sl * LANES:(sl + 1) * LANES]
                b_ref[sl, i0:i0 + rows, :] = b[:, sl * LANES:(sl + 1) * LANES]

    def step_rows(cidx):
        fwd = pl.ds(pl.multiple_of(cidx * SUBLANES, SUBLANES), SUBLANES)
        bwd = pl.ds(pl.multiple_of((seg_len - 1 - cidx) * SUBLANES, SUBLANES), SUBLANES)
        return fwd, bwd

    zero = jnp.zeros((SUBLANES, LANES), F32)
    one = jnp.ones((SUBLANES, LANES), F32)

    def totals(cidx, carry):
        fwd, bwd = step_rows(cidx)
        out = []
        for sl in range(n_slab):
            hf, pf, hb, pb = carry[sl]
            af, ab = af_ref[sl, fwd, :], ab_ref[sl, bwd, :]
            out.append((af * hf + bf_ref[sl, fwd, :], af * pf, ab * hb + bb_ref[sl, bwd, :], ab * pb))
        return tuple(out)
    tot = lax.fori_loop(0, seg_len, totals, tuple((zero, one, zero, one) for _ in range(n_slab)), unroll=8)

    enter = []
    for sl in range(n_slab):
        hf, pf, hb, pb = tot[sl]
        cf, cbk = zero, zero
        for _ in range(SUBLANES - 1):
            cf = jnp.where(row == 0, 0.0, pltpu.roll(hf + pf * cf, 1, 0))
            cbk = jnp.where(row == SUBLANES - 1, 0.0, pltpu.roll(hb + pb * cbk, SUBLANES - 1, 0))
        enter.append((cf, cbk))

    def states(meet, cidx, carry):
        fwd, bwd = step_rows(cidx)
        out = []
        for sl in range(n_slab):
            hf, hb = carry[sl]
            hf = af_ref[sl, fwd, :] * hf + bf_ref[sl, fwd, :]
            hb = ab_ref[sl, bwd, :] * hb + bb_ref[sl, bwd, :]
            if meet:
                sum_ref[sl, fwd, :] = hf + hb_ref[sl, fwd, :]
                sum_ref[sl, bwd, :] = hb + hf_ref[sl, bwd, :]
            else:
                hf_ref[sl, fwd, :] = hf
                hb_ref[sl, bwd, :] = hb
            out.append((hf, hb))
        return tuple(out)
    mid = lax.fori_loop(0, seg_len // 2, functools.partial(states, False), tuple(enter), unroll=8)
    lax.fori_loop(seg_len // 2, seg_len, functools.partial(states, True), mid, unroll=8)

    for g in range(SUBLANES):
        for sl in range(n_slab):
            hs_ref[g * seg_len:(g + 1) * seg_len, sl * LANES:(sl + 1) * LANES] = (
                sum_ref[sl, pl.ds(g, seg_len, stride=SUBLANES), :]).astype(BF16)


def _lru(xb, conv_w, conv_b, wcat, bcat, lam):
    bsz, seq, _ = xb.shape
    c = RNN_BW
    blk = lambda b, n: (b, 0, n)
    return pl.pallas_call(
        _lru_kernel,
        grid=(bsz, RNN_BLOCKS),
        in_specs=[
            pl.BlockSpec((None, seq, c), blk),
            pl.BlockSpec((CONV_W, c), lambda b, n: (0, n)),
            pl.BlockSpec((1, c), lambda b, n: (0, n)),
            pl.BlockSpec((None, c, 4 * c), lambda b, n: (n, 0, 0)),
            pl.BlockSpec((None, 1, 4 * c), lambda b, n: (n, 0, 0)),
            pl.BlockSpec((2, c), lambda b, n: (0, n)),
        ],
        out_specs=pl.BlockSpec((None, seq, c), blk),
        out_shape=jax.ShapeDtypeStruct((bsz, seq, D_RNN), BF16),
        scratch_shapes=[pltpu.VMEM((c // LANES, seq + (CONV_W - 1) * SUBLANES, LANES), F32)]
        + [pltpu.VMEM((c // LANES, seq, LANES), F32)] * 7,
        compiler_params=_params("arbitrary", "arbitrary"),
        name="rglru_scan",
    )(xb, conv_w, conv_b, wcat, bcat, lam)


def _mla_weights(w_in, w_q_b, w_kv_b, q_norm, k_norm):
    half = QK_ROPE // 2

    def slab(t):
        return jnp.pad(t, [(0, 0)] * (t.ndim - 1) + [(0, LANES - QK_HEAD)])

    def rot_slab(t):
        rope = t[..., QK_NOPE:]
        swapped = jnp.concatenate([jnp.zeros_like(t[..., :QK_NOPE]), rope[..., half:], rope[..., :half]], axis=-1)
        return slab(swapped)

    kpe = jnp.pad(w_in[:, Q_LORA + KV_LORA:], ((0, 0), (QK_NOPE, 0)))
    w_in_p = jnp.concatenate([w_in[:, :Q_LORA + KV_LORA], slab(kpe), rot_slab(kpe)], axis=1).astype(BF16)
    wq = w_q_b.reshape(Q_LORA, N_HEADS, QK_HEAD)
    wq_p = slab(wq).reshape(Q_LORA, N_HEADS * LANES).astype(BF16)
    wq_rot = rot_slab(wq).reshape(Q_LORA, N_HEADS * LANES).astype(BF16)
    wkv = w_kv_b.reshape(KV_LORA, N_HEADS, QK_NOPE + V_HEAD)
    wk = jnp.pad(wkv[:, :, :QK_NOPE], ((0, 0), (0, 0), (0, LANES - QK_NOPE))).reshape(KV_LORA, N_HEADS * LANES)
    wv = wkv[:, :, QK_NOPE:].reshape(KV_LORA, N_HEADS * V_HEAD)
    w_kv_p = jnp.concatenate([wk, wv], axis=1).astype(BF16)
    gains = lambda g: jnp.stack([slab(g), rot_slab(g)], axis=0)
    return w_in_p, wq_p, wq_rot, w_kv_p, gains(q_norm), gains(k_norm)


def _rope_kernel(pos_ref, freq_ref, cos_out, sin_out):
    ang = freq_ref[...] * pos_ref[...].astype(F32)
    cos, sin = jnp.cos(ang), jnp.sin(ang)
    seq = ang.shape[1]
    fill = lambda value, n: jnp.full((n, seq), value, F32)
    cos_out[...] = jnp.concatenate([fill(1.0, QK_NOPE), cos, cos, fill(1.0, LANES - QK_HEAD)], axis=0).T
    sin_out[...] = jnp.concatenate([fill(0.0, QK_NOPE), -sin, sin, fill(0.0, LANES - QK_HEAD)], axis=0).T


def _rope_tables(positions):
    half = QK_ROPE // 2
    inv_freq = ROPE_THETA ** (-jnp.arange(half, dtype=F32) / half)
    bsz, seq = positions.shape
    table = jax.ShapeDtypeStruct((bsz, seq, LANES), F32)
    return pl.pallas_call(
        _rope_kernel,
        grid=(bsz,),
        in_specs=[pl.BlockSpec((None, 1, seq), lambda b: (b, 0, 0)), pl.BlockSpec((half, 1), lambda b: (0, 0))],
        out_specs=[pl.BlockSpec((None, seq, LANES), lambda b: (b, 0, 0))] * 2,
        out_shape=[table, table],
        compiler_params=_params("arbitrary"),
        name="rope_tables",
    )(positions.reshape(bsz, 1, seq), inv_freq.reshape(half, 1))


def _router_weights(w_router, router_bias):
    perm = (jnp.arange(N_EXPERTS) % N_GROUPS) * EXPERTS_PER_GROUP + jnp.arange(N_EXPERTS) // N_GROUPS
    w = w_router[:, perm]
    hi = w.astype(BF16)
    lo = (w - hi.astype(F32)).astype(BF16)
    z = jnp.zeros_like(hi)
    wr1 = jnp.concatenate([hi, lo, z, z], axis=1)
    wr2 = jnp.concatenate([z, z, hi, z], axis=1)
    return wr1, wr2, router_bias[perm].reshape(N_EXPERTS, 1).astype(F32)


def kernel(x, c, positions, norm_mix, norm_ffn, w_ada, b_ada, mla_w_in, mla_q_a_norm, mla_kv_a_norm, mla_w_q_b, mla_w_kv_b, mla_q_norm, mla_k_norm, mla_w_o, rnn_w_in, rnn_conv_w, rnn_conv_b, rnn_lam_f, rnn_w_rf, rnn_b_rf, rnn_w_if, rnn_b_if, rnn_lam_b, rnn_w_rb, rnn_b_rb, rnn_w_ib, rnn_b_ib, rnn_w_o, w_router, router_bias, moe_w_gu, moe_w_dn):
    bsz, seq, d = x.shape
    depth = w_ada.shape[0]
    mod = _ada(c, w_ada, b_ada)
    wr1, wr2, rbias = _router_weights(w_router, router_bias)
    cos_t, sin_t = _rope_tables(positions)
    vec = lambda v: v.reshape(1, -1)
    for i in range(depth):
        sh1, sc1, g1, sh2, sc2, g2 = [mod[i, :, k * d:(k + 1) * d].reshape(bsz, 1, d) for k in range(6)]
        j = i // 2
        if i % 2 == 0:
            w_in_p, wq, wq_rot, wkv, qn, kn = _mla_weights(mla_w_in[j], mla_w_q_b[j], mla_w_kv_b[j],
                                                           mla_q_norm[j], mla_k_norm[j])
            q, k, v = _mla_in(x, vec(norm_mix[i]), sh1, sc1, w_in_p, vec(mla_q_a_norm[j]),
                              vec(mla_kv_a_norm[j]), wq, wq_rot, wkv, qn, kn, cos_t, sin_t)
            a = _attention(q, k, v, mla_q_norm[j], mla_k_norm[j])
            hs = None
            w_o = mla_w_o[j].astype(BF16)
        else:
            a, xb = _rnn_in(x, vec(norm_mix[i]), sh1, sc1, rnn_w_in[j].astype(BF16))
            wcat = (0.5 * jnp.concatenate([rnn_w_rf[j], rnn_w_if[j], rnn_w_rb[j], rnn_w_ib[j]], axis=-1)).astype(BF16)
            bcat = jnp.stack([b.reshape(RNN_BLOCKS, RNN_BW) for b in
                              (rnn_b_rf[j], rnn_b_if[j], rnn_b_rb[j], rnn_b_ib[j])], axis=1)
            bcat = 0.5 * bcat.reshape(RNN_BLOCKS, 1, 4 * RNN_BW)
            lam = jnp.stack([rnn_lam_f[j], rnn_lam_b[j]], axis=0)
            hs = _lru(xb, rnn_conv_w[j], vec(rnn_conv_b[j]), wcat, bcat, lam)
            w_o = rnn_w_o[j].astype(BF16)
        x, h2, idx, wts = _mix_out(a, hs, x, w_o, g1, vec(norm_ffn[i]), sh2, sc2, wr1, wr2, rbias)
        x = _moe(h2, idx, wts, x, g2, moe_w_gu, moe_w_dn, i)
    return x
```

```python
import functools

import jax
import jax.numpy as jnp
from jax import lax
from jax.experimental import pallas as pl
from jax.experimental.pallas import tpu as pltpu

F32 = jnp.float32
BF16 = jnp.bfloat16

D_MODEL = 1024
N_HEADS = 16
Q_LORA = 384
KV_LORA = 256
QK_NOPE = 64
QK_ROPE = 32
QK_HEAD = QK_NOPE + QK_ROPE
V_HEAD = 64
ROPE_THETA = 10000.0
D_RNN = D_MODEL
RNN_BLOCKS = 4
RNN_BW = D_RNN // RNN_BLOCKS
CONV_W = 4
LRU_C = 8.0
N_EXPERTS = 32
N_GROUPS = 8
EXPERTS_PER_GROUP = N_EXPERTS // N_GROUPS
TOP_K = 2
D_EXPERT = 512
EPS = 1e-6
LOG2_E = 1.4426950408889634

LANES = 128
SUBLANES = 8
VMEM_LIMIT = 52 * 1024 * 1024

ROW_TILE = 512
ROW_SUB = 256
MIX_TILE = 1024
Q_TILE = 2048
Q_SUB = 256
MAX_SAFE_SHIFT = 60.0
MOE_TILE = 512
SCAN_ROWS = 256
ADA_TILE = 1536
assert MOE_TILE & (MOE_TILE - 1) == 0


def _dot(a, b):
    return jnp.dot(a, b, preferred_element_type=F32)


def _split_bf16(a):
    hi = a.astype(BF16)
    lo = (a - hi.astype(F32)).astype(BF16)
    return hi, lo


def _dot_split(a, b):
    ah, al = _split_bf16(a)
    bh, bl = _split_bf16(b)
    return _dot(ah, bh) + (_dot(ah, bl) + _dot(al, bh))


def _rms(x, gain, n):
    ms = jnp.sum(x * x, axis=-1, keepdims=True) * (1.0 / n)
    return x * lax.rsqrt(ms + EPS) * gain


def _modulate(x, gain, shift, scale):
    return _rms(x, gain, x.shape[-1]) * (1.0 + scale) + shift


def _params(*sem):
    return pltpu.CompilerParams(dimension_semantics=sem, vmem_limit_bytes=VMEM_LIMIT)


def _ada_kernel(c_ref, w_ref, b_ref, o_ref):
    c = c_ref[...]
    o_ref[...] = _dot_split(c * jax.nn.sigmoid(c), w_ref[...]) + b_ref[...]


def _ada(c, w_ada, b_ada):
    depth, d, n = w_ada.shape
    bsz = c.shape[0]
    tn = min(ADA_TILE, n)
    return pl.pallas_call(
        _ada_kernel,
        grid=(depth, n // tn),
        in_specs=[
            pl.BlockSpec((bsz, d), lambda l, j: (0, 0)),
            pl.BlockSpec((None, d, tn), lambda l, j: (l, 0, j)),
            pl.BlockSpec((None, 1, tn), lambda l, j: (l, 0, j)),
        ],
        out_specs=pl.BlockSpec((None, bsz, tn), lambda l, j: (l, 0, j)),
        out_shape=jax.ShapeDtypeStruct((depth, bsz, n), F32),
        compiler_params=_params("arbitrary", "arbitrary"),
        name="adaln_mod",
    )(c, w_ada, b_ada.reshape(depth, 1, n))


def _head_scale(s):
    return lax.rsqrt(jnp.sum(s * s, axis=-1, keepdims=True) * (1.0 / QK_HEAD) + EPS)


def _mla_in_kernel(x_ref, g_ref, sh_ref, sc_ref, win_ref, qan_ref, kvan_ref, wq_ref, wqr_ref, wkv_ref,
                   qn_ref, kn_ref, cos_ref, sin_ref, q_out, k_out, v_out):
    h = _modulate(x_ref[...], g_ref[...], sh_ref[...], sc_ref[...])
    lat = _dot(h.astype(BF16), win_ref[...])
    q_lat = lat[:, :Q_LORA]
    kv_lat = lat[:, Q_LORA:Q_LORA + KV_LORA]
    kpe = lat[:, Q_LORA + KV_LORA:Q_LORA + KV_LORA + LANES]
    kpe_rot = lat[:, Q_LORA + KV_LORA + LANES:]
    qn = _rms(q_lat, qan_ref[...], Q_LORA).astype(BF16)
    q_all = _dot(qn, wq_ref[...])
    q_rot = _dot(qn, wqr_ref[...])
    kv_all = _dot(_rms(kv_lat, kvan_ref[...], KV_LORA).astype(BF16), wkv_ref[...])
    cos_t = cos_ref[...]
    sin_t = sin_ref[...]
    q_scale = LOG2_E * QK_HEAD ** -0.5
    cq = cos_t * (qn_ref[0:1, :] * q_scale)
    sq = sin_t * (qn_ref[1:2, :] * q_scale)
    ck = cos_t * kn_ref[0:1, :]
    k_rot_term = kpe_rot * (sin_t * kn_ref[1:2, :])
    for hh in range(N_HEADS):
        sl = slice(hh * LANES, (hh + 1) * LANES)
        s = q_all[:, sl]
        rot = q_rot[:, (hh // 2) * LANES:(hh // 2 + 1) * LANES]
        if hh % 2:
            rot = pltpu.roll(rot, LANES - QK_ROPE, 1)
        q_out[hh] = ((s * cq + rot * sq) * _head_scale(s)).astype(BF16)
        s = kv_all[:, sl] + kpe
        k_out[hh] = ((s * ck + k_rot_term) * _head_scale(s)).astype(BF16)
    v_out[...] = kv_all[:, N_HEADS * LANES:].astype(BF16)


def _mla_in(x, gain, shift, scale, w_in, q_a_norm, kv_a_norm, w_q, w_q_rot, w_kv, q_norm, k_norm, cos_t, sin_t):
    bsz, seq, d = x.shape
    tm = min(ROW_TILE, seq)
    row = lambda b, i: (b, i, 0)
    per_b = lambda b, i: (b, 0, 0)
    const = lambda b, i: (0, 0)
    full = lambda a: pl.BlockSpec(a.shape, const)
    return pl.pallas_call(
        _mla_in_kernel,
        grid=(bsz, seq // tm),
        in_specs=[
            pl.BlockSpec((None, tm, d), row),
            full(gain),
            pl.BlockSpec((None, 1, d), per_b),
            pl.BlockSpec((None, 1, d), per_b),
            full(w_in), full(q_a_norm), full(kv_a_norm), full(w_q), full(w_q_rot), full(w_kv),
            full(q_norm), full(k_norm),
            pl.BlockSpec((None, tm, LANES), row),
            pl.BlockSpec((None, tm, LANES), row),
        ],
        out_specs=[
            pl.BlockSpec((None, N_HEADS, tm, LANES), lambda b, i: (b, 0, i, 0)),
            pl.BlockSpec((None, N_HEADS, tm, LANES), lambda b, i: (b, 0, i, 0)),
            pl.BlockSpec((None, tm, N_HEADS * V_HEAD), row),
        ],
        out_shape=[
            jax.ShapeDtypeStruct((bsz, N_HEADS, seq, LANES), BF16),
            jax.ShapeDtypeStruct((bsz, N_HEADS, seq, LANES), BF16),
            jax.ShapeDtypeStruct((bsz, seq, N_HEADS * V_HEAD), BF16),
        ],
        compiler_params=_params("arbitrary", "arbitrary"),
        name="mla_in",
    )(x, gain, shift, scale, w_in, q_a_norm, kv_a_norm, w_q, w_q_rot, w_kv, q_norm, k_norm, cos_t, sin_t)


def _attn_kernel(bounded, shift_ref, q_ref, k_ref, v_ref, o_ref):
    v = v_ref[...]
    lane_v = lax.broadcasted_iota(jnp.int32, v.shape, 1)
    v_heads = [jnp.where(lane_v < V_HEAD, v, jnp.ones((), BF16)), jnp.where(lane_v >= V_HEAD, v, jnp.ones((), BF16))]
    lane = lax.broadcasted_iota(jnp.int32, (Q_SUB, LANES), 1)
    for i in range(q_ref.shape[1] // Q_SUB):
        rows = slice(i * Q_SUB, (i + 1) * Q_SUB)
        outs = []
        for j in range(2):
            s = lax.dot_general(q_ref[j, rows, :], k_ref[j], (((1,), (1,)), ((), ())),
                                preferred_element_type=F32)
            m = shift_ref[0] if bounded else jnp.max(s, axis=-1, keepdims=True)
            o = _dot(jnp.exp2(s - m).astype(BF16), v_heads[j])
            denom = o[:, V_HEAD:V_HEAD + 1] if j == 0 else o[:, 0:1]
            outs.append(o / denom)
        o_ref[rows, :] = jnp.where(lane < V_HEAD, outs[0], outs[1]).astype(BF16)


def _attention_call(bounded, shift, q, k, v):
    bsz, _, seq, _ = q.shape
    tq = min(Q_TILE, seq)
    assert tq % Q_SUB == 0
    grid_spec = pltpu.PrefetchScalarGridSpec(
        num_scalar_prefetch=1,
        grid=(bsz, N_HEADS // 2, seq // tq),
        in_specs=[
            pl.BlockSpec((None, 2, tq, LANES), lambda b, h, i, *_: (b, h, i, 0)),
            pl.BlockSpec((None, 2, seq, LANES), lambda b, h, i, *_: (b, h, 0, 0)),
            pl.BlockSpec((None, seq, LANES), lambda b, h, i, *_: (b, 0, h)),
        ],
        out_specs=pl.BlockSpec((None, tq, LANES), lambda b, h, i, *_: (b, i, h)),
    )
    return pl.pallas_call(
        functools.partial(_attn_kernel, bounded),
        grid_spec=grid_spec,
        out_shape=jax.ShapeDtypeStruct((bsz, seq, N_HEADS * V_HEAD), BF16),
        compiler_params=_params("arbitrary", "arbitrary", "arbitrary"),
        name="mla_attention",
    )(shift, q, k, v)


def _attention(q, k, v, q_gain, k_gain):
    score_bound = 1.02 * LOG2_E * QK_HEAD ** 0.5 * jnp.max(jnp.abs(q_gain)) * jnp.max(jnp.abs(k_gain))
    shift = score_bound.reshape(1).astype(F32)
    return lax.cond(score_bound <= MAX_SAFE_SHIFT,
                    functools.partial(_attention_call, True), functools.partial(_attention_call, False),
                    shift, q, k, v)


def _first_index_of_max(vals):
    m = vals[0]
    for v in vals[1:]:
        m = jnp.maximum(m, v)
    idx = jnp.full(m.shape, float(len(vals) - 1), F32)
    for j in range(len(vals) - 2, -1, -1):
        idx = jnp.where(vals[j] == m, float(j), idx)
    return m, idx


def _route(h2, wr1_ref, wr2_ref, rb_ref):
    hh, hl = _split_bf16(h2)
    logits = (_dot(hh, wr1_ref[...]) + _dot(hl, wr2_ref[...])).T
    logit = logits[0:N_EXPERTS] + logits[N_EXPERTS:2 * N_EXPERTS] + logits[2 * N_EXPERTS:3 * N_EXPERTS]
    score = jax.nn.sigmoid(logit)
    biased = score + rb_ref[...]
    a = [biased[j * N_GROUPS:(j + 1) * N_GROUPS] for j in range(EXPERTS_PER_GROUP)]
    sc = [score[j * N_GROUPS:(j + 1) * N_GROUPS] for j in range(EXPERTS_PER_GROUP)]
    hi1, lo1 = jnp.maximum(a[0], a[1]), jnp.minimum(a[0], a[1])
    hi2, lo2 = jnp.maximum(a[2], a[3]), jnp.minimum(a[2], a[3])
    gscore = jnp.maximum(hi1, hi2) + jnp.maximum(jnp.minimum(hi1, hi2), jnp.maximum(lo1, lo2))
    gmax = jnp.max(gscore, axis=0, keepdims=True)
    giota = lax.broadcasted_iota(jnp.int32, gscore.shape, 0).astype(F32)
    gsel = jnp.min(jnp.where(gscore == gmax, giota, float(N_GROUPS)), axis=0, keepdims=True)
    onehot = giota == gsel
    pick = lambda t: jnp.sum(jnp.where(onehot, t, 0.0), axis=0, keepdims=True)
    bj = [pick(t) for t in a]
    sj = [pick(t) for t in sc]
    _, i1 = _first_index_of_max(bj)
    bj2 = [jnp.where(i1 == float(j), -jnp.inf, bj[j]) for j in range(EXPERTS_PER_GROUP)]
    _, i2 = _first_index_of_max(bj2)
    sel = lambda i: jnp.where(i == 0.0, sj[0], jnp.where(i == 1.0, sj[1], jnp.where(i == 2.0, sj[2], sj[3])))
    w1, w2 = sel(i1), sel(i2)
    den = w1 + w2
    base = gsel * float(EXPERTS_PER_GROUP)
    return ((base + i1).astype(jnp.int32), (base + i2).astype(jnp.int32)), (w1 / den, w2 / den)


def _mix_out_kernel(has_gate, *refs):
    if has_gate:
        a_ref, hs_ref, x_ref, wo_ref, g1_ref, g_ref, sh_ref, sc_ref, wr1_ref, wr2_ref, rb_ref, \
            x_out, h_out, idx_out, wts_out = refs
    else:
        a_ref, x_ref, wo_ref, g1_ref, g_ref, sh_ref, sc_ref, wr1_ref, wr2_ref, rb_ref, \
            x_out, h_out, idx_out, wts_out = refs
    tm = x_ref.shape[0]
    sub = min(ROW_SUB, tm)
    for i in range(tm // sub):
        rows = slice(i * sub, (i + 1) * sub)
        if has_gate:
            a = (a_ref[rows, :].astype(F32) * hs_ref[rows, :]).astype(BF16)
        else:
            a = a_ref[rows, :]
        x1 = x_ref[rows, :] + g1_ref[...] * _dot(a, wo_ref[...])
        x_out[rows, :] = x1
        h2 = _modulate(x1, g_ref[...], sh_ref[...], sc_ref[...])
        _to_tiles(h_out, h2, i * sub)
        idx, wts = _route(h2, wr1_ref, wr2_ref, rb_ref)
        for k in range(TOP_K):
            idx_out[k:k + 1, rows] = idx[k]
            wts_out[k:k + 1, rows] = wts[k]


def _mix_out(a, hs, x, w_o, gate1, gain, shift, scale, wr1, wr2, rbias):
    bsz, seq, d = x.shape
    tm = min(MIX_TILE, seq)
    row = lambda b, i: (b, i, 0)
    per_b = lambda b, i: (b, 0, 0)
    const = lambda b, i: (0, 0)
    full = lambda t: pl.BlockSpec(t.shape, const)
    vec = pl.BlockSpec((None, 1, d), per_b)
    acts = [a] if hs is None else [a, hs]
    return pl.pallas_call(
        functools.partial(_mix_out_kernel, hs is not None),
        grid=(bsz, seq // tm),
        in_specs=[pl.BlockSpec((None, tm, t.shape[-1]), row) for t in acts] + [
            pl.BlockSpec((None, tm, d), row), full(w_o), vec, full(gain), vec, vec,
            full(wr1), full(wr2), full(rbias),
        ],
        out_specs=[
            pl.BlockSpec((None, tm, d), row),
            pl.BlockSpec((None, tm * N_SUB, LANES), row),
            pl.BlockSpec((None, TOP_K, tm), lambda b, i: (b, 0, i)),
            pl.BlockSpec((None, TOP_K, tm), lambda b, i: (b, 0, i)),
        ],
        out_shape=[
            jax.ShapeDtypeStruct((bsz, seq, d), F32),
            jax.ShapeDtypeStruct((bsz, seq * N_SUB, LANES), U32),
            jax.ShapeDtypeStruct((bsz, TOP_K, seq), jnp.int32),
            jax.ShapeDtypeStruct((bsz, TOP_K, seq), F32),
        ],
        compiler_params=_params("arbitrary", "arbitrary"),
        name="mix_out_route",
    )(*acts, x, w_o, gate1, gain, shift, scale, wr1, wr2, rbias)


N_SUB = D_MODEL // (2 * LANES)
U32 = jnp.uint32
TABLE_CHUNK = 512


def _to_tiles(ref, val, lo=0):
    n, d = val.shape
    bits = lambda t: lax.bitcast_convert_type(t.astype(BF16).astype(F32), U32)
    for s in range(N_SUB):
        hi = bits(val[:, s * LANES:(s + 1) * LANES])
        lo_half = bits(val[:, d // 2 + s * LANES:d // 2 + (s + 1) * LANES])
        ref[pl.ds(lo * N_SUB + s, n, stride=N_SUB), :] = hi | (lo_half >> 16)


def _from_tiles(ref, lo, n):
    words = [ref[pl.ds(lo * N_SUB + s, n, stride=N_SUB), :] for s in range(N_SUB)]
    hi = [lax.bitcast_convert_type(w & jnp.uint32(0xFFFF0000), F32) for w in words]
    lo_half = [lax.bitcast_convert_type(w << 16, F32) for w in words]
    return jnp.concatenate(hi + lo_half, axis=1)


def _tables_kernel(idx_ref, rank_ref, cnt_ref, carry):
    @pl.when(pl.program_id(0) == 0)
    def _():
        carry[...] = jnp.zeros_like(carry)

    seq = idx_ref.shape[-1]
    ch = min(TABLE_CHUNK, seq)
    tri = jnp.where(lax.broadcasted_iota(jnp.int32, (ch, ch), 0) <= lax.broadcasted_iota(jnp.int32, (ch, ch), 1),
                    1.0, 0.0).astype(BF16)
    eiota = lax.broadcasted_iota(jnp.int32, (N_EXPERTS, ch), 0)
    cnt = carry[...]
    for k in range(TOP_K):
        for c in range(seq // ch):
            sel = eiota == idx_ref[k:k + 1, c * ch:(c + 1) * ch]
            pref = _dot(jnp.where(sel, 1.0, 0.0).astype(BF16), tri) + cnt
            rank = jnp.sum(jnp.where(sel, pref, 0.0), axis=0, keepdims=True) - 1.0
            rank_ref[k:k + 1, c * ch:(c + 1) * ch] = rank.astype(jnp.int32)
            cnt = pref[:, ch - 1:ch]
    carry[...] = cnt
    cnt_ref[...] = jnp.broadcast_to(cnt, cnt_ref.shape)


def _tables(idx):
    bsz, _, seq = idx.shape
    return pl.pallas_call(
        _tables_kernel,
        grid=(bsz,),
        in_specs=[pl.BlockSpec((None, TOP_K, seq), lambda b: (b, 0, 0))],
        out_specs=[pl.BlockSpec((None, TOP_K, seq), lambda b: (b, 0, 0)),
                   pl.BlockSpec((N_EXPERTS, LANES), lambda b: (0, 0))],
        out_shape=[jax.ShapeDtypeStruct((bsz, TOP_K, seq), jnp.int32),
                   jax.ShapeDtypeStruct((N_EXPERTS, LANES), F32)],
        scratch_shapes=[pltpu.VMEM((N_EXPERTS, 1), F32)],
        compiler_params=_params("arbitrary"),
        name="moe_tables",
    )(idx)


def _zero_runs(step, total, pad_start_ref, pad_len_ref, tail_ref):
    ops = []
    for m in range(-(-2 * N_EXPERTS // total)):
        u = step + m * total
        e = jnp.minimum(u, N_EXPERTS - 1)
        length = jnp.where(u < N_EXPERTS, pad_len_ref[e], 0)
        first = pad_start_ref[e]
        for bit in reversed(range(MOE_TILE.bit_length() - 1)):
            done = lax.shift_left(lax.shift_right_logical(length, bit + 1), bit + 1)
            ops.append((lax.bitwise_and(lax.shift_right_logical(length, bit), 1) == 1, first + done, 1 << bit))
        t = u - N_EXPERTS
        ops.append(((t >= 0) & (t < tail_ref[1]), tail_ref[0] + t * MOE_TILE, MOE_TILE))
    return ops


def _scatter_kernel(total, pad_start_ref, pad_len_ref, tail_ref, dest_ref, src_hbm, dst_hbm,
                    buf, zbuf, sem_in, sem_out, sem_z):
    n = pl.program_id(0) * pl.num_programs(1) + pl.program_id(1)
    tm = buf.shape[1] // N_SUB
    slot = lax.rem(n, 3)

    def load(step, sl):
        return pltpu.make_async_copy(src_hbm.at[pl.ds(step * tm * N_SUB, tm * N_SUB)], buf.at[sl], sem_in.at[sl])

    def drain(sl):
        for _ in range(TOP_K):
            pltpu.make_async_copy(buf.at[sl], dst_hbm.at[pl.ds(0, tm * N_SUB)], sem_out.at[sl]).wait()

    def zero_fill(step, wait):
        for pred, first, rows in _zero_runs(step, total, pad_start_ref, pad_len_ref, tail_ref):
            @pl.when(pred)
            def _(first=first, rows=rows):
                cp = pltpu.make_async_copy(zbuf.at[pl.ds(0, rows * N_SUB)],
                                           dst_hbm.at[pl.ds(pl.multiple_of(first * N_SUB, N_SUB), rows * N_SUB)], sem_z)
                cp.wait() if wait else cp.start()

    @pl.when(n == 0)
    def _():
        zbuf[...] = jnp.zeros_like(zbuf)
        load(0, 0).start()
        if total > 1:
            load(1, 1).start()

    load(n, slot).wait()
    for k in range(TOP_K):
        for c in range(tm // LANES):
            def start(j, carry, k=k, c=c):
                src = buf.at[slot, pl.ds(pl.multiple_of((c * LANES + j) * N_SUB, N_SUB), N_SUB)]
                dst = dst_hbm.at[pl.ds(pl.multiple_of(dest_ref[0, k * tm + c * LANES + j], N_SUB), N_SUB)]
                pltpu.make_async_copy(src, dst, sem_out.at[slot]).start(priority=k)
                return carry
            lax.fori_loop(0, LANES, start, 0, unroll=8)
    zero_fill(n, wait=False)

    @pl.when(n > 0)
    def _():
        drain(lax.rem(n + 2, 3))
        zero_fill(n - 1, wait=True)

    @pl.when(n + 2 < total)
    def _():
        load(n + 2, lax.rem(n + 2, 3)).start()

    @pl.when(n == total - 1)
    def _():
        drain(slot)
        zero_fill(n, wait=True)


def _index_blocks(table, tm):
    bsz, _, seq = table.shape
    nt = seq // tm
    t = table.reshape(bsz, TOP_K, nt, tm).transpose(0, 2, 1, 3)
    return t.reshape(bsz * nt, 1, TOP_K * tm), (None, 1, TOP_K * tm)


def _scatter(dest, pad_start, pad_len, tail, h2t, n_rows):
    bsz, _, seq = dest.shape
    tm = min(ROW_TILE, seq)
    nt = seq // tm
    dest4, dest_block = _index_blocks(dest, tm)
    grid_spec = pltpu.PrefetchScalarGridSpec(
        num_scalar_prefetch=3,
        grid=(bsz, nt),
        in_specs=[
            pl.BlockSpec(dest_block, lambda b, i, *_: (b * nt + i, 0, 0), memory_space=pltpu.SMEM),
            pl.BlockSpec(memory_space=pl.ANY),
        ],
        out_specs=pl.BlockSpec(memory_space=pl.ANY),
        scratch_shapes=[pltpu.VMEM((3, tm * N_SUB, LANES), U32), pltpu.VMEM((MOE_TILE * N_SUB, LANES), U32),
                        pltpu.SemaphoreType.DMA((3,)), pltpu.SemaphoreType.DMA((3,)), pltpu.SemaphoreType.DMA(())],
    )
    return pl.pallas_call(
        functools.partial(_scatter_kernel, bsz * nt),
        grid_spec=grid_spec,
        out_shape=jax.ShapeDtypeStruct((n_rows * N_SUB, LANES), U32),
        compiler_params=_params("arbitrary", "arbitrary"),
        name="moe_scatter",
    )(pad_start, pad_len, tail, dest4, h2t.reshape(bsz * seq * N_SUB, LANES))


def _expert_kernel(layer, blk_exp_ref, blk_first_ref, blk_next_ref, blk_slot_ref, n_used_ref,
                   xs_ref, wgu_hbm, wdn_hbm, ys_ref, wgu_f32, wdn_f32, wgu_bf, wdn_bf, sem):
    i = pl.program_id(0)

    def fetch(e, sl):
        return (pltpu.make_async_copy(wgu_hbm.at[layer, e], wgu_f32.at[sl], sem.at[0, sl]),
                pltpu.make_async_copy(wdn_hbm.at[layer, e], wdn_f32.at[sl], sem.at[1, sl]))

    @pl.when(i < n_used_ref[0])
    def _():
        @pl.when(blk_first_ref[i] == 1)
        def _():
            e, sl, nxt = blk_exp_ref[i], blk_slot_ref[i], blk_next_ref[i]

            @pl.when(i == 0)
            def _():
                for cp in fetch(e, sl):
                    cp.start()

            for cp in fetch(e, sl):
                cp.wait()
            wgu_bf[...] = wgu_f32[sl].astype(BF16)
            wdn_bf[...] = wdn_f32[sl].astype(BF16)

            @pl.when(nxt >= 0)
            def _():
                for cp in fetch(nxt, 1 - sl):
                    cp.start()

        x = _from_tiles(xs_ref, 0, MOE_TILE).astype(BF16)
        gu = _dot(x, wgu_bf[...])
        g = gu[:, :D_EXPERT]
        u = gu[:, D_EXPERT:]
        mid = (g * jax.nn.sigmoid(g) * u).astype(BF16)
        _to_tiles(ys_ref, _dot(mid, wdn_bf[...]))

    @pl.when(i >= n_used_ref[0])
    def _():
        ys_ref[...] = jnp.zeros_like(ys_ref)


def _experts(blk_exp, blk_first, blk_next, blk_slot, n_used, xs, w_gu, w_dn, layer):
    d = D_MODEL
    nb = xs.shape[0] // (MOE_TILE * N_SUB)
    tile = lambda i, *_: (i, 0)
    grid_spec = pltpu.PrefetchScalarGridSpec(
        num_scalar_prefetch=5,
        grid=(nb,),
        in_specs=[
            pl.BlockSpec((MOE_TILE * N_SUB, LANES), tile),
            pl.BlockSpec(memory_space=pl.ANY),
            pl.BlockSpec(memory_space=pl.ANY),
        ],
        out_specs=pl.BlockSpec((MOE_TILE * N_SUB, LANES), tile),
        scratch_shapes=[pltpu.VMEM((2, d, 2 * D_EXPERT), F32), pltpu.VMEM((2, D_EXPERT, d), F32),
                        pltpu.VMEM((d, 2 * D_EXPERT), BF16), pltpu.VMEM((D_EXPERT, d), BF16),
                        pltpu.SemaphoreType.DMA((2, 2))],
    )
    return pl.pallas_call(
        functools.partial(_expert_kernel, layer),
        grid_spec=grid_spec,
        out_shape=jax.ShapeDtypeStruct(xs.shape, U32),
        compiler_params=_params("arbitrary"),
        name="moe_experts",
    )(blk_exp, blk_first, blk_next, blk_slot, n_used, xs, w_gu, w_dn)


def _combine_kernel(dcur_ref, dnxt_ref, ys_hbm, x_ref, wts_ref, g2_ref, x_out, buf0, buf1, sem):
    nt = pl.num_programs(1)
    n = pl.program_id(0) * nt + pl.program_id(1)
    total = pl.num_programs(0) * nt
    tm = x_ref.shape[0]
    bufs = (buf0, buf1)

    def copy(d_ref, sl, k, r):
        src = ys_hbm.at[pl.ds(pl.multiple_of(d_ref[0, k * tm + r], N_SUB), N_SUB)]
        dst = bufs[sl].at[pl.ds(pl.multiple_of((k * tm + r) * N_SUB, N_SUB), N_SUB)]
        return pltpu.make_async_copy(src, dst, sem.at[sl])

    def drain(sl):
        pltpu.make_async_copy(ys_hbm.at[pl.ds(0, TOP_K * tm * N_SUB)], bufs[sl], sem.at[sl]).wait()

    @pl.when(n == 0)
    def _():
        for k in range(TOP_K):
            def start(r, carry, k=k):
                copy(dcur_ref, 0, k, r).start(priority=k)
                return carry
            lax.fori_loop(0, tm, start, 0, unroll=8)

    def step(sl):
        drain(sl)
        for r in range(tm):
            for k in range(TOP_K):
                copy(dnxt_ref, 1 - sl, k, r).start(priority=k)
        w = wts_ref[...]
        y = w[:, 0:1] * _from_tiles(bufs[sl], 0, tm) + w[:, 1:2] * _from_tiles(bufs[sl], tm, tm)
        x_out[...] = x_ref[...] + g2_ref[...] * y

        @pl.when(n == total - 1)
        def _():
            drain(1 - sl)

    for sl in range(2):
        pl.when(lax.rem(n, 2) == sl)(functools.partial(step, sl))


def _combine(dest_row, ys, x, wts_col, gate2):
    bsz, seq, d = x.shape
    tm = min(ROW_TILE, seq)
    nt = seq // tm

    def nxt(b, i):
        return (jnp.minimum(b * nt + i + 1, bsz * nt - 1), 0, 0)

    dest_row, dest_block = _index_blocks(dest_row, tm)
    return pl.pallas_call(
        _combine_kernel,
        grid=(bsz, nt),
        in_specs=[
            pl.BlockSpec(dest_block, lambda b, i: (b * nt + i, 0, 0), memory_space=pltpu.SMEM),
            pl.BlockSpec(dest_block, nxt, memory_space=pltpu.SMEM),
            pl.BlockSpec(memory_space=pl.ANY),
            pl.BlockSpec((None, tm, d), lambda b, i: (b, i, 0)),
            pl.BlockSpec((None, tm, TOP_K), lambda b, i: (b, i, 0)),
            pl.BlockSpec((None, 1, d), lambda b, i: (b, 0, 0)),
        ],
        out_specs=pl.BlockSpec((None, tm, d), lambda b, i: (b, i, 0)),
        out_shape=jax.ShapeDtypeStruct((bsz, seq, d), F32),
        scratch_shapes=[pltpu.VMEM((TOP_K * tm * N_SUB, LANES), U32), pltpu.VMEM((TOP_K * tm * N_SUB, LANES), U32),
                        pltpu.SemaphoreType.DMA((2,))],
        compiler_params=_params("arbitrary", "arbitrary"),
        name="moe_combine",
    )(dest_row, dest_row, ys, x, wts_col, gate2)


def _lookup(table, keys):
    hit = keys[..., None] == jnp.arange(table.shape[0], dtype=jnp.int32)
    return jnp.sum(jnp.where(hit, table, 0), axis=-1).astype(jnp.int32)


def _count_le(bounds, q):
    return jnp.sum((bounds <= q[..., None]).astype(jnp.int32), axis=-1)


def _moe(h2t, idx, wts, x, gate2, w_gu, w_dn, layer):
    bsz, seq, _ = x.shape
    n_rows = bsz * seq * TOP_K + N_EXPERTS * MOE_TILE
    nb = n_rows // MOE_TILE
    rank, cnt = _tables(idx)
    counts = cnt[:, 0].astype(jnp.int32)
    padded = ((counts + MOE_TILE - 1) // MOE_TILE) * MOE_TILE
    pend = jnp.cumsum(padded)
    pstart = pend - padded
    dest = _lookup(pstart, idx) + rank
    blk_row = jnp.arange(nb, dtype=jnp.int32) * MOE_TILE
    blk_exp = jnp.minimum(_count_le(pend, blk_row), N_EXPERTS - 1)
    blk_first = (blk_row == _lookup(pstart, blk_exp)).astype(jnp.int32)
    n_used = (pend[-1:] // MOE_TILE).astype(jnp.int32)
    owns = counts > 0
    eid = jnp.arange(N_EXPERTS, dtype=jnp.int32)
    later = lax.cummin(jnp.where(owns, eid, N_EXPERTS), axis=0, reverse=True)
    nxt = jnp.concatenate([later[1:], jnp.full((1,), N_EXPERTS, jnp.int32)])
    blk_next = _lookup(jnp.where(nxt < N_EXPERTS, nxt, -1), blk_exp)
    blk_slot = _lookup((jnp.cumsum(owns.astype(jnp.int32)) - 1) % 2, blk_exp)
    tail = jnp.concatenate([pend[-1:], (n_rows - pend[-1:]) // MOE_TILE]).astype(jnp.int32)
    dest_row = dest * N_SUB
    xs = _scatter(dest_row, (pstart + counts).astype(jnp.int32), (padded - counts).astype(jnp.int32), tail, h2t, n_rows)
    ys = _experts(blk_exp, blk_first, blk_next, blk_slot, n_used, xs, w_gu, w_dn, layer)
    return _combine(dest_row, ys, x, wts.transpose(0, 2, 1), gate2)


def _rnn_in_kernel(x_ref, g_ref, sh_ref, sc_ref, w_ref, gate_out, xb_out):
    h = _modulate(x_ref[...], g_ref[...], sh_ref[...], sc_ref[...])
    u = _dot(h.astype(BF16), w_ref[...])
    gate_out[...] = jax.nn.gelu(u[:, :D_RNN]).astype(BF16)
    xb_out[...] = u[:, D_RNN:]


def _rnn_in(x, gain, shift, scale, w_in):
    bsz, seq, d = x.shape
    tm = min(MIX_TILE, seq)
    row = lambda b, i: (b, i, 0)
    per_b = lambda b, i: (b, 0, 0)
    const = lambda b, i: (0, 0)
    return pl.pallas_call(
        _rnn_in_kernel,
        grid=(bsz, seq // tm),
        in_specs=[
            pl.BlockSpec((None, tm, d), row),
            pl.BlockSpec(gain.shape, const),
            pl.BlockSpec((None, 1, d), per_b),
            pl.BlockSpec((None, 1, d), per_b),
            pl.BlockSpec(w_in.shape, const),
        ],
        out_specs=[pl.BlockSpec((None, tm, D_RNN), row), pl.BlockSpec((None, tm, D_RNN), row)],
        out_shape=[jax.ShapeDtypeStruct((bsz, seq, D_RNN), BF16),
                   jax.ShapeDtypeStruct((bsz, seq, D_RNN), F32)],
        compiler_params=_params("arbitrary", "arbitrary"),
        name="rnn_in",
    )(x, gain, shift, scale, w_in)


def _lru_kernel(xb_ref, cw_ref, cb_ref, wcat_ref, bcat_ref, lam_ref, hs_ref,
                xi_ref, af_ref, bf_ref, ab_ref, bb_ref, hf_ref, hb_ref, sum_ref):
    seq, c = xb_ref.shape
    seg_len = seq // SUBLANES
    n_slab = c // LANES
    n_rows = seg_len * SUBLANES
    halo = (CONV_W // 2) * SUBLANES
    row = lax.broadcasted_iota(jnp.int32, (SUBLANES, LANES), 0)
    for sl in range(n_slab):
        lanes = slice(sl * LANES, (sl + 1) * LANES)
        for g in range(SUBLANES):
            xi_ref[sl, pl.ds(halo + g, seg_len, stride=SUBLANES), :] = xb_ref[g * seg_len:(g + 1) * seg_len, lanes]
        for back in (1, 2):
            prev = xi_ref[sl, halo + (seg_len - back) * SUBLANES:halo + (seg_len - back + 1) * SUBLANES, :]
            xi_ref[sl, halo - back * SUBLANES:halo - (back - 1) * SUBLANES, :] = jnp.where(
                row == 0, 0.0, pltpu.roll(prev, 1, 0))
        nxt = xi_ref[sl, halo:halo + SUBLANES, :]
        xi_ref[sl, halo + n_rows:halo + n_rows + SUBLANES, :] = jnp.where(
            row == SUBLANES - 1, 0.0, pltpu.roll(nxt, SUBLANES - 1, 0))

    cw = cw_ref[...]
    cb = cb_ref[...]
    lam = lam_ref[...]
    neg = -lam
    softplus = jnp.maximum(neg, 0.0) + jnp.log1p(jnp.exp(-jnp.abs(neg)))
    half_rate = (-0.5 * LRU_C) * softplus
    rows = min(SCAN_ROWS, n_rows)
    n_chunks = n_rows // rows

    for ci in range(n_chunks):
        i0 = ci * rows
        taps = []
        for k in range(CONV_W):
            lo = halo + i0 + (k - CONV_W // 2) * SUBLANES
            taps.append(jnp.concatenate([xi_ref[sl, lo:lo + rows, :] for sl in range(n_slab)], axis=1))
        xc = cb
        for k in range(CONV_W):
            xc = xc + taps[k] * cw[k:k + 1, :]
        xcb = xc.astype(BF16)
        xh = 0.5 * xc
        for dirn, (a_ref, b_ref) in enumerate(((af_ref, bf_ref), (ab_ref, bb_ref))):
            cols = slice(2 * dirn * c, 2 * (dirn + 1) * c)
            th = jnp.tanh(_dot(xcb, wcat_ref[:, cols]) + bcat_ref[:, cols])
            hr = half_rate[dirn:dirn + 1, :]
            log_a = hr * th[:, :c] + hr
            a = jnp.exp(log_a)
            m2 = jnp.tanh(log_a) * (-1.0 - a * a)
            mult = jnp.where(m2 > 0.0, m2 * lax.rsqrt(m2), 0.0)
            if dirn == 0 and ci == 0:
                mult = jnp.where(lax.broadcasted_iota(jnp.int32, mult.shape, 0) == 0, 1.0, mult)
            if dirn == 1 and ci == n_chunks - 1:
                mult = jnp.where(lax.broadcasted_iota(jnp.int32, mult.shape, 0) == rows - 1, 1.0, mult)
            b = mult * (th[:, c:] + 1.0) * xh
            for sl in range(n_slab):
                a_ref[sl, i0:i0 + rows, :] = a[:, sl * LANES:(sl + 1) * LANES]
                b_ref[sl, i0:i0 + rows, :] = b[:, sl * LANES:(sl + 1) * LANES]

    def step_rows(cidx):
        fwd = pl.ds(pl.multiple_of(cidx * SUBLANES, SUBLANES), SUBLANES)
        bwd = pl.ds(pl.multiple_of((seg_len - 1 - cidx) * SUBLANES, SUBLANES), SUBLANES)
        return fwd, bwd

    zero = jnp.zeros((SUBLANES, LANES), F32)
    one = jnp.ones((SUBLANES, LANES), F32)

    def totals(cidx, carry):
        fwd, bwd = step_rows(cidx)
        out = []
        for sl in range(n_slab):
            hf, pf, hb, pb = carry[sl]
            af, ab = af_ref[sl, fwd, :], ab_ref[sl, bwd, :]
            out.append((af * hf + bf_ref[sl, fwd, :], af * pf, ab * hb + bb_ref[sl, bwd, :], ab * pb))
        return tuple(out)
    tot = lax.fori_loop(0, seg_len, totals, tuple((zero, one, zero, one) for _ in range(n_slab)), unroll=8)

    enter = []
    for sl in range(n_slab):
        hf, pf, hb, pb = tot[sl]
        cf, cbk = zero, zero
        for _ in range(SUBLANES - 1):
            cf = jnp.where(row == 0, 0.0, pltpu.roll(hf + pf * cf, 1, 0))
            cbk = jnp.where(row == SUBLANES - 1, 0.0, pltpu.roll(hb + pb * cbk, SUBLANES - 1, 0))
        enter.append((cf, cbk))

    def states(meet, cidx, carry):
        fwd, bwd = step_rows(cidx)
        out = []
        for sl in range(n_slab):
            hf, hb = carry[sl]
            hf = af_ref[sl, fwd, :] * hf + bf_ref[sl, fwd, :]
            hb = ab_ref[sl, bwd, :] * hb + bb_ref[sl, bwd, :]
            if meet:
                sum_ref[sl, fwd, :] = hf + hb_ref[sl, fwd, :]
                sum_ref[sl, bwd, :] = hb + hf_ref[sl, bwd, :]
            else:
                hf_ref[sl, fwd, :] = hf
                hb_ref[sl, bwd, :] = hb
            out.append((hf, hb))
        return tuple(out)
    mid = lax.fori_loop(0, seg_len // 2, functools.partial(states, False), tuple(enter), unroll=8)
    lax.fori_loop(seg_len // 2, seg_len, functools.partial(states, True), mid, unroll=8)

    for g in range(SUBLANES):
        for sl in range(n_slab):
            hs_ref[g * seg_len:(g + 1) * seg_len, sl * LANES:(sl + 1) * LANES] = (
                sum_ref[sl, pl.ds(g, seg_len, stride=SUBLANES), :]).astype(BF16)


def _lru(xb, conv_w, conv_b, wcat, bcat, lam):
    bsz, seq, _ = xb.shape
    c = RNN_BW
    blk = lambda b, n: (b, 0, n)
    return pl.pallas_call(
        _lru_kernel,
        grid=(bsz, RNN_BLOCKS),
        in_specs=[
            pl.BlockSpec((None, seq, c), blk),
            pl.BlockSpec((CONV_W, c), lambda b, n: (0, n)),
            pl.BlockSpec((1, c), lambda b, n: (0, n)),
            pl.BlockSpec((None, c, 4 * c), lambda b, n: (n, 0, 0)),
            pl.BlockSpec((None, 1, 4 * c), lambda b, n: (n, 0, 0)),
            pl.BlockSpec((2, c), lambda b, n: (0, n)),
        ],
        out_specs=pl.BlockSpec((None, seq, c), blk),
        out_shape=jax.ShapeDtypeStruct((bsz, seq, D_RNN), BF16),
        scratch_shapes=[pltpu.VMEM((c // LANES, seq + (CONV_W - 1) * SUBLANES, LANES), F32)]
        + [pltpu.VMEM((c // LANES, seq, LANES), F32)] * 7,
        compiler_params=_params("arbitrary", "arbitrary"),
        name="rglru_scan",
    )(xb, conv_w, conv_b, wcat, bcat, lam)


def _mla_weights(w_in, w_q_b, w_kv_b, q_norm, k_norm):
    half = QK_ROPE // 2

    def slab(t):
        return jnp.pad(t, [(0, 0)] * (t.ndim - 1) + [(0, LANES - QK_HEAD)])

    def rot_slab(t):
        rope = t[..., QK_NOPE:]
        swapped = jnp.concatenate([jnp.zeros_like(t[..., :QK_NOPE]), rope[..., half:], rope[..., :half]], axis=-1)
        return slab(swapped)

    kpe = jnp.pad(w_in[:, Q_LORA + KV_LORA:], ((0, 0), (QK_NOPE, 0)))
    w_in_p = jnp.concatenate([w_in[:, :Q_LORA + KV_LORA], slab(kpe), rot_slab(kpe)], axis=1).astype(BF16)
    wq = w_q_b.reshape(Q_LORA, N_HEADS, QK_HEAD)
    wq_p = slab(wq).reshape(Q_LORA, N_HEADS * LANES).astype(BF16)
    band = jnp.concatenate([wq[..., QK_NOPE + half:], wq[..., QK_NOPE:QK_NOPE + half]], axis=-1)
    band = band.reshape(Q_LORA, N_HEADS // 2, 2 * QK_ROPE)
    wq_rot = jnp.pad(band, ((0, 0), (0, 0), (QK_NOPE, 0))).reshape(Q_LORA, N_HEADS // 2 * LANES).astype(BF16)
    wkv = w_kv_b.reshape(KV_LORA, N_HEADS, QK_NOPE + V_HEAD)
    wk = jnp.pad(wkv[:, :, :QK_NOPE], ((0, 0), (0, 0), (0, LANES - QK_NOPE))).reshape(KV_LORA, N_HEADS * LANES)
    wv = wkv[:, :, QK_NOPE:].reshape(KV_LORA, N_HEADS * V_HEAD)
    w_kv_p = jnp.concatenate([wk, wv], axis=1).astype(BF16)
    gains = lambda g: jnp.stack([slab(g), rot_slab(g)], axis=0)
    return w_in_p, wq_p, wq_rot, w_kv_p, gains(q_norm), gains(k_norm)


def _rope_kernel(pos_ref, freq_ref, cos_out, sin_out):
    ang = freq_ref[...] * pos_ref[...].astype(F32)
    cos, sin = jnp.cos(ang), jnp.sin(ang)
    seq = ang.shape[1]
    fill = lambda value, n: jnp.full((n, seq), value, F32)
    cos_out[...] = jnp.concatenate([fill(1.0, QK_NOPE), cos, cos, fill(1.0, LANES - QK_HEAD)], axis=0).T
    sin_out[...] = jnp.concatenate([fill(0.0, QK_NOPE), -sin, sin, fill(0.0, LANES - QK_HEAD)], axis=0).T


def _rope_tables(positions):
    half = QK_ROPE // 2
    inv_freq = ROPE_THETA ** (-jnp.arange(half, dtype=F32) / half)
    bsz, seq = positions.shape
    table = jax.ShapeDtypeStruct((bsz, seq, LANES), F32)
    return pl.pallas_call(
        _rope_kernel,
        grid=(bsz,),
        in_specs=[pl.BlockSpec((None, 1, seq), lambda b: (b, 0, 0)), pl.BlockSpec((half, 1), lambda b: (0, 0))],
        out_specs=[pl.BlockSpec((None, seq, LANES), lambda b: (b, 0, 0))] * 2,
        out_shape=[table, table],
        compiler_params=_params("arbitrary"),
        name="rope_tables",
    )(positions.reshape(bsz, 1, seq), inv_freq.reshape(half, 1))


def _router_weights(w_router, router_bias):
    perm = (jnp.arange(N_EXPERTS) % N_GROUPS) * EXPERTS_PER_GROUP + jnp.arange(N_EXPERTS) // N_GROUPS
    w = w_router[:, perm]
    hi = w.astype(BF16)
    lo = (w - hi.astype(F32)).astype(BF16)
    z = jnp.zeros_like(hi)
    wr1 = jnp.concatenate([hi, lo, z, z], axis=1)
    wr2 = jnp.concatenate([z, z, hi, z], axis=1)
    return wr1, wr2, router_bias[perm].reshape(N_EXPERTS, 1).astype(F32)


def kernel(x, c, positions, norm_mix, norm_ffn, w_ada, b_ada, mla_w_in, mla_q_a_norm, mla_kv_a_norm, mla_w_q_b, mla_w_kv_b, mla_q_norm, mla_k_norm, mla_w_o, rnn_w_in, rnn_conv_w, rnn_conv_b, rnn_lam_f, rnn_w_rf, rnn_b_rf, rnn_w_if, rnn_b_if, rnn_lam_b, rnn_w_rb, rnn_b_rb, rnn_w_ib, rnn_b_ib, rnn_w_o, w_router, router_bias, moe_w_gu, moe_w_dn):
    bsz, seq, d = x.shape
    depth = w_ada.shape[0]
    mod = _ada(c, w_ada, b_ada)
    wr1, wr2, rbias = _router_weights(w_router, router_bias)
    cos_t, sin_t = _rope_tables(positions)
    vec = lambda v: v.reshape(1, -1)
    for i in range(depth):
        sh1, sc1, g1, sh2, sc2, g2 = [mod[i, :, k * d:(k + 1) * d].reshape(bsz, 1, d) for k in range(6)]
        j = i // 2
        if i % 2 == 0:
            w_in_p, wq, wq_rot, wkv, qn, kn = _mla_weights(mla_w_in[j], mla_w_q_b[j], mla_w_kv_b[j],
                                                           mla_q_norm[j], mla_k_norm[j])
            q, k, v = _mla_in(x, vec(norm_mix[i]), sh1, sc1, w_in_p, vec(mla_q_a_norm[j]),
                              vec(mla_kv_a_norm[j]), wq, wq_rot, wkv, qn, kn, cos_t, sin_t)
            a = _attention(q, k, v, mla_q_norm[j], mla_k_norm[j])
            hs = None
            w_o = mla_w_o[j].astype(BF16)
        else:
            a, xb = _rnn_in(x, vec(norm_mix[i]), sh1, sc1, rnn_w_in[j].astype(BF16))
            wcat = (0.5 * jnp.concatenate([rnn_w_rf[j], rnn_w_if[j], rnn_w_rb[j], rnn_w_ib[j]], axis=-1)).astype(BF16)
            bcat = jnp.stack([b.reshape(RNN_BLOCKS, RNN_BW) for b in
                              (rnn_b_rf[j], rnn_b_if[j], rnn_b_rb[j], rnn_b_ib[j])], axis=1)
            bcat = 0.5 * bcat.reshape(RNN_BLOCKS, 1, 4 * RNN_BW)
            lam = jnp.stack([rnn_lam_f[j], rnn_lam_b[j]], axis=0)
            hs = _lru(xb, rnn_conv_w[j], vec(rnn_conv_b[j]), wcat, bcat, lam)
            w_o = rnn_w_o[j].astype(BF16)
        x, h2, idx, wts = _mix_out(a, hs, x, w_o, g1, vec(norm_ffn[i]), sh2, sc2, wr1, wr2, rbias)
        x = _moe(h2, idx, wts, x, g2, moe_w_gu, moe_w_dn, i)
    return x
```

```python
import functools

import jax
import jax.numpy as jnp
from jax import lax
from jax.experimental import pallas as pl
from jax.experimental.pallas import tpu as pltpu

F32 = jnp.float32
BF16 = jnp.bfloat16

D_MODEL = 1024
N_HEADS = 16
Q_LORA = 384
KV_LORA = 256
QK_NOPE = 64
QK_ROPE = 32
QK_HEAD = QK_NOPE + QK_ROPE
V_HEAD = 64
ROPE_THETA = 10000.0
D_RNN = D_MODEL
RNN_BLOCKS = 4
RNN_BW = D_RNN // RNN_BLOCKS
CONV_W = 4
LRU_C = 8.0
N_EXPERTS = 32
N_GROUPS = 8
EXPERTS_PER_GROUP = N_EXPERTS // N_GROUPS
TOP_K = 2
D_EXPERT = 512
EPS = 1e-6
LOG2_E = 1.4426950408889634

LANES = 128
SUBLANES = 8
VMEM_LIMIT = 52 * 1024 * 1024

ROW_TILE = 512
ROW_SUB = 256
MIX_TILE = 1024
Q_TILE = 2048
Q_SUB = 256
MAX_SAFE_SHIFT = 60.0
MOE_TILE = 512
SCAN_ROWS = 512
ADA_TILE = 1536
assert MOE_TILE & (MOE_TILE - 1) == 0


def _dot(a, b):
    return jnp.dot(a, b, preferred_element_type=F32)


def _split_bf16(a):
    hi = a.astype(BF16)
    lo = (a - hi.astype(F32)).astype(BF16)
    return hi, lo


def _dot_split(a, b):
    ah, al = _split_bf16(a)
    bh, bl = _split_bf16(b)
    return _dot(ah, bh) + (_dot(ah, bl) + _dot(al, bh))


def _rms(x, gain, n):
    ms = jnp.sum(x * x, axis=-1, keepdims=True) * (1.0 / n)
    return x * lax.rsqrt(ms + EPS) * gain


def _modulate(x, gain, shift, scale):
    return _rms(x, gain, x.shape[-1]) * (1.0 + scale) + shift


def _params(*sem):
    return pltpu.CompilerParams(dimension_semantics=sem, vmem_limit_bytes=VMEM_LIMIT)


def _ada_kernel(c_ref, w_ref, b_ref, o_ref):
    c = c_ref[...]
    o_ref[...] = _dot_split(c * jax.nn.sigmoid(c), w_ref[...]) + b_ref[...]


def _ada(c, w_ada, b_ada):
    depth, d, n = w_ada.shape
    bsz = c.shape[0]
    tn = min(ADA_TILE, n)
    return pl.pallas_call(
        _ada_kernel,
        grid=(depth, n // tn),
        in_specs=[
            pl.BlockSpec((bsz, d), lambda l, j: (0, 0)),
            pl.BlockSpec((None, d, tn), lambda l, j: (l, 0, j)),
            pl.BlockSpec((None, 1, tn), lambda l, j: (l, 0, j)),
        ],
        out_specs=pl.BlockSpec((None, bsz, tn), lambda l, j: (l, 0, j)),
        out_shape=jax.ShapeDtypeStruct((depth, bsz, n), F32),
        compiler_params=_params("arbitrary", "arbitrary"),
        name="adaln_mod",
    )(c, w_ada, b_ada.reshape(depth, 1, n))


def _head_scale(s):
    return lax.rsqrt(jnp.sum(s * s, axis=-1, keepdims=True) * (1.0 / QK_HEAD) + EPS)


def _mla_in_kernel(x_ref, g_ref, sh_ref, sc_ref, win_ref, qan_ref, kvan_ref, wq_ref, wqr_ref, wkv_ref,
                   qn_ref, kn_ref, cos_ref, sin_ref, q_out, k_out, v_out):
    h = _modulate(x_ref[...], g_ref[...], sh_ref[...], sc_ref[...])
    lat = _dot(h.astype(BF16), win_ref[...])
    q_lat = lat[:, :Q_LORA]
    kv_lat = lat[:, Q_LORA:Q_LORA + KV_LORA]
    kpe = lat[:, Q_LORA + KV_LORA:Q_LORA + KV_LORA + LANES]
    kpe_rot = lat[:, Q_LORA + KV_LORA + LANES:]
    qn = _rms(q_lat, qan_ref[...], Q_LORA).astype(BF16)
    q_all = _dot(qn, wq_ref[...])
    q_rot = _dot(qn, wqr_ref[...])
    kv_all = _dot(_rms(kv_lat, kvan_ref[...], KV_LORA).astype(BF16), wkv_ref[...])
    cos_t = cos_ref[...]
    sin_t = sin_ref[...]
    q_scale = LOG2_E * QK_HEAD ** -0.5
    cq = cos_t * (qn_ref[0:1, :] * q_scale)
    sq = sin_t * (qn_ref[1:2, :] * q_scale)
    ck = cos_t * kn_ref[0:1, :]
    k_rot_term = kpe_rot * (sin_t * kn_ref[1:2, :])
    for hh in range(N_HEADS):
        sl = slice(hh * LANES, (hh + 1) * LANES)
        s = q_all[:, sl]
        rot = q_rot[:, (hh // 2) * LANES:(hh // 2 + 1) * LANES]
        if hh % 2:
            rot = pltpu.roll(rot, LANES - QK_ROPE, 1)
        q_out[hh] = ((s * cq + rot * sq) * _head_scale(s)).astype(BF16)
        s = kv_all[:, sl] + kpe
        k_out[hh] = ((s * ck + k_rot_term) * _head_scale(s)).astype(BF16)
    v_out[...] = kv_all[:, N_HEADS * LANES:].astype(BF16)


def _mla_in(x, gain, shift, scale, w_in, q_a_norm, kv_a_norm, w_q, w_q_rot, w_kv, q_norm, k_norm, cos_t, sin_t):
    bsz, seq, d = x.shape
    tm = min(ROW_TILE, seq)
    row = lambda b, i: (b, i, 0)
    per_b = lambda b, i: (b, 0, 0)
    const = lambda b, i: (0, 0)
    full = lambda a: pl.BlockSpec(a.shape, const)
    return pl.pallas_call(
        _mla_in_kernel,
        grid=(bsz, seq // tm),
        in_specs=[
            pl.BlockSpec((None, tm, d), row),
            full(gain),
            pl.BlockSpec((None, 1, d), per_b),
            pl.BlockSpec((None, 1, d), per_b),
            full(w_in), full(q_a_norm), full(kv_a_norm), full(w_q), full(w_q_rot), full(w_kv),
            full(q_norm), full(k_norm),
            pl.BlockSpec((None, tm, LANES), row),
            pl.BlockSpec((None, tm, LANES), row),
        ],
        out_specs=[
            pl.BlockSpec((None, N_HEADS, tm, LANES), lambda b, i: (b, 0, i, 0)),
            pl.BlockSpec((None, N_HEADS, tm, LANES), lambda b, i: (b, 0, i, 0)),
            pl.BlockSpec((None, tm, N_HEADS * V_HEAD), row),
        ],
        out_shape=[
            jax.ShapeDtypeStruct((bsz, N_HEADS, seq, LANES), BF16),
            jax.ShapeDtypeStruct((bsz, N_HEADS, seq, LANES), BF16),
            jax.ShapeDtypeStruct((bsz, seq, N_HEADS * V_HEAD), BF16),
        ],
        compiler_params=_params("arbitrary", "arbitrary"),
        name="mla_in",
    )(x, gain, shift, scale, w_in, q_a_norm, kv_a_norm, w_q, w_q_rot, w_kv, q_norm, k_norm, cos_t, sin_t)


def _attn_kernel(bounded, shift_ref, q_ref, k_ref, v_ref, o_ref):
    v = v_ref[...]
    lane_v = lax.broadcasted_iota(jnp.int32, v.shape, 1)
    v_heads = [jnp.where(lane_v < V_HEAD, v, jnp.ones((), BF16)), jnp.where(lane_v >= V_HEAD, v, jnp.ones((), BF16))]
    lane = lax.broadcasted_iota(jnp.int32, (Q_SUB, LANES), 1)
    for i in range(q_ref.shape[1] // Q_SUB):
        rows = slice(i * Q_SUB, (i + 1) * Q_SUB)
        outs = []
        for j in range(2):
            s = lax.dot_general(q_ref[j, rows, :], k_ref[j], (((1,), (1,)), ((), ())),
                                preferred_element_type=F32)
            m = shift_ref[0] if bounded else jnp.max(s, axis=-1, keepdims=True)
            o = _dot(jnp.exp2(s - m).astype(BF16), v_heads[j])
            denom = o[:, V_HEAD:V_HEAD + 1] if j == 0 else o[:, 0:1]
            outs.append(o / denom)
        o_ref[rows, :] = jnp.where(lane < V_HEAD, outs[0], outs[1]).astype(BF16)


def _attention_call(bounded, shift, q, k, v):
    bsz, _, seq, _ = q.shape
    tq = min(Q_TILE, seq)
    assert tq % Q_SUB == 0
    grid_spec = pltpu.PrefetchScalarGridSpec(
        num_scalar_prefetch=1,
        grid=(bsz, N_HEADS // 2, seq // tq),
        in_specs=[
            pl.BlockSpec((None, 2, tq, LANES), lambda b, h, i, *_: (b, h, i, 0)),
            pl.BlockSpec((None, 2, seq, LANES), lambda b, h, i, *_: (b, h, 0, 0)),
            pl.BlockSpec((None, seq, LANES), lambda b, h, i, *_: (b, 0, h)),
        ],
        out_specs=pl.BlockSpec((None, tq, LANES), lambda b, h, i, *_: (b, i, h)),
    )
    return pl.pallas_call(
        functools.partial(_attn_kernel, bounded),
        grid_spec=grid_spec,
        out_shape=jax.ShapeDtypeStruct((bsz, seq, N_HEADS * V_HEAD), BF16),
        compiler_params=_params("arbitrary", "arbitrary", "arbitrary"),
        name="mla_attention",
    )(shift, q, k, v)


def _attention(q, k, v, q_gain, k_gain):
    score_bound = 1.02 * LOG2_E * QK_HEAD ** 0.5 * jnp.max(jnp.abs(q_gain)) * jnp.max(jnp.abs(k_gain))
    shift = score_bound.reshape(1).astype(F32)
    return lax.cond(score_bound <= MAX_SAFE_SHIFT,
                    functools.partial(_attention_call, True), functools.partial(_attention_call, False),
                    shift, q, k, v)


def _first_index_of_max(vals):
    m = vals[0]
    for v in vals[1:]:
        m = jnp.maximum(m, v)
    idx = jnp.full(m.shape, float(len(vals) - 1), F32)
    for j in range(len(vals) - 2, -1, -1):
        idx = jnp.where(vals[j] == m, float(j), idx)
    return m, idx


def _route(h2, wr1_ref, wr2_ref, rb_ref):
    hh, hl = _split_bf16(h2)
    logits = (_dot(hh, wr1_ref[...]) + _dot(hl, wr2_ref[...])).T
    logit = logits[0:N_EXPERTS] + logits[N_EXPERTS:2 * N_EXPERTS] + logits[2 * N_EXPERTS:3 * N_EXPERTS]
    score = jax.nn.sigmoid(logit)
    biased = score + rb_ref[...]
    a = [biased[j * N_GROUPS:(j + 1) * N_GROUPS] for j in range(EXPERTS_PER_GROUP)]
    sc = [score[j * N_GROUPS:(j + 1) * N_GROUPS] for j in range(EXPERTS_PER_GROUP)]
    hi1, lo1 = jnp.maximum(a[0], a[1]), jnp.minimum(a[0], a[1])
    hi2, lo2 = jnp.maximum(a[2], a[3]), jnp.minimum(a[2], a[3])
    gscore = jnp.maximum(hi1, hi2) + jnp.maximum(jnp.minimum(hi1, hi2), jnp.maximum(lo1, lo2))
    gmax = jnp.max(gscore, axis=0, keepdims=True)
    giota = lax.broadcasted_iota(jnp.int32, gscore.shape, 0).astype(F32)
    gsel = jnp.min(jnp.where(gscore == gmax, giota, float(N_GROUPS)), axis=0, keepdims=True)
    onehot = giota == gsel
    pick = lambda t: jnp.sum(jnp.where(onehot, t, 0.0), axis=0, keepdims=True)
    bj = [pick(t) for t in a]
    sj = [pick(t) for t in sc]
    _, i1 = _first_index_of_max(bj)
    bj2 = [jnp.where(i1 == float(j), -jnp.inf, bj[j]) for j in range(EXPERTS_PER_GROUP)]
    _, i2 = _first_index_of_max(bj2)
    sel = lambda i: jnp.where(i == 0.0, sj[0], jnp.where(i == 1.0, sj[1], jnp.where(i == 2.0, sj[2], sj[3])))
    w1, w2 = sel(i1), sel(i2)
    den = w1 + w2
    base = gsel * float(EXPERTS_PER_GROUP)
    return ((base + i1).astype(jnp.int32), (base + i2).astype(jnp.int32)), (w1 / den, w2 / den)


def _mix_out_kernel(has_gate, *refs):
    if has_gate:
        a_ref, hs_ref, x_ref, wo_ref, g1_ref, g_ref, sh_ref, sc_ref, wr1_ref, wr2_ref, rb_ref, \
            x_out, h_out, idx_out, wts_out = refs
    else:
        a_ref, x_ref, wo_ref, g1_ref, g_ref, sh_ref, sc_ref, wr1_ref, wr2_ref, rb_ref, \
            x_out, h_out, idx_out, wts_out = refs
    tm = x_ref.shape[0]
    sub = min(ROW_SUB, tm)
    for i in range(tm // sub):
        rows = slice(i * sub, (i + 1) * sub)
        if has_gate:
            a = (a_ref[rows, :].astype(F32) * hs_ref[rows, :]).astype(BF16)
        else:
            a = a_ref[rows, :]
        x1 = x_ref[rows, :] + g1_ref[...] * _dot(a, wo_ref[...])
        x_out[rows, :] = x1
        h2 = _modulate(x1, g_ref[...], sh_ref[...], sc_ref[...])
        _to_tiles(h_out, h2, i * sub)
        idx, wts = _route(h2, wr1_ref, wr2_ref, rb_ref)
        for k in range(TOP_K):
            idx_out[k:k + 1, rows] = idx[k]
            wts_out[k:k + 1, rows] = wts[k]


def _mix_out(a, hs, x, w_o, gate1, gain, shift, scale, wr1, wr2, rbias):
    bsz, seq, d = x.shape
    tm = min(MIX_TILE, seq)
    row = lambda b, i: (b, i, 0)
    per_b = lambda b, i: (b, 0, 0)
    const = lambda b, i: (0, 0)
    full = lambda t: pl.BlockSpec(t.shape, const)
    vec = pl.BlockSpec((None, 1, d), per_b)
    acts = [a] if hs is None else [a, hs]
    return pl.pallas_call(
        functools.partial(_mix_out_kernel, hs is not None),
        grid=(bsz, seq // tm),
        in_specs=[pl.BlockSpec((None, tm, t.shape[-1]), row) for t in acts] + [
            pl.BlockSpec((None, tm, d), row), full(w_o), vec, full(gain), vec, vec,
            full(wr1), full(wr2), full(rbias),
        ],
        out_specs=[
            pl.BlockSpec((None, tm, d), row),
            pl.BlockSpec((None, tm * N_SUB, LANES), row),
            pl.BlockSpec((None, TOP_K, tm), lambda b, i: (b, 0, i)),
            pl.BlockSpec((None, TOP_K, tm), lambda b, i: (b, 0, i)),
        ],
        out_shape=[
            jax.ShapeDtypeStruct((bsz, seq, d), F32),
            jax.ShapeDtypeStruct((bsz, seq * N_SUB, LANES), U32),
            jax.ShapeDtypeStruct((bsz, TOP_K, seq), jnp.int32),
            jax.ShapeDtypeStruct((bsz, TOP_K, seq), F32),
        ],
        compiler_params=_params("arbitrary", "arbitrary"),
        name="mix_out_route",
    )(*acts, x, w_o, gate1, gain, shift, scale, wr1, wr2, rbias)


N_SUB = D_MODEL // (2 * LANES)
U32 = jnp.uint32
TABLE_CHUNK = 512


def _to_tiles(ref, val, lo=0):
    n, d = val.shape
    bits = lambda t: lax.bitcast_convert_type(t.astype(BF16).astype(F32), U32)
    for s in range(N_SUB):
        hi = bits(val[:, s * LANES:(s + 1) * LANES])
        lo_half = bits(val[:, d // 2 + s * LANES:d // 2 + (s + 1) * LANES])
        ref[pl.ds(lo * N_SUB + s, n, stride=N_SUB), :] = hi | (lo_half >> 16)


def _from_tiles(ref, lo, n):
    words = [ref[pl.ds(lo * N_SUB + s, n, stride=N_SUB), :] for s in range(N_SUB)]
    hi = [lax.bitcast_convert_type(w & jnp.uint32(0xFFFF0000), F32) for w in words]
    lo_half = [lax.bitcast_convert_type(w << 16, F32) for w in words]
    return jnp.concatenate(hi + lo_half, axis=1)


def _tables_kernel(idx_ref, rank_ref, cnt_ref, carry):
    @pl.when(pl.program_id(0) == 0)
    def _():
        carry[...] = jnp.zeros_like(carry)

    seq = idx_ref.shape[-1]
    ch = min(TABLE_CHUNK, seq)
    tri = jnp.where(lax.broadcasted_iota(jnp.int32, (ch, ch), 0) <= lax.broadcasted_iota(jnp.int32, (ch, ch), 1),
                    1.0, 0.0).astype(BF16)
    eiota = lax.broadcasted_iota(jnp.int32, (N_EXPERTS, ch), 0)
    cnt = carry[...]
    for k in range(TOP_K):
        for c in range(seq // ch):
            sel = eiota == idx_ref[k:k + 1, c * ch:(c + 1) * ch]
            pref = _dot(jnp.where(sel, 1.0, 0.0).astype(BF16), tri) + cnt
            rank = jnp.sum(jnp.where(sel, pref, 0.0), axis=0, keepdims=True) - 1.0
            rank_ref[k:k + 1, c * ch:(c + 1) * ch] = rank.astype(jnp.int32)
            cnt = pref[:, ch - 1:ch]
    carry[...] = cnt
    cnt_ref[...] = jnp.broadcast_to(cnt, cnt_ref.shape)


def _tables(idx):
    bsz, _, seq = idx.shape
    return pl.pallas_call(
        _tables_kernel,
        grid=(bsz,),
        in_specs=[pl.BlockSpec((None, TOP_K, seq), lambda b: (b, 0, 0))],
        out_specs=[pl.BlockSpec((None, TOP_K, seq), lambda b: (b, 0, 0)),
                   pl.BlockSpec((N_EXPERTS, LANES), lambda b: (0, 0))],
        out_shape=[jax.ShapeDtypeStruct((bsz, TOP_K, seq), jnp.int32),
                   jax.ShapeDtypeStruct((N_EXPERTS, LANES), F32)],
        scratch_shapes=[pltpu.VMEM((N_EXPERTS, 1), F32)],
        compiler_params=_params("arbitrary"),
        name="moe_tables",
    )(idx)


def _zero_runs(step, total, pad_start_ref, pad_len_ref, tail_ref):
    ops = []
    for m in range(-(-2 * N_EXPERTS // total)):
        u = step + m * total
        e = jnp.minimum(u, N_EXPERTS - 1)
        length = jnp.where(u < N_EXPERTS, pad_len_ref[e], 0)
        first = pad_start_ref[e]
        for bit in reversed(range(MOE_TILE.bit_length() - 1)):
            done = lax.shift_left(lax.shift_right_logical(length, bit + 1), bit + 1)
            ops.append((lax.bitwise_and(lax.shift_right_logical(length, bit), 1) == 1, first + done, 1 << bit))
        t = u - N_EXPERTS
        ops.append(((t >= 0) & (t < tail_ref[1]), tail_ref[0] + t * MOE_TILE, MOE_TILE))
    return ops


def _scatter_kernel(total, pad_start_ref, pad_len_ref, tail_ref, dest_ref, src_hbm, dst_hbm,
                    buf, zbuf, sem_in, sem_out, sem_z):
    n = pl.program_id(0) * pl.num_programs(1) + pl.program_id(1)
    tm = buf.shape[1] // N_SUB
    slot = lax.rem(n, 3)

    def load(step, sl):
        return pltpu.make_async_copy(src_hbm.at[pl.ds(step * tm * N_SUB, tm * N_SUB)], buf.at[sl], sem_in.at[sl])

    def drain(sl):
        for _ in range(TOP_K):
            pltpu.make_async_copy(buf.at[sl], dst_hbm.at[pl.ds(0, tm * N_SUB)], sem_out.at[sl]).wait()

    def zero_fill(step, wait):
        for pred, first, rows in _zero_runs(step, total, pad_start_ref, pad_len_ref, tail_ref):
            @pl.when(pred)
            def _(first=first, rows=rows):
                cp = pltpu.make_async_copy(zbuf.at[pl.ds(0, rows * N_SUB)],
                                           dst_hbm.at[pl.ds(pl.multiple_of(first * N_SUB, N_SUB), rows * N_SUB)], sem_z)
                cp.wait() if wait else cp.start()

    @pl.when(n == 0)
    def _():
        zbuf[...] = jnp.zeros_like(zbuf)
        load(0, 0).start()
        if total > 1:
            load(1, 1).start()

    load(n, slot).wait()
    for k in range(TOP_K):
        for c in range(tm // LANES):
            def start(j, carry, k=k, c=c):
                src = buf.at[slot, pl.ds(pl.multiple_of((c * LANES + j) * N_SUB, N_SUB), N_SUB)]
                dst = dst_hbm.at[pl.ds(pl.multiple_of(dest_ref[0, k * tm + c * LANES + j], N_SUB), N_SUB)]
                pltpu.make_async_copy(src, dst, sem_out.at[slot]).start(priority=k)
                return carry
            lax.fori_loop(0, LANES, start, 0, unroll=8)
    zero_fill(n, wait=False)

    @pl.when(n > 0)
    def _():
        drain(lax.rem(n + 2, 3))
        zero_fill(n - 1, wait=True)

    @pl.when(n + 2 < total)
    def _():
        load(n + 2, lax.rem(n + 2, 3)).start()

    @pl.when(n == total - 1)
    def _():
        drain(slot)
        zero_fill(n, wait=True)


def _index_blocks(table, tm):
    bsz, _, seq = table.shape
    nt = seq // tm
    t = table.reshape(bsz, TOP_K, nt, tm).transpose(0, 2, 1, 3)
    return t.reshape(bsz * nt, 1, TOP_K * tm), (None, 1, TOP_K * tm)


def _scatter(dest, pad_start, pad_len, tail, h2t, n_rows):
    bsz, _, seq = dest.shape
    tm = min(ROW_TILE, seq)
    nt = seq // tm
    dest4, dest_block = _index_blocks(dest, tm)
    grid_spec = pltpu.PrefetchScalarGridSpec(
        num_scalar_prefetch=3,
        grid=(bsz, nt),
        in_specs=[
            pl.BlockSpec(dest_block, lambda b, i, *_: (b * nt + i, 0, 0), memory_space=pltpu.SMEM),
            pl.BlockSpec(memory_space=pl.ANY),
        ],
        out_specs=pl.BlockSpec(memory_space=pl.ANY),
        scratch_shapes=[pltpu.VMEM((3, tm * N_SUB, LANES), U32), pltpu.VMEM((MOE_TILE * N_SUB, LANES), U32),
                        pltpu.SemaphoreType.DMA((3,)), pltpu.SemaphoreType.DMA((3,)), pltpu.SemaphoreType.DMA(())],
    )
    return pl.pallas_call(
        functools.partial(_scatter_kernel, bsz * nt),
        grid_spec=grid_spec,
        out_shape=jax.ShapeDtypeStruct((n_rows * N_SUB, LANES), U32),
        compiler_params=_params("arbitrary", "arbitrary"),
        name="moe_scatter",
    )(pad_start, pad_len, tail, dest4, h2t.reshape(bsz * seq * N_SUB, LANES))


def _expert_kernel(layer, blk_exp_ref, blk_first_ref, blk_next_ref, blk_slot_ref, n_used_ref,
                   xs_ref, wgu_hbm, wdn_hbm, ys_ref, wgu_f32, wdn_f32, wgu_bf, wdn_bf, sem):
    i = pl.program_id(0)

    def fetch(e, sl):
        return (pltpu.make_async_copy(wgu_hbm.at[layer, e], wgu_f32.at[sl], sem.at[0, sl]),
                pltpu.make_async_copy(wdn_hbm.at[layer, e], wdn_f32.at[sl], sem.at[1, sl]))

    @pl.when(i < n_used_ref[0])
    def _():
        @pl.when(blk_first_ref[i] == 1)
        def _():
            e, sl, nxt = blk_exp_ref[i], blk_slot_ref[i], blk_next_ref[i]

            @pl.when(i == 0)
            def _():
                for cp in fetch(e, sl):
                    cp.start()

            for cp in fetch(e, sl):
                cp.wait()
            wgu_bf[...] = wgu_f32[sl].astype(BF16)
            wdn_bf[...] = wdn_f32[sl].astype(BF16)

            @pl.when(nxt >= 0)
            def _():
                for cp in fetch(nxt, 1 - sl):
                    cp.start()

        x = _from_tiles(xs_ref, 0, MOE_TILE).astype(BF16)
        gu = _dot(x, wgu_bf[...])
        g = gu[:, :D_EXPERT]
        u = gu[:, D_EXPERT:]
        mid = (g * jax.nn.sigmoid(g) * u).astype(BF16)
        _to_tiles(ys_ref, _dot(mid, wdn_bf[...]))

    @pl.when(i >= n_used_ref[0])
    def _():
        ys_ref[...] = jnp.zeros_like(ys_ref)


def _experts(blk_exp, blk_first, blk_next, blk_slot, n_used, xs, w_gu, w_dn, layer):
    d = D_MODEL
    nb = xs.shape[0] // (MOE_TILE * N_SUB)
    tile = lambda i, *_: (i, 0)
    grid_spec = pltpu.PrefetchScalarGridSpec(
        num_scalar_prefetch=5,
        grid=(nb,),
        in_specs=[
            pl.BlockSpec((MOE_TILE * N_SUB, LANES), tile),
            pl.BlockSpec(memory_space=pl.ANY),
            pl.BlockSpec(memory_space=pl.ANY),
        ],
        out_specs=pl.BlockSpec((MOE_TILE * N_SUB, LANES), tile),
        scratch_shapes=[pltpu.VMEM((2, d, 2 * D_EXPERT), F32), pltpu.VMEM((2, D_EXPERT, d), F32),
                        pltpu.VMEM((d, 2 * D_EXPERT), BF16), pltpu.VMEM((D_EXPERT, d), BF16),
                        pltpu.SemaphoreType.DMA((2, 2))],
    )
    return pl.pallas_call(
        functools.partial(_expert_kernel, layer),
        grid_spec=grid_spec,
        out_shape=jax.ShapeDtypeStruct(xs.shape, U32),
        compiler_params=_params("arbitrary"),
        name="moe_experts",
    )(blk_exp, blk_first, blk_next, blk_slot, n_used, xs, w_gu, w_dn)


def _combine_kernel(dcur_ref, dnxt_ref, ys_hbm, x_ref, wts_ref, g2_ref, x_out, buf0, buf1, sem):
    nt = pl.num_programs(1)
    n = pl.program_id(0) * nt + pl.program_id(1)
    total = pl.num_programs(0) * nt
    tm = x_ref.shape[0]
    bufs = (buf0, buf1)

    def copy(d_ref, sl, k, r):
        src = ys_hbm.at[pl.ds(pl.multiple_of(d_ref[0, k * tm + r], N_SUB), N_SUB)]
        dst = bufs[sl].at[pl.ds(pl.multiple_of((k * tm + r) * N_SUB, N_SUB), N_SUB)]
        return pltpu.make_async_copy(src, dst, sem.at[sl])

    def drain(sl):
        pltpu.make_async_copy(ys_hbm.at[pl.ds(0, TOP_K * tm * N_SUB)], bufs[sl], sem.at[sl]).wait()

    @pl.when(n == 0)
    def _():
        for k in range(TOP_K):
            def start(r, carry, k=k):
                copy(dcur_ref, 0, k, r).start(priority=k)
                return carry
            lax.fori_loop(0, tm, start, 0, unroll=8)

    def step(sl):
        drain(sl)
        for r in range(tm):
            for k in range(TOP_K):
                copy(dnxt_ref, 1 - sl, k, r).start(priority=k)
        w = wts_ref[...]
        y = w[:, 0:1] * _from_tiles(bufs[sl], 0, tm) + w[:, 1:2] * _from_tiles(bufs[sl], tm, tm)
        x_out[...] = x_ref[...] + g2_ref[...] * y

        @pl.when(n == total - 1)
        def _():
            drain(1 - sl)

    for sl in range(2):
        pl.when(lax.rem(n, 2) == sl)(functools.partial(step, sl))


def _combine(dest_row, ys, x, wts_col, gate2):
    bsz, seq, d = x.shape
    tm = min(ROW_TILE, seq)
    nt = seq // tm

    def nxt(b, i):
        return (jnp.minimum(b * nt + i + 1, bsz * nt - 1), 0, 0)

    dest_row, dest_block = _index_blocks(dest_row, tm)
    return pl.pallas_call(
        _combine_kernel,
        grid=(bsz, nt),
        in_specs=[
            pl.BlockSpec(dest_block, lambda b, i: (b * nt + i, 0, 0), memory_space=pltpu.SMEM),
            pl.BlockSpec(dest_block, nxt, memory_space=pltpu.SMEM),
            pl.BlockSpec(memory_space=pl.ANY),
            pl.BlockSpec((None, tm, d), lambda b, i: (b, i, 0)),
            pl.BlockSpec((None, tm, TOP_K), lambda b, i: (b, i, 0)),
            pl.BlockSpec((None, 1, d), lambda b, i: (b, 0, 0)),
        ],
        out_specs=pl.BlockSpec((None, tm, d), lambda b, i: (b, i, 0)),
        out_shape=jax.ShapeDtypeStruct((bsz, seq, d), F32),
        scratch_shapes=[pltpu.VMEM((TOP_K * tm * N_SUB, LANES), U32), pltpu.VMEM((TOP_K * tm * N_SUB, LANES), U32),
                        pltpu.SemaphoreType.DMA((2,))],
        compiler_params=_params("arbitrary", "arbitrary"),
        name="moe_combine",
    )(dest_row, dest_row, ys, x, wts_col, gate2)


def _lookup(table, keys):
    hit = keys[..., None] == jnp.arange(table.shape[0], dtype=jnp.int32)
    return jnp.sum(jnp.where(hit, table, 0), axis=-1).astype(jnp.int32)


def _count_le(bounds, q):
    return jnp.sum((bounds <= q[..., None]).astype(jnp.int32), axis=-1)


def _moe(h2t, idx, wts, x, gate2, w_gu, w_dn, layer):
    bsz, seq, _ = x.shape
    n_rows = bsz * seq * TOP_K + N_EXPERTS * MOE_TILE
    nb = n_rows // MOE_TILE
    rank, cnt = _tables(idx)
    counts = cnt[:, 0].astype(jnp.int32)
    padded = ((counts + MOE_TILE - 1) // MOE_TILE) * MOE_TILE
    pend = jnp.cumsum(padded)
    pstart = pend - padded
    dest = _lookup(pstart, idx) + rank
    blk_row = jnp.arange(nb, dtype=jnp.int32) * MOE_TILE
    blk_exp = jnp.minimum(_count_le(pend, blk_row), N_EXPERTS - 1)
    blk_first = (blk_row == _lookup(pstart, blk_exp)).astype(jnp.int32)
    n_used = (pend[-1:] // MOE_TILE).astype(jnp.int32)
    owns = counts > 0
    eid = jnp.arange(N_EXPERTS, dtype=jnp.int32)
    later = lax.cummin(jnp.where(owns, eid, N_EXPERTS), axis=0, reverse=True)
    nxt = jnp.concatenate([later[1:], jnp.full((1,), N_EXPERTS, jnp.int32)])
    blk_next = _lookup(jnp.where(nxt < N_EXPERTS, nxt, -1), blk_exp)
    blk_slot = _lookup((jnp.cumsum(owns.astype(jnp.int32)) - 1) % 2, blk_exp)
    tail = jnp.concatenate([pend[-1:], (n_rows - pend[-1:]) // MOE_TILE]).astype(jnp.int32)
    dest_row = dest * N_SUB
    xs = _scatter(dest_row, (pstart + counts).astype(jnp.int32), (padded - counts).astype(jnp.int32), tail, h2t, n_rows)
    ys = _experts(blk_exp, blk_first, blk_next, blk_slot, n_used, xs, w_gu, w_dn, layer)
    return _combine(dest_row, ys, x, wts.transpose(0, 2, 1), gate2)


def _rnn_in_kernel(x_ref, g_ref, sh_ref, sc_ref, w_ref, gate_out, xb_out):
    h = _modulate(x_ref[...], g_ref[...], sh_ref[...], sc_ref[...])
    u = _dot(h.astype(BF16), w_ref[...])
    gate_out[...] = jax.nn.gelu(u[:, :D_RNN]).astype(BF16)
    xb_out[...] = u[:, D_RNN:]


def _rnn_in(x, gain, shift, scale, w_in):
    bsz, seq, d = x.shape
    tm = min(MIX_TILE, seq)
    row = lambda b, i: (b, i, 0)
    per_b = lambda b, i: (b, 0, 0)
    const = lambda b, i: (0, 0)
    return pl.pallas_call(
        _rnn_in_kernel,
        grid=(bsz, seq // tm),
        in_specs=[
            pl.BlockSpec((None, tm, d), row),
            pl.BlockSpec(gain.shape, const),
            pl.BlockSpec((None, 1, d), per_b),
            pl.BlockSpec((None, 1, d), per_b),
            pl.BlockSpec(w_in.shape, const),
        ],
        out_specs=[pl.BlockSpec((None, tm, D_RNN), row), pl.BlockSpec((None, tm, D_RNN), row)],
        out_shape=[jax.ShapeDtypeStruct((bsz, seq, D_RNN), BF16),
                   jax.ShapeDtypeStruct((bsz, seq, D_RNN), F32)],
        compiler_params=_params("arbitrary", "arbitrary"),
        name="rnn_in",
    )(x, gain, shift, scale, w_in)


def _lru_kernel(xb_ref, cw_ref, cb_ref, wcat_ref, bcat_ref, lam_ref, hs_ref,
                xi_ref, af_ref, bf_ref, ab_ref, bb_ref, hf_ref, hb_ref, sum_ref):
    seq, c = xb_ref.shape
    seg_len = seq // SUBLANES
    n_slab = c // LANES
    n_rows = seg_len * SUBLANES
    halo = (CONV_W // 2) * SUBLANES
    row = lax.broadcasted_iota(jnp.int32, (SUBLANES, LANES), 0)
    for sl in range(n_slab):
        lanes = slice(sl * LANES, (sl + 1) * LANES)
        for g in range(SUBLANES):
            xi_ref[sl, pl.ds(halo + g, seg_len, stride=SUBLANES), :] = xb_ref[g * seg_len:(g + 1) * seg_len, lanes]
        for back in (1, 2):
            prev = xi_ref[sl, halo + (seg_len - back) * SUBLANES:halo + (seg_len - back + 1) * SUBLANES, :]
            xi_ref[sl, halo - back * SUBLANES:halo - (back - 1) * SUBLANES, :] = jnp.where(
                row == 0, 0.0, pltpu.roll(prev, 1, 0))
        nxt = xi_ref[sl, halo:halo + SUBLANES, :]
        xi_ref[sl, halo + n_rows:halo + n_rows + SUBLANES, :] = jnp.where(
            row == SUBLANES - 1, 0.0, pltpu.roll(nxt, SUBLANES - 1, 0))

    cw = cw_ref[...]
    cb = cb_ref[...]
    lam = lam_ref[...]
    neg = -lam
    softplus = jnp.maximum(neg, 0.0) + jnp.log1p(jnp.exp(-jnp.abs(neg)))
    half_rate = (-0.5 * LRU_C) * softplus
    rows = min(SCAN_ROWS, n_rows)
    n_chunks = n_rows // rows

    for ci in range(n_chunks):
        i0 = ci * rows
        taps = []
        for k in range(CONV_W):
            lo = halo + i0 + (k - CONV_W // 2) * SUBLANES
            taps.append(jnp.concatenate([xi_ref[sl, lo:lo + rows, :] for sl in range(n_slab)], axis=1))
        xc = cb
        for k in range(CONV_W):
            xc = xc + taps[k] * cw[k:k + 1, :]
        xcb = xc.astype(BF16)
        xh = 0.5 * xc
        for dirn, (a_ref, b_ref) in enumerate(((af_ref, bf_ref), (ab_ref, bb_ref))):
            cols = slice(2 * dirn * c, 2 * (dirn + 1) * c)
            th = jnp.tanh(_dot(xcb, wcat_ref[:, cols]) + bcat_ref[:, cols])
            hr = half_rate[dirn:dirn + 1, :]
            log_a = hr * th[:, :c] + hr
            a = jnp.exp(log_a)
            m2 = jnp.tanh(log_a) * (-1.0 - a * a)
            mult = jnp.where(m2 > 0.0, m2 * lax.rsqrt(m2), 0.0)
            if dirn == 0 and ci == 0:
                mult = jnp.where(lax.broadcasted_iota(jnp.int32, mult.shape, 0) == 0, 1.0, mult)
            if dirn == 1 and ci == n_chunks - 1:
                mult = jnp.where(lax.broadcasted_iota(jnp.int32, mult.shape, 0) == rows - 1, 1.0, mult)
            b = mult * (th[:, c:] + 1.0) * xh
            for sl in range(n_slab):
                a_ref[sl, i0:i0 + rows, :] = a[:, sl * LANES:(sl + 1) * LANES]
                b_ref[sl, i0:i0 + rows, :] = b[:, sl * LANES:(sl + 1) * LANES]

    def step_rows(cidx):
        fwd = pl.ds(pl.multiple_of(cidx * SUBLANES, SUBLANES), SUBLANES)
        bwd = pl.ds(pl.multiple_of((seg_len - 1 - cidx) * SUBLANES, SUBLANES), SUBLANES)
        return fwd, bwd

    zero = jnp.zeros((SUBLANES, LANES), F32)
    one = jnp.ones((SUBLANES, LANES), F32)

    def totals(cidx, carry):
        fwd, bwd = step_rows(cidx)
        out = []
        for sl in range(n_slab):
            hf, pf, hb, pb = carry[sl]
            af, ab = af_ref[sl, fwd, :], ab_ref[sl, bwd, :]
            out.append((af * hf + bf_ref[sl, fwd, :], af * pf, ab * hb + bb_ref[sl, bwd, :], ab * pb))
        return tuple(out)
    tot = lax.fori_loop(0, seg_len, totals, tuple((zero, one, zero, one) for _ in range(n_slab)), unroll=8)

    enter = []
    for sl in range(n_slab):
        hf, pf, hb, pb = tot[sl]
        cf, cbk = zero, zero
        for _ in range(SUBLANES - 1):
            cf = jnp.where(row == 0, 0.0, pltpu.roll(hf + pf * cf, 1, 0))
            cbk = jnp.where(row == SUBLANES - 1, 0.0, pltpu.roll(hb + pb * cbk, SUBLANES - 1, 0))
        enter.append((cf, cbk))

    def states(meet, cidx, carry):
        fwd, bwd = step_rows(cidx)
        out = []
        for sl in range(n_slab):
            hf, hb = carry[sl]
            hf = af_ref[sl, fwd, :] * hf + bf_ref[sl, fwd, :]
            hb = ab_ref[sl, bwd, :] * hb + bb_ref[sl, bwd, :]
            if meet:
                sum_ref[sl, fwd, :] = hf + hb_ref[sl, fwd, :]
                sum_ref[sl, bwd, :] = hb + hf_ref[sl, bwd, :]
            else:
                hf_ref[sl, fwd, :] = hf
                hb_ref[sl, bwd, :] = hb
            out.append((hf, hb))
        return tuple(out)
    mid = lax.fori_loop(0, seg_len // 2, functools.partial(states, False), tuple(enter), unroll=8)
    lax.fori_loop(seg_len // 2, seg_len, functools.partial(states, True), mid, unroll=8)

    for g in range(SUBLANES):
        for sl in range(n_slab):
            hs_ref[g * seg_len:(g + 1) * seg_len, sl * LANES:(sl + 1) * LANES] = (
                sum_ref[sl, pl.ds(g, seg_len, stride=SUBLANES), :]).astype(BF16)


def _lru(xb, conv_w, conv_b, wcat, bcat, lam):
    bsz, seq, _ = xb.shape
    c = RNN_BW
    blk = lambda b, n: (b, 0, n)
    return pl.pallas_call(
        _lru_kernel,
        grid=(bsz, RNN_BLOCKS),
        in_specs=[
            pl.BlockSpec((None, seq, c), blk),
            pl.BlockSpec((CONV_W, c), lambda b, n: (0, n)),
            pl.BlockSpec((1, c), lambda b, n: (0, n)),
            pl.BlockSpec((None, c, 4 * c), lambda b, n: (n, 0, 0)),
            pl.BlockSpec((None, 1, 4 * c), lambda b, n: (n, 0, 0)),
            pl.BlockSpec((2, c), lambda b, n: (0, n)),
        ],
        out_specs=pl.BlockSpec((None, seq, c), blk),
        out_shape=jax.ShapeDtypeStruct((bsz, seq, D_RNN), BF16),
        scratch_shapes=[pltpu.VMEM((c // LANES, seq + (CONV_W - 1) * SUBLANES, LANES), F32)]
        + [pltpu.VMEM((c // LANES, seq, LANES), F32)] * 7,
        compiler_params=_params("arbitrary", "arbitrary"),
        name="rglru_scan",
    )(xb, conv_w, conv_b, wcat, bcat, lam)


def _mla_weights(w_in, w_q_b, w_kv_b, q_norm, k_norm):
    half = QK_ROPE // 2

    def slab(t):
        return jnp.pad(t, [(0, 0)] * (t.ndim - 1) + [(0, LANES - QK_HEAD)])

    def rot_slab(t):
        rope = t[..., QK_NOPE:]
        swapped = jnp.concatenate([jnp.zeros_like(t[..., :QK_NOPE]), rope[..., half:], rope[..., :half]], axis=-1)
        return slab(swapped)

    kpe = jnp.pad(w_in[:, Q_LORA + KV_LORA:], ((0, 0), (QK_NOPE, 0)))
    w_in_p = jnp.concatenate([w_in[:, :Q_LORA + KV_LORA], slab(kpe), rot_slab(kpe)], axis=1).astype(BF16)
    wq = w_q_b.reshape(Q_LORA, N_HEADS, QK_HEAD)
    wq_p = slab(wq).reshape(Q_LORA, N_HEADS * LANES).astype(BF16)
    band = jnp.concatenate([wq[..., QK_NOPE + half:], wq[..., QK_NOPE:QK_NOPE + half]], axis=-1)
    band = band.reshape(Q_LORA, N_HEADS // 2, 2 * QK_ROPE)
    wq_rot = jnp.pad(band, ((0, 0), (0, 0), (QK_NOPE, 0))).reshape(Q_LORA, N_HEADS // 2 * LANES).astype(BF16)
    wkv = w_kv_b.reshape(KV_LORA, N_HEADS, QK_NOPE + V_HEAD)
    wk = jnp.pad(wkv[:, :, :QK_NOPE], ((0, 0), (0, 0), (0, LANES - QK_NOPE))).reshape(KV_LORA, N_HEADS * LANES)
    wv = wkv[:, :, QK_NOPE:].reshape(KV_LORA, N_HEADS * V_HEAD)
    w_kv_p = jnp.concatenate([wk, wv], axis=1).astype(BF16)
    gains = lambda g: jnp.stack([slab(g), rot_slab(g)], axis=0)
    return w_in_p, wq_p, wq_rot, w_kv_p, gains(q_norm), gains(k_norm)


def _rope_kernel(pos_ref, freq_ref, cos_out, sin_out):
    ang = freq_ref[...] * pos_ref[...].astype(F32)
    cos, sin = jnp.cos(ang), jnp.sin(ang)
    seq = ang.shape[1]
    fill = lambda value, n: jnp.full((n, seq), value, F32)
    cos_out[...] = jnp.concatenate([fill(1.0, QK_NOPE), cos, cos, fill(1.0, LANES - QK_HEAD)], axis=0).T
    sin_out[...] = jnp.concatenate([fill(0.0, QK_NOPE), -sin, sin, fill(0.0, LANES - QK_HEAD)], axis=0).T


def _rope_tables(positions):
    half = QK_ROPE // 2
    inv_freq = ROPE_THETA ** (-jnp.arange(half, dtype=F32) / half)
    bsz, seq = positions.shape
    table = jax.ShapeDtypeStruct((bsz, seq, LANES), F32)
    return pl.pallas_call(
        _rope_kernel,
        grid=(bsz,),
        in_specs=[pl.BlockSpec((None, 1, seq), lambda b: (b, 0, 0)), pl.BlockSpec((half, 1), lambda b: (0, 0))],
        out_specs=[pl.BlockSpec((None, seq, LANES), lambda b: (b, 0, 0))] * 2,
        out_shape=[table, table],
        compiler_params=_params("arbitrary"),
        name="rope_tables",
    )(positions.reshape(bsz, 1, seq), inv_freq.reshape(half, 1))


def _router_weights(w_router, router_bias):
    perm = (jnp.arange(N_EXPERTS) % N_GROUPS) * EXPERTS_PER_GROUP + jnp.arange(N_EXPERTS) // N_GROUPS
    w = w_router[:, perm]
    hi = w.astype(BF16)
    lo = (w - hi.astype(F32)).astype(BF16)
    z = jnp.zeros_like(hi)
    wr1 = jnp.concatenate([hi, lo, z, z], axis=1)
    wr2 = jnp.concatenate([z, z, hi, z], axis=1)
    return wr1, wr2, router_bias[perm].reshape(N_EXPERTS, 1).astype(F32)


def kernel(x, c, positions, norm_mix, norm_ffn, w_ada, b_ada, mla_w_in, mla_q_a_norm, mla_kv_a_norm, mla_w_q_b, mla_w_kv_b, mla_q_norm, mla_k_norm, mla_w_o, rnn_w_in, rnn_conv_w, rnn_conv_b, rnn_lam_f, rnn_w_rf, rnn_b_rf, rnn_w_if, rnn_b_if, rnn_lam_b, rnn_w_rb, rnn_b_rb, rnn_w_ib, rnn_b_ib, rnn_w_o, w_router, router_bias, moe_w_gu, moe_w_dn):
    bsz, seq, d = x.shape
    depth = w_ada.shape[0]
    mod = _ada(c, w_ada, b_ada)
    wr1, wr2, rbias = _router_weights(w_router, router_bias)
    cos_t, sin_t = _rope_tables(positions)
    vec = lambda v: v.reshape(1, -1)
    for i in range(depth):
        sh1, sc1, g1, sh2, sc2, g2 = [mod[i, :, k * d:(k + 1) * d].reshape(bsz, 1, d) for k in range(6)]
        j = i // 2
        if i % 2 == 0:
            w_in_p, wq, wq_rot, wkv, qn, kn = _mla_weights(mla_w_in[j], mla_w_q_b[j], mla_w_kv_b[j],
                                                           mla_q_norm[j], mla_k_norm[j])
            q, k, v = _mla_in(x, vec(norm_mix[i]), sh1, sc1, w_in_p, vec(mla_q_a_norm[j]),
                              vec(mla_kv_a_norm[j]), wq, wq_rot, wkv, qn, kn, cos_t, sin_t)
            a = _attention(q, k, v, mla_q_norm[j], mla_k_norm[j])
            hs = None
            w_o = mla_w_o[j].astype(BF16)
        else:
            a, xb = _rnn_in(x, vec(norm_mix[i]), sh1, sc1, rnn_w_in[j].astype(BF16))
            wcat = (0.5 * jnp.concatenate([rnn_w_rf[j], rnn_w_if[j], rnn_w_rb[j], rnn_w_ib[j]], axis=-1)).astype(BF16)
            bcat = jnp.stack([b.reshape(RNN_BLOCKS, RNN_BW) for b in
                              (rnn_b_rf[j], rnn_b_if[j], rnn_b_rb[j], rnn_b_ib[j])], axis=1)
            bcat = 0.5 * bcat.reshape(RNN_BLOCKS, 1, 4 * RNN_BW)
            lam = jnp.stack([rnn_lam_f[j], rnn_lam_b[j]], axis=0)
            hs = _lru(xb, rnn_conv_w[j], vec(rnn_conv_b[j]), wcat, bcat, lam)
            w_o = rnn_w_o[j].astype(BF16)
        x, h2, idx, wts = _mix_out(a, hs, x, w_o, g1, vec(norm_ffn[i]), sh2, sc2, wr1, wr2, rbias)
        x = _moe(h2, idx, wts, x, g2, moe_w_gu, moe_w_dn, i)
    return x
```
